```python
import math
import jax, jax.numpy as jnp
from jax import lax
import numpy as np

D_MODEL = 1024
BATCH = 8
SEQ = 16384
DEPTH = 2

CONV_DIM = 512
CONV_WIDTH = 31
GDN_HEADS = 4
GDN_DK = 128
GDN_DV = 128
GDN_QK = GDN_HEADS * GDN_DK
GDN_V = GDN_HEADS * GDN_DV
GDN_CONV = 4
GDN_CHUNK = 64
SGU_GROUPS = 4
SGU_GROUP_DIM = 128
SGU_DIM = SGU_GROUPS * SGU_GROUP_DIM
SGU_CHUNK = 128
D_FF = 4 * D_MODEL
DN_ALPHA = (2 * DEPTH) ** 0.25
DN_BETA = (8 * DEPTH) ** -0.25
LN_EPS = 1e-5
RMS_EPS = 1e-6
PROJ_SIZES = (2 * CONV_DIM, GDN_QK, GDN_QK, GDN_V, GDN_V, GDN_HEADS, GDN_HEADS, 2 * SGU_DIM, 3 * D_MODEL)
PROJ_COLS = sum(PROJ_SIZES)

kernel_name = "gated_parallel_conv_deltanet_sgu_deepnorm"


def layer_norm(x, g, b):
    xf = x.astype(jnp.float32)
    mu = jnp.mean(xf, -1, keepdims=True)
    var = jnp.mean(jnp.square(xf - mu), -1, keepdims=True)
    return ((xf - mu) * lax.rsqrt(var + LN_EPS) * g.astype(jnp.float32) + b.astype(jnp.float32)).astype(x.dtype)


def causal_depthwise_conv(x, w):
    width, ch = w.shape
    return lax.conv_general_dilated(
        x, w[:, None, :].astype(x.dtype), window_strides=(1,), padding=[(width - 1, 0)],
        dimension_numbers=("NWC", "WIO", "NWC"), feature_group_count=ch)


def l2norm(x):
    xf = x.astype(jnp.float32)
    return xf * lax.rsqrt(jnp.sum(xf * xf, -1, keepdims=True) + RMS_EPS)


def conformer_conv(a, w_dw, b_dw, ln_g, ln_b):
    a1, a2 = jnp.split(a, 2, axis=-1)
    h = a1 * jax.nn.sigmoid(a2)
    h = causal_depthwise_conv(h, w_dw) + b_dw
    h = layer_norm(h, ln_g, ln_b)
    return jax.nn.silu(h)


def chunked_gated_delta_rule(q, k, v, g, beta):
    bsz, seq, nh, dk = q.shape
    dv = v.shape[-1]
    c = GDN_CHUNK
    n = seq // c

    def chunks(t):
        return jnp.swapaxes(t.reshape(bsz, n, c, nh, *t.shape[3:]), 2, 3)

    q, k, v, g, beta = chunks(q), chunks(k), chunks(v), chunks(g), chunks(beta)
    gam = jnp.cumsum(g, axis=-1)
    causal = jnp.tril(jnp.ones((c, c), bool))
    strict = jnp.tril(jnp.ones((c, c), bool), -1)
    decay = jnp.exp(jnp.where(causal, gam[..., :, None] - gam[..., None, :], -jnp.inf))
    kk = jnp.einsum("bnhid,bnhjd->bnhij", k, k)
    lower = jnp.where(strict, beta[..., None] * kk * decay, 0.0)
    t_mat = lower + jnp.eye(c, dtype=lower.dtype)
    rhs = jnp.concatenate([beta[..., None] * v, beta[..., None] * k * jnp.exp(gam)[..., None]], -1)
    sol = lax.linalg.triangular_solve(t_mat, rhs, left_side=True, lower=True, unit_diagonal=True)
    u, w = sol[..., :dv], sol[..., dv:]
    a_qk = jnp.einsum("bnhid,bnhjd->bnhij", q, k) * decay
    q_dec = q * jnp.exp(gam)[..., None]
    g_last = gam[..., -1]
    k_dec = k * jnp.exp(g_last[..., None] - gam)[..., None]

    def step(state, inp):
        q_i, k_i, u_i, w_i, a_i, gl = inp
        v_new = u_i - jnp.einsum("bhcd,bhde->bhce", w_i, state)
        o = jnp.einsum("bhcd,bhde->bhce", q_i, state) + jnp.einsum("bhij,bhje->bhie", a_i, v_new)
        state = state * jnp.exp(gl)[..., None, None] + jnp.einsum("bhcd,bhce->bhde", k_i, v_new)
        return state, o

    xs = (jnp.moveaxis(q_dec, 1, 0), jnp.moveaxis(k_dec, 1, 0), jnp.moveaxis(u, 1, 0),
          jnp.moveaxis(w, 1, 0), jnp.moveaxis(a_qk, 1, 0), jnp.moveaxis(g_last, 1, 0))
    s0 = jnp.zeros((bsz, nh, dk, dv), jnp.float32)
    _, o = lax.scan(step, s0, xs)
    return jnp.swapaxes(jnp.moveaxis(o, 0, 1), 2, 3).reshape(bsz, seq, nh, dv)


def gated_deltanet(q, k, v, z, b_logit, a_logit, conv_q, conv_k, conv_v, a_log, dt_bias, norm_g):
    bsz, seq, _ = q.shape
    dtype = q.dtype
    q = jax.nn.silu(causal_depthwise_conv(q, conv_q)).reshape(bsz, seq, GDN_HEADS, GDN_DK)
    k = jax.nn.silu(causal_depthwise_conv(k, conv_k)).reshape(bsz, seq, GDN_HEADS, GDN_DK)
    v = jax.nn.silu(causal_depthwise_conv(v, conv_v)).reshape(bsz, seq, GDN_HEADS, GDN_DV)
    q = l2norm(q) * (GDN_DK ** -0.5)
    k = l2norm(k)
    beta = jax.nn.sigmoid(b_logit.astype(jnp.float32))
    g = -jnp.exp(a_log.astype(jnp.float32)) * jax.nn.softplus(
        a_logit.astype(jnp.float32) + dt_bias.astype(jnp.float32))
    o = chunked_gated_delta_rule(q, k, v.astype(jnp.float32), g, beta)
    zf = z.reshape(bsz, seq, GDN_HEADS, GDN_DV).astype(jnp.float32)
    o = o * lax.rsqrt(jnp.mean(o * o, -1, keepdims=True) + RMS_EPS) * norm_g.astype(jnp.float32)
    o = o * jax.nn.silu(zf)
    return o.reshape(bsz, seq, GDN_V).astype(dtype)


def spatial_gating(uv, ln_g, ln_b, w_s, b_s):
    bsz, seq, _ = uv.shape
    n = seq // SGU_CHUNK
    uv = jax.nn.gelu(uv)
    u, v = jnp.split(uv, 2, axis=-1)
    v = layer_norm(v, ln_g, ln_b)
    v = v.reshape(bsz, n, SGU_CHUNK, SGU_GROUPS, SGU_GROUP_DIM)
    w_causal = jnp.tril(w_s).astype(v.dtype)
    mixed = jnp.einsum("gpq,bnqgc->bnpgc", w_causal, v) + jnp.swapaxes(b_s, 0, 1)[None, None, :, :, None].astype(v.dtype)
    return u * mixed.reshape(bsz, seq, SGU_DIM)


def token_mixer(x, w_in, b_gate, conv_dw_w, conv_dw_b, conv_ln_g, conv_ln_b, w_pa,
                gdn_conv_q, gdn_conv_k, gdn_conv_v, gdn_a_log, gdn_dt_bias, gdn_norm_g, w_pb,
                sgu_ln_g, sgu_ln_b, sgu_w_s, sgu_b_s, w_pc, w_o):
    p = x @ w_in
    split_points = [int(s) for s in np.cumsum(PROJ_SIZES)[:-1]]
    a, q, k, v, z, b_logit, a_logit, uv, gate_logit = jnp.split(p, split_points, axis=-1)
    y_a = conformer_conv(a, conv_dw_w, conv_dw_b, conv_ln_g, conv_ln_b) @ w_pa
    y_b = gated_deltanet(q, k, v, z, b_logit, a_logit, gdn_conv_q, gdn_conv_k, gdn_conv_v,
                         gdn_a_log, gdn_dt_bias, gdn_norm_g) @ w_pb
    y_c = spatial_gating(uv, sgu_ln_g, sgu_ln_b, sgu_w_s, sgu_b_s) @ w_pc
    s_a, s_b, s_c = jnp.split(jax.nn.sigmoid(gate_logit + b_gate), 3, axis=-1)
    return (s_a * y_a + s_b * y_b + s_c * y_c) @ w_o


def _fwd_setup_inputs(seed: int = 0) -> dict:
    key = jax.random.key(seed)
    ks = jax.random.split(key, 32)
    L = DEPTH
    f32 = jnp.float32

    def nrm(i, shape, scale):
        return jax.random.normal(ks[i], shape, f32) * scale

    dt = jnp.exp(jax.random.uniform(ks[14], (L, GDN_HEADS), f32) * (math.log(0.1) - math.log(0.001)) + math.log(0.001))
    return {
        "x": nrm(0, (BATCH, SEQ, D_MODEL), 1.0),
        "ln_in_g": 1.0 + nrm(1, (D_MODEL,), 0.02),
        "ln_in_b": nrm(2, (D_MODEL,), 0.02),
        "w_in": nrm(3, (L, D_MODEL, PROJ_COLS), D_MODEL ** -0.5),
        "b_gate": nrm(4, (L, 3 * D_MODEL), 0.1),
        "conv_dw_w": nrm(5, (L, CONV_WIDTH, CONV_DIM), CONV_WIDTH ** -0.5),
        "conv_dw_b": nrm(6, (L, CONV_DIM), 0.02),
        "conv_ln_g": 1.0 + nrm(7, (L, CONV_DIM), 0.02),
        "conv_ln_b": nrm(8, (L, CONV_DIM), 0.02),
        "w_pa": nrm(9, (L, CONV_DIM, D_MODEL), DN_BETA * CONV_DIM ** -0.5),
        "gdn_conv_q": nrm(10, (L, GDN_CONV, GDN_QK), GDN_CONV ** -0.5),
        "gdn_conv_k": nrm(11, (L, GDN_CONV, GDN_QK), GDN_CONV ** -0.5),
        "gdn_conv_v": nrm(12, (L, GDN_CONV, GDN_V), GDN_CONV ** -0.5),
        "gdn_a_log": jnp.log(jax.random.uniform(ks[13], (L, GDN_HEADS), f32, minval=1.0, maxval=16.0)),
        "gdn_dt_bias": dt + jnp.log(-jnp.expm1(-dt)),
        "gdn_norm_g": 1.0 + nrm(15, (L, GDN_DV), 0.02),
        "w_pb": nrm(16, (L, GDN_V, D_MODEL), DN_BETA * GDN_V ** -0.5),
        "sgu_ln_g": 1.0 + nrm(17, (L, SGU_DIM), 0.02),
        "sgu_ln_b": nrm(18, (L, SGU_DIM), 0.02),
        "sgu_w_s": nrm(19, (L, SGU_GROUPS, SGU_CHUNK, SGU_CHUNK), SGU_CHUNK ** -0.5),
        "sgu_b_s": 1.0 + nrm(20, (L, SGU_GROUPS, SGU_CHUNK), 0.02),
        "w_pc": nrm(21, (L, SGU_DIM, D_MODEL), DN_BETA * SGU_DIM ** -0.5),
        "w_o": nrm(22, (L, D_MODEL, D_MODEL), DN_BETA * D_MODEL ** -0.5),
        "ln1_g": 1.0 + nrm(23, (L, D_MODEL), 0.02),
        "ln1_b": nrm(24, (L, D_MODEL), 0.02),
        "w_ff1": nrm(25, (L, D_MODEL, D_FF), D_MODEL ** -0.5),
        "b_ff1": nrm(26, (L, D_FF), 0.02),
        "w_ff2": nrm(27, (L, D_FF, D_MODEL), DN_BETA * D_FF ** -0.5),
        "b_ff2": nrm(28, (L, D_MODEL), 0.02),
        "ln2_g": 1.0 + nrm(29, (L, D_MODEL), 0.02),
        "ln2_b": nrm(30, (L, D_MODEL), 0.02),
    }


def _fwd_reference(x, ln_in_g, ln_in_b, w_in, b_gate, conv_dw_w, conv_dw_b, conv_ln_g, conv_ln_b, w_pa,
              gdn_conv_q, gdn_conv_k, gdn_conv_v, gdn_a_log, gdn_dt_bias, gdn_norm_g, w_pb,
              sgu_ln_g, sgu_ln_b, sgu_w_s, sgu_b_s, w_pc, w_o, ln1_g, ln1_b,
              w_ff1, b_ff1, w_ff2, b_ff2, ln2_g, ln2_b):
    x = layer_norm(x, ln_in_g, ln_in_b)
    for l in range(DEPTH):
        m = token_mixer(x, w_in[l], b_gate[l], conv_dw_w[l], conv_dw_b[l], conv_ln_g[l], conv_ln_b[l], w_pa[l],
                        gdn_conv_q[l], gdn_conv_k[l], gdn_conv_v[l], gdn_a_log[l], gdn_dt_bias[l], gdn_norm_g[l], w_pb[l],
                        sgu_ln_g[l], sgu_ln_b[l], sgu_w_s[l], sgu_b_s[l], w_pc[l], w_o[l])
        x = layer_norm(DN_ALPHA * x + m, ln1_g[l], ln1_b[l])
        h = jnp.square(jax.nn.relu(x @ w_ff1[l] + b_ff1[l]))
        x = layer_norm(DN_ALPHA * x + (h @ w_ff2[l] + b_ff2[l]), ln2_g[l], ln2_b[l])
    return x


import jax as _jax
import jax.numpy as _jnp

TWIN_FORMAT = 'train_step'
FWD_PARAMS = ['x', 'ln_in_g', 'ln_in_b', 'w_in', 'b_gate', 'conv_dw_w', 'conv_dw_b', 'conv_ln_g', 'conv_ln_b', 'w_pa', 'gdn_conv_q', 'gdn_conv_k', 'gdn_conv_v', 'gdn_a_log', 'gdn_dt_bias', 'gdn_norm_g', 'w_pb', 'sgu_ln_g', 'sgu_ln_b', 'sgu_w_s', 'sgu_b_s', 'w_pc', 'w_o', 'ln1_g', 'ln1_b', 'w_ff1', 'b_ff1', 'w_ff2', 'b_ff2', 'ln2_g', 'ln2_b']
TWIN_WEIGHTS = ['ln_in_g', 'ln_in_b', 'w_in', 'b_gate', 'conv_dw_w', 'conv_dw_b', 'conv_ln_g', 'conv_ln_b', 'w_pa', 'gdn_conv_q', 'gdn_conv_k', 'gdn_conv_v', 'gdn_a_log', 'gdn_dt_bias', 'gdn_norm_g', 'w_pb', 'sgu_ln_g', 'sgu_ln_b', 'sgu_w_s', 'sgu_b_s', 'w_pc', 'w_o', 'ln1_g', 'ln1_b', 'w_ff1', 'b_ff1', 'w_ff2', 'b_ff2', 'ln2_g', 'ln2_b']
TWIN_DIFF_INPUT = 'x'
TWIN_INPUTS = ['x', 'ln_in_g', 'ln_in_b', 'w_in', 'b_gate', 'conv_dw_w', 'conv_dw_b', 'conv_ln_g', 'conv_ln_b', 'w_pa', 'gdn_conv_q', 'gdn_conv_k', 'gdn_conv_v', 'gdn_a_log', 'gdn_dt_bias', 'gdn_norm_g', 'w_pb', 'sgu_ln_g', 'sgu_ln_b', 'sgu_w_s', 'sgu_b_s', 'w_pc', 'w_o', 'ln1_g', 'ln1_b', 'w_ff1', 'b_ff1', 'w_ff2', 'b_ff2', 'ln2_g', 'ln2_b', 'loss_target', 'm_ln_in_g', 'm_ln_in_b', 'm_w_in', 'm_b_gate', 'm_conv_dw_w', 'm_conv_dw_b', 'm_conv_ln_g', 'm_conv_ln_b', 'm_w_pa', 'm_gdn_conv_q', 'm_gdn_conv_k', 'm_gdn_conv_v', 'm_gdn_a_log', 'm_gdn_dt_bias', 'm_gdn_norm_g', 'm_w_pb', 'm_sgu_ln_g', 'm_sgu_ln_b', 'm_sgu_w_s', 'm_sgu_b_s', 'm_w_pc', 'm_w_o', 'm_ln1_g', 'm_ln1_b', 'm_w_ff1', 'm_b_ff1', 'm_w_ff2', 'm_b_ff2', 'm_ln2_g', 'm_ln2_b', 'v_ln_in_g', 'v_ln_in_b', 'v_w_in', 'v_b_gate', 'v_conv_dw_w', 'v_conv_dw_b', 'v_conv_ln_g', 'v_conv_ln_b', 'v_w_pa', 'v_gdn_conv_q', 'v_gdn_conv_k', 'v_gdn_conv_v', 'v_gdn_a_log', 'v_gdn_dt_bias', 'v_gdn_norm_g', 'v_w_pb', 'v_sgu_ln_g', 'v_sgu_ln_b', 'v_sgu_w_s', 'v_sgu_b_s', 'v_w_pc', 'v_w_o', 'v_ln1_g', 'v_ln1_b', 'v_w_ff1', 'v_b_ff1', 'v_w_ff2', 'v_b_ff2', 'v_ln2_g', 'v_ln2_b']
TWIN_OUTPUTS = ['loss', 'grad_x', 'grad_ln_in_g', 'grad_ln_in_b', 'grad_w_in', 'grad_b_gate', 'grad_conv_dw_w', 'grad_conv_dw_b', 'grad_conv_ln_g', 'grad_conv_ln_b', 'grad_w_pa', 'grad_gdn_conv_q', 'grad_gdn_conv_k', 'grad_gdn_conv_v', 'grad_gdn_a_log', 'grad_gdn_dt_bias', 'grad_gdn_norm_g', 'grad_w_pb', 'grad_sgu_ln_g', 'grad_sgu_ln_b', 'grad_sgu_w_s', 'grad_sgu_b_s', 'grad_w_pc', 'grad_w_o', 'grad_ln1_g', 'grad_ln1_b', 'grad_w_ff1', 'grad_b_ff1', 'grad_w_ff2', 'grad_b_ff2', 'grad_ln2_g', 'grad_ln2_b', 'delta_ln_in_g', 'delta_ln_in_b', 'delta_w_in', 'delta_b_gate', 'delta_conv_dw_w', 'delta_conv_dw_b', 'delta_conv_ln_g', 'delta_conv_ln_b', 'delta_w_pa', 'delta_gdn_conv_q', 'delta_gdn_conv_k', 'delta_gdn_conv_v', 'delta_gdn_a_log', 'delta_gdn_dt_bias', 'delta_gdn_norm_g', 'delta_w_pb', 'delta_sgu_ln_g', 'delta_sgu_ln_b', 'delta_sgu_w_s', 'delta_sgu_b_s', 'delta_w_pc', 'delta_w_o', 'delta_ln1_g', 'delta_ln1_b', 'delta_w_ff1', 'delta_b_ff1', 'delta_w_ff2', 'delta_b_ff2', 'delta_ln2_g', 'delta_ln2_b', 'new_m_ln_in_g', 'new_m_ln_in_b', 'new_m_w_in', 'new_m_b_gate', 'new_m_conv_dw_w', 'new_m_conv_dw_b', 'new_m_conv_ln_g', 'new_m_conv_ln_b', 'new_m_w_pa', 'new_m_gdn_conv_q', 'new_m_gdn_conv_k', 'new_m_gdn_conv_v', 'new_m_gdn_a_log', 'new_m_gdn_dt_bias', 'new_m_gdn_norm_g', 'new_m_w_pb', 'new_m_sgu_ln_g', 'new_m_sgu_ln_b', 'new_m_sgu_w_s', 'new_m_sgu_b_s', 'new_m_w_pc', 'new_m_w_o', 'new_m_ln1_g', 'new_m_ln1_b', 'new_m_w_ff1', 'new_m_b_ff1', 'new_m_w_ff2', 'new_m_b_ff2', 'new_m_ln2_g', 'new_m_ln2_b', 'new_v_ln_in_g', 'new_v_ln_in_b', 'new_v_w_in', 'new_v_b_gate', 'new_v_conv_dw_w', 'new_v_conv_dw_b', 'new_v_conv_ln_g', 'new_v_conv_ln_b', 'new_v_w_pa', 'new_v_gdn_conv_q', 'new_v_gdn_conv_k', 'new_v_gdn_conv_v', 'new_v_gdn_a_log', 'new_v_gdn_dt_bias', 'new_v_gdn_norm_g', 'new_v_w_pb', 'new_v_sgu_ln_g', 'new_v_sgu_ln_b', 'new_v_sgu_w_s', 'new_v_sgu_b_s', 'new_v_w_pc', 'new_v_w_o', 'new_v_ln1_g', 'new_v_ln1_b', 'new_v_w_ff1', 'new_v_b_ff1', 'new_v_w_ff2', 'new_v_b_ff2', 'new_v_ln2_g', 'new_v_ln2_b']
TWIN_LEAF_KINDS = {'loss': 'loss', 'grad_x': 'grad_x', 'grad_ln_in_g': 'grad_w', 'grad_ln_in_b': 'grad_w', 'grad_w_in': 'grad_w', 'grad_b_gate': 'grad_w', 'grad_conv_dw_w': 'grad_w', 'grad_conv_dw_b': 'grad_w', 'grad_conv_ln_g': 'grad_w', 'grad_conv_ln_b': 'grad_w', 'grad_w_pa': 'grad_w', 'grad_gdn_conv_q': 'grad_w', 'grad_gdn_conv_k': 'grad_w', 'grad_gdn_conv_v': 'grad_w', 'grad_gdn_a_log': 'grad_w', 'grad_gdn_dt_bias': 'grad_w', 'grad_gdn_norm_g': 'grad_w', 'grad_w_pb': 'grad_w', 'grad_sgu_ln_g': 'grad_w', 'grad_sgu_ln_b': 'grad_w', 'grad_sgu_w_s': 'grad_w', 'grad_sgu_b_s': 'grad_w', 'grad_w_pc': 'grad_w', 'grad_w_o': 'grad_w', 'grad_ln1_g': 'grad_w', 'grad_ln1_b': 'grad_w', 'grad_w_ff1': 'grad_w', 'grad_b_ff1': 'grad_w', 'grad_w_ff2': 'grad_w', 'grad_b_ff2': 'grad_w', 'grad_ln2_g': 'grad_w', 'grad_ln2_b': 'grad_w', 'delta_ln_in_g': 'delta_w', 'delta_ln_in_b': 'delta_w', 'delta_w_in': 'delta_w', 'delta_b_gate': 'delta_w', 'delta_conv_dw_w': 'delta_w', 'delta_conv_dw_b': 'delta_w', 'delta_conv_ln_g': 'delta_w', 'delta_conv_ln_b': 'delta_w', 'delta_w_pa': 'delta_w', 'delta_gdn_conv_q': 'delta_w', 'delta_gdn_conv_k': 'delta_w', 'delta_gdn_conv_v': 'delta_w', 'delta_gdn_a_log': 'delta_w', 'delta_gdn_dt_bias': 'delta_w', 'delta_gdn_norm_g': 'delta_w', 'delta_w_pb': 'delta_w', 'delta_sgu_ln_g': 'delta_w', 'delta_sgu_ln_b': 'delta_w', 'delta_sgu_w_s': 'delta_w', 'delta_sgu_b_s': 'delta_w', 'delta_w_pc': 'delta_w', 'delta_w_o': 'delta_w', 'delta_ln1_g': 'delta_w', 'delta_ln1_b': 'delta_w', 'delta_w_ff1': 'delta_w', 'delta_b_ff1': 'delta_w', 'delta_w_ff2': 'delta_w', 'delta_b_ff2': 'delta_w', 'delta_ln2_g': 'delta_w', 'delta_ln2_b': 'delta_w', 'new_m_ln_in_g': 'new_m', 'new_m_ln_in_b': 'new_m', 'new_m_w_in': 'new_m', 'new_m_b_gate': 'new_m', 'new_m_conv_dw_w': 'new_m', 'new_m_conv_dw_b': 'new_m', 'new_m_conv_ln_g': 'new_m', 'new_m_conv_ln_b': 'new_m', 'new_m_w_pa': 'new_m', 'new_m_gdn_conv_q': 'new_m', 'new_m_gdn_conv_k': 'new_m', 'new_m_gdn_conv_v': 'new_m', 'new_m_gdn_a_log': 'new_m', 'new_m_gdn_dt_bias': 'new_m', 'new_m_gdn_norm_g': 'new_m', 'new_m_w_pb': 'new_m', 'new_m_sgu_ln_g': 'new_m', 'new_m_sgu_ln_b': 'new_m', 'new_m_sgu_w_s': 'new_m', 'new_m_sgu_b_s': 'new_m', 'new_m_w_pc': 'new_m', 'new_m_w_o': 'new_m', 'new_m_ln1_g': 'new_m', 'new_m_ln1_b': 'new_m', 'new_m_w_ff1': 'new_m', 'new_m_b_ff1': 'new_m', 'new_m_w_ff2': 'new_m', 'new_m_b_ff2': 'new_m', 'new_m_ln2_g': 'new_m', 'new_m_ln2_b': 'new_m', 'new_v_ln_in_g': 'new_v', 'new_v_ln_in_b': 'new_v', 'new_v_w_in': 'new_v', 'new_v_b_gate': 'new_v', 'new_v_conv_dw_w': 'new_v', 'new_v_conv_dw_b': 'new_v', 'new_v_conv_ln_g': 'new_v', 'new_v_conv_ln_b': 'new_v', 'new_v_w_pa': 'new_v', 'new_v_gdn_conv_q': 'new_v', 'new_v_gdn_conv_k': 'new_v', 'new_v_gdn_conv_v': 'new_v', 'new_v_gdn_a_log': 'new_v', 'new_v_gdn_dt_bias': 'new_v', 'new_v_gdn_norm_g': 'new_v', 'new_v_w_pb': 'new_v', 'new_v_sgu_ln_g': 'new_v', 'new_v_sgu_ln_b': 'new_v', 'new_v_sgu_w_s': 'new_v', 'new_v_sgu_b_s': 'new_v', 'new_v_w_pc': 'new_v', 'new_v_w_o': 'new_v', 'new_v_ln1_g': 'new_v', 'new_v_ln1_b': 'new_v', 'new_v_w_ff1': 'new_v', 'new_v_b_ff1': 'new_v', 'new_v_w_ff2': 'new_v', 'new_v_b_ff2': 'new_v', 'new_v_ln2_g': 'new_v', 'new_v_ln2_b': 'new_v'}


def _forward(args):
    return _fwd_reference(*[args[k] for k in FWD_PARAMS])


def _output_shape():
    def fwd():
        inp = _fwd_setup_inputs(0)
        return _fwd_reference(*[inp[k] for k in FWD_PARAMS])
    out = _jax.eval_shape(fwd)
    return out.shape, out.dtype

N_MICROBATCH = 1
ADAM_LR = 0.001
ADAM_B1 = 0.9
ADAM_B2 = 0.999
ADAM_EPS = 1e-08
ADAM_WD = 0.01
ADAM_STEP = 10
PER_EXAMPLE_BATCH_AXIS = {'x': 0, 'loss_target': 0}
SHARED_INPUTS = []
_WEIGHT_DTYPES = {'ln_in_g': _jnp.float32, 'ln_in_b': _jnp.float32, 'w_in': _jnp.float32, 'b_gate': _jnp.float32, 'conv_dw_w': _jnp.float32, 'conv_dw_b': _jnp.float32, 'conv_ln_g': _jnp.float32, 'conv_ln_b': _jnp.float32, 'w_pa': _jnp.float32, 'gdn_conv_q': _jnp.float32, 'gdn_conv_k': _jnp.float32, 'gdn_conv_v': _jnp.float32, 'gdn_a_log': _jnp.float32, 'gdn_dt_bias': _jnp.float32, 'gdn_norm_g': _jnp.float32, 'w_pb': _jnp.float32, 'sgu_ln_g': _jnp.float32, 'sgu_ln_b': _jnp.float32, 'sgu_w_s': _jnp.float32, 'sgu_b_s': _jnp.float32, 'w_pc': _jnp.float32, 'w_o': _jnp.float32, 'ln1_g': _jnp.float32, 'ln1_b': _jnp.float32, 'w_ff1': _jnp.float32, 'b_ff1': _jnp.float32, 'w_ff2': _jnp.float32, 'b_ff2': _jnp.float32, 'ln2_g': _jnp.float32, 'ln2_b': _jnp.float32}
MOMENT_SCALE = {'ln_in_g': 2.690591e+00, 'ln_in_b': 1.696726e+00, 'w_in': 2.173288e-02, 'b_gate': 1.306971e-02, 'conv_dw_w': 3.208074e-02, 'conv_dw_b': 1.555682e-01, 'conv_ln_g': 5.893294e-02, 'conv_ln_b': 1.020625e-01, 'w_pa': 6.145334e-02, 'gdn_conv_q': 1.977725e-02, 'gdn_conv_k': 2.061924e-02, 'gdn_conv_v': 3.256202e-02, 'gdn_a_log': 1.985629e-01, 'gdn_dt_bias': 2.046663e-01, 'gdn_norm_g': 6.422275e-02, 'w_pb': 4.854703e-02, 'sgu_ln_g': 2.472380e-02, 'sgu_ln_b': 2.517274e-02, 'sgu_w_s': 2.305990e-02, 'sgu_b_s': 3.370591e-02, 'w_pc': 8.109852e-02, 'w_o': 1.107796e-01, 'ln1_g': 3.223310e+00, 'ln1_b': 1.780558e+00, 'w_ff1': 8.530896e-02, 'b_ff1': 2.145190e-01, 'w_ff2': 5.186101e-01, 'b_ff2': 1.265918e+00, 'ln2_g': 9.097993e+01, 'ln2_b': 2.042686e+01}


def _to_microbatches(a, axis):
    t = _jnp.moveaxis(a, axis, 0)
    t = t.reshape((N_MICROBATCH, t.shape[0] // N_MICROBATCH) + t.shape[1:])
    return _jnp.moveaxis(t, 1, axis + 1)


def setup_inputs(seed: int = 0) -> dict:
    inp = _fwd_setup_inputs(seed)
    key = _jax.random.fold_in(_jax.random.key(seed), 7919)
    shape, _ = _output_shape()
    out = dict(inp)
    out["loss_target"] = _jax.random.normal(_jax.random.fold_in(key, 0), shape, _jnp.float32)
    for i, name in enumerate(TWIN_WEIGHTS):
        w = inp[name].astype(_jnp.float32)
        if MOMENT_SCALE is None:
            s = _jnp.sqrt(_jnp.mean(_jnp.square(w)) + 1e-30)
        else:
            s = MOMENT_SCALE[name]
        km, kv = _jax.random.split(_jax.random.fold_in(key, i + 1))
        out[name] = w
        out["m_" + name] = s * _jax.random.normal(km, w.shape, _jnp.float32)
        out["v_" + name] = (s * s) * _jax.random.uniform(kv, w.shape, _jnp.float32, 0.5, 1.5)
    if N_MICROBATCH > 1:
        for name, axis in PER_EXAMPLE_BATCH_AXIS.items():
            out[name] = _to_microbatches(out[name], axis)
    return {'x': out['x'], 'ln_in_g': out['ln_in_g'], 'ln_in_b': out['ln_in_b'], 'w_in': out['w_in'], 'b_gate': out['b_gate'], 'conv_dw_w': out['conv_dw_w'], 'conv_dw_b': out['conv_dw_b'], 'conv_ln_g': out['conv_ln_g'], 'conv_ln_b': out['conv_ln_b'], 'w_pa': out['w_pa'], 'gdn_conv_q': out['gdn_conv_q'], 'gdn_conv_k': out['gdn_conv_k'], 'gdn_conv_v': out['gdn_conv_v'], 'gdn_a_log': out['gdn_a_log'], 'gdn_dt_bias': out['gdn_dt_bias'], 'gdn_norm_g': out['gdn_norm_g'], 'w_pb': out['w_pb'], 'sgu_ln_g': out['sgu_ln_g'], 'sgu_ln_b': out['sgu_ln_b'], 'sgu_w_s': out['sgu_w_s'], 'sgu_b_s': out['sgu_b_s'], 'w_pc': out['w_pc'], 'w_o': out['w_o'], 'ln1_g': out['ln1_g'], 'ln1_b': out['ln1_b'], 'w_ff1': out['w_ff1'], 'b_ff1': out['b_ff1'], 'w_ff2': out['w_ff2'], 'b_ff2': out['b_ff2'], 'ln2_g': out['ln2_g'], 'ln2_b': out['ln2_b'], 'loss_target': out['loss_target'], 'm_ln_in_g': out['m_ln_in_g'], 'm_ln_in_b': out['m_ln_in_b'], 'm_w_in': out['m_w_in'], 'm_b_gate': out['m_b_gate'], 'm_conv_dw_w': out['m_conv_dw_w'], 'm_conv_dw_b': out['m_conv_dw_b'], 'm_conv_ln_g': out['m_conv_ln_g'], 'm_conv_ln_b': out['m_conv_ln_b'], 'm_w_pa': out['m_w_pa'], 'm_gdn_conv_q': out['m_gdn_conv_q'], 'm_gdn_conv_k': out['m_gdn_conv_k'], 'm_gdn_conv_v': out['m_gdn_conv_v'], 'm_gdn_a_log': out['m_gdn_a_log'], 'm_gdn_dt_bias': out['m_gdn_dt_bias'], 'm_gdn_norm_g': out['m_gdn_norm_g'], 'm_w_pb': out['m_w_pb'], 'm_sgu_ln_g': out['m_sgu_ln_g'], 'm_sgu_ln_b': out['m_sgu_ln_b'], 'm_sgu_w_s': out['m_sgu_w_s'], 'm_sgu_b_s': out['m_sgu_b_s'], 'm_w_pc': out['m_w_pc'], 'm_w_o': out['m_w_o'], 'm_ln1_g': out['m_ln1_g'], 'm_ln1_b': out['m_ln1_b'], 'm_w_ff1': out['m_w_ff1'], 'm_b_ff1': out['m_b_ff1'], 'm_w_ff2': out['m_w_ff2'], 'm_b_ff2': out['m_b_ff2'], 'm_ln2_g': out['m_ln2_g'], 'm_ln2_b': out['m_ln2_b'], 'v_ln_in_g': out['v_ln_in_g'], 'v_ln_in_b': out['v_ln_in_b'], 'v_w_in': out['v_w_in'], 'v_b_gate': out['v_b_gate'], 'v_conv_dw_w': out['v_conv_dw_w'], 'v_conv_dw_b': out['v_conv_dw_b'], 'v_conv_ln_g': out['v_conv_ln_g'], 'v_conv_ln_b': out['v_conv_ln_b'], 'v_w_pa': out['v_w_pa'], 'v_gdn_conv_q': out['v_gdn_conv_q'], 'v_gdn_conv_k': out['v_gdn_conv_k'], 'v_gdn_conv_v': out['v_gdn_conv_v'], 'v_gdn_a_log': out['v_gdn_a_log'], 'v_gdn_dt_bias': out['v_gdn_dt_bias'], 'v_gdn_norm_g': out['v_gdn_norm_g'], 'v_w_pb': out['v_w_pb'], 'v_sgu_ln_g': out['v_sgu_ln_g'], 'v_sgu_ln_b': out['v_sgu_ln_b'], 'v_sgu_w_s': out['v_sgu_w_s'], 'v_sgu_b_s': out['v_sgu_b_s'], 'v_w_pc': out['v_w_pc'], 'v_w_o': out['v_w_o'], 'v_ln1_g': out['v_ln1_g'], 'v_ln1_b': out['v_ln1_b'], 'v_w_ff1': out['v_w_ff1'], 'v_b_ff1': out['v_b_ff1'], 'v_w_ff2': out['v_w_ff2'], 'v_b_ff2': out['v_b_ff2'], 'v_ln2_g': out['v_ln2_g'], 'v_ln2_b': out['v_ln2_b']}


def _loss(weights, diff, rest, loss_target):
    with _jax.named_scope("forward"):
        args = {**rest, TWIN_DIFF_INPUT: diff, **{k: w.astype(_WEIGHT_DTYPES[k]) for k, w in weights.items()}}
        y = _forward(args)
    with _jax.named_scope("loss_head"):
        err = _jnp.square(y.astype(_jnp.float32) - loss_target)
        return 0.5 * _jnp.sum(_jnp.mean(err, axis=-1)) if err.ndim else 0.5 * err


def _adamw(w, g, m, v):
    m = ADAM_B1 * m + (1.0 - ADAM_B1) * g
    v = ADAM_B2 * v + (1.0 - ADAM_B2) * _jnp.square(g)
    m_hat = m / (1.0 - ADAM_B1 ** ADAM_STEP)
    v_hat = v / (1.0 - ADAM_B2 ** ADAM_STEP)
    delta = -ADAM_LR * (m_hat / (_jnp.sqrt(v_hat) + ADAM_EPS) + ADAM_WD * w)
    return delta, m, v


def reference(x, ln_in_g, ln_in_b, w_in, b_gate, conv_dw_w, conv_dw_b, conv_ln_g, conv_ln_b, w_pa, gdn_conv_q, gdn_conv_k, gdn_conv_v, gdn_a_log, gdn_dt_bias, gdn_norm_g, w_pb, sgu_ln_g, sgu_ln_b, sgu_w_s, sgu_b_s, w_pc, w_o, ln1_g, ln1_b, w_ff1, b_ff1, w_ff2, b_ff2, ln2_g, ln2_b, loss_target, m_ln_in_g, m_ln_in_b, m_w_in, m_b_gate, m_conv_dw_w, m_conv_dw_b, m_conv_ln_g, m_conv_ln_b, m_w_pa, m_gdn_conv_q, m_gdn_conv_k, m_gdn_conv_v, m_gdn_a_log, m_gdn_dt_bias, m_gdn_norm_g, m_w_pb, m_sgu_ln_g, m_sgu_ln_b, m_sgu_w_s, m_sgu_b_s, m_w_pc, m_w_o, m_ln1_g, m_ln1_b, m_w_ff1, m_b_ff1, m_w_ff2, m_b_ff2, m_ln2_g, m_ln2_b, v_ln_in_g, v_ln_in_b, v_w_in, v_b_gate, v_conv_dw_w, v_conv_dw_b, v_conv_ln_g, v_conv_ln_b, v_w_pa, v_gdn_conv_q, v_gdn_conv_k, v_gdn_conv_v, v_gdn_a_log, v_gdn_dt_bias, v_gdn_norm_g, v_w_pb, v_sgu_ln_g, v_sgu_ln_b, v_sgu_w_s, v_sgu_b_s, v_w_pc, v_w_o, v_ln1_g, v_ln1_b, v_w_ff1, v_b_ff1, v_w_ff2, v_b_ff2, v_ln2_g, v_ln2_b):
    given = dict(x=x, ln_in_g=ln_in_g, ln_in_b=ln_in_b, w_in=w_in, b_gate=b_gate, conv_dw_w=conv_dw_w, conv_dw_b=conv_dw_b, conv_ln_g=conv_ln_g, conv_ln_b=conv_ln_b, w_pa=w_pa, gdn_conv_q=gdn_conv_q, gdn_conv_k=gdn_conv_k, gdn_conv_v=gdn_conv_v, gdn_a_log=gdn_a_log, gdn_dt_bias=gdn_dt_bias, gdn_norm_g=gdn_norm_g, w_pb=w_pb, sgu_ln_g=sgu_ln_g, sgu_ln_b=sgu_ln_b, sgu_w_s=sgu_w_s, sgu_b_s=sgu_b_s, w_pc=w_pc, w_o=w_o, ln1_g=ln1_g, ln1_b=ln1_b, w_ff1=w_ff1, b_ff1=b_ff1, w_ff2=w_ff2, b_ff2=b_ff2, ln2_g=ln2_g, ln2_b=ln2_b, loss_target=loss_target, m_ln_in_g=m_ln_in_g, m_ln_in_b=m_ln_in_b, m_w_in=m_w_in, m_b_gate=m_b_gate, m_conv_dw_w=m_conv_dw_w, m_conv_dw_b=m_conv_dw_b, m_conv_ln_g=m_conv_ln_g, m_conv_ln_b=m_conv_ln_b, m_w_pa=m_w_pa, m_gdn_conv_q=m_gdn_conv_q, m_gdn_conv_k=m_gdn_conv_k, m_gdn_conv_v=m_gdn_conv_v, m_gdn_a_log=m_gdn_a_log, m_gdn_dt_bias=m_gdn_dt_bias, m_gdn_norm_g=m_gdn_norm_g, m_w_pb=m_w_pb, m_sgu_ln_g=m_sgu_ln_g, m_sgu_ln_b=m_sgu_ln_b, m_sgu_w_s=m_sgu_w_s, m_sgu_b_s=m_sgu_b_s, m_w_pc=m_w_pc, m_w_o=m_w_o, m_ln1_g=m_ln1_g, m_ln1_b=m_ln1_b, m_w_ff1=m_w_ff1, m_b_ff1=m_b_ff1, m_w_ff2=m_w_ff2, m_b_ff2=m_b_ff2, m_ln2_g=m_ln2_g, m_ln2_b=m_ln2_b, v_ln_in_g=v_ln_in_g, v_ln_in_b=v_ln_in_b, v_w_in=v_w_in, v_b_gate=v_b_gate, v_conv_dw_w=v_conv_dw_w, v_conv_dw_b=v_conv_dw_b, v_conv_ln_g=v_conv_ln_g, v_conv_ln_b=v_conv_ln_b, v_w_pa=v_w_pa, v_gdn_conv_q=v_gdn_conv_q, v_gdn_conv_k=v_gdn_conv_k, v_gdn_conv_v=v_gdn_conv_v, v_gdn_a_log=v_gdn_a_log, v_gdn_dt_bias=v_gdn_dt_bias, v_gdn_norm_g=v_gdn_norm_g, v_w_pb=v_w_pb, v_sgu_ln_g=v_sgu_ln_g, v_sgu_ln_b=v_sgu_ln_b, v_sgu_w_s=v_sgu_w_s, v_sgu_b_s=v_sgu_b_s, v_w_pc=v_w_pc, v_w_o=v_w_o, v_ln1_g=v_ln1_g, v_ln1_b=v_ln1_b, v_w_ff1=v_w_ff1, v_b_ff1=v_b_ff1, v_w_ff2=v_w_ff2, v_b_ff2=v_b_ff2, v_ln2_g=v_ln2_g, v_ln2_b=v_ln2_b)
    weights = {n: given[n] for n in TWIN_WEIGHTS}
    shared = {n: given[n] for n in SHARED_INPUTS}
    per_example = {n: given[n] for n in ['x']}
    grad_fn = _jax.value_and_grad(_loss, argnums=(0, 1))

    def one_microbatch(ex, loss_target):
        ex = dict(ex)
        diff = ex.pop(TWIN_DIFF_INPUT)
        return grad_fn(weights, diff, {**shared, **ex}, loss_target)

    if N_MICROBATCH == 1:
        loss, (grad_w, grad_x) = one_microbatch(per_example, given["loss_target"])
    else:
        def body(carry, xs):
            loss_sum, grad_sum = carry
            l_k, (gw_k, gx_k) = one_microbatch(xs[0], xs[1])
            with _jax.named_scope("update"):
                return (loss_sum + l_k, _jax.tree.map(_jnp.add, grad_sum, gw_k)), gx_k

        init = (_jnp.zeros((), _jnp.float32), _jax.tree.map(_jnp.zeros_like, weights))
        (loss, grad_w), grad_x = _jax.lax.scan(body, init, (per_example, given["loss_target"]))
    with _jax.named_scope("update"):
        delta_w, new_m, new_v = {}, {}, {}
        for n in TWIN_WEIGHTS:
            delta_w[n], new_m[n], new_v[n] = _adamw(weights[n], grad_w[n], given["m_" + n], given["v_" + n])
    return (loss, grad_x, *[grad_w[n] for n in TWIN_WEIGHTS], *[delta_w[n] for n in TWIN_WEIGHTS],
            *[new_m[n] for n in TWIN_WEIGHTS], *[new_v[n] for n in TWIN_WEIGHTS])
```

```python
import functools

import jax
import jax.numpy as jnp
from jax import lax
from jax.experimental import pallas as pl
from jax.experimental.pallas import tpu as pltpu

F32 = jnp.float32
BF16 = jnp.bfloat16
HIGHEST = lax.Precision.HIGHEST

D_MODEL = 1024
DEPTH = 2
CONV_DIM = 512
CONV_WIDTH = 31
CONV_HALO = 32
GDN_HEADS = 4
GDN_HD = 128
GDN_CHUNK = 64
GDN_PAIR = 2 * GDN_CHUNK
GDN_CONV = 4
GDN_HALO = 8
SGU_GROUPS = 4
SGU_GD = 128
SGU_CHUNK = 128
D_FF = 4096
ALPHA = (2 * DEPTH) ** 0.25
LN_EPS = 1e-5
RMS_EPS = 1e-6
ADAM_LR, ADAM_B1, ADAM_B2, ADAM_EPS, ADAM_WD, ADAM_STEP = 0.001, 0.9, 0.999, 1e-08, 0.01, 10
N_CHIPS = 4
N_DEV = 8
LANES = 128
VMEM_LIMIT = 52 * 1024 * 1024

W_GATE, W_A, W_C, W_B, W_Z = 3 * D_MODEL, 2 * CONV_DIM, 2 * 512, 3 * 512 + LANES, 512


def _dg(a, b, ca, cb):
    return lax.dot_general(a.astype(BF16), b.astype(BF16), (((ca,), (cb,)), ((), ())), preferred_element_type=F32)


@jax.custom_vjp
def mm_nn(a, b):
    return _dg(a, b, 1, 0)


mm_nn.defvjp(lambda a, b: (_dg(a, b, 1, 0), (a, b)), lambda r, g: (_dg(g, r[1], 1, 1), _dg(r[0], g, 0, 0)))


@jax.custom_vjp
def mm_nt(a, b):
    return _dg(a, b, 1, 1)


mm_nt.defvjp(lambda a, b: (_dg(a, b, 1, 1), (a, b)), lambda r, g: (_dg(g, r[1], 1, 0), _dg(g, r[0], 0, 0)))


@jax.custom_vjp
def mm_tn(a, b):
    return _dg(a, b, 0, 0)


mm_tn.defvjp(lambda a, b: (_dg(a, b, 0, 0), (a, b)), lambda r, g: (_dg(r[1], g, 1, 1), _dg(r[0], g, 1, 0)))


def mm_hi(a, b):
    return jnp.dot(a, b, precision=HIGHEST, preferred_element_type=F32)


def _ln(x, g, b):
    mu = jnp.mean(x, -1, keepdims=True)
    xc = x - mu
    var = jnp.mean(xc * xc, -1, keepdims=True)
    return xc * lax.rsqrt(var + LN_EPS) * g + b


def _iota2(shape, dim):
    return lax.broadcasted_iota(jnp.int32, shape, dim)


def f_ln_in(params, x):
    g, b = params
    return (_ln(x, g, b),)


def f_conformer(params, halo, pa):
    w, b, g, be = params
    tm = pa.shape[0]
    full = jnp.concatenate([halo, pa], axis=0)
    h = full[:, :CONV_DIM] * jax.nn.sigmoid(full[:, CONV_DIM:])
    off = CONV_HALO - (CONV_WIDTH - 1)
    acc = w[0:1, :] * h[off:off + tm, :]
    for j in range(1, CONV_WIDTH):
        acc = acc + w[j:j + 1, :] * h[off + j:off + j + tm, :]
    return (jax.nn.silu(_ln(acc + b, g, be)),)


def f_gdn_pre(params, halo, pb):
    cq, ck, cv, alog, dtb = params
    tm = pb.shape[0]
    full = jnp.concatenate([halo, pb], axis=0)
    off = GDN_HALO - (GDN_CONV - 1)

    def conv(cols, w):
        acc = w[0:1, :] * cols[off:off + tm, :]
        for j in range(1, GDN_CONV):
            acc = acc + w[j:j + 1, :] * cols[off + j:off + j + tm, :]
        return jax.nn.silu(acc)

    def l2n(t, scale):
        parts = []
        for h in range(GDN_HEADS):
            th = t[:, h * GDN_HD:(h + 1) * GDN_HD]
            parts.append(th * (lax.rsqrt(jnp.sum(th * th, -1, keepdims=True) + RMS_EPS) * scale))
        return jnp.concatenate(parts, axis=1)

    q = l2n(conv(full[:, 0:512], cq), GDN_HD ** -0.5)
    k = l2n(conv(full[:, 512:1024], ck), 1.0)
    v = conv(full[:, 1024:1536], cv)
    logits = pb[:, 1536:1536 + LANES]
    z = logits + dtb
    softplus = jnp.maximum(z, 0.0) + jnp.log(1.0 + jnp.exp(-jnp.abs(z)))
    g = -jnp.exp(alog) * softplus
    beta = jax.nn.sigmoid(logits)
    lane = _iota2(logits.shape, 1)
    gb = jnp.where(lane < GDN_HEADS, beta, jnp.where(lane < 2 * GDN_HEADS, g, 0.0))
    return q, k, v, gb


def _inv_unit_lower(low):
    n = low.shape[0]
    eye = jnp.where(_iota2((n, n), 0) == _iota2((n, n), 1), 1.0, 0.0).astype(F32)
    p = -low
    x = eye + p
    for _ in range(5):
        p = mm_hi(p, p)
        x = x + mm_hi(x, p)
    return x


def f_gdn_prep(params, qn, kn, vv, gb):
    del params
    n = GDN_PAIR
    r, c = _iota2((n, n), 0), _iota2((n, n), 1)
    same = (r >= GDN_CHUNK) == (c >= GDN_CHUNK)
    causal = jnp.logical_and(same, r >= c)
    strict = jnp.logical_and(same, r > c)
    gam_all = mm_hi(jnp.where(causal, 1.0, 0.0).astype(F32), gb)
    gam_t = gam_all.T
    first = _iota2((n, LANES), 0) < GDN_CHUNK
    glast = jnp.where(first, gam_all[GDN_CHUNK - 1:GDN_CHUNK, :], gam_all[n - 1:n, :])
    us, ws, as_, qds, kds = [], [], [], [], []
    for h in range(GDN_HEADS):
        hs = slice(h * GDN_HD, (h + 1) * GDN_HD)
        q, k, v = qn[:, hs], kn[:, hs], vv[:, hs]
        gc = gam_all[:, GDN_HEADS + h:GDN_HEADS + h + 1]
        gr = gam_t[GDN_HEADS + h:GDN_HEADS + h + 1, :]
        beta = gb[:, h:h + 1]
        decay = jnp.exp(jnp.where(causal, gc - gr, -jnp.inf))
        low = jnp.where(strict, beta * mm_nt(k, k) * decay, 0.0)
        tinv = _inv_unit_lower(low)
        eg = jnp.exp(gc)
        us.append(mm_nn(tinv, beta * v))
        ws.append(mm_nn(tinv, beta * k * eg))
        as_.append(mm_nt(q, k) * decay)
        qds.append(q * eg)
        kds.append(k * jnp.exp(glast[:, GDN_HEADS + h:GDN_HEADS + h + 1] - gc))
    cat = lambda xs: jnp.concatenate(xs, axis=1)
    return cat(us), cat(ws), cat(as_), cat(qds), cat(kds), jnp.exp(glast)


def f_gdn_scan(params, state, u, w, a, qd, kd, egl, z):
    ng = params[0]
    tm = u.shape[0]
    st = [state[h * GDN_HD:(h + 1) * GDN_HD, :] for h in range(GDN_HEADS)]
    zeros = jnp.zeros((GDN_CHUNK, GDN_HD), F32)
    rows = []
    for ci in range(tm // GDN_CHUNK):
        rs = slice(ci * GDN_CHUNK, (ci + 1) * GDN_CHUNK)
        heads = []
        for h in range(GDN_HEADS):
            hs = slice(h * GDN_HD, (h + 1) * GDN_HD)
            vn = u[rs, hs] - mm_nn(w[rs, hs], st[h])
            vpad = jnp.concatenate([vn, zeros] if ci % 2 == 0 else [zeros, vn], axis=0)
            o = mm_nn(qd[rs, hs], st[h]) + mm_nn(a[rs, hs], vpad)
            e = egl[ci * GDN_CHUNK:ci * GDN_CHUNK + 1, GDN_HEADS + h:GDN_HEADS + h + 1]
            st[h] = st[h] * e + mm_tn(kd[rs, hs], vn)
            on = o * lax.rsqrt(jnp.mean(o * o, -1, keepdims=True) + RMS_EPS) * ng
            heads.append(on * jax.nn.silu(z[rs, hs]))
        rows.append(jnp.concatenate(heads, axis=1))
    return jnp.concatenate(rows, axis=0), jnp.concatenate(st, axis=0)


def f_sgu(params, puv):
    g, b, ws, bst = params
    tm = puv.shape[0]
    uv = jax.nn.gelu(puv)
    u = uv[:, :512]
    v = _ln(uv[:, 512:], g, b)
    r, c = _iota2((SGU_CHUNK, SGU_CHUNK), 0), _iota2((SGU_CHUNK, SGU_CHUNK), 1)
    rows = []
    for ci in range(tm // SGU_CHUNK):
        rs = slice(ci * SGU_CHUNK, (ci + 1) * SGU_CHUNK)
        groups = []
        for gi in range(SGU_GROUPS):
            gs = slice(gi * SGU_GD, (gi + 1) * SGU_GD)
            wt = jnp.where(r >= c, ws[gs, :], 0.0)
            groups.append(mm_nn(wt, v[rs, gs]) + bst[:, gi:gi + 1])
        rows.append(jnp.concatenate(groups, axis=1))
    return (u * jnp.concatenate(rows, axis=0),)


def f_mix(params, ya, yb, yc, pg):
    bg = params[0]
    s = jax.nn.sigmoid(pg + bg)
    return (s[:, :D_MODEL] * ya + s[:, D_MODEL:2 * D_MODEL] * yb + s[:, 2 * D_MODEL:] * yc,)


def f_ln1(params, xin, m):
    g, b = params
    return (_ln(ALPHA * xin + m, g, b),)


def f_ffact(params, hpre):
    return (jnp.square(jax.nn.relu(hpre + params[0])),)


def f_ln2(params, x1, f2):
    b2, g, b = params
    return (_ln(ALPHA * x1 + (f2 + b2), g, b),)


def _cparams(n_axes=1):
    return pltpu.CompilerParams(dimension_semantics=("arbitrary",) * n_axes, vmem_limit_bytes=VMEM_LIMIT)


def _row_spec(tm, width, n_tiles=None):
    if n_tiles is None:
        return pl.BlockSpec((tm, width), lambda i: (i, 0))
    return pl.BlockSpec((tm, width), lambda i: (n_tiles - 1 - i, 0))


def _full_spec(shape):
    return pl.BlockSpec(shape, lambda i: (0,) * len(shape))


def stage_fwd(name, fn, tm, rows, params, outs):
    t = rows[0].shape[0]
    nr, npar = len(rows), len(params)
    widths = jax.eval_shape(lambda p, r: fn(p, *r), [jax.ShapeDtypeStruct(p.shape, F32) for p in params],
                            [jax.ShapeDtypeStruct((tm, r.shape[1]), F32) for r in rows])

    def body(*refs):
        r = [x[...].astype(F32) for x in refs[:nr]]
        p = [x[...] for x in refs[nr:nr + npar]]
        res = fn(p, *r)
        k = nr + npar
        for o, dts in zip(res, outs):
            for dt in dts:
                refs[k][...] = o.astype(dt)
                k += 1

    out_shape, out_specs = [], []
    for wd, dts in zip(widths, outs):
        for dt in dts:
            out_shape.append(jax.ShapeDtypeStruct((t, wd.shape[1]), dt))
            out_specs.append(_row_spec(tm, wd.shape[1]))
    return pl.pallas_call(
        body, name=name, grid=(t // tm,),
        in_specs=[_row_spec(tm, r.shape[1]) for r in rows] + [_full_spec(p.shape) for p in params],
        out_specs=out_specs, out_shape=out_shape, compiler_params=_cparams(),
    )(*rows, *params)


def stage_bwd(name, fn, tm, rows, params, cts, drow_dtypes):
    t = rows[0].shape[0]
    nr, npar = len(rows), len(params)
    flat_cts = [c for cl in cts if cl is not None for c in cl]
    nct = len(flat_cts)
    want = [i for i, dt in enumerate(drow_dtypes) if dt is not None]

    def body(*refs):
        r = [x[...].astype(F32) for x in refs[:nr]]
        p = [x[...] for x in refs[nr:nr + npar]]
        res, vjp = jax.vjp(lambda pp, rr: fn(pp, *rr), p, r)
        k = nr + npar
        ct = []
        for o, cl in zip(res, cts):
            if cl is None:
                ct.append(jnp.zeros_like(o))
            else:
                acc = refs[k][...].astype(F32)
                for j in range(1, len(cl)):
                    acc = acc + refs[k + j][...].astype(F32)
                k += len(cl)
                ct.append(acc)
        dp, dr = vjp(tuple(ct))
        for i in want:
            refs[k][...] = dr[i].astype(drow_dtypes[i])
            k += 1

        @pl.when(pl.program_id(0) == 0)
        def _():
            for j in range(npar):
                refs[k + j][...] = jnp.zeros_like(refs[k + j])

        for j in range(npar):
            refs[k + j][...] += dp[j]

    out_shape = [jax.ShapeDtypeStruct(rows[i].shape, drow_dtypes[i]) for i in want]
    out_specs = [_row_spec(tm, rows[i].shape[1]) for i in want]
    out_shape += [jax.ShapeDtypeStruct(p.shape, F32) for p in params]
    out_specs += [_full_spec(p.shape) for p in params]
    return pl.pallas_call(
        body, name=name, grid=(t // tm,),
        in_specs=[_row_spec(tm, r.shape[1]) for r in rows] + [_full_spec(p.shape) for p in params]
        + [_row_spec(tm, c.shape[1]) for c in flat_cts],
        out_specs=out_specs, out_shape=out_shape, compiler_params=_cparams(),
    )(*rows, *params, *flat_cts)


def _halo_spec(tm, halo, width, n_tiles=None):
    per = tm // halo
    if n_tiles is None:
        return pl.BlockSpec((halo, width), lambda i: (jnp.maximum(i * per - 1, 0), 0))
    return pl.BlockSpec((halo, width), lambda i: (jnp.maximum((n_tiles - 1 - i) * per - 1, 0), 0))


def halo_fwd(name, fn, tm, halo, src, params, outs):
    t, width = src.shape
    npar = len(params)
    widths = jax.eval_shape(lambda p, h, r: fn(p, h, r), [jax.ShapeDtypeStruct(p.shape, F32) for p in params],
                            jax.ShapeDtypeStruct((halo, width), F32), jax.ShapeDtypeStruct((tm, width), F32))

    def body(h_ref, x_ref, *refs):
        hal = jnp.where(pl.program_id(0) == 0, 0.0, h_ref[...])
        res = fn([x[...] for x in refs[:npar]], hal, x_ref[...])
        k = npar
        for o, dts in zip(res, outs):
            for dt in dts:
                refs[k][...] = o.astype(dt)
                k += 1

    out_shape, out_specs = [], []
    for wd, dts in zip(widths, outs):
        for dt in dts:
            out_shape.append(jax.ShapeDtypeStruct((t, wd.shape[1]), dt))
            out_specs.append(_row_spec(tm, wd.shape[1]))
    return pl.pallas_call(
        body, name=name, grid=(t // tm,),
        in_specs=[_halo_spec(tm, halo, width), _row_spec(tm, width)] + [_full_spec(p.shape) for p in params],
        out_specs=out_specs, out_shape=out_shape, compiler_params=_cparams(),
    )(src, src, *params)


def halo_bwd(name, fn, tm, halo, src, params, cts, dsrc_dtype):
    t, width = src.shape
    n_tiles = t // tm
    npar = len(params)

    def body(h_ref, x_ref, *refs):
        carry = refs[-1]
        first_tile = pl.program_id(0) == n_tiles - 1
        hal = jnp.where(first_tile, 0.0, h_ref[...])
        p = [x[...] for x in refs[:npar]]
        res, vjp = jax.vjp(fn, p, hal, x_ref[...])
        k = npar
        ct = tuple(refs[k + j][...].astype(F32) for j in range(len(cts)))
        k += len(cts)
        dp, dh, dx = vjp(ct)

        @pl.when(pl.program_id(0) == 0)
        def _():
            carry[...] = jnp.zeros_like(carry)
            for j in range(npar):
                refs[k + 1 + j][...] = jnp.zeros_like(refs[k + 1 + j])

        dx = jnp.concatenate([dx[:tm - halo, :], dx[tm - halo:, :] + carry[...]], axis=0)
        refs[k][...] = dx.astype(dsrc_dtype)
        carry[...] = dh
        for j in range(npar):
            refs[k + 1 + j][...] += dp[j]

    out_shape = [jax.ShapeDtypeStruct((t, width), dsrc_dtype)] + [jax.ShapeDtypeStruct(p.shape, F32) for p in params]
    out_specs = [_row_spec(tm, width, n_tiles)] + [_full_spec(p.shape) for p in params]
    return pl.pallas_call(
        body, name=name, grid=(n_tiles,),
        in_specs=[_halo_spec(tm, halo, width, n_tiles), _row_spec(tm, width, n_tiles)] + [_full_spec(p.shape) for p in params]
        + [_row_spec(tm, c.shape[1], n_tiles) for c in cts],
        out_specs=out_specs, out_shape=out_shape,
        scratch_shapes=[pltpu.VMEM((halo, width), F32)], compiler_params=_cparams(),
    )(src, src, *params, *cts)


SCAN_TM = 256


def scan_fwd(name, norm_g, u, w, a, qd, kd, egl, z):
    t = u.shape[0]
    n_tiles = t // SCAN_TM
    srows = GDN_HEADS * GDN_HD

    def body(ng_ref, u_ref, w_ref, a_ref, qd_ref, kd_ref, e_ref, z_ref, o_ref, s_ref, state):
        @pl.when(pl.program_id(0) == 0)
        def _():
            state[...] = jnp.zeros_like(state)

        s_ref[0] = state[...]
        o, s_new = f_gdn_scan([ng_ref[...]], state[...], u_ref[...], w_ref[...], a_ref[...], qd_ref[...], kd_ref[...],
                              e_ref[...], z_ref[...])
        o_ref[...] = o.astype(BF16)
        state[...] = s_new

    rows = [u, w, a, qd, kd, egl, z]
    return pl.pallas_call(
        body, name=name, grid=(n_tiles,),
        in_specs=[_full_spec(norm_g.shape)] + [_row_spec(SCAN_TM, r.shape[1]) for r in rows],
        out_specs=[_row_spec(SCAN_TM, 512), pl.BlockSpec((1, srows, GDN_HD), lambda i: (i, 0, 0))],
        out_shape=[jax.ShapeDtypeStruct((t, 512), BF16), jax.ShapeDtypeStruct((n_tiles, srows, GDN_HD), F32)],
        scratch_shapes=[pltpu.VMEM((srows, GDN_HD), F32)], compiler_params=_cparams(),
    )(norm_g, *rows)


def scan_bwd(name, norm_g, states, u, w, a, qd, kd, egl, z, dout):
    t = u.shape[0]
    n_tiles = t // SCAN_TM
    srows = GDN_HEADS * GDN_HD

    def body(ng_ref, s_ref, u_ref, w_ref, a_ref, qd_ref, kd_ref, e_ref, z_ref, do_ref,
             du_ref, dw_ref, da_ref, dqd_ref, dkd_ref, de_ref, dz_ref, dng_ref, dstate):
        @pl.when(pl.program_id(0) == 0)
        def _():
            dstate[...] = jnp.zeros_like(dstate)
            dng_ref[...] = jnp.zeros_like(dng_ref)

        args = ([ng_ref[...]], s_ref[0], u_ref[...], w_ref[...], a_ref[...], qd_ref[...], kd_ref[...], e_ref[...], z_ref[...])
        _, vjp = jax.vjp(f_gdn_scan, *args)
        dp, ds, du, dw, da, dqd, dkd, de, dz = vjp((do_ref[...].astype(F32), dstate[...]))
        du_ref[...] = du
        dw_ref[...] = dw
        da_ref[...] = da
        dqd_ref[...] = dqd
        dkd_ref[...] = dkd
        de_ref[...] = de
        dz_ref[...] = dz.astype(BF16)
        dng_ref[...] += dp[0]
        dstate[...] = ds

    rows = [u, w, a, qd, kd, egl, z, dout]
    return pl.pallas_call(
        body, name=name, grid=(n_tiles,),
        in_specs=[_full_spec(norm_g.shape), pl.BlockSpec((1, srows, GDN_HD), lambda i: (n_tiles - 1 - i, 0, 0))]
        + [_row_spec(SCAN_TM, r.shape[1], n_tiles) for r in rows],
        out_specs=[_row_spec(SCAN_TM, r.shape[1], n_tiles) for r in rows[:7]] + [_full_spec(norm_g.shape)],
        out_shape=[jax.ShapeDtypeStruct(r.shape, F32) for r in rows[:6]] + [jax.ShapeDtypeStruct(z.shape, BF16)]
        + [jax.ShapeDtypeStruct(norm_g.shape, F32)],
        scratch_shapes=[pltpu.VMEM((srows, GDN_HD), F32)], compiler_params=_cparams(),
    )(norm_g, states, *rows)


def loss_stage(name, y, target, tm):
    t, d = y.shape

    def body(y_ref, t_ref, l_ref, dy_ref):
        @pl.when(pl.program_id(0) == 0)
        def _():
            l_ref[...] = jnp.zeros_like(l_ref)

        e = y_ref[...] - t_ref[...]
        dy_ref[...] = e * (1.0 / d)
        sq = e * e
        part = sq[:, 0:LANES]
        for j in range(1, d // LANES):
            part = part + sq[:, j * LANES:(j + 1) * LANES]
        acc = part[0:8, :]
        for j in range(1, tm // 8):
            acc = acc + part[j * 8:(j + 1) * 8, :]
        l_ref[...] += acc * (0.5 / d)

    return pl.pallas_call(
        body, name=name, grid=(t // tm,),
        in_specs=[_row_spec(tm, d), _row_spec(tm, d)],
        out_specs=[_full_spec((8, LANES)), _row_spec(tm, d)],
        out_shape=[jax.ShapeDtypeStruct((8, LANES), F32), jax.ShapeDtypeStruct((t, d), F32)],
        compiler_params=_cparams(),
    )(y, target)


def _pick(n, cands):
    for c in cands:
        if n % c == 0:
            return c
    return n


def matmul(name, a, b, form, out_dtype, acc=None):
    if form == "nn":
        (m, k), n = a.shape, b.shape[1]
    elif form == "nt":
        (m, k), n = a.shape, b.shape[0]
    else:
        (k, m), n = a.shape, b.shape[1]
    tm = _pick(m, (512, 256, 128))
    tn = n if n <= 2048 else _pick(n, (1536, 1024, 512, 256, 128))
    tk = k if k <= 2048 else _pick(k, (1024, 512))
    if form == "tn":
        tm = _pick(m, (1024, 512, 256, 128))
        tn = n if n <= 2048 else _pick(n, (1024, 512, 256, 128))
        tk = _pick(k, (512, 256, 128))
    nk = k // tk
    ca, cb = {"nn": (1, 0), "nt": (1, 1), "tn": (0, 0)}[form]
    a_spec = (pl.BlockSpec((tm, tk), lambda j, i, kk: (i, kk)) if form != "tn"
              else pl.BlockSpec((tk, tm), lambda j, i, kk: (kk, i)))
    b_spec = (pl.BlockSpec((tn, tk), lambda j, i, kk: (j, kk)) if form == "nt"
              else pl.BlockSpec((tk, tn), lambda j, i, kk: (kk, j)))
    o_spec = pl.BlockSpec((tm, tn), lambda j, i, kk: (i, j))
    has_acc = acc is not None

    def body(*refs):
        a_ref, b_ref = refs[0], refs[1]
        o_ref, sc = refs[-2], refs[-1]
        kk = pl.program_id(2)
        prod = lax.dot_general(a_ref[...], b_ref[...], (((ca,), (cb,)), ((), ())), preferred_element_type=F32)

        @pl.when(kk == 0)
        def _():
            sc[...] = prod + refs[2][...] if has_acc else prod

        @pl.when(kk > 0)
        def _():
            sc[...] += prod

        @pl.when(kk == nk - 1)
        def _():
            o_ref[...] = sc[...].astype(out_dtype)

    ins = [a.astype(BF16), b.astype(BF16)] + ([acc] if has_acc else [])
    return pl.pallas_call(
        body, name=name, grid=(n // tn, m // tm, nk),
        in_specs=[a_spec, b_spec] + ([o_spec] if has_acc else []),
        out_specs=o_spec, out_shape=jax.ShapeDtypeStruct((m, n), out_dtype),
        scratch_shapes=[pltpu.VMEM((tm, tn), F32)],
        input_output_aliases={2: 0} if has_acc else {},
        compiler_params=_cparams(3),
    )(*ins)


MESH_ID = pl.DeviceIdType.MESH
ANY_SPEC = pl.BlockSpec(memory_space=pl.ANY)


def _place():
    return lax.axis_index("x"), lax.axis_index("y"), lax.axis_index("c")


def chip_exchange(name, arrs, scatter):
    n = len(arrs)

    def body(*refs):
        ins, outs = refs[:n], refs[n:2 * n]
        send_sems, recv_sems, local_sems = refs[2 * n:]
        x, y, c = _place()
        me = 2 * x + y
        others = [(1 - x, y), (x, 1 - y), (1 - x, 1 - y)]

        def remote(i, j, landing_from_me):
            ox, oy = others[j]
            peer = 2 * ox + oy
            src = ins[i].at[peer] if scatter else ins[i]
            dst = outs[i].at[me if landing_from_me else peer]
            return pltpu.make_async_remote_copy(src_ref=src, dst_ref=dst, send_sem=send_sems.at[3 * i + j],
                                                recv_sem=recv_sems.at[3 * i + j], device_id=(ox, oy, c),
                                                device_id_type=MESH_ID)

        local = [pltpu.make_async_copy(ins[i].at[me] if scatter else ins[i], outs[i].at[me], local_sems.at[i])
                 for i in range(n)]
        for cp in local:
            cp.start()
        sent = [remote(i, j, True) for i in range(n) for j in range(3)]
        for cp in sent:
            cp.start()
        for i in range(n):
            for j in range(3):
                remote(i, j, False).wait_recv()
        for cp in sent:
            cp.wait_send()
        for cp in local:
            cp.wait()

    def out_of(a):
        return jax.ShapeDtypeStruct(a.shape if scatter else (N_CHIPS,) + a.shape, a.dtype)

    return pl.pallas_call(
        body, name=name, in_specs=[ANY_SPEC] * n, out_specs=[ANY_SPEC] * n, out_shape=[out_of(a) for a in arrs],
        scratch_shapes=[pltpu.SemaphoreType.DMA((3 * n,)), pltpu.SemaphoreType.DMA((3 * n,)), pltpu.SemaphoreType.DMA((n,))],
    )(*arrs)


def sibling_exchange(name, arrs):
    n = len(arrs)

    def body(*refs):
        ins, outs = refs[:n], refs[n:2 * n]
        send_sems, recv_sems = refs[2 * n:]
        x, y, c = _place()
        cps = [pltpu.make_async_remote_copy(src_ref=ins[i], dst_ref=outs[i], send_sem=send_sems.at[i], recv_sem=recv_sems.at[i],
                                            device_id=(x, y, 1 - c), device_id_type=MESH_ID) for i in range(n)]
        for cp in cps:
            cp.start()
        for cp in cps:
            cp.wait_recv()
        for cp in cps:
            cp.wait_send()

    return pl.pallas_call(
        body, name=name, in_specs=[ANY_SPEC] * n, out_specs=[ANY_SPEC] * n,
        out_shape=[jax.ShapeDtypeStruct(a.shape, a.dtype) for a in arrs],
        scratch_shapes=[pltpu.SemaphoreType.DMA((n,)), pltpu.SemaphoreType.DMA((n,))],
    )(*arrs)


def gather_all(name, vec):
    def body(in_ref, out_ref, send_sems, recv_sems, local_sem):
        x, y, c = _place()
        me = 4 * x + 2 * y + c

        def peer(mask):
            return (1 - x if mask & 4 else x, 1 - y if mask & 2 else y, 1 - c if mask & 1 else c)

        def remote(mask, landing_from_me):
            px, py, pc = peer(mask)
            slot = me if landing_from_me else 4 * px + 2 * py + pc
            return pltpu.make_async_remote_copy(src_ref=in_ref, dst_ref=out_ref.at[slot], send_sem=send_sems.at[mask - 1],
                                                recv_sem=recv_sems.at[mask - 1], device_id=(px, py, pc), device_id_type=MESH_ID)

        local = pltpu.make_async_copy(in_ref, out_ref.at[me], local_sem)
        local.start()
        sent = [remote(mask, True) for mask in range(1, N_DEV)]
        for cp in sent:
            cp.start()
        for mask in range(1, N_DEV):
            remote(mask, False).wait_recv()
        for cp in sent:
            cp.wait_send()
        local.wait()

    return pl.pallas_call(
        body, name=name, in_specs=[ANY_SPEC], out_specs=ANY_SPEC,
        out_shape=jax.ShapeDtypeStruct((N_DEV,) + vec.shape, vec.dtype),
        scratch_shapes=[pltpu.SemaphoreType.DMA((N_DEV - 1,)), pltpu.SemaphoreType.DMA((N_DEV - 1,)), pltpu.SemaphoreType.DMA],
    )(vec)


def sum_slots(name, arr, tr):
    k, r, c = arr.shape

    def body(a_ref, o_ref):
        acc = a_ref[0]
        for j in range(1, k):
            acc = acc + a_ref[j]
        o_ref[...] = acc

    return pl.pallas_call(
        body, name=name, grid=(r // tr,),
        in_specs=[pl.BlockSpec((k, tr, c), lambda i: (0, i, 0))], out_specs=pl.BlockSpec((tr, c), lambda i: (i, 0)),
        out_shape=jax.ShapeDtypeStruct((r, c), F32), compiler_params=_cparams(),
    )(arr)


def adamw(name, parts, w, m, v, tr):
    r, c = w.shape
    npart = len(parts)
    bc1 = 1.0 - ADAM_B1 ** ADAM_STEP
    bc2 = 1.0 - ADAM_B2 ** ADAM_STEP

    def body(*refs):
        g = refs[0][...]
        for j in range(1, npart):
            g = g + refs[j][...]
        w_ref, m_ref, v_ref, g_out, d_out, m_out, v_out = refs[npart:]
        m_new = ADAM_B1 * m_ref[...] + (1.0 - ADAM_B1) * g
        v_new = ADAM_B2 * v_ref[...] + (1.0 - ADAM_B2) * jnp.square(g)
        m_hat = m_new / bc1
        v_hat = v_new / bc2
        g_out[...] = g
        d_out[...] = -ADAM_LR * (m_hat / (jnp.sqrt(v_hat) + ADAM_EPS) + ADAM_WD * w_ref[...])
        m_out[...] = m_new
        v_out[...] = v_new

    spec = pl.BlockSpec((tr, c), lambda i: (i, 0))
    return pl.pallas_call(
        body, name=name, grid=(r // tr,), in_specs=[spec] * (npart + 3), out_specs=[spec] * 4,
        out_shape=[jax.ShapeDtypeStruct((r, c), F32)] * 4, compiler_params=_cparams(),
    )(*parts, w, m, v)


ROW_TM = 256


def _row(v):
    return v.reshape(1, -1)


def _lane_pad(vals, at):
    return jnp.zeros((1, LANES), F32).at[0, at:at + vals.shape[0]].set(vals)


def _layer_params(sm, l):
    return dict(
        conf=[sm["conv_dw_w"][l], _row(sm["conv_dw_b"][l]), _row(sm["conv_ln_g"][l]), _row(sm["conv_ln_b"][l])],
        pre=[sm["gdn_conv_q"][l], sm["gdn_conv_k"][l], sm["gdn_conv_v"][l],
             _lane_pad(sm["gdn_a_log"][l], GDN_HEADS), _lane_pad(sm["gdn_dt_bias"][l], GDN_HEADS)],
        scan=_row(sm["gdn_norm_g"][l]),
        sgu=[_row(sm["sgu_ln_g"][l]), _row(sm["sgu_ln_b"][l]), sm["sgu_w_s"][l].reshape(SGU_GROUPS * SGU_CHUNK, SGU_CHUNK),
             sm["sgu_b_s"][l].T],
        mix=[_row(sm["b_gate"][l])],
        ln1=[_row(sm["ln1_g"][l]), _row(sm["ln1_b"][l])],
        ff=[_row(sm["b_ff1"][l])],
        ln2=[_row(sm["b_ff2"][l]), _row(sm["ln2_g"][l]), _row(sm["ln2_b"][l])],
    )


def _split_w_in(w):
    zpad = jnp.zeros((D_MODEL, LANES - 2 * GDN_HEADS), w.dtype)
    return dict(A=w[:, 0:1024], B=jnp.concatenate([w[:, 1024:2560], w[:, 3072:3080], zpad], axis=1), Z=w[:, 2560:3072],
                C=w[:, 3080:4104], G=w[:, 4104:7176])


def _join_w_in(d):
    return jnp.concatenate([d["A"], d["B"][:, 0:1536], d["Z"], d["B"][:, 1536:1544], d["C"], d["G"]], axis=1)


def layer_forward(l, xin, xin_b, big, lp):
    tag = f"l{l}_"
    wi = big["w_in"][l]
    sv = dict(xin=xin, xin_b=xin_b)
    for r in "GACBZ":
        sv["p" + r] = matmul(tag + "proj_" + r, xin_b, wi[r], "nn", F32)
    (sv["ya_in"],) = halo_fwd(tag + "conformer", f_conformer, ROW_TM, CONV_HALO, sv["pA"], lp["conf"], [[BF16]])
    sv["qn"], sv["kn"], sv["vv"], sv["gb"] = halo_fwd(tag + "gdn_pre", f_gdn_pre, ROW_TM, GDN_HALO, sv["pB"], lp["pre"], [[F32]] * 4)
    prep = stage_fwd(tag + "gdn_prep", f_gdn_prep, GDN_PAIR, [sv["qn"], sv["kn"], sv["vv"], sv["gb"]], [], [[F32]] * 6)
    sv["prep"] = prep
    sv["yb_in"], sv["states"] = scan_fwd(tag + "gdn_scan", lp["scan"], *prep, sv["pZ"])
    (sv["yc_in"],) = stage_fwd(tag + "sgu", f_sgu, ROW_TM, [sv["pC"]], lp["sgu"], [[BF16]])
    sv["ya"] = matmul(tag + "out_a", sv["ya_in"], big["w_pa"][l], "nn", F32)
    sv["yb"] = matmul(tag + "out_b", sv["yb_in"], big["w_pb"][l], "nn", F32)
    sv["yc"] = matmul(tag + "out_c", sv["yc_in"], big["w_pc"][l], "nn", F32)
    (sv["mix"],) = stage_fwd(tag + "mix", f_mix, ROW_TM, [sv["ya"], sv["yb"], sv["yc"], sv["pG"]], lp["mix"], [[BF16]])
    sv["m"] = matmul(tag + "w_o", sv["mix"], big["w_o"][l], "nn", F32)
    sv["x1"], sv["x1_b"] = stage_fwd(tag + "ln1", f_ln1, ROW_TM, [xin, sv["m"]], lp["ln1"], [[F32, BF16]])
    sv["hpre"] = matmul(tag + "ff1", sv["x1_b"], big["w_ff1"][l], "nn", F32)
    (sv["h"],) = stage_fwd(tag + "ffact", f_ffact, ROW_TM, [sv["hpre"]], lp["ff"], [[BF16]])
    sv["f2"] = matmul(tag + "ff2", sv["h"], big["w_ff2"][l], "nn", F32)
    x2, x2_b = stage_fwd(tag + "ln2", f_ln2, ROW_TM, [sv["x1"], sv["f2"]], lp["ln2"], [[F32, BF16]])
    return x2, x2_b, sv


def layer_backward(l, dx2, sv, big, lp):
    tag = f"l{l}_b_"
    g = {}
    dx1_a, df2, db2, dg2, dbe2 = stage_bwd(tag + "ln2", f_ln2, ROW_TM, [sv["x1"], sv["f2"]], lp["ln2"], [dx2], [F32, BF16])
    g["b_ff2"], g["ln2_g"], g["ln2_b"] = db2[0], dg2[0], dbe2[0]
    g["w_ff2"] = matmul(tag + "dw_ff2", sv["h"], df2, "tn", F32)
    dh = matmul(tag + "dx_ff2", df2, big["w_ff2"][l], "nt", F32)
    dhpre, db1 = stage_bwd(tag + "ffact", f_ffact, ROW_TM, [sv["hpre"]], lp["ff"], [[dh]], [BF16])
    g["b_ff1"] = db1[0]
    g["w_ff1"] = matmul(tag + "dw_ff1", sv["x1_b"], dhpre, "tn", F32)
    dx1_b = matmul(tag + "dx_ff1", dhpre, big["w_ff1"][l], "nt", F32)
    dxin_a, dm, dg1, dbe1 = stage_bwd(tag + "ln1", f_ln1, ROW_TM, [sv["xin"], sv["m"]], lp["ln1"], [[dx1_a, dx1_b]], [F32, BF16])
    g["ln1_g"], g["ln1_b"] = dg1[0], dbe1[0]
    g["w_o"] = matmul(tag + "dw_o", sv["mix"], dm, "tn", F32)
    dmix = matmul(tag + "dx_o", dm, big["w_o"][l], "nt", F32)
    dya, dyb, dyc, dp_g, dbg = stage_bwd(tag + "mix", f_mix, ROW_TM, [sv["ya"], sv["yb"], sv["yc"], sv["pG"]], lp["mix"],
                                         [[dmix]], [BF16] * 4)
    g["b_gate"] = dbg[0]
    g["w_pa"] = matmul(tag + "dw_pa", sv["ya_in"], dya, "tn", F32)
    g["w_pb"] = matmul(tag + "dw_pb", sv["yb_in"], dyb, "tn", F32)
    g["w_pc"] = matmul(tag + "dw_pc", sv["yc_in"], dyc, "tn", F32)
    dya_in = matmul(tag + "dx_pa", dya, big["w_pa"][l], "nt", F32)
    dyb_in = matmul(tag + "dx_pb", dyb, big["w_pb"][l], "nt", F32)
    dyc_in = matmul(tag + "dx_pc", dyc, big["w_pc"][l], "nt", F32)
    dp_a, dcw, dcb, dcg, dcbe = halo_bwd(tag + "conformer", f_conformer, ROW_TM, CONV_HALO, sv["pA"], lp["conf"], [dya_in], BF16)
    g["conv_dw_w"], g["conv_dw_b"], g["conv_ln_g"], g["conv_ln_b"] = dcw, dcb[0], dcg[0], dcbe[0]
    dp_c, dsg, dsb, dsw, dsbs = stage_bwd(tag + "sgu", f_sgu, ROW_TM, [sv["pC"]], lp["sgu"], [[dyc_in]], [BF16])
    g["sgu_ln_g"], g["sgu_ln_b"] = dsg[0], dsb[0]
    g["sgu_w_s"] = dsw.reshape(SGU_GROUPS, SGU_CHUNK, SGU_CHUNK)
    g["sgu_b_s"] = dsbs.T
    *dprep, dp_z, dng = scan_bwd(tag + "gdn_scan", lp["scan"], sv["states"], *sv["prep"], sv["pZ"], dyb_in)
    g["gdn_norm_g"] = dng[0]
    dqn, dkn, dvv, dgb = stage_bwd(tag + "gdn_prep", f_gdn_prep, GDN_PAIR, [sv["qn"], sv["kn"], sv["vv"], sv["gb"]], [],
                                   [[d] for d in dprep], [F32] * 4)
    dp_b, dq, dk, dv, dal, ddt = halo_bwd(tag + "gdn_pre", f_gdn_pre, ROW_TM, GDN_HALO, sv["pB"], lp["pre"], [dqn, dkn, dvv, dgb], BF16)
    g["gdn_conv_q"], g["gdn_conv_k"], g["gdn_conv_v"] = dq, dk, dv
    g["gdn_a_log"] = dal[0, GDN_HEADS:2 * GDN_HEADS]
    g["gdn_dt_bias"] = ddt[0, GDN_HEADS:2 * GDN_HEADS]
    dps = dict(G=dp_g, A=dp_a, C=dp_c, B=dp_b, Z=dp_z)
    wi = big["w_in"][l]
    g["w_in"] = _join_w_in({r: matmul(tag + "dw_in_" + r, sv["xin_b"], dps[r], "tn", F32) for r in "GACBZ"})
    dxin_b = None
    for r in "GACBZ":
        dxin_b = matmul(tag + "dx_in_" + r, dps[r], wi[r], "nt", F32, acc=dxin_b)
    return [dxin_a, dxin_b], g


SMALL_PER_LAYER = ("b_gate", "conv_dw_w", "conv_dw_b", "conv_ln_g", "conv_ln_b", "gdn_conv_q", "gdn_conv_k", "gdn_conv_v",
                   "gdn_a_log", "gdn_dt_bias", "gdn_norm_g", "sgu_ln_g", "sgu_ln_b", "sgu_w_s", "sgu_b_s", "ln1_g", "ln1_b",
                   "b_ff1", "b_ff2", "ln2_g", "ln2_b")
BIG = ("w_in", "w_pa", "w_pb", "w_pc", "w_o", "w_ff1", "w_ff2")


def local_step(x, target, big, sm):
    lps = [_layer_params(sm, l) for l in range(DEPTH)]
    p_in = [_row(sm["ln_in_g"]), _row(sm["ln_in_b"])]
    xc, xc_b = stage_fwd("ln_in", f_ln_in, ROW_TM, [x], p_in, [[F32, BF16]])
    saved = []
    for l in range(DEPTH):
        xc, xc_b, sv = layer_forward(l, xc, xc_b, big, lps[l])
        saved.append(sv)
    loss_part, dy = loss_stage("loss", xc, target, ROW_TM)
    dx = [dy]
    per_layer = [None] * DEPTH
    for l in reversed(range(DEPTH)):
        dx, per_layer[l] = layer_backward(l, dx, saved[l], big, lps[l])
    grad_x, dgi, dbi = stage_bwd("ln_in_b", f_ln_in, ROW_TM, [x], p_in, [dx], [F32])
    grads = {n: jnp.stack([per_layer[l][n] for l in range(DEPTH)]) for n in SMALL_PER_LAYER + BIG}
    grads["ln_in_g"], grads["ln_in_b"] = dgi[0], dbi[0]
    return loss_part, grad_x, grads


PARAM_NAMES = ("ln_in_g", "ln_in_b", "w_in", "b_gate", "conv_dw_w", "conv_dw_b", "conv_ln_g", "conv_ln_b", "w_pa",
               "gdn_conv_q", "gdn_conv_k", "gdn_conv_v", "gdn_a_log", "gdn_dt_bias", "gdn_norm_g", "w_pb", "sgu_ln_g",
               "sgu_ln_b", "sgu_w_s", "sgu_b_s", "w_pc", "w_o", "ln1_g", "ln1_b", "w_ff1", "b_ff1", "w_ff2", "b_ff2",
               "ln2_g", "ln2_b")
SMALL = tuple(n for n in PARAM_NAMES if n not in BIG)
COL_SHARDED_SMALL = ("conv_dw_w", "gdn_conv_q", "gdn_conv_k", "gdn_conv_v")
ROW_SHARDED_BIG = ("w_o", "w_ff2")
ADAM_ROWS = 128


def _unshard(g, rows):
    n, l, r, c = g.shape
    if rows:
        return jnp.moveaxis(g, 0, 1).reshape(l, n * r, c)
    return jnp.moveaxis(g, 0, 2).reshape(l, r, n * c)


def _to_shards(w, rows):
    l, r, c = w.shape
    if rows:
        return jnp.moveaxis(w.reshape(l, N_CHIPS, r // N_CHIPS, c), 1, 0).reshape(N_CHIPS, l * (r // N_CHIPS), c)
    return jnp.moveaxis(w.reshape(l, r, N_CHIPS, c // N_CHIPS), 2, 0).reshape(N_CHIPS, l * r, c // N_CHIPS)


def _pack(vals, rows_multiple=8):
    flat = jnp.concatenate([v.reshape(-1) for v in vals])
    per = LANES * rows_multiple
    padded = -(-flat.shape[0] // per) * per
    return jnp.pad(flat, (0, padded - flat.shape[0])).reshape(-1, LANES)


def _unpack(packed, shapes):
    flat = packed.reshape(-1)
    out, at = [], 0
    for s in shapes:
        size = 1
        for d in s:
            size *= d
        out.append(flat[at:at + size].reshape(s))
        at += size
    return out


def kernel(x, ln_in_g, ln_in_b, w_in, b_gate, conv_dw_w, conv_dw_b, conv_ln_g, conv_ln_b, w_pa, gdn_conv_q, gdn_conv_k, gdn_conv_v, gdn_a_log, gdn_dt_bias, gdn_norm_g, w_pb, sgu_ln_g, sgu_ln_b, sgu_w_s, sgu_b_s, w_pc, w_o, ln1_g, ln1_b, w_ff1, b_ff1, w_ff2, b_ff2, ln2_g, ln2_b, loss_target, m_ln_in_g, m_ln_in_b, m_w_in, m_b_gate, m_conv_dw_w, m_conv_dw_b, m_conv_ln_g, m_conv_ln_b, m_w_pa, m_gdn_conv_q, m_gdn_conv_k, m_gdn_conv_v, m_gdn_a_log, m_gdn_dt_bias, m_gdn_norm_g, m_w_pb, m_sgu_ln_g, m_sgu_ln_b, m_sgu_w_s, m_sgu_b_s, m_w_pc, m_w_o, m_ln1_g, m_ln1_b, m_w_ff1, m_b_ff1, m_w_ff2, m_b_ff2, m_ln2_g, m_ln2_b, v_ln_in_g, v_ln_in_b, v_w_in, v_b_gate, v_conv_dw_w, v_conv_dw_b, v_conv_ln_g, v_conv_ln_b, v_w_pa, v_gdn_conv_q, v_gdn_conv_k, v_gdn_conv_v, v_gdn_a_log, v_gdn_dt_bias, v_gdn_norm_g, v_w_pb, v_sgu_ln_g, v_sgu_ln_b, v_sgu_w_s, v_sgu_b_s, v_w_pc, v_w_o, v_ln1_g, v_ln1_b, v_w_ff1, v_b_ff1, v_w_ff2, v_b_ff2, v_ln2_g, v_ln2_b):
    w = dict(zip(PARAM_NAMES, (ln_in_g, ln_in_b, w_in, b_gate, conv_dw_w, conv_dw_b, conv_ln_g, conv_ln_b, w_pa, gdn_conv_q, gdn_conv_k, gdn_conv_v, gdn_a_log, gdn_dt_bias, gdn_norm_g, w_pb, sgu_ln_g, sgu_ln_b, sgu_w_s, sgu_b_s, w_pc, w_o, ln1_g, ln1_b, w_ff1, b_ff1, w_ff2, b_ff2, ln2_g, ln2_b)))
    m = dict(zip(PARAM_NAMES, (m_ln_in_g, m_ln_in_b, m_w_in, m_b_gate, m_conv_dw_w, m_conv_dw_b, m_conv_ln_g, m_conv_ln_b, m_w_pa, m_gdn_conv_q, m_gdn_conv_k, m_gdn_conv_v, m_gdn_a_log, m_gdn_dt_bias, m_gdn_norm_g, m_w_pb, m_sgu_ln_g, m_sgu_ln_b, m_sgu_w_s, m_sgu_b_s, m_w_pc, m_w_o, m_ln1_g, m_ln1_b, m_w_ff1, m_b_ff1, m_w_ff2, m_b_ff2, m_ln2_g, m_ln2_b)))
    v = dict(zip(PARAM_NAMES, (v_ln_in_g, v_ln_in_b, v_w_in, v_b_gate, v_conv_dw_w, v_conv_dw_b, v_conv_ln_g, v_conv_ln_b, v_w_pa, v_gdn_conv_q, v_gdn_conv_k, v_gdn_conv_v, v_gdn_a_log, v_gdn_dt_bias, v_gdn_norm_g, v_w_pb, v_sgu_ln_g, v_sgu_ln_b, v_sgu_w_s, v_sgu_b_s, v_w_pc, v_w_o, v_ln1_g, v_ln1_b, v_w_ff1, v_b_ff1, v_w_ff2, v_b_ff2, v_ln2_g, v_ln2_b)))
    chip = 2 * lax.axis_index("x") + lax.axis_index("y")

    col_small_shapes = [w[n].shape for n in COL_SHARDED_SMALL]
    gathered = chip_exchange("gather_weights", [w[n].astype(BF16) for n in BIG] + [_pack([w[n] for n in COL_SHARDED_SMALL])],
                             scatter=False)
    big = {}
    for n, g in zip(BIG, gathered):
        whole = _unshard(g, n in ROW_SHARDED_BIG)
        big[n] = [_split_w_in(whole[l]) for l in range(DEPTH)] if n == "w_in" else whole
    sm = {n: w[n] for n in SMALL}
    per_chip = [_unpack(gathered[-1][s], col_small_shapes) for s in range(N_CHIPS)]
    for i, n in enumerate(COL_SHARDED_SMALL):
        sm[n] = jnp.concatenate([per_chip[s][i] for s in range(N_CHIPS)], axis=-1)

    loss_part, grad_x, grads = local_step(x[0], loss_target[0], big, sm)

    sends = [_to_shards(grads[n], n in ROW_SHARDED_BIG) for n in BIG]
    received = chip_exchange("scatter_grads", sends, scatter=True)
    partial = [sum_slots("sum_chips_" + n, r, ADAM_ROWS) for n, r in zip(BIG, received)]
    sibling = sibling_exchange("swap_cores", partial)
    out = {}
    for n, mine, theirs in zip(BIG, partial, sibling):
        shape = w[n].shape
        res = adamw("adamw_" + n, [mine, theirs], w[n].reshape(-1, shape[-1]), m[n].reshape(-1, shape[-1]),
                    v[n].reshape(-1, shape[-1]), ADAM_ROWS)
        out[n] = [r.reshape(shape) for r in res]

    small_shapes = [grads[n].shape for n in SMALL]
    vec = _pack([grads[n] for n in SMALL] + [jnp.sum(loss_part).reshape(1)])
    total = sum_slots("sum_small", gather_all("gather_small", vec), 8)
    whole = _unpack(total, small_shapes + [(1,)])
    loss = whole[-1][0]
    g_small = {}
    for n, g in zip(SMALL, whole[:-1]):
        if n in COL_SHARDED_SMALL:
            width = g.shape[-1] // N_CHIPS
            g = lax.dynamic_slice_in_dim(g, chip * width, width, axis=g.ndim - 1)
        g_small[n] = g
    local_shapes = [w[n].shape for n in SMALL]
    res = adamw("adamw_small", [_pack([g_small[n] for n in SMALL])], _pack([w[n] for n in SMALL]),
                _pack([m[n] for n in SMALL]), _pack([v[n] for n in SMALL]), 8)
    unpacked = [_unpack(r, local_shapes) for r in res]
    for i, n in enumerate(SMALL):
        out[n] = [unpacked[k][i] for k in range(4)]

    return (loss, grad_x[None], *[out[n][0] for n in PARAM_NAMES], *[out[n][1] for n in PARAM_NAMES],
            *[out[n][2] for n in PARAM_NAMES], *[out[n][3] for n in PARAM_NAMES])
```

```python
import functools

import jax
import jax.numpy as jnp
from jax import lax
from jax.experimental import pallas as pl
from jax.experimental.pallas import tpu as pltpu

F32 = jnp.float32
BF16 = jnp.bfloat16

D_MODEL = 1024
DEPTH = 2
CONV_DIM = 512
CONV_WIDTH = 31
CONV_HALO = 32
GDN_HEADS = 4
GDN_HD = 128
GDN_CHUNK = 64
GDN_PAIR = 2 * GDN_CHUNK
GDN_CONV = 4
GDN_HALO = 8
SGU_GROUPS = 4
SGU_GD = 128
SGU_CHUNK = 128
D_FF = 4096
ALPHA = (2 * DEPTH) ** 0.25
LN_EPS = 1e-5
RMS_EPS = 1e-6
ADAM_LR, ADAM_B1, ADAM_B2, ADAM_EPS, ADAM_WD, ADAM_STEP = 0.001, 0.9, 0.999, 1e-08, 0.01, 10
N_CHIPS = 4
N_DEV = 8
LANES = 128
VMEM_LIMIT = 52 * 1024 * 1024

W_GATE, W_A, W_C, W_B, W_Z = 3 * D_MODEL, 2 * CONV_DIM, 2 * 512, 3 * 512 + LANES, 512


def _dg(a, b, ca, cb):
    return lax.dot_general(a.astype(BF16), b.astype(BF16), (((ca,), (cb,)), ((), ())), preferred_element_type=F32)


@jax.custom_vjp
def mm_nn(a, b):
    return _dg(a, b, 1, 0)


mm_nn.defvjp(lambda a, b: (_dg(a, b, 1, 0), (a, b)), lambda r, g: (_dg(g, r[1], 1, 1), _dg(r[0], g, 0, 0)))


@jax.custom_vjp
def mm_nt(a, b):
    return _dg(a, b, 1, 1)


mm_nt.defvjp(lambda a, b: (_dg(a, b, 1, 1), (a, b)), lambda r, g: (_dg(g, r[1], 1, 0), _dg(g, r[0], 0, 0)))


@jax.custom_vjp
def mm_tn(a, b):
    return _dg(a, b, 0, 0)


mm_tn.defvjp(lambda a, b: (_dg(a, b, 0, 0), (a, b)), lambda r, g: (_dg(r[1], g, 1, 1), _dg(r[0], g, 1, 0)))


def _ln(x, g, b):
    mu = jnp.mean(x, -1, keepdims=True)
    xc = x - mu
    var = jnp.mean(xc * xc, -1, keepdims=True)
    return xc * lax.rsqrt(var + LN_EPS) * g + b


def _iota2(shape, dim):
    return lax.broadcasted_iota(jnp.int32, shape, dim)


def f_ln_in(params, x):
    g, b = params
    return (_ln(x, g, b),)


def f_conformer(params, halo, pa):
    w, b, g, be = params
    tm = pa.shape[0]
    full = jnp.concatenate([halo, pa], axis=0)
    h = full[:, :CONV_DIM] * jax.nn.sigmoid(full[:, CONV_DIM:])
    off = CONV_HALO - (CONV_WIDTH - 1)
    acc = w[0:1, :] * h[off:off + tm, :]
    for j in range(1, CONV_WIDTH):
        acc = acc + w[j:j + 1, :] * h[off + j:off + j + tm, :]
    return (jax.nn.silu(_ln(acc + b, g, be)),)


def f_gdn_pre(params, halo, pb):
    cq, ck, cv, alog, dtb = params
    tm = pb.shape[0]
    full = jnp.concatenate([halo, pb], axis=0)
    off = GDN_HALO - (GDN_CONV - 1)

    def conv(cols, w):
        acc = w[0:1, :] * cols[off:off + tm, :]
        for j in range(1, GDN_CONV):
            acc = acc + w[j:j + 1, :] * cols[off + j:off + j + tm, :]
        return jax.nn.silu(acc)

    def l2n(t, scale):
        parts = []
        for h in range(GDN_HEADS):
            th = t[:, h * GDN_HD:(h + 1) * GDN_HD]
            parts.append(th * (lax.rsqrt(jnp.sum(th * th, -1, keepdims=True) + RMS_EPS) * scale))
        return jnp.concatenate(parts, axis=1)

    q = l2n(conv(full[:, 0:512], cq), GDN_HD ** -0.5)
    k = l2n(conv(full[:, 512:1024], ck), 1.0)
    v = conv(full[:, 1024:1536], cv)
    logits = pb[:, 1536:1536 + LANES]
    z = logits + dtb
    softplus = jnp.maximum(z, 0.0) + jnp.log(1.0 + jnp.exp(-jnp.abs(z)))
    g = -jnp.exp(alog) * softplus
    beta = jax.nn.sigmoid(logits)
    lane = _iota2(logits.shape, 1)
    gb = jnp.where(lane < GDN_HEADS, beta, jnp.where(lane < 2 * GDN_HEADS, g, 0.0))
    return q, k, v, gb


def _split2(x):
    hi = x.astype(BF16)
    return hi, (x - hi.astype(F32)).astype(BF16)


def _dg3(a, b, ca, cb):
    ah, al = _split2(a)
    bh, bl = _split2(b)
    d = lambda p, q: lax.dot_general(p, q, (((ca,), (cb,)), ((), ())), preferred_element_type=F32)
    return d(ah, bh) + (d(ah, bl) + d(al, bh))


def _inv_unit_lower(low):
    n = low.shape[0]
    eye = jnp.where(_iota2((n, n), 0) == _iota2((n, n), 1), 1.0, 0.0).astype(F32)
    p = -low
    x = eye + p
    for _ in range(5):
        p = _dg3(p, p, 1, 0)
        x = x + _dg3(x, p, 1, 0)
    return x


def _pair_masks():
    n = GDN_PAIR
    r, c = _iota2((n, n), 0), _iota2((n, n), 1)
    same = (r >= GDN_CHUNK) == (c >= GDN_CHUNK)
    return same, jnp.logical_and(same, r >= c), jnp.logical_and(same, r > c), jnp.logical_and(same, r <= c)


def _masked_sum(mask, g):
    mk = jnp.where(mask, 1.0, 0.0).astype(BF16)
    g1 = g.astype(BF16)
    r1 = g - g1.astype(F32)
    g2 = r1.astype(BF16)
    g3 = (r1 - g2.astype(F32)).astype(BF16)
    d = lambda q: jnp.dot(mk, q, preferred_element_type=F32)
    return d(g1) + (d(g2) + d(g3))


@jax.custom_vjp
def chunk_cumsum(g):
    return _masked_sum(_pair_masks()[1], g)


chunk_cumsum.defvjp(lambda g: (_masked_sum(_pair_masks()[1], g), None), lambda _, ct: (_masked_sum(_pair_masks()[3], ct),))


def _prep_gates(gb):
    gam_all = chunk_cumsum(gb)
    first = _iota2((GDN_PAIR, LANES), 0) < GDN_CHUNK
    glast = jnp.where(first, gam_all[GDN_CHUNK - 1:GDN_CHUNK, :], gam_all[GDN_PAIR - 1:GDN_PAIR, :])
    return gam_all, gam_all.T, glast


def _head_gates(gb, gam_all, gam_t, h, causal):
    gc = gam_all[:, GDN_HEADS + h:GDN_HEADS + h + 1]
    gr = gam_t[GDN_HEADS + h:GDN_HEADS + h + 1, :]
    return gc, gb[:, h:h + 1], jnp.exp(jnp.where(causal, gc - gr, -jnp.inf))


def f_prep_low(kn, gb):
    _, causal, strict, _ = _pair_masks()
    gam_all, gam_t, _ = _prep_gates(gb)
    lows = []
    for h in range(GDN_HEADS):
        k = kn[:, h * GDN_HD:(h + 1) * GDN_HD]
        _, beta, decay = _head_gates(gb, gam_all, gam_t, h, causal)
        lows.append(jnp.where(strict, beta * mm_nt(k, k) * decay, 0.0))
    return jnp.concatenate(lows, axis=1)


def f_prep_rest(qn, kn, vv, gb, tinv_all):
    _, causal, _, _ = _pair_masks()
    gam_all, gam_t, glast = _prep_gates(gb)
    us, ws, as_, qds, kds = [], [], [], [], []
    for h in range(GDN_HEADS):
        hs = slice(h * GDN_HD, (h + 1) * GDN_HD)
        q, k, v, tinv = qn[:, hs], kn[:, hs], vv[:, hs], tinv_all[:, hs]
        gc, beta, decay = _head_gates(gb, gam_all, gam_t, h, causal)
        eg = jnp.exp(gc)
        us.append(mm_nn(tinv, beta * v))
        ws.append(mm_nn(tinv, beta * k * eg))
        as_.append(mm_nt(q, k) * decay)
        qds.append(q * eg)
        kds.append(k * jnp.exp(glast[:, GDN_HEADS + h:GDN_HEADS + h + 1] - gc))
    cat = lambda xs: jnp.concatenate(xs, axis=1)
    return cat(us), cat(ws), cat(as_), cat(qds), cat(kds), jnp.exp(glast)


def prep_fwd(name, qn, kn, vv, gb):
    t = qn.shape[0]
    hw = GDN_HEADS * GDN_HD

    def body(q_ref, k_ref, v_ref, gb_ref, *outs):
        low = f_prep_low(k_ref[...], gb_ref[...])
        tinv = jnp.concatenate([_inv_unit_lower(low[:, h * GDN_HD:(h + 1) * GDN_HD]) for h in range(GDN_HEADS)], axis=1)
        res = f_prep_rest(q_ref[...], k_ref[...], v_ref[...], gb_ref[...], tinv)
        for o_ref, val in zip(outs, res + (tinv,)):
            o_ref[...] = val

    widths = [hw, hw, hw, hw, hw, LANES, hw]
    return pl.pallas_call(
        body, name=name, grid=(t // GDN_PAIR,),
        in_specs=[_row_spec(GDN_PAIR, hw)] * 3 + [_row_spec(GDN_PAIR, LANES)],
        out_specs=[_row_spec(GDN_PAIR, wd) for wd in widths],
        out_shape=[jax.ShapeDtypeStruct((t, wd), F32) for wd in widths], compiler_params=_cparams(),
    )(qn, kn, vv, gb)


def prep_bwd(name, qn, kn, vv, gb, tinv, cts):
    t = qn.shape[0]
    hw = GDN_HEADS * GDN_HD

    def body(q_ref, k_ref, v_ref, gb_ref, x_ref, *refs):
        ct = tuple(r[...] for r in refs[:6])
        outs = refs[6:]
        kn_v, gb_v, x_all = k_ref[...], gb_ref[...], x_ref[...]
        _, vjp_low = jax.vjp(f_prep_low, kn_v, gb_v)
        _, vjp_rest = jax.vjp(f_prep_rest, q_ref[...], kn_v, v_ref[...], gb_v, x_all)
        dq, dk, dv, dgb, dx_all = vjp_rest(ct)
        dlows = []
        for h in range(GDN_HEADS):
            hs = slice(h * GDN_HD, (h + 1) * GDN_HD)
            dlows.append(-_dg3(_dg3(x_all[:, hs], dx_all[:, hs], 0, 0), x_all[:, hs], 1, 1))
        dk2, dgb2 = vjp_low(jnp.concatenate(dlows, axis=1))
        outs[0][...] = dq
        outs[1][...] = dk + dk2
        outs[2][...] = dv
        outs[3][...] = dgb + dgb2

    in_w = [hw, hw, hw, LANES, hw] + [hw, hw, hw, hw, hw, LANES]
    out_w = [hw, hw, hw, LANES]
    return pl.pallas_call(
        body, name=name, grid=(t // GDN_PAIR,),
        in_specs=[_row_spec(GDN_PAIR, wd) for wd in in_w], out_specs=[_row_spec(GDN_PAIR, wd) for wd in out_w],
        out_shape=[jax.ShapeDtypeStruct((t, wd), F32) for wd in out_w], compiler_params=_cparams(),
    )(qn, kn, vv, gb, tinv, *cts)


def f_gdn_scan(params, state, u, w, a, qd, kd, egl, z):
    ng = params[0]
    tm = u.shape[0]
    st = [state[h * GDN_HD:(h + 1) * GDN_HD, :] for h in range(GDN_HEADS)]
    zeros = jnp.zeros((GDN_CHUNK, GDN_HD), F32)
    rows = []
    for ci in range(tm // GDN_CHUNK):
        rs = slice(ci * GDN_CHUNK, (ci + 1) * GDN_CHUNK)
        heads = []
        for h in range(GDN_HEADS):
            hs = slice(h * GDN_HD, (h + 1) * GDN_HD)
            vn = u[rs, hs] - mm_nn(w[rs, hs], st[h])
            vpad = jnp.concatenate([vn, zeros] if ci % 2 == 0 else [zeros, vn], axis=0)
            o = mm_nn(qd[rs, hs], st[h]) + mm_nn(a[rs, hs], vpad)
            e = egl[ci * GDN_CHUNK:ci * GDN_CHUNK + 1, GDN_HEADS + h:GDN_HEADS + h + 1]
            st[h] = st[h] * e + mm_tn(kd[rs, hs], vn)
            on = o * lax.rsqrt(jnp.mean(o * o, -1, keepdims=True) + RMS_EPS) * ng
            heads.append(on * jax.nn.silu(z[rs, hs]))
        rows.append(jnp.concatenate(heads, axis=1))
    return jnp.concatenate(rows, axis=0), jnp.concatenate(st, axis=0)


def f_sgu(params, puv):
    g, b, ws, bst = params
    tm = puv.shape[0]
    uv = jax.nn.gelu(puv)
    u = uv[:, :512]
    v = _ln(uv[:, 512:], g, b)
    r, c = _iota2((SGU_CHUNK, SGU_CHUNK), 0), _iota2((SGU_CHUNK, SGU_CHUNK), 1)
    rows = []
    for ci in range(tm // SGU_CHUNK):
        rs = slice(ci * SGU_CHUNK, (ci + 1) * SGU_CHUNK)
        groups = []
        for gi in range(SGU_GROUPS):
            gs = slice(gi * SGU_GD, (gi + 1) * SGU_GD)
            wt = jnp.where(r >= c, ws[gs, :], 0.0)
            groups.append(mm_nn(wt, v[rs, gs]) + bst[:, gi:gi + 1])
        rows.append(jnp.concatenate(groups, axis=1))
    return (u * jnp.concatenate(rows, axis=0),)


def f_mix(params, ya, yb, yc, pg):
    bg = params[0]
    s = jax.nn.sigmoid(pg + bg)
    return (s[:, :D_MODEL] * ya + s[:, D_MODEL:2 * D_MODEL] * yb + s[:, 2 * D_MODEL:] * yc,)


def f_ln1(params, xin, m):
    g, b = params
    return (_ln(ALPHA * xin + m, g, b),)


def f_ffact(params, hpre):
    return (jnp.square(jax.nn.relu(hpre + params[0])),)


def f_ln2(params, x1, f2):
    b2, g, b = params
    return (_ln(ALPHA * x1 + (f2 + b2), g, b),)


def _cparams(n_axes=1):
    return pltpu.CompilerParams(dimension_semantics=("arbitrary",) * n_axes, vmem_limit_bytes=VMEM_LIMIT)


def _row_spec(tm, width, n_tiles=None):
    if n_tiles is None:
        return pl.BlockSpec((tm, width), lambda i: (i, 0))
    return pl.BlockSpec((tm, width), lambda i: (n_tiles - 1 - i, 0))


def _full_spec(shape):
    return pl.BlockSpec(shape, lambda i: (0,) * len(shape))


def stage_fwd(name, fn, tm, rows, params, outs):
    t = rows[0].shape[0]
    nr, npar = len(rows), len(params)
    widths = jax.eval_shape(lambda p, r: fn(p, *r), [jax.ShapeDtypeStruct(p.shape, F32) for p in params],
                            [jax.ShapeDtypeStruct((tm, r.shape[1]), F32) for r in rows])

    def body(*refs):
        r = [x[...].astype(F32) for x in refs[:nr]]
        p = [x[...] for x in refs[nr:nr + npar]]
        res = fn(p, *r)
        k = nr + npar
        for o, dts in zip(res, outs):
            for dt in dts:
                refs[k][...] = o.astype(dt)
                k += 1

    out_shape, out_specs = [], []
    for wd, dts in zip(widths, outs):
        for dt in dts:
            out_shape.append(jax.ShapeDtypeStruct((t, wd.shape[1]), dt))
            out_specs.append(_row_spec(tm, wd.shape[1]))
    return pl.pallas_call(
        body, name=name, grid=(t // tm,),
        in_specs=[_row_spec(tm, r.shape[1]) for r in rows] + [_full_spec(p.shape) for p in params],
        out_specs=out_specs, out_shape=out_shape, compiler_params=_cparams(),
    )(*rows, *params)


def stage_bwd(name, fn, tm, rows, params, cts, drow_dtypes):
    t = rows[0].shape[0]
    nr, npar = len(rows), len(params)
    flat_cts = [c for cl in cts if cl is not None for c in cl]
    nct = len(flat_cts)
    want = [i for i, dt in enumerate(drow_dtypes) if dt is not None]

    def body(*refs):
        r = [x[...].astype(F32) for x in refs[:nr]]
        p = [x[...] for x in refs[nr:nr + npar]]
        res, vjp = jax.vjp(lambda pp, rr: fn(pp, *rr), p, r)
        k = nr + npar
        ct = []
        for o, cl in zip(res, cts):
            if cl is None:
                ct.append(jnp.zeros_like(o))
            else:
                acc = refs[k][...].astype(F32)
                for j in range(1, len(cl)):
                    acc = acc + refs[k + j][...].astype(F32)
                k += len(cl)
                ct.append(acc)
        dp, dr = vjp(tuple(ct))
        for i in want:
            refs[k][...] = dr[i].astype(drow_dtypes[i])
            k += 1

        @pl.when(pl.program_id(0) == 0)
        def _():
            for j in range(npar):
                refs[k + j][...] = jnp.zeros_like(refs[k + j])

        for j in range(npar):
            refs[k + j][...] += dp[j]

    out_shape = [jax.ShapeDtypeStruct(rows[i].shape, drow_dtypes[i]) for i in want]
    out_specs = [_row_spec(tm, rows[i].shape[1]) for i in want]
    out_shape += [jax.ShapeDtypeStruct(p.shape, F32) for p in params]
    out_specs += [_full_spec(p.shape) for p in params]
    return pl.pallas_call(
        body, name=name, grid=(t // tm,),
        in_specs=[_row_spec(tm, r.shape[1]) for r in rows] + [_full_spec(p.shape) for p in params]
        + [_row_spec(tm, c.shape[1]) for c in flat_cts],
        out_specs=out_specs, out_shape=out_shape, compiler_params=_cparams(),
    )(*rows, *params, *flat_cts)


def _halo_spec(tm, halo, width, n_tiles=None):
    per = tm // halo
    if n_tiles is None:
        return pl.BlockSpec((halo, width), lambda i: (jnp.maximum(i * per - 1, 0), 0))
    return pl.BlockSpec((halo, width), lambda i: (jnp.maximum((n_tiles - 1 - i) * per - 1, 0), 0))


def halo_fwd(name, fn, tm, halo, src, params, outs):
    t, width = src.shape
    npar = len(params)
    widths = jax.eval_shape(lambda p, h, r: fn(p, h, r), [jax.ShapeDtypeStruct(p.shape, F32) for p in params],
                            jax.ShapeDtypeStruct((halo, width), F32), jax.ShapeDtypeStruct((tm, width), F32))

    def body(h_ref, x_ref, *refs):
        hal = jnp.where(pl.program_id(0) == 0, 0.0, h_ref[...])
        res = fn([x[...] for x in refs[:npar]], hal, x_ref[...])
        k = npar
        for o, dts in zip(res, outs):
            for dt in dts:
                refs[k][...] = o.astype(dt)
                k += 1

    out_shape, out_specs = [], []
    for wd, dts in zip(widths, outs):
        for dt in dts:
            out_shape.append(jax.ShapeDtypeStruct((t, wd.shape[1]), dt))
            out_specs.append(_row_spec(tm, wd.shape[1]))
    return pl.pallas_call(
        body, name=name, grid=(t // tm,),
        in_specs=[_halo_spec(tm, halo, width), _row_spec(tm, width)] + [_full_spec(p.shape) for p in params],
        out_specs=out_specs, out_shape=out_shape, compiler_params=_cparams(),
    )(src, src, *params)


def halo_bwd(name, fn, tm, halo, src, params, cts, dsrc_dtype):
    t, width = src.shape
    n_tiles = t // tm
    npar = len(params)

    def body(h_ref, x_ref, *refs):
        carry = refs[-1]
        first_tile = pl.program_id(0) == n_tiles - 1
        hal = jnp.where(first_tile, 0.0, h_ref[...])
        p = [x[...] for x in refs[:npar]]
        res, vjp = jax.vjp(fn, p, hal, x_ref[...])
        k = npar
        ct = tuple(refs[k + j][...].astype(F32) for j in range(len(cts)))
        k += len(cts)
        dp, dh, dx = vjp(ct)

        @pl.when(pl.program_id(0) == 0)
        def _():
            carry[...] = jnp.zeros_like(carry)
            for j in range(npar):
                refs[k + 1 + j][...] = jnp.zeros_like(refs[k + 1 + j])

        dx = jnp.concatenate([dx[:tm - halo, :], dx[tm - halo:, :] + carry[...]], axis=0)
        refs[k][...] = dx.astype(dsrc_dtype)
        carry[...] = dh
        for j in range(npar):
            refs[k + 1 + j][...] += dp[j]

    out_shape = [jax.ShapeDtypeStruct((t, width), dsrc_dtype)] + [jax.ShapeDtypeStruct(p.shape, F32) for p in params]
    out_specs = [_row_spec(tm, width, n_tiles)] + [_full_spec(p.shape) for p in params]
    return pl.pallas_call(
        body, name=name, grid=(n_tiles,),
        in_specs=[_halo_spec(tm, halo, width, n_tiles), _row_spec(tm, width, n_tiles)] + [_full_spec(p.shape) for p in params]
        + [_row_spec(tm, c.shape[1], n_tiles) for c in cts],
        out_specs=out_specs, out_shape=out_shape,
        scratch_shapes=[pltpu.VMEM((halo, width), F32)], compiler_params=_cparams(),
    )(src, src, *params, *cts)


SCAN_TM = 256


def scan_fwd(name, norm_g, u, w, a, qd, kd, egl, z):
    t = u.shape[0]
    n_tiles = t // SCAN_TM
    srows = GDN_HEADS * GDN_HD

    def body(ng_ref, u_ref, w_ref, a_ref, qd_ref, kd_ref, e_ref, z_ref, o_ref, s_ref, state):
        @pl.when(pl.program_id(0) == 0)
        def _():
            state[...] = jnp.zeros_like(state)

        s_ref[0] = state[...]
        o, s_new = f_gdn_scan([ng_ref[...]], state[...], u_ref[...], w_ref[...], a_ref[...], qd_ref[...], kd_ref[...],
                              e_ref[...], z_ref[...])
        o_ref[...] = o.astype(BF16)
        state[...] = s_new

    rows = [u, w, a, qd, kd, egl, z]
    return pl.pallas_call(
        body, name=name, grid=(n_tiles,),
        in_specs=[_full_spec(norm_g.shape)] + [_row_spec(SCAN_TM, r.shape[1]) for r in rows],
        out_specs=[_row_spec(SCAN_TM, 512), pl.BlockSpec((1, srows, GDN_HD), lambda i: (i, 0, 0))],
        out_shape=[jax.ShapeDtypeStruct((t, 512), BF16), jax.ShapeDtypeStruct((n_tiles, srows, GDN_HD), F32)],
        scratch_shapes=[pltpu.VMEM((srows, GDN_HD), F32)], compiler_params=_cparams(),
    )(norm_g, *rows)


def scan_bwd(name, norm_g, states, u, w, a, qd, kd, egl, z, dout):
    t = u.shape[0]
    n_tiles = t // SCAN_TM
    srows = GDN_HEADS * GDN_HD

    def body(ng_ref, s_ref, u_ref, w_ref, a_ref, qd_ref, kd_ref, e_ref, z_ref, do_ref,
             du_ref, dw_ref, da_ref, dqd_ref, dkd_ref, de_ref, dz_ref, dng_ref, dstate):
        @pl.when(pl.program_id(0) == 0)
        def _():
            dstate[...] = jnp.zeros_like(dstate)
            dng_ref[...] = jnp.zeros_like(dng_ref)

        args = ([ng_ref[...]], s_ref[0], u_ref[...], w_ref[...], a_ref[...], qd_ref[...], kd_ref[...], e_ref[...], z_ref[...])
        _, vjp = jax.vjp(f_gdn_scan, *args)
        dp, ds, du, dw, da, dqd, dkd, de, dz = vjp((do_ref[...].astype(F32), dstate[...]))
        du_ref[...] = du
        dw_ref[...] = dw
        da_ref[...] = da
        dqd_ref[...] = dqd
        dkd_ref[...] = dkd
        de_ref[...] = de
        dz_ref[...] = dz.astype(BF16)
        dng_ref[...] += dp[0]
        dstate[...] = ds

    rows = [u, w, a, qd, kd, egl, z, dout]
    return pl.pallas_call(
        body, name=name, grid=(n_tiles,),
        in_specs=[_full_spec(norm_g.shape), pl.BlockSpec((1, srows, GDN_HD), lambda i: (n_tiles - 1 - i, 0, 0))]
        + [_row_spec(SCAN_TM, r.shape[1], n_tiles) for r in rows],
        out_specs=[_row_spec(SCAN_TM, r.shape[1], n_tiles) for r in rows[:7]] + [_full_spec(norm_g.shape)],
        out_shape=[jax.ShapeDtypeStruct(r.shape, F32) for r in rows[:6]] + [jax.ShapeDtypeStruct(z.shape, BF16)]
        + [jax.ShapeDtypeStruct(norm_g.shape, F32)],
        scratch_shapes=[pltpu.VMEM((srows, GDN_HD), F32)], compiler_params=_cparams(),
    )(norm_g, states, *rows)


def loss_stage(name, y, target, tm):
    t, d = y.shape

    def body(y_ref, t_ref, l_ref, dy_ref):
        @pl.when(pl.program_id(0) == 0)
        def _():
            l_ref[...] = jnp.zeros_like(l_ref)

        e = y_ref[...] - t_ref[...]
        dy_ref[...] = e * (1.0 / d)
        sq = e * e
        part = sq[:, 0:LANES]
        for j in range(1, d // LANES):
            part = part + sq[:, j * LANES:(j + 1) * LANES]
        acc = part[0:8, :]
        for j in range(1, tm // 8):
            acc = acc + part[j * 8:(j + 1) * 8, :]
        l_ref[...] += acc * (0.5 / d)

    return pl.pallas_call(
        body, name=name, grid=(t // tm,),
        in_specs=[_row_spec(tm, d), _row_spec(tm, d)],
        out_specs=[_full_spec((8, LANES)), _row_spec(tm, d)],
        out_shape=[jax.ShapeDtypeStruct((8, LANES), F32), jax.ShapeDtypeStruct((t, d), F32)],
        compiler_params=_cparams(),
    )(y, target)


def _pick(n, cands):
    for c in cands:
        if n % c == 0:
            return c
    return n


def matmul(name, a, b, form, out_dtype, acc=None):
    if form == "nn":
        (m, k), n = a.shape, b.shape[1]
    elif form == "nt":
        (m, k), n = a.shape, b.shape[0]
    else:
        (k, m), n = a.shape, b.shape[1]
    tm = _pick(m, (512, 256, 128))
    tn = n if n <= 2048 else _pick(n, (1536, 1024, 512, 256, 128))
    tk = k if k <= 2048 else _pick(k, (1024, 512))
    if form == "tn":
        tm = _pick(m, (1024, 512, 256, 128))
        tn = n if n <= 2048 else _pick(n, (1024, 512, 256, 128))
        tk = _pick(k, (512, 256, 128))
    nk = k // tk
    ca, cb = {"nn": (1, 0), "nt": (1, 1), "tn": (0, 0)}[form]
    a_spec = (pl.BlockSpec((tm, tk), lambda j, i, kk: (i, kk)) if form != "tn"
              else pl.BlockSpec((tk, tm), lambda j, i, kk: (kk, i)))
    b_spec = (pl.BlockSpec((tn, tk), lambda j, i, kk: (j, kk)) if form == "nt"
              else pl.BlockSpec((tk, tn), lambda j, i, kk: (kk, j)))
    o_spec = pl.BlockSpec((tm, tn), lambda j, i, kk: (i, j))
    has_acc = acc is not None

    assert nk == 1 or out_dtype == F32

    def body(*refs):
        a_ref, b_ref, o_ref = refs[0], refs[1], refs[-1]
        prod = lax.dot_general(a_ref[...], b_ref[...], (((ca,), (cb,)), ((), ())), preferred_element_type=F32)
        if nk == 1:
            o_ref[...] = (prod + refs[2][...] if has_acc else prod).astype(out_dtype)
            return
        kk = pl.program_id(2)

        @pl.when(kk == 0)
        def _():
            o_ref[...] = prod + refs[2][...] if has_acc else prod

        @pl.when(kk > 0)
        def _():
            o_ref[...] += prod

    ins = [a.astype(BF16), b.astype(BF16)] + ([acc] if has_acc else [])
    return pl.pallas_call(
        body, name=name, grid=(n // tn, m // tm, nk),
        in_specs=[a_spec, b_spec] + ([o_spec] if has_acc else []),
        out_specs=o_spec, out_shape=jax.ShapeDtypeStruct((m, n), out_dtype),
        input_output_aliases={2: 0} if has_acc else {},
        compiler_params=_cparams(3),
    )(*ins)


MESH_ID = pl.DeviceIdType.MESH
ANY_SPEC = pl.BlockSpec(memory_space=pl.ANY)


def _place():
    return lax.axis_index("x"), lax.axis_index("y"), lax.axis_index("c")


def chip_exchange(name, arrs, scatter):
    n = len(arrs)

    def body(*refs):
        ins, outs = refs[:n], refs[n:2 * n]
        send_sems, recv_sems, local_sems = refs[2 * n:]
        x, y, c = _place()
        me = 2 * x + y
        others = [(1 - x, y), (x, 1 - y), (1 - x, 1 - y)]

        def remote(i, j, landing_from_me):
            ox, oy = others[j]
            peer = 2 * ox + oy
            src = ins[i].at[peer] if scatter else ins[i]
            dst = outs[i].at[me if landing_from_me else peer]
            return pltpu.make_async_remote_copy(src_ref=src, dst_ref=dst, send_sem=send_sems.at[3 * i + j],
                                                recv_sem=recv_sems.at[3 * i + j], device_id=(ox, oy, c),
                                                device_id_type=MESH_ID)

        local = [pltpu.make_async_copy(ins[i].at[me] if scatter else ins[i], outs[i].at[me], local_sems.at[i])
                 for i in range(n)]
        for cp in local:
            cp.start()
        sent = [remote(i, j, True) for i in range(n) for j in range(3)]
        for cp in sent:
            cp.start()
        for i in range(n):
            for j in range(3):
                remote(i, j, False).wait_recv()
        for cp in sent:
            cp.wait_send()
        for cp in local:
            cp.wait()

    def out_of(a):
        return jax.ShapeDtypeStruct(a.shape if scatter else (N_CHIPS,) + a.shape, a.dtype)

    return pl.pallas_call(
        body, name=name, in_specs=[ANY_SPEC] * n, out_specs=[ANY_SPEC] * n, out_shape=[out_of(a) for a in arrs],
        scratch_shapes=[pltpu.SemaphoreType.DMA((3 * n,)), pltpu.SemaphoreType.DMA((3 * n,)), pltpu.SemaphoreType.DMA((n,))],
    )(*arrs)


def sibling_exchange(name, arrs):
    n = len(arrs)

    def body(*refs):
        ins, outs = refs[:n], refs[n:2 * n]
        send_sems, recv_sems = refs[2 * n:]
        x, y, c = _place()
        cps = [pltpu.make_async_remote_copy(src_ref=ins[i], dst_ref=outs[i], send_sem=send_sems.at[i], recv_sem=recv_sems.at[i],
                                            device_id=(x, y, 1 - c), device_id_type=MESH_ID) for i in range(n)]
        for cp in cps:
            cp.start()
        for cp in cps:
            cp.wait_recv()
        for cp in cps:
            cp.wait_send()

    return pl.pallas_call(
        body, name=name, in_specs=[ANY_SPEC] * n, out_specs=[ANY_SPEC] * n,
        out_shape=[jax.ShapeDtypeStruct(a.shape, a.dtype) for a in arrs],
        scratch_shapes=[pltpu.SemaphoreType.DMA((n,)), pltpu.SemaphoreType.DMA((n,))],
    )(*arrs)


def gather_all(name, vec):
    def body(in_ref, out_ref, send_sems, recv_sems, local_sem):
        x, y, c = _place()
        me = 4 * x + 2 * y + c

        def peer(mask):
            return (1 - x if mask & 4 else x, 1 - y if mask & 2 else y, 1 - c if mask & 1 else c)

        def remote(mask, landing_from_me):
            px, py, pc = peer(mask)
            slot = me if landing_from_me else 4 * px + 2 * py + pc
            return pltpu.make_async_remote_copy(src_ref=in_ref, dst_ref=out_ref.at[slot], send_sem=send_sems.at[mask - 1],
                                                recv_sem=recv_sems.at[mask - 1], device_id=(px, py, pc), device_id_type=MESH_ID)

        local = pltpu.make_async_copy(in_ref, out_ref.at[me], local_sem)
        local.start()
        sent = [remote(mask, True) for mask in range(1, N_DEV)]
        for cp in sent:
            cp.start()
        for mask in range(1, N_DEV):
            remote(mask, False).wait_recv()
        for cp in sent:
            cp.wait_send()
        local.wait()

    return pl.pallas_call(
        body, name=name, in_specs=[ANY_SPEC], out_specs=ANY_SPEC,
        out_shape=jax.ShapeDtypeStruct((N_DEV,) + vec.shape, vec.dtype),
        scratch_shapes=[pltpu.SemaphoreType.DMA((N_DEV - 1,)), pltpu.SemaphoreType.DMA((N_DEV - 1,)), pltpu.SemaphoreType.DMA],
    )(vec)


def sum_slots(name, arr, tr):
    k, r, c = arr.shape

    def body(a_ref, o_ref):
        acc = a_ref[0].astype(F32)
        for j in range(1, k):
            acc = acc + a_ref[j].astype(F32)
        o_ref[...] = acc

    return pl.pallas_call(
        body, name=name, grid=(r // tr,),
        in_specs=[pl.BlockSpec((k, tr, c), lambda i: (0, i, 0))], out_specs=pl.BlockSpec((tr, c), lambda i: (i, 0)),
        out_shape=jax.ShapeDtypeStruct((r, c), F32), compiler_params=_cparams(),
    )(arr)


def adamw(name, parts, w, m, v, tr):
    r, c = w.shape
    npart = len(parts)
    bc1 = 1.0 - ADAM_B1 ** ADAM_STEP
    bc2 = 1.0 - ADAM_B2 ** ADAM_STEP

    def body(*refs):
        g = refs[0][...]
        for j in range(1, npart):
            g = g + refs[j][...]
        w_ref, m_ref, v_ref, g_out, d_out, m_out, v_out = refs[npart:]
        m_new = ADAM_B1 * m_ref[...] + (1.0 - ADAM_B1) * g
        v_new = ADAM_B2 * v_ref[...] + (1.0 - ADAM_B2) * jnp.square(g)
        m_hat = m_new / bc1
        v_hat = v_new / bc2
        g_out[...] = g
        d_out[...] = -ADAM_LR * (m_hat / (jnp.sqrt(v_hat) + ADAM_EPS) + ADAM_WD * w_ref[...])
        m_out[...] = m_new
        v_out[...] = v_new

    spec = pl.BlockSpec((tr, c), lambda i: (i, 0))
    return pl.pallas_call(
        body, name=name, grid=(r // tr,), in_specs=[spec] * (npart + 3), out_specs=[spec] * 4,
        out_shape=[jax.ShapeDtypeStruct((r, c), F32)] * 4, compiler_params=_cparams(),
    )(*parts, w, m, v)


ROW_TM = 256


def _row(v):
    return v.reshape(1, -1)


def _lane_pad(vals, at):
    return jnp.concatenate([jnp.zeros((at,), F32), vals, jnp.zeros((LANES - at - vals.shape[0],), F32)]).reshape(1, LANES)


def _layer_params(sm, l):
    return dict(
        conf=[sm["conv_dw_w"][l], _row(sm["conv_dw_b"][l]), _row(sm["conv_ln_g"][l]), _row(sm["conv_ln_b"][l])],
        pre=[sm["gdn_conv_q"][l], sm["gdn_conv_k"][l], sm["gdn_conv_v"][l],
             _lane_pad(sm["gdn_a_log"][l], GDN_HEADS), _lane_pad(sm["gdn_dt_bias"][l], GDN_HEADS)],
        scan=_row(sm["gdn_norm_g"][l]),
        sgu=[_row(sm["sgu_ln_g"][l]), _row(sm["sgu_ln_b"][l]), sm["sgu_w_s"][l].reshape(SGU_GROUPS * SGU_CHUNK, SGU_CHUNK),
             sm["sgu_b_s"][l].T],
        mix=[_row(sm["b_gate"][l])],
        ln1=[_row(sm["ln1_g"][l]), _row(sm["ln1_b"][l])],
        ff=[_row(sm["b_ff1"][l])],
        ln2=[_row(sm["b_ff2"][l]), _row(sm["ln2_g"][l]), _row(sm["ln2_b"][l])],
    )


def _split_w_in(w):
    zpad = jnp.zeros((D_MODEL, LANES - 2 * GDN_HEADS), w.dtype)
    return dict(A=w[:, 0:1024], B=jnp.concatenate([w[:, 1024:2560], w[:, 3072:3080], zpad], axis=1), Z=w[:, 2560:3072],
                C=w[:, 3080:4104], G=w[:, 4104:7176])


def _join_w_in(d):
    return jnp.concatenate([d["A"], d["B"][:, 0:1536], d["Z"], d["B"][:, 1536:1544], d["C"], d["G"]], axis=1)


def layer_forward(l, xin, xin_b, big, lp):
    tag = f"l{l}_"
    wi = big["w_in"][l]
    sv = dict(xin=xin, xin_b=xin_b)
    for r in "GACBZ":
        sv["p" + r] = matmul(tag + "proj_" + r, xin_b, wi[r], "nn", F32)
    (sv["ya_in"],) = halo_fwd(tag + "conformer", f_conformer, ROW_TM, CONV_HALO, sv["pA"], lp["conf"], [[BF16]])
    sv["qn"], sv["kn"], sv["vv"], sv["gb"] = halo_fwd(tag + "gdn_pre", f_gdn_pre, ROW_TM, GDN_HALO, sv["pB"], lp["pre"], [[F32]] * 4)
    *prep, sv["tinv"] = prep_fwd(tag + "gdn_prep", sv["qn"], sv["kn"], sv["vv"], sv["gb"])
    sv["prep"] = prep
    sv["yb_in"], sv["states"] = scan_fwd(tag + "gdn_scan", lp["scan"], *prep, sv["pZ"])
    (sv["yc_in"],) = stage_fwd(tag + "sgu", f_sgu, ROW_TM, [sv["pC"]], lp["sgu"], [[BF16]])
    sv["ya"] = matmul(tag + "out_a", sv["ya_in"], big["w_pa"][l], "nn", F32)
    sv["yb"] = matmul(tag + "out_b", sv["yb_in"], big["w_pb"][l], "nn", F32)
    sv["yc"] = matmul(tag + "out_c", sv["yc_in"], big["w_pc"][l], "nn", F32)
    (sv["mix"],) = stage_fwd(tag + "mix", f_mix, ROW_TM, [sv["ya"], sv["yb"], sv["yc"], sv["pG"]], lp["mix"], [[BF16]])
    sv["m"] = matmul(tag + "w_o", sv["mix"], big["w_o"][l], "nn", F32)
    sv["x1"], sv["x1_b"] = stage_fwd(tag + "ln1", f_ln1, ROW_TM, [xin, sv["m"]], lp["ln1"], [[F32, BF16]])
    sv["hpre"] = matmul(tag + "ff1", sv["x1_b"], big["w_ff1"][l], "nn", F32)
    (sv["h"],) = stage_fwd(tag + "ffact", f_ffact, ROW_TM, [sv["hpre"]], lp["ff"], [[BF16]])
    sv["f2"] = matmul(tag + "ff2", sv["h"], big["w_ff2"][l], "nn", F32)
    x2, x2_b = stage_fwd(tag + "ln2", f_ln2, ROW_TM, [sv["x1"], sv["f2"]], lp["ln2"], [[F32, BF16]])
    return x2, x2_b, sv


def layer_backward(l, dx2, sv, big, lp):
    tag = f"l{l}_b_"
    g = {}
    dx1_a, df2, db2, dg2, dbe2 = stage_bwd(tag + "ln2", f_ln2, ROW_TM, [sv["x1"], sv["f2"]], lp["ln2"], [dx2], [F32, BF16])
    g["b_ff2"], g["ln2_g"], g["ln2_b"] = db2[0], dg2[0], dbe2[0]
    g["w_ff2"] = matmul(tag + "dw_ff2", sv["h"], df2, "tn", F32)
    dh = matmul(tag + "dx_ff2", df2, big["w_ff2"][l], "nt", F32)
    dhpre, db1 = stage_bwd(tag + "ffact", f_ffact, ROW_TM, [sv["hpre"]], lp["ff"], [[dh]], [BF16])
    g["b_ff1"] = db1[0]
    g["w_ff1"] = matmul(tag + "dw_ff1", sv["x1_b"], dhpre, "tn", F32)
    dx1_b = matmul(tag + "dx_ff1", dhpre, big["w_ff1"][l], "nt", F32)
    dxin_a, dm, dg1, dbe1 = stage_bwd(tag + "ln1", f_ln1, ROW_TM, [sv["xin"], sv["m"]], lp["ln1"], [[dx1_a, dx1_b]], [F32, BF16])
    g["ln1_g"], g["ln1_b"] = dg1[0], dbe1[0]
    g["w_o"] = matmul(tag + "dw_o", sv["mix"], dm, "tn", F32)
    dmix = matmul(tag + "dx_o", dm, big["w_o"][l], "nt", F32)
    dya, dyb, dyc, dp_g, dbg = stage_bwd(tag + "mix", f_mix, ROW_TM, [sv["ya"], sv["yb"], sv["yc"], sv["pG"]], lp["mix"],
                                         [[dmix]], [BF16] * 4)
    g["b_gate"] = dbg[0]
    g["w_pa"] = matmul(tag + "dw_pa", sv["ya_in"], dya, "tn", F32)
    g["w_pb"] = matmul(tag + "dw_pb", sv["yb_in"], dyb, "tn", F32)
    g["w_pc"] = matmul(tag + "dw_pc", sv["yc_in"], dyc, "tn", F32)
    dya_in = matmul(tag + "dx_pa", dya, big["w_pa"][l], "nt", F32)
    dyb_in = matmul(tag + "dx_pb", dyb, big["w_pb"][l], "nt", F32)
    dyc_in = matmul(tag + "dx_pc", dyc, big["w_pc"][l], "nt", F32)
    dp_a, dcw, dcb, dcg, dcbe = halo_bwd(tag + "conformer", f_conformer, ROW_TM, CONV_HALO, sv["pA"], lp["conf"], [dya_in], BF16)
    g["conv_dw_w"], g["conv_dw_b"], g["conv_ln_g"], g["conv_ln_b"] = dcw, dcb[0], dcg[0], dcbe[0]
    dp_c, dsg, dsb, dsw, dsbs = stage_bwd(tag + "sgu", f_sgu, ROW_TM, [sv["pC"]], lp["sgu"], [[dyc_in]], [BF16])
    g["sgu_ln_g"], g["sgu_ln_b"] = dsg[0], dsb[0]
    g["sgu_w_s"] = dsw.reshape(SGU_GROUPS, SGU_CHUNK, SGU_CHUNK)
    g["sgu_b_s"] = dsbs.T
    *dprep, dp_z, dng = scan_bwd(tag + "gdn_scan", lp["scan"], sv["states"], *sv["prep"], sv["pZ"], dyb_in)
    g["gdn_norm_g"] = dng[0]
    dqn, dkn, dvv, dgb = prep_bwd(tag + "gdn_prep", sv["qn"], sv["kn"], sv["vv"], sv["gb"], sv["tinv"], dprep)
    dp_b, dq, dk, dv, dal, ddt = halo_bwd(tag + "gdn_pre", f_gdn_pre, ROW_TM, GDN_HALO, sv["pB"], lp["pre"], [dqn, dkn, dvv, dgb], BF16)
    g["gdn_conv_q"], g["gdn_conv_k"], g["gdn_conv_v"] = dq, dk, dv
    g["gdn_a_log"] = dal[0, GDN_HEADS:2 * GDN_HEADS]
    g["gdn_dt_bias"] = ddt[0, GDN_HEADS:2 * GDN_HEADS]
    dps = dict(G=dp_g, A=dp_a, C=dp_c, B=dp_b, Z=dp_z)
    wi = big["w_in"][l]
    g["w_in"] = _join_w_in({r: matmul(tag + "dw_in_" + r, sv["xin_b"], dps[r], "tn", F32) for r in "GACBZ"})
    dxin_b = None
    for r in "GACBZ":
        dxin_b = matmul(tag + "dx_in_" + r, dps[r], wi[r], "nt", F32, acc=dxin_b)
    return [dxin_a, dxin_b], g


SMALL_PER_LAYER = ("b_gate", "conv_dw_w", "conv_dw_b", "conv_ln_g", "conv_ln_b", "gdn_conv_q", "gdn_conv_k", "gdn_conv_v",
                   "gdn_a_log", "gdn_dt_bias", "gdn_norm_g", "sgu_ln_g", "sgu_ln_b", "sgu_w_s", "sgu_b_s", "ln1_g", "ln1_b",
                   "b_ff1", "b_ff2", "ln2_g", "ln2_b")
BIG = ("w_in", "w_pa", "w_pb", "w_pc", "w_o", "w_ff1", "w_ff2")


def local_step(x, target, big, sm):
    lps = [_layer_params(sm, l) for l in range(DEPTH)]
    p_in = [_row(sm["ln_in_g"]), _row(sm["ln_in_b"])]
    xc, xc_b = stage_fwd("ln_in", f_ln_in, ROW_TM, [x], p_in, [[F32, BF16]])
    saved = []
    for l in range(DEPTH):
        xc, xc_b, sv = layer_forward(l, xc, xc_b, big, lps[l])
        saved.append(sv)
    loss_part, dy = loss_stage("loss", xc, target, ROW_TM)
    dx = [dy]
    per_layer = [None] * DEPTH
    for l in reversed(range(DEPTH)):
        dx, per_layer[l] = layer_backward(l, dx, saved[l], big, lps[l])
    grad_x, dgi, dbi = stage_bwd("ln_in_b", f_ln_in, ROW_TM, [x], p_in, [dx], [F32])
    grads = {n: jnp.stack([per_layer[l][n] for l in range(DEPTH)]) for n in SMALL_PER_LAYER + BIG}
    grads["ln_in_g"], grads["ln_in_b"] = dgi[0], dbi[0]
    return loss_part, grad_x, grads


PARAM_NAMES = ("ln_in_g", "ln_in_b", "w_in", "b_gate", "conv_dw_w", "conv_dw_b", "conv_ln_g", "conv_ln_b", "w_pa",
               "gdn_conv_q", "gdn_conv_k", "gdn_conv_v", "gdn_a_log", "gdn_dt_bias", "gdn_norm_g", "w_pb", "sgu_ln_g",
               "sgu_ln_b", "sgu_w_s", "sgu_b_s", "w_pc", "w_o", "ln1_g", "ln1_b", "w_ff1", "b_ff1", "w_ff2", "b_ff2",
               "ln2_g", "ln2_b")
SMALL = tuple(n for n in PARAM_NAMES if n not in BIG)
COL_SHARDED_SMALL = ("conv_dw_w", "gdn_conv_q", "gdn_conv_k", "gdn_conv_v")
ROW_SHARDED_BIG = ("w_o", "w_ff2")
ADAM_ROWS = 128


def _unshard(g, rows):
    n, l, r, c = g.shape
    if rows:
        return jnp.moveaxis(g, 0, 1).reshape(l, n * r, c)
    return jnp.moveaxis(g, 0, 2).reshape(l, r, n * c)


def _to_shards(w, rows):
    l, r, c = w.shape
    if rows:
        return jnp.moveaxis(w.reshape(l, N_CHIPS, r // N_CHIPS, c), 1, 0).reshape(N_CHIPS, l * (r // N_CHIPS), c)
    return jnp.moveaxis(w.reshape(l, r, N_CHIPS, c // N_CHIPS), 2, 0).reshape(N_CHIPS, l * r, c // N_CHIPS)


def _pack(vals, rows_multiple=8):
    flat = jnp.concatenate([v.reshape(-1) for v in vals])
    per = LANES * rows_multiple
    padded = -(-flat.shape[0] // per) * per
    return jnp.pad(flat, (0, padded - flat.shape[0])).reshape(-1, LANES)


def _unpack(packed, shapes):
    flat = packed.reshape(-1)
    out, at = [], 0
    for s in shapes:
        size = 1
        for d in s:
            size *= d
        out.append(flat[at:at + size].reshape(s))
        at += size
    return out


def kernel(x, ln_in_g, ln_in_b, w_in, b_gate, conv_dw_w, conv_dw_b, conv_ln_g, conv_ln_b, w_pa, gdn_conv_q, gdn_conv_k, gdn_conv_v, gdn_a_log, gdn_dt_bias, gdn_norm_g, w_pb, sgu_ln_g, sgu_ln_b, sgu_w_s, sgu_b_s, w_pc, w_o, ln1_g, ln1_b, w_ff1, b_ff1, w_ff2, b_ff2, ln2_g, ln2_b, loss_target, m_ln_in_g, m_ln_in_b, m_w_in, m_b_gate, m_conv_dw_w, m_conv_dw_b, m_conv_ln_g, m_conv_ln_b, m_w_pa, m_gdn_conv_q, m_gdn_conv_k, m_gdn_conv_v, m_gdn_a_log, m_gdn_dt_bias, m_gdn_norm_g, m_w_pb, m_sgu_ln_g, m_sgu_ln_b, m_sgu_w_s, m_sgu_b_s, m_w_pc, m_w_o, m_ln1_g, m_ln1_b, m_w_ff1, m_b_ff1, m_w_ff2, m_b_ff2, m_ln2_g, m_ln2_b, v_ln_in_g, v_ln_in_b, v_w_in, v_b_gate, v_conv_dw_w, v_conv_dw_b, v_conv_ln_g, v_conv_ln_b, v_w_pa, v_gdn_conv_q, v_gdn_conv_k, v_gdn_conv_v, v_gdn_a_log, v_gdn_dt_bias, v_gdn_norm_g, v_w_pb, v_sgu_ln_g, v_sgu_ln_b, v_sgu_w_s, v_sgu_b_s, v_w_pc, v_w_o, v_ln1_g, v_ln1_b, v_w_ff1, v_b_ff1, v_w_ff2, v_b_ff2, v_ln2_g, v_ln2_b):
    w = dict(zip(PARAM_NAMES, (ln_in_g, ln_in_b, w_in, b_gate, conv_dw_w, conv_dw_b, conv_ln_g, conv_ln_b, w_pa, gdn_conv_q, gdn_conv_k, gdn_conv_v, gdn_a_log, gdn_dt_bias, gdn_norm_g, w_pb, sgu_ln_g, sgu_ln_b, sgu_w_s, sgu_b_s, w_pc, w_o, ln1_g, ln1_b, w_ff1, b_ff1, w_ff2, b_ff2, ln2_g, ln2_b)))
    m = dict(zip(PARAM_NAMES, (m_ln_in_g, m_ln_in_b, m_w_in, m_b_gate, m_conv_dw_w, m_conv_dw_b, m_conv_ln_g, m_conv_ln_b, m_w_pa, m_gdn_conv_q, m_gdn_conv_k, m_gdn_conv_v, m_gdn_a_log, m_gdn_dt_bias, m_gdn_norm_g, m_w_pb, m_sgu_ln_g, m_sgu_ln_b, m_sgu_w_s, m_sgu_b_s, m_w_pc, m_w_o, m_ln1_g, m_ln1_b, m_w_ff1, m_b_ff1, m_w_ff2, m_b_ff2, m_ln2_g, m_ln2_b)))
    v = dict(zip(PARAM_NAMES, (v_ln_in_g, v_ln_in_b, v_w_in, v_b_gate, v_conv_dw_w, v_conv_dw_b, v_conv_ln_g, v_conv_ln_b, v_w_pa, v_gdn_conv_q, v_gdn_conv_k, v_gdn_conv_v, v_gdn_a_log, v_gdn_dt_bias, v_gdn_norm_g, v_w_pb, v_sgu_ln_g, v_sgu_ln_b, v_sgu_w_s, v_sgu_b_s, v_w_pc, v_w_o, v_ln1_g, v_ln1_b, v_w_ff1, v_b_ff1, v_w_ff2, v_b_ff2, v_ln2_g, v_ln2_b)))
    chip = 2 * lax.axis_index("x") + lax.axis_index("y")

    col_small_shapes = [w[n].shape for n in COL_SHARDED_SMALL]
    gathered = chip_exchange("gather_weights", [w[n].astype(BF16) for n in BIG] + [_pack([w[n] for n in COL_SHARDED_SMALL])],
                             scatter=False)
    big = {}
    for n, g in zip(BIG, gathered):
        whole = _unshard(g, n in ROW_SHARDED_BIG)
        big[n] = [_split_w_in(whole[l]) for l in range(DEPTH)] if n == "w_in" else whole
    sm = {n: w[n] for n in SMALL}
    per_chip = [_unpack(gathered[-1][s], col_small_shapes) for s in range(N_CHIPS)]
    for i, n in enumerate(COL_SHARDED_SMALL):
        sm[n] = jnp.concatenate([per_chip[s][i] for s in range(N_CHIPS)], axis=-1)

    loss_part, grad_x, grads = local_step(x[0], loss_target[0], big, sm)

    sends = [_to_shards(grads[n], n in ROW_SHARDED_BIG).astype(BF16) for n in BIG]
    received = chip_exchange("scatter_grads", sends, scatter=True)
    partial = [sum_slots("sum_chips_" + n, r, ADAM_ROWS) for n, r in zip(BIG, received)]
    sibling = sibling_exchange("swap_cores", partial)
    out = {}
    for n, mine, theirs in zip(BIG, partial, sibling):
        shape = w[n].shape
        res = adamw("adamw_" + n, [mine, theirs], w[n].reshape(-1, shape[-1]), m[n].reshape(-1, shape[-1]),
                    v[n].reshape(-1, shape[-1]), ADAM_ROWS)
        out[n] = [r.reshape(shape) for r in res]

    small_shapes = [grads[n].shape for n in SMALL]
    vec = _pack([grads[n] for n in SMALL] + [jnp.sum(loss_part).reshape(1)])
    total = sum_slots("sum_small", gather_all("gather_small", vec), 8)
    whole = _unpack(total, small_shapes + [(1,)])
    loss = whole[-1][0]
    g_small = {}
    for n, g in zip(SMALL, whole[:-1]):
        if n in COL_SHARDED_SMALL:
            width = g.shape[-1] // N_CHIPS
            g = lax.dynamic_slice_in_dim(g, chip * width, width, axis=g.ndim - 1)
        g_small[n] = g
    local_shapes = [w[n].shape for n in SMALL]
    res = adamw("adamw_small", [_pack([g_small[n] for n in SMALL])], _pack([w[n] for n in SMALL]),
                _pack([m[n] for n in SMALL]), _pack([v[n] for n in SMALL]), 8)
    unpacked = [_unpack(r, local_shapes) for r in res]
    for i, n in enumerate(SMALL):
        out[n] = [unpacked[k][i] for k in range(4)]

    return (loss, grad_x[None], *[out[n][0] for n in PARAM_NAMES], *[out[n][1] for n in PARAM_NAMES],
            *[out[n][2] for n in PARAM_NAMES], *[out[n][3] for n in PARAM_NAMES])
```

```python
import functools

import jax
import jax.numpy as jnp
from jax import lax
from jax.experimental import pallas as pl
from jax.experimental.pallas import tpu as pltpu

F32 = jnp.float32
BF16 = jnp.bfloat16

D_MODEL = 1024
DEPTH = 2
CONV_DIM = 512
CONV_WIDTH = 31
CONV_HALO = 32
GDN_HEADS = 4
GDN_HD = 128
GDN_CHUNK = 64
GDN_PAIR = 2 * GDN_CHUNK
GDN_CONV = 4
GDN_HALO = 8
SGU_GROUPS = 4
SGU_GD = 128
SGU_CHUNK = 128
D_FF = 4096
ALPHA = (2 * DEPTH) ** 0.25
LN_EPS = 1e-5
RMS_EPS = 1e-6
ADAM_LR, ADAM_B1, ADAM_B2, ADAM_EPS, ADAM_WD, ADAM_STEP = 0.001, 0.9, 0.999, 1e-08, 0.01, 10
N_CHIPS = 4
N_DEV = 8
LANES = 128
VMEM_LIMIT = 52 * 1024 * 1024

W_GATE, W_A, W_C, W_B, W_Z = 3 * D_MODEL, 2 * CONV_DIM, 2 * 512, 3 * 512 + LANES, 512


def _dg(a, b, ca, cb):
    return lax.dot_general(a.astype(BF16), b.astype(BF16), (((ca,), (cb,)), ((), ())), preferred_element_type=F32)


@jax.custom_vjp
def mm_nn(a, b):
    return _dg(a, b, 1, 0)


mm_nn.defvjp(lambda a, b: (_dg(a, b, 1, 0), (a, b)), lambda r, g: (_dg(g, r[1], 1, 1), _dg(r[0], g, 0, 0)))


@jax.custom_vjp
def mm_nt(a, b):
    return _dg(a, b, 1, 1)


mm_nt.defvjp(lambda a, b: (_dg(a, b, 1, 1), (a, b)), lambda r, g: (_dg(g, r[1], 1, 0), _dg(g, r[0], 0, 0)))


@jax.custom_vjp
def mm_tn(a, b):
    return _dg(a, b, 0, 0)


mm_tn.defvjp(lambda a, b: (_dg(a, b, 0, 0), (a, b)), lambda r, g: (_dg(r[1], g, 1, 1), _dg(r[0], g, 1, 0)))


def _ln(x, g, b):
    mu = jnp.mean(x, -1, keepdims=True)
    xc = x - mu
    var = jnp.mean(xc * xc, -1, keepdims=True)
    return xc * lax.rsqrt(var + LN_EPS) * g + b


def _iota2(shape, dim):
    return lax.broadcasted_iota(jnp.int32, shape, dim)


def f_ln_in(params, x):
    g, b = params
    return (_ln(x, g, b),)


def _roll_rows(x, s):
    return x if s == 0 else pltpu.roll(x, s, 0)


@functools.partial(jax.custom_vjp, nondiff_argnums=(1,))
def shift_rows(x, s):
    return _roll_rows(x, s)


shift_rows.defvjp(lambda x, s: (_roll_rows(x, s), None),
                  lambda s, _, ct: (_roll_rows(ct, (ct.shape[0] - s) % ct.shape[0]),))


def _causal_conv(full, w, halo):
    width = w.shape[0]
    groups = []
    for g in range(full.shape[1] // LANES):
        ls = slice(g * LANES, (g + 1) * LANES)
        xg, wg = full[:, ls], w[:, ls]
        acc = wg[width - 1:width, :] * xg[halo:, :]
        for j in range(width - 1):
            acc = acc + wg[j:j + 1, :] * shift_rows(xg, width - 1 - j)[halo:, :]
        groups.append(acc)
    return jnp.concatenate(groups, axis=1)


def f_conformer(params, halo, pa):
    w, b, g, be = params
    full = jnp.concatenate([halo, pa], axis=0)
    h = full[:, :CONV_DIM] * jax.nn.sigmoid(full[:, CONV_DIM:])
    return (jax.nn.silu(_ln(_causal_conv(h, w, CONV_HALO) + b, g, be)),)


def f_gdn_pre(params, halo, pb):
    cq, ck, cv, alog, dtb = params
    full = jnp.concatenate([halo, pb], axis=0)

    def conv(cols, w):
        return jax.nn.silu(_causal_conv(cols, w, GDN_HALO))

    def l2n(t, scale):
        parts = []
        for h in range(GDN_HEADS):
            th = t[:, h * GDN_HD:(h + 1) * GDN_HD]
            parts.append(th * (lax.rsqrt(jnp.sum(th * th, -1, keepdims=True) + RMS_EPS) * scale))
        return jnp.concatenate(parts, axis=1)

    q = l2n(conv(full[:, 0:512], cq), GDN_HD ** -0.5)
    k = l2n(conv(full[:, 512:1024], ck), 1.0)
    v = conv(full[:, 1024:1536], cv)
    logits = pb[:, 1536:1536 + LANES]
    z = logits + dtb
    softplus = jnp.maximum(z, 0.0) + jnp.log(1.0 + jnp.exp(-jnp.abs(z)))
    g = -jnp.exp(alog) * softplus
    beta = jax.nn.sigmoid(logits)
    lane = _iota2(logits.shape, 1)
    gb = jnp.where(lane < GDN_HEADS, beta, jnp.where(lane < 2 * GDN_HEADS, g, 0.0))
    return q, k, v, gb


def _split2(x):
    hi = x.astype(BF16)
    return hi, (x - hi.astype(F32)).astype(BF16)


def _dg3(a, b, ca, cb):
    ah, al = _split2(a)
    bh, bl = _split2(b)
    d = lambda p, q: lax.dot_general(p, q, (((ca,), (cb,)), ((), ())), preferred_element_type=F32)
    return d(ah, bh) + (d(ah, bl) + d(al, bh))


def _inv_unit_lower(low):
    n = low.shape[0]
    eye = jnp.where(_iota2((n, n), 0) == _iota2((n, n), 1), 1.0, 0.0).astype(F32)
    p = -low
    x = eye + p
    for _ in range(5):
        p = _dg3(p, p, 1, 0)
        x = x + _dg3(x, p, 1, 0)
    return x


def _pair_masks():
    n = GDN_PAIR
    r, c = _iota2((n, n), 0), _iota2((n, n), 1)
    same = (r >= GDN_CHUNK) == (c >= GDN_CHUNK)
    return same, jnp.logical_and(same, r >= c), jnp.logical_and(same, r > c), jnp.logical_and(same, r <= c)


def _masked_sum(mask, g):
    mk = jnp.where(mask, 1.0, 0.0).astype(BF16)
    g1 = g.astype(BF16)
    r1 = g - g1.astype(F32)
    g2 = r1.astype(BF16)
    g3 = (r1 - g2.astype(F32)).astype(BF16)
    d = lambda q: jnp.dot(mk, q, preferred_element_type=F32)
    return d(g1) + (d(g2) + d(g3))


@jax.custom_vjp
def chunk_cumsum(g):
    return _masked_sum(_pair_masks()[1], g)


chunk_cumsum.defvjp(lambda g: (_masked_sum(_pair_masks()[1], g), None), lambda _, ct: (_masked_sum(_pair_masks()[3], ct),))


def _prep_gates(gb):
    gam_all = chunk_cumsum(gb)
    first = _iota2((GDN_PAIR, LANES), 0) < GDN_CHUNK
    glast = jnp.where(first, gam_all[GDN_CHUNK - 1:GDN_CHUNK, :], gam_all[GDN_PAIR - 1:GDN_PAIR, :])
    return gam_all, gam_all.T, glast


def _head_gates(gb, gam_all, gam_t, h, causal):
    gc = gam_all[:, GDN_HEADS + h:GDN_HEADS + h + 1]
    gr = gam_t[GDN_HEADS + h:GDN_HEADS + h + 1, :]
    return gc, gb[:, h:h + 1], jnp.exp(jnp.where(causal, gc - gr, -jnp.inf))


def f_prep_low(kn, gb):
    _, causal, strict, _ = _pair_masks()
    gam_all, gam_t, _ = _prep_gates(gb)
    lows = []
    for h in range(GDN_HEADS):
        k = kn[:, h * GDN_HD:(h + 1) * GDN_HD]
        _, beta, decay = _head_gates(gb, gam_all, gam_t, h, causal)
        lows.append(jnp.where(strict, beta * mm_nt(k, k) * decay, 0.0))
    return jnp.concatenate(lows, axis=1)


def f_prep_rest(qn, kn, vv, gb, tinv_all):
    _, causal, _, _ = _pair_masks()
    gam_all, gam_t, glast = _prep_gates(gb)
    us, ws, as_, qds, kds = [], [], [], [], []
    for h in range(GDN_HEADS):
        hs = slice(h * GDN_HD, (h + 1) * GDN_HD)
        q, k, v, tinv = qn[:, hs], kn[:, hs], vv[:, hs], tinv_all[:, hs]
        gc, beta, decay = _head_gates(gb, gam_all, gam_t, h, causal)
        eg = jnp.exp(gc)
        us.append(mm_nn(tinv, beta * v))
        ws.append(mm_nn(tinv, beta * k * eg))
        as_.append(mm_nt(q, k) * decay)
        qds.append(q * eg)
        kds.append(k * jnp.exp(glast[:, GDN_HEADS + h:GDN_HEADS + h + 1] - gc))
    cat = lambda xs: jnp.concatenate(xs, axis=1)
    return cat(us), cat(ws), cat(as_), cat(qds), cat(kds), jnp.exp(glast)


def prep_fwd(name, qn, kn, vv, gb):
    t = qn.shape[0]
    hw = GDN_HEADS * GDN_HD

    def body(q_ref, k_ref, v_ref, gb_ref, *outs):
        low = f_prep_low(k_ref[...], gb_ref[...])
        tinv = jnp.concatenate([_inv_unit_lower(low[:, h * GDN_HD:(h + 1) * GDN_HD]) for h in range(GDN_HEADS)], axis=1)
        res = f_prep_rest(q_ref[...], k_ref[...], v_ref[...], gb_ref[...], tinv)
        for o_ref, val in zip(outs, res + (tinv,)):
            o_ref[...] = val

    widths = [hw, hw, hw, hw, hw, LANES, hw]
    return pl.pallas_call(
        body, name=name, grid=(t // GDN_PAIR,),
        in_specs=[_row_spec(GDN_PAIR, hw)] * 3 + [_row_spec(GDN_PAIR, LANES)],
        out_specs=[_row_spec(GDN_PAIR, wd) for wd in widths],
        out_shape=[jax.ShapeDtypeStruct((t, wd), F32) for wd in widths], compiler_params=_cparams(),
    )(qn, kn, vv, gb)


def prep_bwd(name, qn, kn, vv, gb, tinv, cts):
    t = qn.shape[0]
    hw = GDN_HEADS * GDN_HD

    def body(q_ref, k_ref, v_ref, gb_ref, x_ref, *refs):
        ct = tuple(r[...] for r in refs[:6])
        outs = refs[6:]
        kn_v, gb_v, x_all = k_ref[...], gb_ref[...], x_ref[...]
        _, vjp_low = jax.vjp(f_prep_low, kn_v, gb_v)
        _, vjp_rest = jax.vjp(f_prep_rest, q_ref[...], kn_v, v_ref[...], gb_v, x_all)
        dq, dk, dv, dgb, dx_all = vjp_rest(ct)
        dlows = []
        for h in range(GDN_HEADS):
            hs = slice(h * GDN_HD, (h + 1) * GDN_HD)
            dlows.append(-_dg3(_dg3(x_all[:, hs], dx_all[:, hs], 0, 0), x_all[:, hs], 1, 1))
        dk2, dgb2 = vjp_low(jnp.concatenate(dlows, axis=1))
        outs[0][...] = dq
        outs[1][...] = dk + dk2
        outs[2][...] = dv
        outs[3][...] = dgb + dgb2

    in_w = [hw, hw, hw, LANES, hw] + [hw, hw, hw, hw, hw, LANES]
    out_w = [hw, hw, hw, LANES]
    return pl.pallas_call(
        body, name=name, grid=(t // GDN_PAIR,),
        in_specs=[_row_spec(GDN_PAIR, wd) for wd in in_w], out_specs=[_row_spec(GDN_PAIR, wd) for wd in out_w],
        out_shape=[jax.ShapeDtypeStruct((t, wd), F32) for wd in out_w], compiler_params=_cparams(),
    )(qn, kn, vv, gb, tinv, *cts)


def f_gdn_scan(params, state, u, w, a, qd, kd, egl, z):
    ng = params[0]
    tm = u.shape[0]
    st = [state[h * GDN_HD:(h + 1) * GDN_HD, :] for h in range(GDN_HEADS)]
    zeros = jnp.zeros((GDN_CHUNK, GDN_HD), F32)
    rows = []
    for ci in range(tm // GDN_CHUNK):
        rs = slice(ci * GDN_CHUNK, (ci + 1) * GDN_CHUNK)
        heads = []
        for h in range(GDN_HEADS):
            hs = slice(h * GDN_HD, (h + 1) * GDN_HD)
            vn = u[rs, hs] - mm_nn(w[rs, hs], st[h])
            vpad = jnp.concatenate([vn, zeros] if ci % 2 == 0 else [zeros, vn], axis=0)
            o = mm_nn(qd[rs, hs], st[h]) + mm_nn(a[rs, hs], vpad)
            e = egl[ci * GDN_CHUNK:ci * GDN_CHUNK + 1, GDN_HEADS + h:GDN_HEADS + h + 1]
            st[h] = st[h] * e + mm_tn(kd[rs, hs], vn)
            on = o * lax.rsqrt(jnp.mean(o * o, -1, keepdims=True) + RMS_EPS) * ng
            heads.append(on * jax.nn.silu(z[rs, hs]))
        rows.append(jnp.concatenate(heads, axis=1))
    return jnp.concatenate(rows, axis=0), jnp.concatenate(st, axis=0)


def f_sgu(params, puv):
    g, b, ws, bst = params
    tm = puv.shape[0]
    uv = jax.nn.gelu(puv)
    u = uv[:, :512]
    v = _ln(uv[:, 512:], g, b)
    r, c = _iota2((SGU_CHUNK, SGU_CHUNK), 0), _iota2((SGU_CHUNK, SGU_CHUNK), 1)
    rows = []
    for ci in range(tm // SGU_CHUNK):
        rs = slice(ci * SGU_CHUNK, (ci + 1) * SGU_CHUNK)
        groups = []
        for gi in range(SGU_GROUPS):
            gs = slice(gi * SGU_GD, (gi + 1) * SGU_GD)
            wt = jnp.where(r >= c, ws[gs, :], 0.0)
            groups.append(mm_nn(wt, v[rs, gs]) + bst[:, gi:gi + 1])
        rows.append(jnp.concatenate(groups, axis=1))
    return (u * jnp.concatenate(rows, axis=0),)


def f_mix(params, ya, yb, yc, pg):
    bg = params[0]
    s = jax.nn.sigmoid(pg + bg)
    return (s[:, :D_MODEL] * ya + s[:, D_MODEL:2 * D_MODEL] * yb + s[:, 2 * D_MODEL:] * yc,)


def f_ln1(params, xin, m):
    g, b = params
    return (_ln(ALPHA * xin + m, g, b),)


def f_ln2(params, x1, f2):
    b2, g, b = params
    return (_ln(ALPHA * x1 + (f2 + b2), g, b),)


def _cparams(n_axes=1):
    return pltpu.CompilerParams(dimension_semantics=("arbitrary",) * n_axes, vmem_limit_bytes=VMEM_LIMIT)


def _row_spec(tm, width, n_tiles=None):
    if n_tiles is None:
        return pl.BlockSpec((tm, width), lambda i: (i, 0))
    return pl.BlockSpec((tm, width), lambda i: (n_tiles - 1 - i, 0))


def _full_spec(shape):
    return pl.BlockSpec(shape, lambda i: (0,) * len(shape))


def stage_fwd(name, fn, tm, rows, params, outs):
    t = rows[0].shape[0]
    nr, npar = len(rows), len(params)
    widths = jax.eval_shape(lambda p, r: fn(p, *r), [jax.ShapeDtypeStruct(p.shape, F32) for p in params],
                            [jax.ShapeDtypeStruct((tm, r.shape[1]), F32) for r in rows])

    def body(*refs):
        r = [x[...].astype(F32) for x in refs[:nr]]
        p = [x[...] for x in refs[nr:nr + npar]]
        res = fn(p, *r)
        k = nr + npar
        for o, dts in zip(res, outs):
            for dt in dts:
                refs[k][...] = o.astype(dt)
                k += 1

    out_shape, out_specs = [], []
    for wd, dts in zip(widths, outs):
        for dt in dts:
            out_shape.append(jax.ShapeDtypeStruct((t, wd.shape[1]), dt))
            out_specs.append(_row_spec(tm, wd.shape[1]))
    return pl.pallas_call(
        body, name=name, grid=(t // tm,),
        in_specs=[_row_spec(tm, r.shape[1]) for r in rows] + [_full_spec(p.shape) for p in params],
        out_specs=out_specs, out_shape=out_shape, compiler_params=_cparams(),
    )(*rows, *params)


def stage_bwd(name, fn, tm, rows, params, cts, drow_dtypes):
    t = rows[0].shape[0]
    nr, npar = len(rows), len(params)
    flat_cts = [c for cl in cts if cl is not None for c in cl]
    nct = len(flat_cts)
    want = [i for i, dt in enumerate(drow_dtypes) if dt is not None]

    def body(*refs):
        r = [x[...].astype(F32) for x in refs[:nr]]
        p = [x[...] for x in refs[nr:nr + npar]]
        res, vjp = jax.vjp(lambda pp, rr: fn(pp, *rr), p, r)
        k = nr + npar
        ct = []
        for o, cl in zip(res, cts):
            if cl is None:
                ct.append(jnp.zeros_like(o))
            else:
                acc = refs[k][...].astype(F32)
                for j in range(1, len(cl)):
                    acc = acc + refs[k + j][...].astype(F32)
                k += len(cl)
                ct.append(acc)
        dp, dr = vjp(tuple(ct))
        for i in want:
            refs[k][...] = dr[i].astype(drow_dtypes[i])
            k += 1

        @pl.when(pl.program_id(0) == 0)
        def _():
            for j in range(npar):
                refs[k + j][...] = jnp.zeros_like(refs[k + j])

        for j in range(npar):
            refs[k + j][...] += dp[j]

    out_shape = [jax.ShapeDtypeStruct(rows[i].shape, drow_dtypes[i]) for i in want]
    out_specs = [_row_spec(tm, rows[i].shape[1]) for i in want]
    out_shape += [jax.ShapeDtypeStruct(p.shape, F32) for p in params]
    out_specs += [_full_spec(p.shape) for p in params]
    return pl.pallas_call(
        body, name=name, grid=(t // tm,),
        in_specs=[_row_spec(tm, r.shape[1]) for r in rows] + [_full_spec(p.shape) for p in params]
        + [_row_spec(tm, c.shape[1]) for c in flat_cts],
        out_specs=out_specs, out_shape=out_shape, compiler_params=_cparams(),
    )(*rows, *params, *flat_cts)


def _halo_spec(tm, halo, width, n_tiles=None):
    per = tm // halo
    if n_tiles is None:
        return pl.BlockSpec((halo, width), lambda i: (jnp.maximum(i * per - 1, 0), 0))
    return pl.BlockSpec((halo, width), lambda i: (jnp.maximum((n_tiles - 1 - i) * per - 1, 0), 0))


def halo_fwd(name, fn, tm, halo, src, params, outs):
    t, width = src.shape
    npar = len(params)
    widths = jax.eval_shape(lambda p, h, r: fn(p, h, r), [jax.ShapeDtypeStruct(p.shape, F32) for p in params],
                            jax.ShapeDtypeStruct((halo, width), F32), jax.ShapeDtypeStruct((tm, width), F32))

    def body(h_ref, x_ref, *refs):
        hal = jnp.where(pl.program_id(0) == 0, 0.0, h_ref[...])
        res = fn([x[...] for x in refs[:npar]], hal, x_ref[...])
        k = npar
        for o, dts in zip(res, outs):
            for dt in dts:
                refs[k][...] = o.astype(dt)
                k += 1

    out_shape, out_specs = [], []
    for wd, dts in zip(widths, outs):
        for dt in dts:
            out_shape.append(jax.ShapeDtypeStruct((t, wd.shape[1]), dt))
            out_specs.append(_row_spec(tm, wd.shape[1]))
    return pl.pallas_call(
        body, name=name, grid=(t // tm,),
        in_specs=[_halo_spec(tm, halo, width), _row_spec(tm, width)] + [_full_spec(p.shape) for p in params],
        out_specs=out_specs, out_shape=out_shape, compiler_params=_cparams(),
    )(src, src, *params)


def halo_bwd(name, fn, tm, halo, src, params, cts, dsrc_dtype):
    t, width = src.shape
    n_tiles = t // tm
    npar = len(params)

    def body(h_ref, x_ref, *refs):
        carry = refs[-1]
        first_tile = pl.program_id(0) == n_tiles - 1
        hal = jnp.where(first_tile, 0.0, h_ref[...])
        p = [x[...] for x in refs[:npar]]
        res, vjp = jax.vjp(fn, p, hal, x_ref[...])
        k = npar
        ct = tuple(refs[k + j][...].astype(F32) for j in range(len(cts)))
        k += len(cts)
        dp, dh, dx = vjp(ct)

        @pl.when(pl.program_id(0) == 0)
        def _():
            carry[...] = jnp.zeros_like(carry)
            for j in range(npar):
                refs[k + 1 + j][...] = jnp.zeros_like(refs[k + 1 + j])

        dx = jnp.concatenate([dx[:tm - halo, :], dx[tm - halo:, :] + carry[...]], axis=0)
        refs[k][...] = dx.astype(dsrc_dtype)
        carry[...] = dh
        for j in range(npar):
            refs[k + 1 + j][...] += dp[j]

    out_shape = [jax.ShapeDtypeStruct((t, width), dsrc_dtype)] + [jax.ShapeDtypeStruct(p.shape, F32) for p in params]
    out_specs = [_row_spec(tm, width, n_tiles)] + [_full_spec(p.shape) for p in params]
    return pl.pallas_call(
        body, name=name, grid=(n_tiles,),
        in_specs=[_halo_spec(tm, halo, width, n_tiles), _row_spec(tm, width, n_tiles)] + [_full_spec(p.shape) for p in params]
        + [_row_spec(tm, c.shape[1], n_tiles) for c in cts],
        out_specs=out_specs, out_shape=out_shape,
        scratch_shapes=[pltpu.VMEM((halo, width), F32)], compiler_params=_cparams(),
    )(src, src, *params, *cts)


SCAN_TM = 256


def scan_fwd(name, norm_g, u, w, a, qd, kd, egl, z):
    t = u.shape[0]
    n_tiles = t // SCAN_TM
    srows = GDN_HEADS * GDN_HD

    def body(ng_ref, u_ref, w_ref, a_ref, qd_ref, kd_ref, e_ref, z_ref, o_ref, s_ref, state):
        @pl.when(pl.program_id(0) == 0)
        def _():
            state[...] = jnp.zeros_like(state)

        s_ref[0] = state[...]
        o, s_new = f_gdn_scan([ng_ref[...]], state[...], u_ref[...], w_ref[...], a_ref[...], qd_ref[...], kd_ref[...],
                              e_ref[...], z_ref[...])
        o_ref[...] = o.astype(BF16)
        state[...] = s_new

    rows = [u, w, a, qd, kd, egl, z]
    return pl.pallas_call(
        body, name=name, grid=(n_tiles,),
        in_specs=[_full_spec(norm_g.shape)] + [_row_spec(SCAN_TM, r.shape[1]) for r in rows],
        out_specs=[_row_spec(SCAN_TM, 512), pl.BlockSpec((1, srows, GDN_HD), lambda i: (i, 0, 0))],
        out_shape=[jax.ShapeDtypeStruct((t, 512), BF16), jax.ShapeDtypeStruct((n_tiles, srows, GDN_HD), F32)],
        scratch_shapes=[pltpu.VMEM((srows, GDN_HD), F32)], compiler_params=_cparams(),
    )(norm_g, *rows)


def scan_bwd(name, norm_g, states, u, w, a, qd, kd, egl, z, dout):
    t = u.shape[0]
    n_tiles = t // SCAN_TM
    srows = GDN_HEADS * GDN_HD

    def body(ng_ref, s_ref, u_ref, w_ref, a_ref, qd_ref, kd_ref, e_ref, z_ref, do_ref,
             du_ref, dw_ref, da_ref, dqd_ref, dkd_ref, de_ref, dz_ref, dng_ref, dstate):
        @pl.when(pl.program_id(0) == 0)
        def _():
            dstate[...] = jnp.zeros_like(dstate)
            dng_ref[...] = jnp.zeros_like(dng_ref)

        args = ([ng_ref[...]], s_ref[0], u_ref[...], w_ref[...], a_ref[...], qd_ref[...], kd_ref[...], e_ref[...], z_ref[...])
        _, vjp = jax.vjp(f_gdn_scan, *args)
        dp, ds, du, dw, da, dqd, dkd, de, dz = vjp((do_ref[...].astype(F32), dstate[...]))
        du_ref[...] = du
        dw_ref[...] = dw
        da_ref[...] = da
        dqd_ref[...] = dqd
        dkd_ref[...] = dkd
        de_ref[...] = de
        dz_ref[...] = dz.astype(BF16)
        dng_ref[...] += dp[0]
        dstate[...] = ds

    rows = [u, w, a, qd, kd, egl, z, dout]
    return pl.pallas_call(
        body, name=name, grid=(n_tiles,),
        in_specs=[_full_spec(norm_g.shape), pl.BlockSpec((1, srows, GDN_HD), lambda i: (n_tiles - 1 - i, 0, 0))]
        + [_row_spec(SCAN_TM, r.shape[1], n_tiles) for r in rows],
        out_specs=[_row_spec(SCAN_TM, r.shape[1], n_tiles) for r in rows[:7]] + [_full_spec(norm_g.shape)],
        out_shape=[jax.ShapeDtypeStruct(r.shape, F32) for r in rows[:6]] + [jax.ShapeDtypeStruct(z.shape, BF16)]
        + [jax.ShapeDtypeStruct(norm_g.shape, F32)],
        scratch_shapes=[pltpu.VMEM((srows, GDN_HD), F32)], compiler_params=_cparams(),
    )(norm_g, states, *rows)


def loss_stage(name, y, target, tm):
    t, d = y.shape

    def body(y_ref, t_ref, l_ref, dy_ref):
        @pl.when(pl.program_id(0) == 0)
        def _():
            l_ref[...] = jnp.zeros_like(l_ref)

        e = y_ref[...] - t_ref[...]
        dy_ref[...] = e * (1.0 / d)
        sq = e * e
        part = sq[:, 0:LANES]
        for j in range(1, d // LANES):
            part = part + sq[:, j * LANES:(j + 1) * LANES]
        acc = part[0:8, :]
        for j in range(1, tm // 8):
            acc = acc + part[j * 8:(j + 1) * 8, :]
        l_ref[...] += acc * (0.5 / d)

    return pl.pallas_call(
        body, name=name, grid=(t // tm,),
        in_specs=[_row_spec(tm, d), _row_spec(tm, d)],
        out_specs=[_full_spec((8, LANES)), _row_spec(tm, d)],
        out_shape=[jax.ShapeDtypeStruct((8, LANES), F32), jax.ShapeDtypeStruct((t, d), F32)],
        compiler_params=_cparams(),
    )(y, target)


def _pick(n, cands):
    for c in cands:
        if n % c == 0:
            return c
    return n


def matmul(name, a, b, form, out_dtype, acc=None, fuse=None):
    if form == "nn":
        (m, k), n = a.shape, b.shape[1]
    elif form == "nt":
        (m, k), n = a.shape, b.shape[0]
    else:
        (k, m), n = a.shape, b.shape[1]
    if form == "tn":
        tm = _pick(m, (1024, 512, 256, 128))
        tn = n if n <= 2048 else _pick(n, (1536, 1024, 512, 256, 128))
        tk = _pick(k, (2048, 1024, 512, 256, 128))
    else:
        tm = _pick(m, (512, 256, 128))
        tk = k if k <= 4096 else _pick(k, (2048, 1024, 512))
        cap = 2048 * 2048 if fuse is None else 2048 * 1024
        tn = n if n * tk <= cap else _pick(n, (2048, 1536, 1024, 512, 256, 128))
    nk = k // tk
    ca, cb = {"nn": (1, 0), "nt": (1, 1), "tn": (0, 0)}[form]
    a_spec = (pl.BlockSpec((tm, tk), lambda j, i, kk: (i, kk)) if form != "tn"
              else pl.BlockSpec((tk, tm), lambda j, i, kk: (kk, i)))
    b_spec = (pl.BlockSpec((tn, tk), lambda j, i, kk: (j, kk)) if form == "nt"
              else pl.BlockSpec((tk, tn), lambda j, i, kk: (kk, j)))
    o_spec = pl.BlockSpec((tm, tn), lambda j, i, kk: (i, j))
    col_spec = pl.BlockSpec((1, tn), lambda j, i, kk: (0, j))
    has_acc = acc is not None

    assert nk == 1 or (out_dtype == F32 and fuse is None)

    def body(*refs):
        a_ref, b_ref = refs[0], refs[1]
        prod = lax.dot_general(a_ref[...], b_ref[...], (((ca,), (cb,)), ((), ())), preferred_element_type=F32)
        if fuse is not None and fuse[0] == "relu2":
            pre = prod + refs[2][...]
            refs[3][...] = jnp.square(jnp.maximum(pre, 0.0)).astype(BF16)
            refs[4][...] = pre.astype(BF16)
            return
        if fuse is not None and fuse[0] == "relu2_bwd":
            d = prod * (2.0 * jnp.maximum(refs[2][...].astype(F32), 0.0))
            refs[3][...] = d.astype(BF16)

            @pl.when(pl.program_id(1) == 0)
            def _():
                refs[4][...] = jnp.zeros_like(refs[4])

            refs[4][...] += jnp.sum(d, axis=0, keepdims=True)
            return
        o_ref = refs[-1]
        if nk == 1:
            o_ref[...] = (prod + refs[2][...] if has_acc else prod).astype(out_dtype)
            return
        kk = pl.program_id(2)

        @pl.when(kk == 0)
        def _():
            o_ref[...] = prod + refs[2][...] if has_acc else prod

        @pl.when(kk > 0)
        def _():
            o_ref[...] += prod

    ins = [a.astype(BF16), b.astype(BF16)] + ([acc] if has_acc else [])
    in_specs = [a_spec, b_spec] + ([o_spec] if has_acc else [])
    out_specs, out_shape = o_spec, jax.ShapeDtypeStruct((m, n), out_dtype)
    if fuse is not None and fuse[0] == "relu2":
        ins, in_specs = ins + [fuse[1]], in_specs + [col_spec]
        out_specs, out_shape = [o_spec, o_spec], [jax.ShapeDtypeStruct((m, n), BF16)] * 2
    elif fuse is not None:
        ins, in_specs = ins + [fuse[1]], in_specs + [o_spec]
        out_specs = [o_spec, col_spec]
        out_shape = [jax.ShapeDtypeStruct((m, n), BF16), jax.ShapeDtypeStruct((1, n), F32)]
    return pl.pallas_call(
        body, name=name, grid=(n // tn, m // tm, nk),
        in_specs=in_specs, out_specs=out_specs, out_shape=out_shape,
        input_output_aliases={2: 0} if has_acc else {},
        compiler_params=_cparams(3),
    )(*ins)


MESH_ID = pl.DeviceIdType.MESH
ANY_SPEC = pl.BlockSpec(memory_space=pl.ANY)


def _place():
    return lax.axis_index("x"), lax.axis_index("y"), lax.axis_index("c")


def chip_exchange(name, arrs, scatter):
    n = len(arrs)

    def body(*refs):
        ins, outs = refs[:n], refs[n:2 * n]
        send_sems, recv_sems, local_sems = refs[2 * n:]
        x, y, c = _place()
        me = 2 * x + y
        others = [(1 - x, y), (x, 1 - y), (1 - x, 1 - y)]

        def remote(i, j, landing_from_me):
            ox, oy = others[j]
            peer = 2 * ox + oy
            src = ins[i].at[peer] if scatter else ins[i]
            dst = outs[i].at[me if landing_from_me else peer]
            return pltpu.make_async_remote_copy(src_ref=src, dst_ref=dst, send_sem=send_sems.at[3 * i + j],
                                                recv_sem=recv_sems.at[3 * i + j], device_id=(ox, oy, c),
                                                device_id_type=MESH_ID)

        local = [pltpu.make_async_copy(ins[i].at[me] if scatter else ins[i], outs[i].at[me], local_sems.at[i])
                 for i in range(n)]
        for cp in local:
            cp.start()
        sent = [remote(i, j, True) for i in range(n) for j in range(3)]
        for cp in sent:
            cp.start()
        for i in range(n):
            for j in range(3):
                remote(i, j, False).wait_recv()
        for cp in sent:
            cp.wait_send()
        for cp in local:
            cp.wait()

    def out_of(a):
        return jax.ShapeDtypeStruct(a.shape if scatter else (N_CHIPS,) + a.shape, a.dtype)

    return pl.pallas_call(
        body, name=name, in_specs=[ANY_SPEC] * n, out_specs=[ANY_SPEC] * n, out_shape=[out_of(a) for a in arrs],
        scratch_shapes=[pltpu.SemaphoreType.DMA((3 * n,)), pltpu.SemaphoreType.DMA((3 * n,)), pltpu.SemaphoreType.DMA((n,))],
    )(*arrs)


def gather_weights(name, layered, flat):
    nl, n = len(layered), len(layered) + len(flat)
    assert all(a.shape[0] == 2 for a in layered)

    def body(*refs):
        ins, outs = refs[:n], refs[n:2 * n]
        ici_send, ici_recv, d2d_send, d2d_recv, local_sems = refs[2 * n:]
        x, y, c = _place()
        me = 2 * x + y
        others = [(1 - x, y), (x, 1 - y), (1 - x, 1 - y)]

        def ici(i, j, landing_from_me):
            ox, oy = others[j]
            slot = me if landing_from_me else 2 * ox + oy
            src, dst = (ins[i].at[c], outs[i].at[slot, c]) if i < nl else (ins[i], outs[i].at[slot])
            return pltpu.make_async_remote_copy(src_ref=src, dst_ref=dst, send_sem=ici_send.at[3 * i + j],
                                                recv_sem=ici_recv.at[3 * i + j], device_id=(ox, oy, c), device_id_type=MESH_ID)

        def d2d(i, j, slab):
            ox, oy = others[j]
            ref = outs[i].at[2 * ox + oy, slab]
            return pltpu.make_async_remote_copy(src_ref=ref, dst_ref=ref, send_sem=d2d_send.at[3 * i + j],
                                                recv_sem=d2d_recv.at[3 * i + j], device_id=(x, y, 1 - c), device_id_type=MESH_ID)

        local = [pltpu.make_async_copy(ins[i], outs[i].at[me], local_sems.at[i]) for i in range(n)]
        for cp in local:
            cp.start()
        sent = [ici(i, j, True) for i in range(n) for j in range(3)]
        for cp in sent:
            cp.start()
        for i in range(n):
            for j in range(3):
                ici(i, j, False).wait_recv()
                if i < nl:
                    fwd = d2d(i, j, c)
                    fwd.start()
                    sent.append(fwd)
        for i in range(nl):
            for j in range(3):
                d2d(i, j, 1 - c).wait_recv()
        for cp in sent:
            cp.wait_send()
        for cp in local:
            cp.wait()

    arrs = list(layered) + list(flat)
    return pl.pallas_call(
        body, name=name, in_specs=[ANY_SPEC] * n, out_specs=[ANY_SPEC] * n,
        out_shape=[jax.ShapeDtypeStruct((N_CHIPS,) + a.shape, a.dtype) for a in arrs],
        scratch_shapes=[pltpu.SemaphoreType.DMA((3 * n,)), pltpu.SemaphoreType.DMA((3 * n,)),
                        pltpu.SemaphoreType.DMA((3 * nl,)), pltpu.SemaphoreType.DMA((3 * nl,)), pltpu.SemaphoreType.DMA((n,))],
    )(*arrs)


def sibling_exchange(name, arrs):
    n = len(arrs)

    def body(*refs):
        ins, outs = refs[:n], refs[n:2 * n]
        send_sems, recv_sems = refs[2 * n:]
        x, y, c = _place()
        cps = [pltpu.make_async_remote_copy(src_ref=ins[i], dst_ref=outs[i], send_sem=send_sems.at[i], recv_sem=recv_sems.at[i],
                                            device_id=(x, y, 1 - c), device_id_type=MESH_ID) for i in range(n)]
        for cp in cps:
            cp.start()
        for cp in cps:
            cp.wait_recv()
        for cp in cps:
            cp.wait_send()

    return pl.pallas_call(
        body, name=name, in_specs=[ANY_SPEC] * n, out_specs=[ANY_SPEC] * n,
        out_shape=[jax.ShapeDtypeStruct(a.shape, a.dtype) for a in arrs],
        scratch_shapes=[pltpu.SemaphoreType.DMA((n,)), pltpu.SemaphoreType.DMA((n,))],
    )(*arrs)


def gather_all(name, vec):
    def body(in_ref, out_ref, send_sems, recv_sems, local_sem):
        x, y, c = _place()
        me = 4 * x + 2 * y + c

        def peer(mask):
            return (1 - x if mask & 4 else x, 1 - y if mask & 2 else y, 1 - c if mask & 1 else c)

        def remote(mask, landing_from_me):
            px, py, pc = peer(mask)
            slot = me if landing_from_me else 4 * px + 2 * py + pc
            return pltpu.make_async_remote_copy(src_ref=in_ref, dst_ref=out_ref.at[slot], send_sem=send_sems.at[mask - 1],
                                                recv_sem=recv_sems.at[mask - 1], device_id=(px, py, pc), device_id_type=MESH_ID)

        local = pltpu.make_async_copy(in_ref, out_ref.at[me], local_sem)
        local.start()
        sent = [remote(mask, True) for mask in range(1, N_DEV)]
        for cp in sent:
            cp.start()
        for mask in range(1, N_DEV):
            remote(mask, False).wait_recv()
        for cp in sent:
            cp.wait_send()
        local.wait()

    return pl.pallas_call(
        body, name=name, in_specs=[ANY_SPEC], out_specs=ANY_SPEC,
        out_shape=jax.ShapeDtypeStruct((N_DEV,) + vec.shape, vec.dtype),
        scratch_shapes=[pltpu.SemaphoreType.DMA((N_DEV - 1,)), pltpu.SemaphoreType.DMA((N_DEV - 1,)), pltpu.SemaphoreType.DMA],
    )(vec)


def sum_slots(name, arr, tr):
    k, r, c = arr.shape

    def body(a_ref, o_ref):
        acc = a_ref[0].astype(F32)
        for j in range(1, k):
            acc = acc + a_ref[j].astype(F32)
        o_ref[...] = acc

    return pl.pallas_call(
        body, name=name, grid=(r // tr,),
        in_specs=[pl.BlockSpec((k, tr, c), lambda i: (0, i, 0))], out_specs=pl.BlockSpec((tr, c), lambda i: (i, 0)),
        out_shape=jax.ShapeDtypeStruct((r, c), F32), compiler_params=_cparams(),
    )(arr)


def adamw(name, parts, w, m, v, tr):
    r, c = w.shape
    npart = len(parts)
    bc1 = 1.0 - ADAM_B1 ** ADAM_STEP
    bc2 = 1.0 - ADAM_B2 ** ADAM_STEP

    def body(*refs):
        g = refs[0][...]
        for j in range(1, npart):
            g = g + refs[j][...]
        w_ref, m_ref, v_ref, g_out, d_out, m_out, v_out = refs[npart:]
        m_new = ADAM_B1 * m_ref[...] + (1.0 - ADAM_B1) * g
        v_new = ADAM_B2 * v_ref[...] + (1.0 - ADAM_B2) * jnp.square(g)
        m_hat = m_new / bc1
        v_hat = v_new / bc2
        g_out[...] = g
        d_out[...] = -ADAM_LR * (m_hat / (jnp.sqrt(v_hat) + ADAM_EPS) + ADAM_WD * w_ref[...])
        m_out[...] = m_new
        v_out[...] = v_new

    spec = pl.BlockSpec((tr, c), lambda i: (i, 0))
    return pl.pallas_call(
        body, name=name, grid=(r // tr,), in_specs=[spec] * (npart + 3), out_specs=[spec] * 4,
        out_shape=[jax.ShapeDtypeStruct((r, c), F32)] * 4, compiler_params=_cparams(),
    )(*parts, w, m, v)


ROW_TM = 256


def _row(v):
    return v.reshape(1, -1)


def _lane_pad(vals, at):
    return jnp.concatenate([jnp.zeros((at,), F32), vals, jnp.zeros((LANES - at - vals.shape[0],), F32)]).reshape(1, LANES)


def _layer_params(sm, l):
    return dict(
        conf=[sm["conv_dw_w"][l], _row(sm["conv_dw_b"][l]), _row(sm["conv_ln_g"][l]), _row(sm["conv_ln_b"][l])],
        pre=[sm["gdn_conv_q"][l], sm["gdn_conv_k"][l], sm["gdn_conv_v"][l],
             _lane_pad(sm["gdn_a_log"][l], GDN_HEADS), _lane_pad(sm["gdn_dt_bias"][l], GDN_HEADS)],
        scan=_row(sm["gdn_norm_g"][l]),
        sgu=[_row(sm["sgu_ln_g"][l]), _row(sm["sgu_ln_b"][l]), sm["sgu_w_s"][l].reshape(SGU_GROUPS * SGU_CHUNK, SGU_CHUNK),
             sm["sgu_b_s"][l].T],
        mix=[_row(sm["b_gate"][l])],
        ln1=[_row(sm["ln1_g"][l]), _row(sm["ln1_b"][l])],
        ff=[_row(sm["b_ff1"][l])],
        ln2=[_row(sm["b_ff2"][l]), _row(sm["ln2_g"][l]), _row(sm["ln2_b"][l])],
    )


def _split_w_in(w):
    zpad = jnp.zeros((D_MODEL, LANES - 2 * GDN_HEADS), w.dtype)
    return dict(A=w[:, 0:1024], B=jnp.concatenate([w[:, 1024:2560], w[:, 3072:3080], zpad], axis=1), Z=w[:, 2560:3072],
                C=w[:, 3080:4104], G=w[:, 4104:7176])


def _join_w_in(d):
    return jnp.concatenate([d["A"], d["B"][:, 0:1536], d["Z"], d["B"][:, 1536:1544], d["C"], d["G"]], axis=1)


def layer_forward(l, xin, xin_b, big, lp):
    tag = f"l{l}_"
    wi = big["w_in"][l]
    sv = dict(xin=xin, xin_b=xin_b)
    for r in "GACBZ":
        sv["p" + r] = matmul(tag + "proj_" + r, xin_b, wi[r], "nn", F32)
    (sv["ya_in"],) = halo_fwd(tag + "conformer", f_conformer, ROW_TM, CONV_HALO, sv["pA"], lp["conf"], [[BF16]])
    sv["qn"], sv["kn"], sv["vv"], sv["gb"] = halo_fwd(tag + "gdn_pre", f_gdn_pre, ROW_TM, GDN_HALO, sv["pB"], lp["pre"], [[F32]] * 4)
    *prep, sv["tinv"] = prep_fwd(tag + "gdn_prep", sv["qn"], sv["kn"], sv["vv"], sv["gb"])
    sv["prep"] = prep
    sv["yb_in"], sv["states"] = scan_fwd(tag + "gdn_scan", lp["scan"], *prep, sv["pZ"])
    (sv["yc_in"],) = stage_fwd(tag + "sgu", f_sgu, ROW_TM, [sv["pC"]], lp["sgu"], [[BF16]])
    sv["ya"] = matmul(tag + "out_a", sv["ya_in"], big["w_pa"][l], "nn", F32)
    sv["yb"] = matmul(tag + "out_b", sv["yb_in"], big["w_pb"][l], "nn", F32)
    sv["yc"] = matmul(tag + "out_c", sv["yc_in"], big["w_pc"][l], "nn", F32)
    (sv["mix"],) = stage_fwd(tag + "mix", f_mix, ROW_TM, [sv["ya"], sv["yb"], sv["yc"], sv["pG"]], lp["mix"], [[BF16]])
    sv["m"] = matmul(tag + "w_o", sv["mix"], big["w_o"][l], "nn", F32)
    sv["x1"], sv["x1_b"] = stage_fwd(tag + "ln1", f_ln1, ROW_TM, [xin, sv["m"]], lp["ln1"], [[F32, BF16]])
    sv["h"], sv["hpre"] = matmul(tag + "ff1", sv["x1_b"], big["w_ff1"][l], "nn", BF16, fuse=("relu2", lp["ff"][0]))
    sv["f2"] = matmul(tag + "ff2", sv["h"], big["w_ff2"][l], "nn", F32)
    x2, x2_b = stage_fwd(tag + "ln2", f_ln2, ROW_TM, [sv["x1"], sv["f2"]], lp["ln2"], [[F32, BF16]])
    return x2, x2_b, sv


def layer_backward(l, dx2, sv, big, lp):
    tag = f"l{l}_b_"
    g = {}
    dx1_a, df2, db2, dg2, dbe2 = stage_bwd(tag + "ln2", f_ln2, ROW_TM, [sv["x1"], sv["f2"]], lp["ln2"], [dx2], [F32, BF16])
    g["b_ff2"], g["ln2_g"], g["ln2_b"] = db2[0], dg2[0], dbe2[0]
    g["w_ff2"] = matmul(tag + "dw_ff2", sv["h"], df2, "tn", F32)
    dhpre, db1 = matmul(tag + "dx_ff2", df2, big["w_ff2"][l], "nt", BF16, fuse=("relu2_bwd", sv["hpre"]))
    g["b_ff1"] = db1[0]
    g["w_ff1"] = matmul(tag + "dw_ff1", sv["x1_b"], dhpre, "tn", F32)
    dx1_b = matmul(tag + "dx_ff1", dhpre, big["w_ff1"][l], "nt", F32)
    dxin_a, dm, dg1, dbe1 = stage_bwd(tag + "ln1", f_ln1, ROW_TM, [sv["xin"], sv["m"]], lp["ln1"], [[dx1_a, dx1_b]], [F32, BF16])
    g["ln1_g"], g["ln1_b"] = dg1[0], dbe1[0]
    g["w_o"] = matmul(tag + "dw_o", sv["mix"], dm, "tn", F32)
    dmix = matmul(tag + "dx_o", dm, big["w_o"][l], "nt", F32)
    dya, dyb, dyc, dp_g, dbg = stage_bwd(tag + "mix", f_mix, ROW_TM, [sv["ya"], sv["yb"], sv["yc"], sv["pG"]], lp["mix"],
                                         [[dmix]], [BF16] * 4)
    g["b_gate"] = dbg[0]
    g["w_pa"] = matmul(tag + "dw_pa", sv["ya_in"], dya, "tn", F32)
    g["w_pb"] = matmul(tag + "dw_pb", sv["yb_in"], dyb, "tn", F32)
    g["w_pc"] = matmul(tag + "dw_pc", sv["yc_in"], dyc, "tn", F32)
    dya_in = matmul(tag + "dx_pa", dya, big["w_pa"][l], "nt", F32)
    dyb_in = matmul(tag + "dx_pb", dyb, big["w_pb"][l], "nt", F32)
    dyc_in = matmul(tag + "dx_pc", dyc, big["w_pc"][l], "nt", F32)
    dp_a, dcw, dcb, dcg, dcbe = halo_bwd(tag + "conformer", f_conformer, ROW_TM, CONV_HALO, sv["pA"], lp["conf"], [dya_in], BF16)
    g["conv_dw_w"], g["conv_dw_b"], g["conv_ln_g"], g["conv_ln_b"] = dcw, dcb[0], dcg[0], dcbe[0]
    dp_c, dsg, dsb, dsw, dsbs = stage_bwd(tag + "sgu", f_sgu, ROW_TM, [sv["pC"]], lp["sgu"], [[dyc_in]], [BF16])
    g["sgu_ln_g"], g["sgu_ln_b"] = dsg[0], dsb[0]
    g["sgu_w_s"] = dsw.reshape(SGU_GROUPS, SGU_CHUNK, SGU_CHUNK)
    g["sgu_b_s"] = dsbs.T
    *dprep, dp_z, dng = scan_bwd(tag + "gdn_scan", lp["scan"], sv["states"], *sv["prep"], sv["pZ"], dyb_in)
    g["gdn_norm_g"] = dng[0]
    dqn, dkn, dvv, dgb = prep_bwd(tag + "gdn_prep", sv["qn"], sv["kn"], sv["vv"], sv["gb"], sv["tinv"], dprep)
    dp_b, dq, dk, dv, dal, ddt = halo_bwd(tag + "gdn_pre", f_gdn_pre, ROW_TM, GDN_HALO, sv["pB"], lp["pre"], [dqn, dkn, dvv, dgb], BF16)
    g["gdn_conv_q"], g["gdn_conv_k"], g["gdn_conv_v"] = dq, dk, dv
    g["gdn_a_log"] = dal[0, GDN_HEADS:2 * GDN_HEADS]
    g["gdn_dt_bias"] = ddt[0, GDN_HEADS:2 * GDN_HEADS]
    dps = dict(G=dp_g, A=dp_a, C=dp_c, B=dp_b, Z=dp_z)
    wi = big["w_in"][l]
    g["w_in"] = _join_w_in({r: matmul(tag + "dw_in_" + r, sv["xin_b"], dps[r], "tn", F32) for r in "GACBZ"})
    dxin_b = None
    for r in "GACBZ":
        dxin_b = matmul(tag + "dx_in_" + r, dps[r], wi[r], "nt", F32, acc=dxin_b)
    return [dxin_a, dxin_b], g


SMALL_PER_LAYER = ("b_gate", "conv_dw_w", "conv_dw_b", "conv_ln_g", "conv_ln_b", "gdn_conv_q", "gdn_conv_k", "gdn_conv_v",
                   "gdn_a_log", "gdn_dt_bias", "gdn_norm_g", "sgu_ln_g", "sgu_ln_b", "sgu_w_s", "sgu_b_s", "ln1_g", "ln1_b",
                   "b_ff1", "b_ff2", "ln2_g", "ln2_b")
BIG = ("w_in", "w_pa", "w_pb", "w_pc", "w_o", "w_ff1", "w_ff2")


def local_step(x, target, big, sm):
    lps = [_layer_params(sm, l) for l in range(DEPTH)]
    p_in = [_row(sm["ln_in_g"]), _row(sm["ln_in_b"])]
    xc, xc_b = stage_fwd("ln_in", f_ln_in, ROW_TM, [x], p_in, [[F32, BF16]])
    saved = []
    for l in range(DEPTH):
        xc, xc_b, sv = layer_forward(l, xc, xc_b, big, lps[l])
        saved.append(sv)
    loss_part, dy = loss_stage("loss", xc, target, ROW_TM)
    dx = [dy]
    per_layer = [None] * DEPTH
    for l in reversed(range(DEPTH)):
        dx, per_layer[l] = layer_backward(l, dx, saved[l], big, lps[l])
    grad_x, dgi, dbi = stage_bwd("ln_in_b", f_ln_in, ROW_TM, [x], p_in, [dx], [F32])
    grads = {n: jnp.stack([per_layer[l][n] for l in range(DEPTH)]) for n in SMALL_PER_LAYER + BIG}
    grads["ln_in_g"], grads["ln_in_b"] = dgi[0], dbi[0]
    return loss_part, grad_x, grads


PARAM_NAMES = ("ln_in_g", "ln_in_b", "w_in", "b_gate", "conv_dw_w", "conv_dw_b", "conv_ln_g", "conv_ln_b", "w_pa",
               "gdn_conv_q", "gdn_conv_k", "gdn_conv_v", "gdn_a_log", "gdn_dt_bias", "gdn_norm_g", "w_pb", "sgu_ln_g",
               "sgu_ln_b", "sgu_w_s", "sgu_b_s", "w_pc", "w_o", "ln1_g", "ln1_b", "w_ff1", "b_ff1", "w_ff2", "b_ff2",
               "ln2_g", "ln2_b")
SMALL = tuple(n for n in PARAM_NAMES if n not in BIG)
COL_SHARDED_SMALL = ("conv_dw_w", "gdn_conv_q", "gdn_conv_k", "gdn_conv_v")
ROW_SHARDED_BIG = ("w_o", "w_ff2")
ADAM_ROWS = 128


def _unshard(g, rows):
    n, l, r, c = g.shape
    if rows:
        return jnp.moveaxis(g, 0, 1).reshape(l, n * r, c)
    return jnp.moveaxis(g, 0, 2).reshape(l, r, n * c)


def _to_shards(w, rows):
    l, r, c = w.shape
    if rows:
        return jnp.moveaxis(w.reshape(l, N_CHIPS, r // N_CHIPS, c), 1, 0).reshape(N_CHIPS, l * (r // N_CHIPS), c)
    return jnp.moveaxis(w.reshape(l, r, N_CHIPS, c // N_CHIPS), 2, 0).reshape(N_CHIPS, l * r, c // N_CHIPS)


def _pack(vals, rows_multiple=8):
    flat = jnp.concatenate([v.reshape(-1) for v in vals])
    per = LANES * rows_multiple
    padded = -(-flat.shape[0] // per) * per
    return jnp.pad(flat, (0, padded - flat.shape[0])).reshape(-1, LANES)


def _unpack(packed, shapes):
    flat = packed.reshape(-1)
    out, at = [], 0
    for s in shapes:
        size = 1
        for d in s:
            size *= d
        out.append(flat[at:at + size].reshape(s))
        at += size
    return out


def kernel(x, ln_in_g, ln_in_b, w_in, b_gate, conv_dw_w, conv_dw_b, conv_ln_g, conv_ln_b, w_pa, gdn_conv_q, gdn_conv_k, gdn_conv_v, gdn_a_log, gdn_dt_bias, gdn_norm_g, w_pb, sgu_ln_g, sgu_ln_b, sgu_w_s, sgu_b_s, w_pc, w_o, ln1_g, ln1_b, w_ff1, b_ff1, w_ff2, b_ff2, ln2_g, ln2_b, loss_target, m_ln_in_g, m_ln_in_b, m_w_in, m_b_gate, m_conv_dw_w, m_conv_dw_b, m_conv_ln_g, m_conv_ln_b, m_w_pa, m_gdn_conv_q, m_gdn_conv_k, m_gdn_conv_v, m_gdn_a_log, m_gdn_dt_bias, m_gdn_norm_g, m_w_pb, m_sgu_ln_g, m_sgu_ln_b, m_sgu_w_s, m_sgu_b_s, m_w_pc, m_w_o, m_ln1_g, m_ln1_b, m_w_ff1, m_b_ff1, m_w_ff2, m_b_ff2, m_ln2_g, m_ln2_b, v_ln_in_g, v_ln_in_b, v_w_in, v_b_gate, v_conv_dw_w, v_conv_dw_b, v_conv_ln_g, v_conv_ln_b, v_w_pa, v_gdn_conv_q, v_gdn_conv_k, v_gdn_conv_v, v_gdn_a_log, v_gdn_dt_bias, v_gdn_norm_g, v_w_pb, v_sgu_ln_g, v_sgu_ln_b, v_sgu_w_s, v_sgu_b_s, v_w_pc, v_w_o, v_ln1_g, v_ln1_b, v_w_ff1, v_b_ff1, v_w_ff2, v_b_ff2, v_ln2_g, v_ln2_b):
    w = dict(zip(PARAM_NAMES, (ln_in_g, ln_in_b, w_in, b_gate, conv_dw_w, conv_dw_b, conv_ln_g, conv_ln_b, w_pa, gdn_conv_q, gdn_conv_k, gdn_conv_v, gdn_a_log, gdn_dt_bias, gdn_norm_g, w_pb, sgu_ln_g, sgu_ln_b, sgu_w_s, sgu_b_s, w_pc, w_o, ln1_g, ln1_b, w_ff1, b_ff1, w_ff2, b_ff2, ln2_g, ln2_b)))
    m = dict(zip(PARAM_NAMES, (m_ln_in_g, m_ln_in_b, m_w_in, m_b_gate, m_conv_dw_w, m_conv_dw_b, m_conv_ln_g, m_conv_ln_b, m_w_pa, m_gdn_conv_q, m_gdn_conv_k, m_gdn_conv_v, m_gdn_a_log, m_gdn_dt_bias, m_gdn_norm_g, m_w_pb, m_sgu_ln_g, m_sgu_ln_b, m_sgu_w_s, m_sgu_b_s, m_w_pc, m_w_o, m_ln1_g, m_ln1_b, m_w_ff1, m_b_ff1, m_w_ff2, m_b_ff2, m_ln2_g, m_ln2_b)))
    v = dict(zip(PARAM_NAMES, (v_ln_in_g, v_ln_in_b, v_w_in, v_b_gate, v_conv_dw_w, v_conv_dw_b, v_conv_ln_g, v_conv_ln_b, v_w_pa, v_gdn_conv_q, v_gdn_conv_k, v_gdn_conv_v, v_gdn_a_log, v_gdn_dt_bias, v_gdn_norm_g, v_w_pb, v_sgu_ln_g, v_sgu_ln_b, v_sgu_w_s, v_sgu_b_s, v_w_pc, v_w_o, v_ln1_g, v_ln1_b, v_w_ff1, v_b_ff1, v_w_ff2, v_b_ff2, v_ln2_g, v_ln2_b)))
    chip = 2 * lax.axis_index("x") + lax.axis_index("y")

    col_small_shapes = [w[n].shape for n in COL_SHARDED_SMALL]
    gathered = gather_weights("gather_weights", [w[n].astype(BF16) for n in BIG], [_pack([w[n] for n in COL_SHARDED_SMALL])])
    big = {}
    for n, g in zip(BIG, gathered):
        whole = _unshard(g, n in ROW_SHARDED_BIG)
        big[n] = [_split_w_in(whole[l]) for l in range(DEPTH)] if n == "w_in" else whole
    sm = {n: w[n] for n in SMALL}
    per_chip = [_unpack(gathered[-1][s], col_small_shapes) for s in range(N_CHIPS)]
    for i, n in enumerate(COL_SHARDED_SMALL):
        sm[n] = jnp.concatenate([per_chip[s][i] for s in range(N_CHIPS)], axis=-1)

    loss_part, grad_x, grads = local_step(x[0], loss_target[0], big, sm)

    sends = [_to_shards(grads[n], n in ROW_SHARDED_BIG).astype(BF16) for n in BIG]
    received = chip_exchange("scatter_grads", sends, scatter=True)
    partial = [sum_slots("sum_chips_" + n, r, ADAM_ROWS) for n, r in zip(BIG, received)]
    sibling = sibling_exchange("swap_cores", partial)
    out = {}
    for n, mine, theirs in zip(BIG, partial, sibling):
        shape = w[n].shape
        res = adamw("adamw_" + n, [mine, theirs], w[n].reshape(-1, shape[-1]), m[n].reshape(-1, shape[-1]),
                    v[n].reshape(-1, shape[-1]), ADAM_ROWS)
        out[n] = [r.reshape(shape) for r in res]

    small_shapes = [grads[n].shape for n in SMALL]
    vec = _pack([grads[n] for n in SMALL] + [jnp.sum(loss_part).reshape(1)])
    total = sum_slots("sum_small", gather_all("gather_small", vec), 8)
    whole = _unpack(total, small_shapes + [(1,)])
    loss = whole[-1][0]
    g_small = {}
    for n, g in zip(SMALL, whole[:-1]):
        if n in COL_SHARDED_SMALL:
            width = g.shape[-1] // N_CHIPS
            g = lax.dynamic_slice_in_dim(g, chip * width, width, axis=g.ndim - 1)
        g_small[n] = g
    local_shapes = [w[n].shape for n in SMALL]
    res = adamw("adamw_small", [_pack([g_small[n] for n in SMALL])], _pack([w[n] for n in SMALL]),
                _pack([m[n] for n in SMALL]), _pack([v[n] for n in SMALL]), 8)
    unpacked = [_unpack(r, local_shapes) for r in res]
    for i, n in enumerate(SMALL):
        out[n] = [unpacked[k][i] for k in range(4)]

    return (loss, grad_x[None], *[out[n][0] for n in PARAM_NAMES], *[out[n][1] for n in PARAM_NAMES],
            *[out[n][2] for n in PARAM_NAMES], *[out[n][3] for n in PARAM_NAMES])
```

```python
import functools

import jax
import jax.numpy as jnp
from jax import lax
from jax.experimental import pallas as pl
from jax.experimental.pallas import tpu as pltpu

F32 = jnp.float32
BF16 = jnp.bfloat16

D_MODEL = 1024
DEPTH = 2
CONV_DIM = 512
CONV_WIDTH = 31
CONV_HALO = 32
GDN_HEADS = 4
GDN_HD = 128
GDN_CHUNK = 64
GDN_PAIR = 2 * GDN_CHUNK
GDN_CONV = 4
GDN_HALO = 8
SGU_GROUPS = 4
SGU_GD = 128
SGU_CHUNK = 128
D_FF = 4096
ALPHA = (2 * DEPTH) ** 0.25
LN_EPS = 1e-5
RMS_EPS = 1e-6
ADAM_LR, ADAM_B1, ADAM_B2, ADAM_EPS, ADAM_WD, ADAM_STEP = 0.001, 0.9, 0.999, 1e-08, 0.01, 10
N_CHIPS = 4
N_DEV = 8
LANES = 128
VMEM_LIMIT = 52 * 1024 * 1024

W_GATE, W_A, W_C, W_B, W_Z = 3 * D_MODEL, 2 * CONV_DIM, 2 * 512, 3 * 512 + LANES, 512


def _dg(a, b, ca, cb):
    return lax.dot_general(a.astype(BF16), b.astype(BF16), (((ca,), (cb,)), ((), ())), preferred_element_type=F32)


@jax.custom_vjp
def mm_nn(a, b):
    return _dg(a, b, 1, 0)


mm_nn.defvjp(lambda a, b: (_dg(a, b, 1, 0), (a, b)), lambda r, g: (_dg(g, r[1], 1, 1), _dg(r[0], g, 0, 0)))


@jax.custom_vjp
def mm_nt(a, b):
    return _dg(a, b, 1, 1)


mm_nt.defvjp(lambda a, b: (_dg(a, b, 1, 1), (a, b)), lambda r, g: (_dg(g, r[1], 1, 0), _dg(g, r[0], 0, 0)))


@jax.custom_vjp
def mm_tn(a, b):
    return _dg(a, b, 0, 0)


mm_tn.defvjp(lambda a, b: (_dg(a, b, 0, 0), (a, b)), lambda r, g: (_dg(r[1], g, 1, 1), _dg(r[0], g, 1, 0)))


def _ln(x, g, b):
    mu = jnp.mean(x, -1, keepdims=True)
    xc = x - mu
    var = jnp.mean(xc * xc, -1, keepdims=True)
    return xc * lax.rsqrt(var + LN_EPS) * g + b


def _iota2(shape, dim):
    return lax.broadcasted_iota(jnp.int32, shape, dim)


def f_ln_in(params, x):
    g, b = params
    return (_ln(x, g, b),)


def _roll_rows(x, s):
    return x if s == 0 else pltpu.roll(x, s, 0)


@functools.partial(jax.custom_vjp, nondiff_argnums=(1,))
def shift_rows(x, s):
    return _roll_rows(x, s)


shift_rows.defvjp(lambda x, s: (_roll_rows(x, s), None),
                  lambda s, _, ct: (_roll_rows(ct, (ct.shape[0] - s) % ct.shape[0]),))


def _causal_conv(full, w, halo):
    width = w.shape[0]
    groups = []
    for g in range(full.shape[1] // LANES):
        ls = slice(g * LANES, (g + 1) * LANES)
        xg, wg = full[:, ls], w[:, ls]
        acc = wg[width - 1:width, :] * xg[halo:, :]
        for j in range(width - 1):
            acc = acc + wg[j:j + 1, :] * shift_rows(xg, width - 1 - j)[halo:, :]
        groups.append(acc)
    return jnp.concatenate(groups, axis=1)


def f_conformer(params, halo, pa):
    w, b, g, be = params
    full = jnp.concatenate([halo, pa], axis=0)
    h = full[:, :CONV_DIM] * jax.nn.sigmoid(full[:, CONV_DIM:])
    return (jax.nn.silu(_ln(_causal_conv(h, w, CONV_HALO) + b, g, be)),)


def f_gdn_pre(params, halo, pb):
    cq, ck, cv, alog, dtb = params
    full = jnp.concatenate([halo, pb], axis=0)

    def conv(cols, w):
        return jax.nn.silu(_causal_conv(cols, w, GDN_HALO))

    def l2n(t, scale):
        parts = []
        for h in range(GDN_HEADS):
            th = t[:, h * GDN_HD:(h + 1) * GDN_HD]
            parts.append(th * (lax.rsqrt(jnp.sum(th * th, -1, keepdims=True) + RMS_EPS) * scale))
        return jnp.concatenate(parts, axis=1)

    q = l2n(conv(full[:, 0:512], cq), GDN_HD ** -0.5)
    k = l2n(conv(full[:, 512:1024], ck), 1.0)
    v = conv(full[:, 1024:1536], cv)
    logits = pb[:, 1536:1536 + LANES]
    z = logits + dtb
    softplus = jnp.maximum(z, 0.0) + jnp.log(1.0 + jnp.exp(-jnp.abs(z)))
    g = -jnp.exp(alog) * softplus
    beta = jax.nn.sigmoid(logits)
    lane = _iota2(logits.shape, 1)
    gb = jnp.where(lane < GDN_HEADS, beta, jnp.where(lane < 2 * GDN_HEADS, g, 0.0))
    return q, k, v, gb


def _split2(x):
    hi = x.astype(BF16)
    return hi, (x - hi.astype(F32)).astype(BF16)


def _dg3(a, b, ca, cb):
    ah, al = _split2(a)
    bh, bl = _split2(b)
    d = lambda p, q: lax.dot_general(p, q, (((ca,), (cb,)), ((), ())), preferred_element_type=F32)
    return d(ah, bh) + (d(ah, bl) + d(al, bh))


def _inv_unit_lower(low):
    n = low.shape[0]
    eye = jnp.where(_iota2((n, n), 0) == _iota2((n, n), 1), 1.0, 0.0).astype(F32)
    p = -low
    x = eye + p
    for _ in range(5):
        p = _dg3(p, p, 1, 0)
        x = x + _dg3(x, p, 1, 0)
    return x


def _pair_masks():
    n = GDN_PAIR
    r, c = _iota2((n, n), 0), _iota2((n, n), 1)
    same = (r >= GDN_CHUNK) == (c >= GDN_CHUNK)
    return same, jnp.logical_and(same, r >= c), jnp.logical_and(same, r > c), jnp.logical_and(same, r <= c)


def _masked_sum(mask, g):
    mk = jnp.where(mask, 1.0, 0.0).astype(BF16)
    g1 = g.astype(BF16)
    r1 = g - g1.astype(F32)
    g2 = r1.astype(BF16)
    g3 = (r1 - g2.astype(F32)).astype(BF16)
    d = lambda q: jnp.dot(mk, q, preferred_element_type=F32)
    return d(g1) + (d(g2) + d(g3))


@jax.custom_vjp
def chunk_cumsum(g):
    return _masked_sum(_pair_masks()[1], g)


chunk_cumsum.defvjp(lambda g: (_masked_sum(_pair_masks()[1], g), None), lambda _, ct: (_masked_sum(_pair_masks()[3], ct),))


def _prep_gates(gb):
    gam_all = chunk_cumsum(gb)
    first = _iota2((GDN_PAIR, LANES), 0) < GDN_CHUNK
    glast = jnp.where(first, gam_all[GDN_CHUNK - 1:GDN_CHUNK, :], gam_all[GDN_PAIR - 1:GDN_PAIR, :])
    return gam_all, gam_all.T, glast


def _head_gates(gb, gam_all, gam_t, h, causal):
    gc = gam_all[:, GDN_HEADS + h:GDN_HEADS + h + 1]
    gr = gam_t[GDN_HEADS + h:GDN_HEADS + h + 1, :]
    return gc, gb[:, h:h + 1], jnp.exp(jnp.where(causal, gc - gr, -jnp.inf))


def f_prep_low(kn, gb):
    _, causal, strict, _ = _pair_masks()
    gam_all, gam_t, _ = _prep_gates(gb)
    lows = []
    for h in range(GDN_HEADS):
        k = kn[:, h * GDN_HD:(h + 1) * GDN_HD]
        _, beta, decay = _head_gates(gb, gam_all, gam_t, h, causal)
        lows.append(jnp.where(strict, beta * mm_nt(k, k) * decay, 0.0))
    return jnp.concatenate(lows, axis=1)


def f_prep_rest(qn, kn, vv, gb, tinv_all):
    _, causal, _, _ = _pair_masks()
    gam_all, gam_t, glast = _prep_gates(gb)
    us, ws, as_, qds, kds = [], [], [], [], []
    for h in range(GDN_HEADS):
        hs = slice(h * GDN_HD, (h + 1) * GDN_HD)
        q, k, v, tinv = qn[:, hs], kn[:, hs], vv[:, hs], tinv_all[:, hs]
        gc, beta, decay = _head_gates(gb, gam_all, gam_t, h, causal)
        eg = jnp.exp(gc)
        us.append(mm_nn(tinv, beta * v))
        ws.append(mm_nn(tinv, beta * k * eg))
        as_.append(mm_nt(q, k) * decay)
        qds.append(q * eg)
        kds.append(k * jnp.exp(glast[:, GDN_HEADS + h:GDN_HEADS + h + 1] - gc))
    cat = lambda xs: jnp.concatenate(xs, axis=1)
    return cat(us), cat(ws), cat(as_), cat(qds), cat(kds), jnp.exp(glast)


PREP_TM = 2 * GDN_PAIR


def _ride(ride, n_in, n_out, refs, n_steps):
    if ride is None:
        return
    nri, nro = len(ride["ins"]), len(ride["out_shape"])
    r_in = refs[n_in:n_in + nri]
    r_out = refs[n_in + nri + n_out:n_in + nri + n_out + nro]
    sems = refs[len(refs) - len(ride["scratch"]):]
    step = pl.program_id(0)

    @pl.when(step == 0)
    def _():
        ride["start"](r_in, r_out, sems)

    @pl.when(step == (2 * n_steps) // 3)
    def _():
        ride["middle"](r_in, r_out, sems)

    @pl.when(step == n_steps - 1)
    def _():
        ride["finish"](r_in, r_out, sems)


def _ride_args(ride):
    if ride is None:
        return [], [], [], [], []
    n = len(ride["ins"])
    return list(ride["ins"]), [ANY_SPEC] * n, [ANY_SPEC] * len(ride["out_shape"]), list(ride["out_shape"]), list(ride["scratch"])


def prep_fwd(name, qn, kn, vv, gb, ride=None):
    t = qn.shape[0]
    hw = GDN_HEADS * GDN_HD
    n_steps = t // PREP_TM
    r_ins, r_in_specs, r_out_specs, r_out_shape, r_scratch = _ride_args(ride)

    def body(*refs):
        q_ref, k_ref, v_ref, gb_ref = refs[:4]
        outs = refs[4 + len(r_ins):4 + len(r_ins) + 7]
        for pi in range(PREP_TM // GDN_PAIR):
            rs = pl.ds(pi * GDN_PAIR, GDN_PAIR)
            low = f_prep_low(k_ref[rs, :], gb_ref[rs, :])
            tinv = jnp.concatenate([_inv_unit_lower(low[:, h * GDN_HD:(h + 1) * GDN_HD]) for h in range(GDN_HEADS)], axis=1)
            res = f_prep_rest(q_ref[rs, :], k_ref[rs, :], v_ref[rs, :], gb_ref[rs, :], tinv)
            for o_ref, val in zip(outs, res + (tinv,)):
                o_ref[rs, :] = val
        _ride(ride, 4, 7, refs, n_steps)

    widths = [hw, hw, hw, hw, hw, LANES, hw]
    return pl.pallas_call(
        body, name=name, grid=(n_steps,),
        in_specs=[_row_spec(PREP_TM, hw)] * 3 + [_row_spec(PREP_TM, LANES)] + r_in_specs,
        out_specs=[_row_spec(PREP_TM, wd) for wd in widths] + r_out_specs,
        out_shape=[jax.ShapeDtypeStruct((t, wd), F32) for wd in widths] + r_out_shape,
        scratch_shapes=r_scratch, compiler_params=_cparams(),
    )(qn, kn, vv, gb, *r_ins)


def prep_bwd(name, qn, kn, vv, gb, tinv, cts):
    t = qn.shape[0]
    hw = GDN_HEADS * GDN_HD

    def body(q_ref, k_ref, v_ref, gb_ref, x_ref, *refs):
        outs = refs[6:]
        for pi in range(PREP_TM // GDN_PAIR):
            rs = pl.ds(pi * GDN_PAIR, GDN_PAIR)
            ct = tuple(r[rs, :] for r in refs[:6])
            kn_v, gb_v, x_all = k_ref[rs, :], gb_ref[rs, :], x_ref[rs, :]
            _, vjp_low = jax.vjp(f_prep_low, kn_v, gb_v)
            _, vjp_rest = jax.vjp(f_prep_rest, q_ref[rs, :], kn_v, v_ref[rs, :], gb_v, x_all)
            dq, dk, dv, dgb, dx_all = vjp_rest(ct)
            dlows = []
            for h in range(GDN_HEADS):
                hs = slice(h * GDN_HD, (h + 1) * GDN_HD)
                dlows.append(-_dg3(_dg3(x_all[:, hs], dx_all[:, hs], 0, 0), x_all[:, hs], 1, 1))
            dk2, dgb2 = vjp_low(jnp.concatenate(dlows, axis=1))
            outs[0][rs, :] = dq
            outs[1][rs, :] = dk + dk2
            outs[2][rs, :] = dv
            outs[3][rs, :] = dgb + dgb2

    in_w = [hw, hw, hw, LANES, hw] + [hw, hw, hw, hw, hw, LANES]
    out_w = [hw, hw, hw, LANES]
    return pl.pallas_call(
        body, name=name, grid=(t // PREP_TM,),
        in_specs=[_row_spec(PREP_TM, wd) for wd in in_w], out_specs=[_row_spec(PREP_TM, wd) for wd in out_w],
        out_shape=[jax.ShapeDtypeStruct((t, wd), F32) for wd in out_w], compiler_params=_cparams(),
    )(qn, kn, vv, gb, tinv, *cts)


def f_gdn_scan(params, state, u, w, a, qd, kd, egl, z):
    ng = params[0]
    tm = u.shape[0]
    st = [state[h * GDN_HD:(h + 1) * GDN_HD, :] for h in range(GDN_HEADS)]
    zeros = jnp.zeros((GDN_CHUNK, GDN_HD), F32)
    rows = []
    for ci in range(tm // GDN_CHUNK):
        rs = slice(ci * GDN_CHUNK, (ci + 1) * GDN_CHUNK)
        heads = []
        for h in range(GDN_HEADS):
            hs = slice(h * GDN_HD, (h + 1) * GDN_HD)
            vn = u[rs, hs] - mm_nn(w[rs, hs], st[h])
            vpad = jnp.concatenate([vn, zeros] if ci % 2 == 0 else [zeros, vn], axis=0)
            o = mm_nn(qd[rs, hs], st[h]) + mm_nn(a[rs, hs], vpad)
            e = egl[ci * GDN_CHUNK:ci * GDN_CHUNK + 1, GDN_HEADS + h:GDN_HEADS + h + 1]
            st[h] = st[h] * e + mm_tn(kd[rs, hs], vn)
            on = o * lax.rsqrt(jnp.mean(o * o, -1, keepdims=True) + RMS_EPS) * ng
            heads.append(on * jax.nn.silu(z[rs, hs]))
        rows.append(jnp.concatenate(heads, axis=1))
    return jnp.concatenate(rows, axis=0), jnp.concatenate(st, axis=0)


def f_sgu(params, puv):
    g, b, ws, bst = params
    tm = puv.shape[0]
    uv = jax.nn.gelu(puv)
    u = uv[:, :512]
    v = _ln(uv[:, 512:], g, b)
    r, c = _iota2((SGU_CHUNK, SGU_CHUNK), 0), _iota2((SGU_CHUNK, SGU_CHUNK), 1)
    rows = []
    for ci in range(tm // SGU_CHUNK):
        rs = slice(ci * SGU_CHUNK, (ci + 1) * SGU_CHUNK)
        groups = []
        for gi in range(SGU_GROUPS):
            gs = slice(gi * SGU_GD, (gi + 1) * SGU_GD)
            wt = jnp.where(r >= c, ws[gs, :], 0.0)
            groups.append(mm_nn(wt, v[rs, gs]) + bst[:, gi:gi + 1])
        rows.append(jnp.concatenate(groups, axis=1))
    return (u * jnp.concatenate(rows, axis=0),)


def f_mix(params, ya, yb, yc, pg):
    bg = params[0]
    s = jax.nn.sigmoid(pg + bg)
    return (s[:, :D_MODEL] * ya + s[:, D_MODEL:2 * D_MODEL] * yb + s[:, 2 * D_MODEL:] * yc,)


def f_ln1(params, xin, m):
    g, b = params
    return (_ln(ALPHA * xin + m, g, b),)


def f_ln2(params, x1, f2):
    b2, g, b = params
    return (_ln(ALPHA * x1 + (f2 + b2), g, b),)


def _cparams(n_axes=1):
    return pltpu.CompilerParams(dimension_semantics=("arbitrary",) * n_axes, vmem_limit_bytes=VMEM_LIMIT)


def _row_spec(tm, width, n_tiles=None):
    if n_tiles is None:
        return pl.BlockSpec((tm, width), lambda i: (i, 0))
    return pl.BlockSpec((tm, width), lambda i: (n_tiles - 1 - i, 0))


def _full_spec(shape):
    return pl.BlockSpec(shape, lambda i: (0,) * len(shape))


def stage_fwd(name, fn, tm, rows, params, outs):
    t = rows[0].shape[0]
    nr, npar = len(rows), len(params)
    widths = jax.eval_shape(lambda p, r: fn(p, *r), [jax.ShapeDtypeStruct(p.shape, F32) for p in params],
                            [jax.ShapeDtypeStruct((tm, r.shape[1]), F32) for r in rows])

    def body(*refs):
        r = [x[...].astype(F32) for x in refs[:nr]]
        p = [x[...] for x in refs[nr:nr + npar]]
        res = fn(p, *r)
        k = nr + npar
        for o, dts in zip(res, outs):
            for dt in dts:
                refs[k][...] = o.astype(dt)
                k += 1

    out_shape, out_specs = [], []
    for wd, dts in zip(widths, outs):
        for dt in dts:
            out_shape.append(jax.ShapeDtypeStruct((t, wd.shape[1]), dt))
            out_specs.append(_row_spec(tm, wd.shape[1]))
    return pl.pallas_call(
        body, name=name, grid=(t // tm,),
        in_specs=[_row_spec(tm, r.shape[1]) for r in rows] + [_full_spec(p.shape) for p in params],
        out_specs=out_specs, out_shape=out_shape, compiler_params=_cparams(),
    )(*rows, *params)


def stage_bwd(name, fn, tm, rows, params, cts, drow_dtypes):
    t = rows[0].shape[0]
    nr, npar = len(rows), len(params)
    flat_cts = [c for cl in cts if cl is not None for c in cl]
    nct = len(flat_cts)
    want = [i for i, dt in enumerate(drow_dtypes) if dt is not None]

    def body(*refs):
        r = [x[...].astype(F32) for x in refs[:nr]]
        p = [x[...] for x in refs[nr:nr + npar]]
        res, vjp = jax.vjp(lambda pp, rr: fn(pp, *rr), p, r)
        k = nr + npar
        ct = []
        for o, cl in zip(res, cts):
            if cl is None:
                ct.append(jnp.zeros_like(o))
            else:
                acc = refs[k][...].astype(F32)
                for j in range(1, len(cl)):
                    acc = acc + refs[k + j][...].astype(F32)
                k += len(cl)
                ct.append(acc)
        dp, dr = vjp(tuple(ct))
        for i in want:
            refs[k][...] = dr[i].astype(drow_dtypes[i])
            k += 1

        @pl.when(pl.program_id(0) == 0)
        def _():
            for j in range(npar):
                refs[k + j][...] = jnp.zeros_like(refs[k + j])

        for j in range(npar):
            refs[k + j][...] += dp[j]

    out_shape = [jax.ShapeDtypeStruct(rows[i].shape, drow_dtypes[i]) for i in want]
    out_specs = [_row_spec(tm, rows[i].shape[1]) for i in want]
    out_shape += [jax.ShapeDtypeStruct(p.shape, F32) for p in params]
    out_specs += [_full_spec(p.shape) for p in params]
    return pl.pallas_call(
        body, name=name, grid=(t // tm,),
        in_specs=[_row_spec(tm, r.shape[1]) for r in rows] + [_full_spec(p.shape) for p in params]
        + [_row_spec(tm, c.shape[1]) for c in flat_cts],
        out_specs=out_specs, out_shape=out_shape, compiler_params=_cparams(),
    )(*rows, *params, *flat_cts)


def _halo_spec(tm, halo, width, n_tiles=None):
    per = tm // halo
    if n_tiles is None:
        return pl.BlockSpec((halo, width), lambda i: (jnp.maximum(i * per - 1, 0), 0))
    return pl.BlockSpec((halo, width), lambda i: (jnp.maximum((n_tiles - 1 - i) * per - 1, 0), 0))


def halo_fwd(name, fn, tm, halo, src, params, outs):
    t, width = src.shape
    npar = len(params)
    widths = jax.eval_shape(lambda p, h, r: fn(p, h, r), [jax.ShapeDtypeStruct(p.shape, F32) for p in params],
                            jax.ShapeDtypeStruct((halo, width), F32), jax.ShapeDtypeStruct((tm, width), F32))

    def body(h_ref, x_ref, *refs):
        hal = jnp.where(pl.program_id(0) == 0, 0.0, h_ref[...])
        res = fn([x[...] for x in refs[:npar]], hal, x_ref[...])
        k = npar
        for o, dts in zip(res, outs):
            for dt in dts:
                refs[k][...] = o.astype(dt)
                k += 1

    out_shape, out_specs = [], []
    for wd, dts in zip(widths, outs):
        for dt in dts:
            out_shape.append(jax.ShapeDtypeStruct((t, wd.shape[1]), dt))
            out_specs.append(_row_spec(tm, wd.shape[1]))
    return pl.pallas_call(
        body, name=name, grid=(t // tm,),
        in_specs=[_halo_spec(tm, halo, width), _row_spec(tm, width)] + [_full_spec(p.shape) for p in params],
        out_specs=out_specs, out_shape=out_shape, compiler_params=_cparams(),
    )(src, src, *params)


def halo_bwd(name, fn, tm, halo, src, params, cts, dsrc_dtype, ride=None):
    t, width = src.shape
    n_tiles = t // tm
    npar = len(params)
    n_in = 2 + npar + len(cts)
    r_ins, r_in_specs, r_out_specs, r_out_shape, r_scratch = _ride_args(ride)

    def body(h_ref, x_ref, *refs):
        carry = refs[n_in - 2 + len(r_ins) + 1 + npar + len(r_out_shape)]
        first_tile = pl.program_id(0) == n_tiles - 1
        hal = jnp.where(first_tile, 0.0, h_ref[...])
        p = [x[...] for x in refs[:npar]]
        res, vjp = jax.vjp(fn, p, hal, x_ref[...])
        k = npar
        ct = tuple(refs[k + j][...].astype(F32) for j in range(len(cts)))
        k += len(cts) + len(r_ins)
        dp, dh, dx = vjp(ct)

        @pl.when(pl.program_id(0) == 0)
        def _():
            carry[...] = jnp.zeros_like(carry)
            for j in range(npar):
                refs[k + 1 + j][...] = jnp.zeros_like(refs[k + 1 + j])

        dx = jnp.concatenate([dx[:tm - halo, :], dx[tm - halo:, :] + carry[...]], axis=0)
        refs[k][...] = dx.astype(dsrc_dtype)
        carry[...] = dh
        for j in range(npar):
            refs[k + 1 + j][...] += dp[j]
        _ride(ride, n_in, 1 + npar, (h_ref, x_ref) + refs, n_tiles)

    out_shape = [jax.ShapeDtypeStruct((t, width), dsrc_dtype)] + [jax.ShapeDtypeStruct(p.shape, F32) for p in params]
    out_specs = [_row_spec(tm, width, n_tiles)] + [_full_spec(p.shape) for p in params]
    return pl.pallas_call(
        body, name=name, grid=(n_tiles,),
        in_specs=[_halo_spec(tm, halo, width, n_tiles), _row_spec(tm, width, n_tiles)] + [_full_spec(p.shape) for p in params]
        + [_row_spec(tm, c.shape[1], n_tiles) for c in cts] + r_in_specs,
        out_specs=out_specs + r_out_specs, out_shape=out_shape + r_out_shape,
        scratch_shapes=[pltpu.VMEM((halo, width), F32)] + r_scratch, compiler_params=_cparams(),
    )(src, src, *params, *cts, *r_ins)


SCAN_TM = 256


def scan_fwd(name, norm_g, u, w, a, qd, kd, egl, z):
    t = u.shape[0]
    n_tiles = t // SCAN_TM
    srows = GDN_HEADS * GDN_HD

    def body(ng_ref, u_ref, w_ref, a_ref, qd_ref, kd_ref, e_ref, z_ref, o_ref, s_ref, state):
        @pl.when(pl.program_id(0) == 0)
        def _():
            state[...] = jnp.zeros_like(state)

        s_ref[0] = state[...]
        o, s_new = f_gdn_scan([ng_ref[...]], state[...], u_ref[...], w_ref[...], a_ref[...], qd_ref[...], kd_ref[...],
                              e_ref[...], z_ref[...])
        o_ref[...] = o.astype(BF16)
        state[...] = s_new

    rows = [u, w, a, qd, kd, egl, z]
    return pl.pallas_call(
        body, name=name, grid=(n_tiles,),
        in_specs=[_full_spec(norm_g.shape)] + [_row_spec(SCAN_TM, r.shape[1]) for r in rows],
        out_specs=[_row_spec(SCAN_TM, 512), pl.BlockSpec((1, srows, GDN_HD), lambda i: (i, 0, 0))],
        out_shape=[jax.ShapeDtypeStruct((t, 512), BF16), jax.ShapeDtypeStruct((n_tiles, srows, GDN_HD), F32)],
        scratch_shapes=[pltpu.VMEM((srows, GDN_HD), F32)], compiler_params=_cparams(),
    )(norm_g, *rows)


def scan_bwd(name, norm_g, states, u, w, a, qd, kd, egl, z, dout):
    t = u.shape[0]
    n_tiles = t // SCAN_TM
    srows = GDN_HEADS * GDN_HD

    def body(ng_ref, s_ref, u_ref, w_ref, a_ref, qd_ref, kd_ref, e_ref, z_ref, do_ref,
             du_ref, dw_ref, da_ref, dqd_ref, dkd_ref, de_ref, dz_ref, dng_ref, dstate):
        @pl.when(pl.program_id(0) == 0)
        def _():
            dstate[...] = jnp.zeros_like(dstate)
            dng_ref[...] = jnp.zeros_like(dng_ref)

        args = ([ng_ref[...]], s_ref[0], u_ref[...], w_ref[...], a_ref[...], qd_ref[...], kd_ref[...], e_ref[...], z_ref[...])
        _, vjp = jax.vjp(f_gdn_scan, *args)
        dp, ds, du, dw, da, dqd, dkd, de, dz = vjp((do_ref[...].astype(F32), dstate[...]))
        du_ref[...] = du
        dw_ref[...] = dw
        da_ref[...] = da
        dqd_ref[...] = dqd
        dkd_ref[...] = dkd
        de_ref[...] = de
        dz_ref[...] = dz.astype(BF16)
        dng_ref[...] += dp[0]
        dstate[...] = ds

    rows = [u, w, a, qd, kd, egl, z, dout]
    return pl.pallas_call(
        body, name=name, grid=(n_tiles,),
        in_specs=[_full_spec(norm_g.shape), pl.BlockSpec((1, srows, GDN_HD), lambda i: (n_tiles - 1 - i, 0, 0))]
        + [_row_spec(SCAN_TM, r.shape[1], n_tiles) for r in rows],
        out_specs=[_row_spec(SCAN_TM, r.shape[1], n_tiles) for r in rows[:7]] + [_full_spec(norm_g.shape)],
        out_shape=[jax.ShapeDtypeStruct(r.shape, F32) for r in rows[:6]] + [jax.ShapeDtypeStruct(z.shape, BF16)]
        + [jax.ShapeDtypeStruct(norm_g.shape, F32)],
        scratch_shapes=[pltpu.VMEM((srows, GDN_HD), F32)], compiler_params=_cparams(),
    )(norm_g, states, *rows)


def loss_stage(name, y, target, tm):
    t, d = y.shape

    def body(y_ref, t_ref, l_ref, dy_ref):
        @pl.when(pl.program_id(0) == 0)
        def _():
            l_ref[...] = jnp.zeros_like(l_ref)

        e = y_ref[...] - t_ref[...]
        dy_ref[...] = e * (1.0 / d)
        sq = e * e
        part = sq[:, 0:LANES]
        for j in range(1, d // LANES):
            part = part + sq[:, j * LANES:(j + 1) * LANES]
        acc = part[0:8, :]
        for j in range(1, tm // 8):
            acc = acc + part[j * 8:(j + 1) * 8, :]
        l_ref[...] += acc * (0.5 / d)

    return pl.pallas_call(
        body, name=name, grid=(t // tm,),
        in_specs=[_row_spec(tm, d), _row_spec(tm, d)],
        out_specs=[_full_spec((8, LANES)), _row_spec(tm, d)],
        out_shape=[jax.ShapeDtypeStruct((8, LANES), F32), jax.ShapeDtypeStruct((t, d), F32)],
        compiler_params=_cparams(),
    )(y, target)


def _pick(n, cands):
    for c in cands:
        if n % c == 0:
            return c
    return n


def matmul(name, a, b, form, out_dtype, acc=None, fuse=None):
    if form == "nn":
        (m, k), n = a.shape, b.shape[1]
    elif form == "nt":
        (m, k), n = a.shape, b.shape[0]
    else:
        (k, m), n = a.shape, b.shape[1]
    if form == "tn":
        tm = _pick(m, (1024, 512, 256, 128))
        tn = n if n <= 2048 else _pick(n, (1536, 1024, 512, 256, 128))
        tk = _pick(k, (2048, 1024, 512, 256, 128))
    else:
        tm = _pick(m, (512, 256, 128))
        tk = k if k <= 4096 else _pick(k, (2048, 1024, 512))
        cap = 2048 * 2048 if fuse is None else 2048 * 1024
        tn = n if n * tk <= cap else _pick(n, (2048, 1536, 1024, 512, 256, 128))
    nk = k // tk
    ca, cb = {"nn": (1, 0), "nt": (1, 1), "tn": (0, 0)}[form]
    a_spec = (pl.BlockSpec((tm, tk), lambda j, i, kk: (i, kk)) if form != "tn"
              else pl.BlockSpec((tk, tm), lambda j, i, kk: (kk, i)))
    b_spec = (pl.BlockSpec((tn, tk), lambda j, i, kk: (j, kk)) if form == "nt"
              else pl.BlockSpec((tk, tn), lambda j, i, kk: (kk, j)))
    o_spec = pl.BlockSpec((tm, tn), lambda j, i, kk: (i, j))
    col_spec = pl.BlockSpec((1, tn), lambda j, i, kk: (0, j))
    has_acc = acc is not None

    assert nk == 1 or (out_dtype == F32 and fuse is None)

    def body(*refs):
        a_ref, b_ref = refs[0], refs[1]
        prod = lax.dot_general(a_ref[...], b_ref[...], (((ca,), (cb,)), ((), ())), preferred_element_type=F32)
        if fuse is not None and fuse[0] == "relu2":
            pre = prod + refs[2][...]
            refs[3][...] = jnp.square(jnp.maximum(pre, 0.0)).astype(BF16)
            refs[4][...] = pre.astype(BF16)
            return
        if fuse is not None and fuse[0] == "relu2_bwd":
            d = prod * (2.0 * jnp.maximum(refs[2][...].astype(F32), 0.0))
            refs[3][...] = d.astype(BF16)

            @pl.when(pl.program_id(1) == 0)
            def _():
                refs[4][...] = jnp.zeros_like(refs[4])

            refs[4][...] += jnp.sum(d, axis=0, keepdims=True)
            return
        o_ref = refs[-1]
        if nk == 1:
            o_ref[...] = (prod + refs[2][...] if has_acc else prod).astype(out_dtype)
            return
        kk = pl.program_id(2)

        @pl.when(kk == 0)
        def _():
            o_ref[...] = prod + refs[2][...] if has_acc else prod

        @pl.when(kk > 0)
        def _():
            o_ref[...] += prod

    ins = [a.astype(BF16), b.astype(BF16)] + ([acc] if has_acc else [])
    in_specs = [a_spec, b_spec] + ([o_spec] if has_acc else [])
    out_specs, out_shape = o_spec, jax.ShapeDtypeStruct((m, n), out_dtype)
    if fuse is not None and fuse[0] == "relu2":
        ins, in_specs = ins + [fuse[1]], in_specs + [col_spec]
        out_specs, out_shape = [o_spec, o_spec], [jax.ShapeDtypeStruct((m, n), BF16)] * 2
    elif fuse is not None:
        ins, in_specs = ins + [fuse[1]], in_specs + [o_spec]
        out_specs = [o_spec, col_spec]
        out_shape = [jax.ShapeDtypeStruct((m, n), BF16), jax.ShapeDtypeStruct((1, n), F32)]
    return pl.pallas_call(
        body, name=name, grid=(n // tn, m // tm, nk),
        in_specs=in_specs, out_specs=out_specs, out_shape=out_shape,
        input_output_aliases={2: 0} if has_acc else {},
        compiler_params=_cparams(3),
    )(*ins)


MESH_ID = pl.DeviceIdType.MESH
ANY_SPEC = pl.BlockSpec(memory_space=pl.ANY)


def _place():
    return lax.axis_index("x"), lax.axis_index("y"), lax.axis_index("c")


def _chips():
    x, y, c = _place()
    return x, y, c, 2 * x + y, [(1 - x, y), (x, 1 - y), (1 - x, 1 - y)]


def scatter_rider(arrs):
    n = len(arrs)

    def remote(ins, outs, sems, i, j, landing_from_me):
        x, y, c, me, others = _chips()
        ox, oy = others[j]
        peer = 2 * ox + oy
        return pltpu.make_async_remote_copy(src_ref=ins[i].at[peer], dst_ref=outs[i].at[me if landing_from_me else peer],
                                            send_sem=sems[0].at[3 * i + j], recv_sem=sems[1].at[3 * i + j],
                                            device_id=(ox, oy, c), device_id_type=MESH_ID)

    def local(ins, outs, sems, i):
        me = _chips()[3]
        return pltpu.make_async_copy(ins[i].at[me], outs[i].at[me], sems[2].at[i])

    def start(ins, outs, sems):
        for i in range(n):
            local(ins, outs, sems, i).start()
            for j in range(3):
                remote(ins, outs, sems, i, j, True).start()

    def finish(ins, outs, sems):
        for i in range(n):
            for j in range(3):
                remote(ins, outs, sems, i, j, False).wait_recv()
        for i in range(n):
            for j in range(3):
                remote(ins, outs, sems, i, j, True).wait_send()
            local(ins, outs, sems, i).wait()

    return dict(ins=list(arrs), out_shape=[jax.ShapeDtypeStruct(a.shape, a.dtype) for a in arrs],
                scratch=[pltpu.SemaphoreType.DMA((3 * n,)), pltpu.SemaphoreType.DMA((3 * n,)), pltpu.SemaphoreType.DMA((n,))],
                start=start, middle=lambda ins, outs, sems: None, finish=finish)


def gather_rider(split, flat):
    ns, n = len(split), len(split) + len(flat)
    arrs = list(split) + list(flat)
    half = [a.shape[0] // 2 for a in split]

    def ici(ins, outs, sems, i, j, landing_from_me):
        x, y, c, me, others = _chips()
        ox, oy = others[j]
        slot = me if landing_from_me else 2 * ox + oy
        if i < ns:
            rows = pl.ds(c * half[i], half[i])
            src, dst = ins[i].at[rows], outs[i].at[slot, rows]
        else:
            src, dst = ins[i], outs[i].at[slot]
        return pltpu.make_async_remote_copy(src_ref=src, dst_ref=dst, send_sem=sems[0].at[3 * i + j],
                                            recv_sem=sems[1].at[3 * i + j], device_id=(ox, oy, c), device_id_type=MESH_ID)

    def d2d(outs, sems, i, j, mine):
        x, y, c, me, others = _chips()
        ox, oy = others[j]
        ref = outs[i].at[2 * ox + oy, pl.ds((c if mine else 1 - c) * half[i], half[i])]
        return pltpu.make_async_remote_copy(src_ref=ref, dst_ref=ref, send_sem=sems[2].at[3 * i + j],
                                            recv_sem=sems[3].at[3 * i + j], device_id=(x, y, 1 - c), device_id_type=MESH_ID)

    def local(ins, outs, sems, i):
        return pltpu.make_async_copy(ins[i], outs[i].at[_chips()[3]], sems[4].at[i])

    def start(ins, outs, sems):
        for i in range(n):
            local(ins, outs, sems, i).start()
            for j in range(3):
                ici(ins, outs, sems, i, j, True).start()

    def middle(ins, outs, sems):
        for i in range(n):
            for j in range(3):
                ici(ins, outs, sems, i, j, False).wait_recv()
                if i < ns:
                    d2d(outs, sems, i, j, True).start()

    def finish(ins, outs, sems):
        for i in range(ns):
            for j in range(3):
                d2d(outs, sems, i, j, False).wait_recv()
        for i in range(n):
            for j in range(3):
                ici(ins, outs, sems, i, j, True).wait_send()
                if i < ns:
                    d2d(outs, sems, i, j, True).wait_send()
            local(ins, outs, sems, i).wait()

    return dict(ins=arrs, out_shape=[jax.ShapeDtypeStruct((N_CHIPS,) + a.shape, a.dtype) for a in arrs],
                scratch=[pltpu.SemaphoreType.DMA((3 * n,)), pltpu.SemaphoreType.DMA((3 * n,)),
                         pltpu.SemaphoreType.DMA((3 * max(ns, 1),)), pltpu.SemaphoreType.DMA((3 * max(ns, 1),)),
                         pltpu.SemaphoreType.DMA((n,))],
                start=start, middle=middle, finish=finish)


def run_alone(name, ride):
    n_in, n_out = len(ride["ins"]), len(ride["out_shape"])

    def body(*refs):
        parts = refs[:n_in], refs[n_in:n_in + n_out], refs[n_in + n_out:]
        ride["start"](*parts)
        ride["middle"](*parts)
        ride["finish"](*parts)

    return pl.pallas_call(body, name=name, in_specs=[ANY_SPEC] * n_in, out_specs=[ANY_SPEC] * n_out,
                          out_shape=ride["out_shape"], scratch_shapes=ride["scratch"])(*ride["ins"])


def sibling_exchange(name, arrs):
    n = len(arrs)

    def body(*refs):
        ins, outs = refs[:n], refs[n:2 * n]
        send_sems, recv_sems = refs[2 * n:]
        x, y, c = _place()
        cps = [pltpu.make_async_remote_copy(src_ref=ins[i], dst_ref=outs[i], send_sem=send_sems.at[i], recv_sem=recv_sems.at[i],
                                            device_id=(x, y, 1 - c), device_id_type=MESH_ID) for i in range(n)]
        for cp in cps:
            cp.start()
        for cp in cps:
            cp.wait_recv()
        for cp in cps:
            cp.wait_send()

    return pl.pallas_call(
        body, name=name, in_specs=[ANY_SPEC] * n, out_specs=[ANY_SPEC] * n,
        out_shape=[jax.ShapeDtypeStruct(a.shape, a.dtype) for a in arrs],
        scratch_shapes=[pltpu.SemaphoreType.DMA((n,)), pltpu.SemaphoreType.DMA((n,))],
    )(*arrs)


def gather_all(name, vec):
    def body(in_ref, out_ref, send_sems, recv_sems, local_sem):
        x, y, c = _place()
        me = 4 * x + 2 * y + c

        def peer(mask):
            return (1 - x if mask & 4 else x, 1 - y if mask & 2 else y, 1 - c if mask & 1 else c)

        def remote(mask, landing_from_me):
            px, py, pc = peer(mask)
            slot = me if landing_from_me else 4 * px + 2 * py + pc
            return pltpu.make_async_remote_copy(src_ref=in_ref, dst_ref=out_ref.at[slot], send_sem=send_sems.at[mask - 1],
                                                recv_sem=recv_sems.at[mask - 1], device_id=(px, py, pc), device_id_type=MESH_ID)

        local = pltpu.make_async_copy(in_ref, out_ref.at[me], local_sem)
        local.start()
        sent = [remote(mask, True) for mask in range(1, N_DEV)]
        for cp in sent:
            cp.start()
        for mask in range(1, N_DEV):
            remote(mask, False).wait_recv()
        for cp in sent:
            cp.wait_send()
        local.wait()

    return pl.pallas_call(
        body, name=name, in_specs=[ANY_SPEC], out_specs=ANY_SPEC,
        out_shape=jax.ShapeDtypeStruct((N_DEV,) + vec.shape, vec.dtype),
        scratch_shapes=[pltpu.SemaphoreType.DMA((N_DEV - 1,)), pltpu.SemaphoreType.DMA((N_DEV - 1,)), pltpu.SemaphoreType.DMA],
    )(vec)


def sum_slots(name, arrs, tr):
    k, r, c = arrs[0].shape
    nb = r // tr

    def body(*refs):
        o_ref = refs[-1]
        for li, a_ref in enumerate(refs[:-1]):
            @pl.when(pl.program_id(0) == li)
            def _(a_ref=a_ref):
                acc = a_ref[0].astype(F32)
                for j in range(1, k):
                    acc = acc + a_ref[j].astype(F32)
                o_ref[...] = acc

    def in_spec(li):
        return pl.BlockSpec((k, tr, c), lambda l, i: (0, jnp.where(l == li, i, jnp.where(l < li, 0, nb - 1)), 0))

    return pl.pallas_call(
        body, name=name, grid=(len(arrs), nb),
        in_specs=[in_spec(li) for li in range(len(arrs))], out_specs=pl.BlockSpec((tr, c), lambda l, i: (l * nb + i, 0)),
        out_shape=jax.ShapeDtypeStruct((len(arrs) * r, c), F32), compiler_params=_cparams(2),
    )(*arrs)


def adamw(name, parts, w, m, v, tr):
    r, c = w.shape
    npart = len(parts)
    bc1 = 1.0 - ADAM_B1 ** ADAM_STEP
    bc2 = 1.0 - ADAM_B2 ** ADAM_STEP

    def body(*refs):
        g = refs[0][...]
        for j in range(1, npart):
            g = g + refs[j][...]
        w_ref, m_ref, v_ref, g_out, d_out, m_out, v_out = refs[npart:]
        m_new = ADAM_B1 * m_ref[...] + (1.0 - ADAM_B1) * g
        v_new = ADAM_B2 * v_ref[...] + (1.0 - ADAM_B2) * jnp.square(g)
        m_hat = m_new / bc1
        v_hat = v_new / bc2
        g_out[...] = g
        d_out[...] = -ADAM_LR * (m_hat / (jnp.sqrt(v_hat) + ADAM_EPS) + ADAM_WD * w_ref[...])
        m_out[...] = m_new
        v_out[...] = v_new

    spec = pl.BlockSpec((tr, c), lambda i: (i, 0))
    return pl.pallas_call(
        body, name=name, grid=(r // tr,), in_specs=[spec] * (npart + 3), out_specs=[spec] * 4,
        out_shape=[jax.ShapeDtypeStruct((r, c), F32)] * 4, compiler_params=_cparams(),
    )(*parts, w, m, v)


ROW_TM = 256


def _row(v):
    return v.reshape(1, -1)


def _lane_pad(vals, at):
    return jnp.concatenate([jnp.zeros((at,), F32), vals, jnp.zeros((LANES - at - vals.shape[0],), F32)]).reshape(1, LANES)


def _layer_params(sm, l):
    return dict(
        conf=[sm["conv_dw_w"][l], _row(sm["conv_dw_b"][l]), _row(sm["conv_ln_g"][l]), _row(sm["conv_ln_b"][l])],
        pre=[sm["gdn_conv_q"][l], sm["gdn_conv_k"][l], sm["gdn_conv_v"][l],
             _lane_pad(sm["gdn_a_log"][l], GDN_HEADS), _lane_pad(sm["gdn_dt_bias"][l], GDN_HEADS)],
        scan=_row(sm["gdn_norm_g"][l]),
        sgu=[_row(sm["sgu_ln_g"][l]), _row(sm["sgu_ln_b"][l]), sm["sgu_w_s"][l].reshape(SGU_GROUPS * SGU_CHUNK, SGU_CHUNK),
             sm["sgu_b_s"][l].T],
        mix=[_row(sm["b_gate"][l])],
        ln1=[_row(sm["ln1_g"][l]), _row(sm["ln1_b"][l])],
        ff=[_row(sm["b_ff1"][l])],
        ln2=[_row(sm["b_ff2"][l]), _row(sm["ln2_g"][l]), _row(sm["ln2_b"][l])],
    )


def _split_w_in(w):
    zpad = jnp.zeros((D_MODEL, LANES - 2 * GDN_HEADS), w.dtype)
    return dict(A=w[:, 0:1024], B=jnp.concatenate([w[:, 1024:2560], w[:, 3072:3080], zpad], axis=1), Z=w[:, 2560:3072],
                C=w[:, 3080:4104], G=w[:, 4104:7176])


def _join_w_in(d):
    return jnp.concatenate([d["A"], d["B"][:, 0:1536], d["Z"], d["B"][:, 1536:1544], d["C"], d["G"]], axis=1)


def layer_forward(l, xin, xin_b, big, lp, ride=None):
    tag = f"l{l}_"
    wi = big["w_in"][l]
    sv = dict(xin=xin, xin_b=xin_b)
    for r in "GACBZ":
        sv["p" + r] = matmul(tag + "proj_" + r, xin_b, wi[r], "nn", F32)
    (sv["ya_in"],) = halo_fwd(tag + "conformer", f_conformer, ROW_TM, CONV_HALO, sv["pA"], lp["conf"], [[BF16]])
    sv["qn"], sv["kn"], sv["vv"], sv["gb"] = halo_fwd(tag + "gdn_pre", f_gdn_pre, ROW_TM, GDN_HALO, sv["pB"], lp["pre"], [[F32]] * 4)
    prep_out = prep_fwd(tag + "gdn_prep", sv["qn"], sv["kn"], sv["vv"], sv["gb"], ride)
    prep, sv["tinv"], rode = prep_out[:6], prep_out[6], prep_out[7:]
    sv["prep"] = prep
    sv["yb_in"], sv["states"] = scan_fwd(tag + "gdn_scan", lp["scan"], *prep, sv["pZ"])
    (sv["yc_in"],) = stage_fwd(tag + "sgu", f_sgu, ROW_TM, [sv["pC"]], lp["sgu"], [[BF16]])
    sv["ya"] = matmul(tag + "out_a", sv["ya_in"], big["w_pa"][l], "nn", F32)
    sv["yb"] = matmul(tag + "out_b", sv["yb_in"], big["w_pb"][l], "nn", F32)
    sv["yc"] = matmul(tag + "out_c", sv["yc_in"], big["w_pc"][l], "nn", F32)
    (sv["mix"],) = stage_fwd(tag + "mix", f_mix, ROW_TM, [sv["ya"], sv["yb"], sv["yc"], sv["pG"]], lp["mix"], [[BF16]])
    sv["m"] = matmul(tag + "w_o", sv["mix"], big["w_o"][l], "nn", F32)
    sv["x1"], sv["x1_b"] = stage_fwd(tag + "ln1", f_ln1, ROW_TM, [xin, sv["m"]], lp["ln1"], [[F32, BF16]])
    sv["h"], sv["hpre"] = matmul(tag + "ff1", sv["x1_b"], big["w_ff1"][l], "nn", BF16, fuse=("relu2", lp["ff"][0]))
    sv["f2"] = matmul(tag + "ff2", sv["h"], big["w_ff2"][l], "nn", F32)
    x2, x2_b = stage_fwd(tag + "ln2", f_ln2, ROW_TM, [sv["x1"], sv["f2"]], lp["ln2"], [[F32, BF16]])
    return x2, x2_b, sv, rode


def layer_backward(l, dx2, sv, big, lp, ride=None):
    tag = f"l{l}_b_"
    g = {}
    dx1_a, df2, db2, dg2, dbe2 = stage_bwd(tag + "ln2", f_ln2, ROW_TM, [sv["x1"], sv["f2"]], lp["ln2"], [dx2], [F32, BF16])
    g["b_ff2"], g["ln2_g"], g["ln2_b"] = db2[0], dg2[0], dbe2[0]
    g["w_ff2"] = matmul(tag + "dw_ff2", sv["h"], df2, "tn", F32)
    dhpre, db1 = matmul(tag + "dx_ff2", df2, big["w_ff2"][l], "nt", BF16, fuse=("relu2_bwd", sv["hpre"]))
    g["b_ff1"] = db1[0]
    g["w_ff1"] = matmul(tag + "dw_ff1", sv["x1_b"], dhpre, "tn", F32)
    dx1_b = matmul(tag + "dx_ff1", dhpre, big["w_ff1"][l], "nt", F32)
    dxin_a, dm, dg1, dbe1 = stage_bwd(tag + "ln1", f_ln1, ROW_TM, [sv["xin"], sv["m"]], lp["ln1"], [[dx1_a, dx1_b]], [F32, BF16])
    g["ln1_g"], g["ln1_b"] = dg1[0], dbe1[0]
    g["w_o"] = matmul(tag + "dw_o", sv["mix"], dm, "tn", F32)
    dmix = matmul(tag + "dx_o", dm, big["w_o"][l], "nt", F32)
    dya, dyb, dyc, dp_g, dbg = stage_bwd(tag + "mix", f_mix, ROW_TM, [sv["ya"], sv["yb"], sv["yc"], sv["pG"]], lp["mix"],
                                         [[dmix]], [BF16] * 4)
    g["b_gate"] = dbg[0]
    g["w_pa"] = matmul(tag + "dw_pa", sv["ya_in"], dya, "tn", F32)
    g["w_pb"] = matmul(tag + "dw_pb", sv["yb_in"], dyb, "tn", F32)
    g["w_pc"] = matmul(tag + "dw_pc", sv["yc_in"], dyc, "tn", F32)
    dya_in = matmul(tag + "dx_pa", dya, big["w_pa"][l], "nt", F32)
    dyb_in = matmul(tag + "dx_pb", dyb, big["w_pb"][l], "nt", F32)
    dyc_in = matmul(tag + "dx_pc", dyc, big["w_pc"][l], "nt", F32)
    dp_a, dcw, dcb, dcg, dcbe, *rode = halo_bwd(tag + "conformer", f_conformer, ROW_TM, CONV_HALO, sv["pA"], lp["conf"], [dya_in],
                                                BF16, ride)
    g["conv_dw_w"], g["conv_dw_b"], g["conv_ln_g"], g["conv_ln_b"] = dcw, dcb[0], dcg[0], dcbe[0]
    dp_c, dsg, dsb, dsw, dsbs = stage_bwd(tag + "sgu", f_sgu, ROW_TM, [sv["pC"]], lp["sgu"], [[dyc_in]], [BF16])
    g["sgu_ln_g"], g["sgu_ln_b"] = dsg[0], dsb[0]
    g["sgu_w_s"] = dsw.reshape(SGU_GROUPS, SGU_CHUNK, SGU_CHUNK)
    g["sgu_b_s"] = dsbs.T
    *dprep, dp_z, dng = scan_bwd(tag + "gdn_scan", lp["scan"], sv["states"], *sv["prep"], sv["pZ"], dyb_in)
    g["gdn_norm_g"] = dng[0]
    dqn, dkn, dvv, dgb = prep_bwd(tag + "gdn_prep", sv["qn"], sv["kn"], sv["vv"], sv["gb"], sv["tinv"], dprep)
    dp_b, dq, dk, dv, dal, ddt = halo_bwd(tag + "gdn_pre", f_gdn_pre, ROW_TM, GDN_HALO, sv["pB"], lp["pre"], [dqn, dkn, dvv, dgb], BF16)
    g["gdn_conv_q"], g["gdn_conv_k"], g["gdn_conv_v"] = dq, dk, dv
    g["gdn_a_log"] = dal[0, GDN_HEADS:2 * GDN_HEADS]
    g["gdn_dt_bias"] = ddt[0, GDN_HEADS:2 * GDN_HEADS]
    dps = dict(G=dp_g, A=dp_a, C=dp_c, B=dp_b, Z=dp_z)
    wi = big["w_in"][l]
    g["w_in"] = _join_w_in({r: matmul(tag + "dw_in_" + r, sv["xin_b"], dps[r], "tn", F32) for r in "GACBZ"})
    dxin_b = None
    for r in "GACBZ":
        dxin_b = matmul(tag + "dx_in_" + r, dps[r], wi[r], "nt", F32, acc=dxin_b)
    return [dxin_a, dxin_b], g, rode


SMALL_PER_LAYER = ("b_gate", "conv_dw_w", "conv_dw_b", "conv_ln_g", "conv_ln_b", "gdn_conv_q", "gdn_conv_k", "gdn_conv_v",
                   "gdn_a_log", "gdn_dt_bias", "gdn_norm_g", "sgu_ln_g", "sgu_ln_b", "sgu_w_s", "sgu_b_s", "ln1_g", "ln1_b",
                   "b_ff1", "b_ff2", "ln2_g", "ln2_b")
BIG = ("w_in", "w_pa", "w_pb", "w_pc", "w_o", "w_ff1", "w_ff2")


def local_step(x, target, big, sm, fetch_last=None, send_last=None):
    lps = [_layer_params(sm, l) for l in range(DEPTH)]
    p_in = [_row(sm["ln_in_g"]), _row(sm["ln_in_b"])]
    xc, xc_b = stage_fwd("ln_in", f_ln_in, ROW_TM, [x], p_in, [[F32, BF16]])
    saved = []
    for l in range(DEPTH):
        xc, xc_b, sv, rode = layer_forward(l, xc, xc_b, big, lps[l], fetch_last[0] if fetch_last and l == 0 else None)
        if fetch_last and l == 0:
            for n, wl in fetch_last[1](rode).items():
                big[n][DEPTH - 1] = wl
        saved.append(sv)
    loss_part, dy = loss_stage("loss", xc, target, ROW_TM)
    dx = [dy]
    per_layer = [None] * DEPTH
    sent = []
    for l in reversed(range(DEPTH)):
        ride = send_last({n: per_layer[DEPTH - 1][n] for n in BIG}) if send_last and l == 0 else None
        dx, per_layer[l], rode = layer_backward(l, dx, saved[l], big, lps[l], ride)
        if ride is not None:
            sent = rode
    grad_x, dgi, dbi = stage_bwd("ln_in_b", f_ln_in, ROW_TM, [x], p_in, [dx], [F32])
    small = {n: jnp.stack([per_layer[l][n] for l in range(DEPTH)]) for n in SMALL_PER_LAYER}
    small["ln_in_g"], small["ln_in_b"] = dgi[0], dbi[0]
    return loss_part, grad_x, small, [{n: per_layer[l][n] for n in BIG} for l in range(DEPTH)], sent


PARAM_NAMES = ("ln_in_g", "ln_in_b", "w_in", "b_gate", "conv_dw_w", "conv_dw_b", "conv_ln_g", "conv_ln_b", "w_pa",
               "gdn_conv_q", "gdn_conv_k", "gdn_conv_v", "gdn_a_log", "gdn_dt_bias", "gdn_norm_g", "w_pb", "sgu_ln_g",
               "sgu_ln_b", "sgu_w_s", "sgu_b_s", "w_pc", "w_o", "ln1_g", "ln1_b", "w_ff1", "b_ff1", "w_ff2", "b_ff2",
               "ln2_g", "ln2_b")
SMALL = tuple(n for n in PARAM_NAMES if n not in BIG)
COL_SHARDED_SMALL = ("conv_dw_w", "gdn_conv_q", "gdn_conv_k", "gdn_conv_v")
ROW_SHARDED_BIG = ("w_o", "w_ff2")
ADAM_ROWS = 128


def _unshard(g, rows):
    n, r, c = g.shape
    if rows:
        return g.reshape(n * r, c)
    return jnp.moveaxis(g, 0, 1).reshape(r, n * c)


def _to_shards(w, rows):
    r, c = w.shape
    if rows:
        return w.reshape(N_CHIPS, r // N_CHIPS, c)
    return jnp.moveaxis(w.reshape(r, N_CHIPS, c // N_CHIPS), 1, 0)


def _pack(vals, rows_multiple=8):
    flat = jnp.concatenate([v.reshape(-1) for v in vals])
    per = LANES * rows_multiple
    padded = -(-flat.shape[0] // per) * per
    return jnp.pad(flat, (0, padded - flat.shape[0])).reshape(-1, LANES)


def _unpack(packed, shapes):
    flat = packed.reshape(-1)
    out, at = [], 0
    for s in shapes:
        size = 1
        for d in s:
            size *= d
        out.append(flat[at:at + size].reshape(s))
        at += size
    return out


def kernel(x, ln_in_g, ln_in_b, w_in, b_gate, conv_dw_w, conv_dw_b, conv_ln_g, conv_ln_b, w_pa, gdn_conv_q, gdn_conv_k, gdn_conv_v, gdn_a_log, gdn_dt_bias, gdn_norm_g, w_pb, sgu_ln_g, sgu_ln_b, sgu_w_s, sgu_b_s, w_pc, w_o, ln1_g, ln1_b, w_ff1, b_ff1, w_ff2, b_ff2, ln2_g, ln2_b, loss_target, m_ln_in_g, m_ln_in_b, m_w_in, m_b_gate, m_conv_dw_w, m_conv_dw_b, m_conv_ln_g, m_conv_ln_b, m_w_pa, m_gdn_conv_q, m_gdn_conv_k, m_gdn_conv_v, m_gdn_a_log, m_gdn_dt_bias, m_gdn_norm_g, m_w_pb, m_sgu_ln_g, m_sgu_ln_b, m_sgu_w_s, m_sgu_b_s, m_w_pc, m_w_o, m_ln1_g, m_ln1_b, m_w_ff1, m_b_ff1, m_w_ff2, m_b_ff2, m_ln2_g, m_ln2_b, v_ln_in_g, v_ln_in_b, v_w_in, v_b_gate, v_conv_dw_w, v_conv_dw_b, v_conv_ln_g, v_conv_ln_b, v_w_pa, v_gdn_conv_q, v_gdn_conv_k, v_gdn_conv_v, v_gdn_a_log, v_gdn_dt_bias, v_gdn_norm_g, v_w_pb, v_sgu_ln_g, v_sgu_ln_b, v_sgu_w_s, v_sgu_b_s, v_w_pc, v_w_o, v_ln1_g, v_ln1_b, v_w_ff1, v_b_ff1, v_w_ff2, v_b_ff2, v_ln2_g, v_ln2_b):
    w = dict(zip(PARAM_NAMES, (ln_in_g, ln_in_b, w_in, b_gate, conv_dw_w, conv_dw_b, conv_ln_g, conv_ln_b, w_pa, gdn_conv_q, gdn_conv_k, gdn_conv_v, gdn_a_log, gdn_dt_bias, gdn_norm_g, w_pb, sgu_ln_g, sgu_ln_b, sgu_w_s, sgu_b_s, w_pc, w_o, ln1_g, ln1_b, w_ff1, b_ff1, w_ff2, b_ff2, ln2_g, ln2_b)))
    m = dict(zip(PARAM_NAMES, (m_ln_in_g, m_ln_in_b, m_w_in, m_b_gate, m_conv_dw_w, m_conv_dw_b, m_conv_ln_g, m_conv_ln_b, m_w_pa, m_gdn_conv_q, m_gdn_conv_k, m_gdn_conv_v, m_gdn_a_log, m_gdn_dt_bias, m_gdn_norm_g, m_w_pb, m_sgu_ln_g, m_sgu_ln_b, m_sgu_w_s, m_sgu_b_s, m_w_pc, m_w_o, m_ln1_g, m_ln1_b, m_w_ff1, m_b_ff1, m_w_ff2, m_b_ff2, m_ln2_g, m_ln2_b)))
    v = dict(zip(PARAM_NAMES, (v_ln_in_g, v_ln_in_b, v_w_in, v_b_gate, v_conv_dw_w, v_conv_dw_b, v_conv_ln_g, v_conv_ln_b, v_w_pa, v_gdn_conv_q, v_gdn_conv_k, v_gdn_conv_v, v_gdn_a_log, v_gdn_dt_bias, v_gdn_norm_g, v_w_pb, v_sgu_ln_g, v_sgu_ln_b, v_sgu_w_s, v_sgu_b_s, v_w_pc, v_w_o, v_ln1_g, v_ln1_b, v_w_ff1, v_b_ff1, v_w_ff2, v_b_ff2, v_ln2_g, v_ln2_b)))
    chip = 2 * lax.axis_index("x") + lax.axis_index("y")

    shards = {n: w[n].astype(BF16) for n in BIG}
    col_small_shapes = [w[n].shape for n in COL_SHARDED_SMALL]
    first = run_alone("gather_first", gather_rider([shards[n][0] for n in BIG], [_pack([w[n] for n in COL_SHARDED_SMALL])]))

    def whole_layer(gathered):
        wl = {n: _unshard(g, n in ROW_SHARDED_BIG) for n, g in zip(BIG, gathered)}
        wl["w_in"] = _split_w_in(wl["w_in"])
        return wl

    big = {n: [wl] + [None] * (DEPTH - 1) for n, wl in whole_layer(first[:len(BIG)]).items()}
    sm = {n: w[n] for n in SMALL}
    per_chip = [_unpack(first[-1][s], col_small_shapes) for s in range(N_CHIPS)]
    for i, n in enumerate(COL_SHARDED_SMALL):
        sm[n] = jnp.concatenate([per_chip[s][i] for s in range(N_CHIPS)], axis=-1)

    def send(g):
        return scatter_rider([_to_shards(g[n], n in ROW_SHARDED_BIG).astype(BF16) for n in BIG])

    fetch_last = (gather_rider([shards[n][DEPTH - 1] for n in BIG], []), whole_layer)
    loss_part, grad_x, grads, g_big, received_last = local_step(x[0], loss_target[0], big, sm, fetch_last, send)

    received_first = run_alone("scatter_first", send(g_big[0]))
    partial = [sum_slots("sum_chips_" + n, [r0, r1], ADAM_ROWS) for n, r0, r1 in zip(BIG, received_first, received_last)]
    sibling = sibling_exchange("swap_cores", partial)
    out = {}
    for n, mine, theirs in zip(BIG, partial, sibling):
        shape = w[n].shape
        res = adamw("adamw_" + n, [mine, theirs], w[n].reshape(-1, shape[-1]), m[n].reshape(-1, shape[-1]),
                    v[n].reshape(-1, shape[-1]), ADAM_ROWS)
        out[n] = [r.reshape(shape) for r in res]

    small_shapes = [grads[n].shape for n in SMALL]
    vec = _pack([grads[n] for n in SMALL] + [jnp.sum(loss_part).reshape(1)])
    total = sum_slots("sum_small", [gather_all("gather_small", vec)], 8)
    whole = _unpack(total, small_shapes + [(1,)])
    loss = whole[-1][0]
    g_small = {}
    for n, g in zip(SMALL, whole[:-1]):
        if n in COL_SHARDED_SMALL:
            width = g.shape[-1] // N_CHIPS
            g = lax.dynamic_slice_in_dim(g, chip * width, width, axis=g.ndim - 1)
        g_small[n] = g
    local_shapes = [w[n].shape for n in SMALL]
    res = adamw("adamw_small", [_pack([g_small[n] for n in SMALL])], _pack([w[n] for n in SMALL]),
                _pack([m[n] for n in SMALL]), _pack([v[n] for n in SMALL]), 8)
    unpacked = [_unpack(r, local_shapes) for r in res]
    for i, n in enumerate(SMALL):
        out[n] = [unpacked[k][i] for k in range(4)]

    return (loss, grad_x[None], *[out[n][0] for n in PARAM_NAMES], *[out[n][1] for n in PARAM_NAMES],
            *[out[n][2] for n in PARAM_NAMES], *[out[n][3] for n in PARAM_NAMES])
```

```python
import functools

import jax
import jax.numpy as jnp
from jax import lax
from jax.experimental import pallas as pl
from jax.experimental.pallas import tpu as pltpu

F32 = jnp.float32
BF16 = jnp.bfloat16

D_MODEL = 1024
DEPTH = 2
CONV_DIM = 512
CONV_WIDTH = 31
CONV_HALO = 32
GDN_HEADS = 4
GDN_HD = 128
GDN_CHUNK = 64
GDN_PAIR = 2 * GDN_CHUNK
GDN_CONV = 4
GDN_HALO = 8
SGU_GROUPS = 4
SGU_GD = 128
SGU_CHUNK = 128
D_FF = 4096
ALPHA = (2 * DEPTH) ** 0.25
LN_EPS = 1e-5
RMS_EPS = 1e-6
ADAM_LR, ADAM_B1, ADAM_B2, ADAM_EPS, ADAM_WD, ADAM_STEP = 0.001, 0.9, 0.999, 1e-08, 0.01, 10
N_CHIPS = 4
N_DEV = 8
LANES = 128
VMEM_LIMIT = 52 * 1024 * 1024

W_GATE, W_A, W_C, W_B, W_Z = 3 * D_MODEL, 2 * CONV_DIM, 2 * 512, 3 * 512 + LANES, 512


def _dg(a, b, ca, cb):
    return lax.dot_general(a.astype(BF16), b.astype(BF16), (((ca,), (cb,)), ((), ())), preferred_element_type=F32)


@jax.custom_vjp
def mm_nn(a, b):
    return _dg(a, b, 1, 0)


mm_nn.defvjp(lambda a, b: (_dg(a, b, 1, 0), (a, b)), lambda r, g: (_dg(g, r[1], 1, 1), _dg(r[0], g, 0, 0)))


@jax.custom_vjp
def mm_nt(a, b):
    return _dg(a, b, 1, 1)


mm_nt.defvjp(lambda a, b: (_dg(a, b, 1, 1), (a, b)), lambda r, g: (_dg(g, r[1], 1, 0), _dg(g, r[0], 0, 0)))


@jax.custom_vjp
def mm_tn(a, b):
    return _dg(a, b, 0, 0)


mm_tn.defvjp(lambda a, b: (_dg(a, b, 0, 0), (a, b)), lambda r, g: (_dg(r[1], g, 1, 1), _dg(r[0], g, 1, 0)))


def _ln(x, g, b):
    mu = jnp.mean(x, -1, keepdims=True)
    xc = x - mu
    var = jnp.mean(xc * xc, -1, keepdims=True)
    return xc * lax.rsqrt(var + LN_EPS) * g + b


def _iota2(shape, dim):
    return lax.broadcasted_iota(jnp.int32, shape, dim)


def f_ln_in(params, x):
    g, b = params
    return (_ln(x, g, b),)


def _roll_rows(x, s):
    return x if s == 0 else pltpu.roll(x, s, 0)


@functools.partial(jax.custom_vjp, nondiff_argnums=(1,))
def shift_rows(x, s):
    return _roll_rows(x, s)


shift_rows.defvjp(lambda x, s: (_roll_rows(x, s), None),
                  lambda s, _, ct: (_roll_rows(ct, (ct.shape[0] - s) % ct.shape[0]),))


def _causal_conv(full, w, halo):
    width = w.shape[0]
    groups = []
    for g in range(full.shape[1] // LANES):
        ls = slice(g * LANES, (g + 1) * LANES)
        xg, wg = full[:, ls], w[:, ls]
        acc = wg[width - 1:width, :] * xg[halo:, :]
        for j in range(width - 1):
            acc = acc + wg[j:j + 1, :] * shift_rows(xg, width - 1 - j)[halo:, :]
        groups.append(acc)
    return jnp.concatenate(groups, axis=1)


def f_conformer(params, halo, pa):
    w, b, g, be = params
    full = jnp.concatenate([halo, pa], axis=0)
    h = full[:, :CONV_DIM] * jax.nn.sigmoid(full[:, CONV_DIM:])
    return (jax.nn.silu(_ln(_causal_conv(h, w, CONV_HALO) + b, g, be)),)


def f_gdn_pre(params, halo, pb):
    cq, ck, cv, alog, dtb = params
    full = jnp.concatenate([halo, pb], axis=0)

    def conv(cols, w):
        return jax.nn.silu(_causal_conv(cols, w, GDN_HALO))

    def l2n(t, scale):
        parts = []
        for h in range(GDN_HEADS):
            th = t[:, h * GDN_HD:(h + 1) * GDN_HD]
            parts.append(th * (lax.rsqrt(jnp.sum(th * th, -1, keepdims=True) + RMS_EPS) * scale))
        return jnp.concatenate(parts, axis=1)

    q = l2n(conv(full[:, 0:512], cq), GDN_HD ** -0.5)
    k = l2n(conv(full[:, 512:1024], ck), 1.0)
    v = conv(full[:, 1024:1536], cv)
    logits = pb[:, 1536:1536 + LANES]
    z = logits + dtb
    softplus = jnp.maximum(z, 0.0) + jnp.log(1.0 + jnp.exp(-jnp.abs(z)))
    g = -jnp.exp(alog) * softplus
    beta = jax.nn.sigmoid(logits)
    lane = _iota2(logits.shape, 1)
    gb = jnp.where(lane < GDN_HEADS, beta, jnp.where(lane < 2 * GDN_HEADS, g, 0.0))
    return q, k, v, gb


def _split2(x):
    hi = x.astype(BF16)
    return hi, (x - hi.astype(F32)).astype(BF16)


def _dg3(a, b, ca, cb):
    ah, al = _split2(a)
    bh, bl = _split2(b)
    d = lambda p, q: lax.dot_general(p, q, (((ca,), (cb,)), ((), ())), preferred_element_type=F32)
    return d(ah, bh) + (d(ah, bl) + d(al, bh))


def _inv_unit_lower(lows):
    n = lows[0].shape[0]
    eye = jnp.where(_iota2((n, n), 0) == _iota2((n, n), 1), 1.0, 0.0).astype(F32)
    ps = [-low for low in lows]
    xs = [eye + p for p in ps]
    for _ in range(5):
        ps = [_dg3(p, p, 1, 0) for p in ps]
        xs = [x + _dg3(x, p, 1, 0) for x, p in zip(xs, ps)]
    return xs


def _pair_masks():
    n = GDN_PAIR
    r, c = _iota2((n, n), 0), _iota2((n, n), 1)
    same = (r >= GDN_CHUNK) == (c >= GDN_CHUNK)
    return same, jnp.logical_and(same, r >= c), jnp.logical_and(same, r > c), jnp.logical_and(same, r <= c)


def _masked_sum(mask, g):
    mk = jnp.where(mask, 1.0, 0.0).astype(BF16)
    g1 = g.astype(BF16)
    r1 = g - g1.astype(F32)
    g2 = r1.astype(BF16)
    g3 = (r1 - g2.astype(F32)).astype(BF16)
    d = lambda q: jnp.dot(mk, q, preferred_element_type=F32)
    return d(g1) + (d(g2) + d(g3))


@jax.custom_vjp
def chunk_cumsum(g):
    return _masked_sum(_pair_masks()[1], g)


chunk_cumsum.defvjp(lambda g: (_masked_sum(_pair_masks()[1], g), None), lambda _, ct: (_masked_sum(_pair_masks()[3], ct),))


def _prep_gates(gb):
    gam_all = chunk_cumsum(gb)
    first = _iota2((GDN_PAIR, LANES), 0) < GDN_CHUNK
    glast = jnp.where(first, gam_all[GDN_CHUNK - 1:GDN_CHUNK, :], gam_all[GDN_PAIR - 1:GDN_PAIR, :])
    return gam_all, gam_all.T, glast


def _head_gates(gb, gam_all, gam_t, h, causal):
    gc = gam_all[:, GDN_HEADS + h:GDN_HEADS + h + 1]
    gr = gam_t[GDN_HEADS + h:GDN_HEADS + h + 1, :]
    return gc, gb[:, h:h + 1], jnp.exp(jnp.where(causal, gc - gr, -jnp.inf))


def f_prep_low(kn, gb):
    _, causal, strict, _ = _pair_masks()
    gam_all, gam_t, _ = _prep_gates(gb)
    lows = []
    for h in range(GDN_HEADS):
        k = kn[:, h * GDN_HD:(h + 1) * GDN_HD]
        _, beta, decay = _head_gates(gb, gam_all, gam_t, h, causal)
        lows.append(jnp.where(strict, beta * mm_nt(k, k) * decay, 0.0))
    return jnp.concatenate(lows, axis=1)


def f_prep_rest(qn, kn, vv, gb, tinv_all):
    _, causal, _, _ = _pair_masks()
    gam_all, gam_t, glast = _prep_gates(gb)
    us, ws, as_, qds, kds = [], [], [], [], []
    for h in range(GDN_HEADS):
        hs = slice(h * GDN_HD, (h + 1) * GDN_HD)
        q, k, v, tinv = qn[:, hs], kn[:, hs], vv[:, hs], tinv_all[:, hs]
        gc, beta, decay = _head_gates(gb, gam_all, gam_t, h, causal)
        eg = jnp.exp(gc)
        us.append(mm_nn(tinv, beta * v))
        ws.append(mm_nn(tinv, beta * k * eg))
        as_.append(mm_nt(q, k) * decay)
        qds.append(q * eg)
        kds.append(k * jnp.exp(glast[:, GDN_HEADS + h:GDN_HEADS + h + 1] - gc))
    cat = lambda xs: jnp.concatenate(xs, axis=1)
    return cat(us), cat(ws), cat(as_), cat(qds), cat(kds), jnp.exp(glast)


PREP_TM = 2 * GDN_PAIR


def _ride(ride, n_in, n_out, refs, n_steps):
    if ride is None:
        return
    nri, nro = len(ride["ins"]), len(ride["out_shape"])
    r_in = refs[n_in:n_in + nri]
    r_out = refs[n_in + nri + n_out:n_in + nri + n_out + nro]
    sems = refs[len(refs) - len(ride["scratch"]):]
    step = pl.program_id(0)

    @pl.when(step == 0)
    def _():
        ride["start"](r_in, r_out, sems)

    @pl.when(step == (2 * n_steps) // 3)
    def _():
        ride["middle"](r_in, r_out, sems)

    @pl.when(step == n_steps - 1)
    def _():
        ride["finish"](r_in, r_out, sems)


def _ride_args(ride):
    if ride is None:
        return [], [], [], [], []
    n = len(ride["ins"])
    return list(ride["ins"]), [ANY_SPEC] * n, [ANY_SPEC] * len(ride["out_shape"]), list(ride["out_shape"]), list(ride["scratch"])


def prep_fwd(name, qn, kn, vv, gb, ride=None):
    t = qn.shape[0]
    hw = GDN_HEADS * GDN_HD
    n_steps = t // PREP_TM
    r_ins, r_in_specs, r_out_specs, r_out_shape, r_scratch = _ride_args(ride)

    def body(*refs):
        q_ref, k_ref, v_ref, gb_ref = refs[:4]
        outs = refs[4 + len(r_ins):4 + len(r_ins) + 7]
        pairs = [pl.ds(pi * GDN_PAIR, GDN_PAIR) for pi in range(PREP_TM // GDN_PAIR)]
        lows = [f_prep_low(k_ref[rs, :], gb_ref[rs, :]) for rs in pairs]
        invs = _inv_unit_lower([low[:, h * GDN_HD:(h + 1) * GDN_HD] for low in lows for h in range(GDN_HEADS)])
        for pi, rs in enumerate(pairs):
            tinv = jnp.concatenate(invs[pi * GDN_HEADS:(pi + 1) * GDN_HEADS], axis=1)
            res = f_prep_rest(q_ref[rs, :], k_ref[rs, :], v_ref[rs, :], gb_ref[rs, :], tinv)
            for o_ref, val in zip(outs, res + (tinv,)):
                o_ref[rs, :] = val
        _ride(ride, 4, 7, refs, n_steps)

    widths = [hw, hw, hw, hw, hw, LANES, hw]
    return pl.pallas_call(
        body, name=name, grid=(n_steps,),
        in_specs=[_row_spec(PREP_TM, hw)] * 3 + [_row_spec(PREP_TM, LANES)] + r_in_specs,
        out_specs=[_row_spec(PREP_TM, wd) for wd in widths] + r_out_specs,
        out_shape=[jax.ShapeDtypeStruct((t, wd), F32) for wd in widths] + r_out_shape,
        scratch_shapes=r_scratch, compiler_params=_cparams(),
    )(qn, kn, vv, gb, *r_ins)


def prep_bwd(name, qn, kn, vv, gb, tinv, cts):
    t = qn.shape[0]
    hw = GDN_HEADS * GDN_HD

    def body(q_ref, k_ref, v_ref, gb_ref, x_ref, *refs):
        outs = refs[6:]
        for pi in range(PREP_TM // GDN_PAIR):
            rs = pl.ds(pi * GDN_PAIR, GDN_PAIR)
            ct = tuple(r[rs, :] for r in refs[:6])
            kn_v, gb_v, x_all = k_ref[rs, :], gb_ref[rs, :], x_ref[rs, :]
            _, vjp_low = jax.vjp(f_prep_low, kn_v, gb_v)
            _, vjp_rest = jax.vjp(f_prep_rest, q_ref[rs, :], kn_v, v_ref[rs, :], gb_v, x_all)
            dq, dk, dv, dgb, dx_all = vjp_rest(ct)
            dlows = []
            for h in range(GDN_HEADS):
                hs = slice(h * GDN_HD, (h + 1) * GDN_HD)
                dlows.append(-_dg3(_dg3(x_all[:, hs], dx_all[:, hs], 0, 0), x_all[:, hs], 1, 1))
            dk2, dgb2 = vjp_low(jnp.concatenate(dlows, axis=1))
            outs[0][rs, :] = dq
            outs[1][rs, :] = dk + dk2
            outs[2][rs, :] = dv
            outs[3][rs, :] = dgb + dgb2

    in_w = [hw, hw, hw, LANES, hw] + [hw, hw, hw, hw, hw, LANES]
    out_w = [hw, hw, hw, LANES]
    return pl.pallas_call(
        body, name=name, grid=(t // PREP_TM,),
        in_specs=[_row_spec(PREP_TM, wd) for wd in in_w], out_specs=[_row_spec(PREP_TM, wd) for wd in out_w],
        out_shape=[jax.ShapeDtypeStruct((t, wd), F32) for wd in out_w], compiler_params=_cparams(),
    )(qn, kn, vv, gb, tinv, *cts)


def f_gdn_scan(params, state, u, w, a, qd, kd, egl, z):
    ng = params[0]
    tm = u.shape[0]
    st = [state[h * GDN_HD:(h + 1) * GDN_HD, :] for h in range(GDN_HEADS)]
    zeros = jnp.zeros((GDN_CHUNK, GDN_HD), F32)
    rows = []
    hss = [slice(h * GDN_HD, (h + 1) * GDN_HD) for h in range(GDN_HEADS)]
    for ci in range(tm // GDN_CHUNK):
        rs = slice(ci * GDN_CHUNK, (ci + 1) * GDN_CHUNK)
        vns = [u[rs, hs] - mm_nn(w[rs, hs], s) for hs, s in zip(hss, st)]
        qs = [mm_nn(qd[rs, hs], s) for hs, s in zip(hss, st)]
        upd = [mm_tn(kd[rs, hs], vn) for hs, vn in zip(hss, vns)]
        es = [egl[ci * GDN_CHUNK:ci * GDN_CHUNK + 1, GDN_HEADS + h:GDN_HEADS + h + 1] for h in range(GDN_HEADS)]
        st = [s * e + d for s, e, d in zip(st, es, upd)]
        vpads = [jnp.concatenate([vn, zeros] if ci % 2 == 0 else [zeros, vn], axis=0) for vn in vns]
        os_ = [q + mm_nn(a[rs, hs], vp) for q, hs, vp in zip(qs, hss, vpads)]
        heads = [o * lax.rsqrt(jnp.mean(o * o, -1, keepdims=True) + RMS_EPS) * ng * jax.nn.silu(z[rs, hs])
                 for o, hs in zip(os_, hss)]
        rows.append(jnp.concatenate(heads, axis=1))
    return jnp.concatenate(rows, axis=0), jnp.concatenate(st, axis=0)


def f_sgu(params, puv):
    g, b, ws, bst = params
    tm = puv.shape[0]
    uv = jax.nn.gelu(puv)
    u = uv[:, :512]
    v = _ln(uv[:, 512:], g, b)
    r, c = _iota2((SGU_CHUNK, SGU_CHUNK), 0), _iota2((SGU_CHUNK, SGU_CHUNK), 1)
    rows = []
    for ci in range(tm // SGU_CHUNK):
        rs = slice(ci * SGU_CHUNK, (ci + 1) * SGU_CHUNK)
        groups = []
        for gi in range(SGU_GROUPS):
            gs = slice(gi * SGU_GD, (gi + 1) * SGU_GD)
            wt = jnp.where(r >= c, ws[gs, :], 0.0)
            groups.append(mm_nn(wt, v[rs, gs]) + bst[:, gi:gi + 1])
        rows.append(jnp.concatenate(groups, axis=1))
    return (u * jnp.concatenate(rows, axis=0),)


def f_mix(params, ya, yb, yc, pg):
    bg = params[0]
    s = jax.nn.sigmoid(pg + bg)
    return (s[:, :D_MODEL] * ya + s[:, D_MODEL:2 * D_MODEL] * yb + s[:, 2 * D_MODEL:] * yc,)


def f_ln1(params, xin, m):
    g, b = params
    return (_ln(ALPHA * xin + m, g, b),)


def f_ln2(params, x1, f2):
    b2, g, b = params
    return (_ln(ALPHA * x1 + (f2 + b2), g, b),)


def _cparams(n_axes=1):
    return pltpu.CompilerParams(dimension_semantics=("arbitrary",) * n_axes, vmem_limit_bytes=VMEM_LIMIT)


def _row_spec(tm, width, n_tiles=None):
    if n_tiles is None:
        return pl.BlockSpec((tm, width), lambda i: (i, 0))
    return pl.BlockSpec((tm, width), lambda i: (n_tiles - 1 - i, 0))


def _full_spec(shape):
    return pl.BlockSpec(shape, lambda i: (0,) * len(shape))


def stage_fwd(name, fn, tm, rows, params, outs):
    t = rows[0].shape[0]
    nr, npar = len(rows), len(params)
    widths = jax.eval_shape(lambda p, r: fn(p, *r), [jax.ShapeDtypeStruct(p.shape, F32) for p in params],
                            [jax.ShapeDtypeStruct((tm, r.shape[1]), F32) for r in rows])

    def body(*refs):
        r = [x[...].astype(F32) for x in refs[:nr]]
        p = [x[...] for x in refs[nr:nr + npar]]
        res = fn(p, *r)
        k = nr + npar
        for o, dts in zip(res, outs):
            for dt in dts:
                refs[k][...] = o.astype(dt)
                k += 1

    out_shape, out_specs = [], []
    for wd, dts in zip(widths, outs):
        for dt in dts:
            out_shape.append(jax.ShapeDtypeStruct((t, wd.shape[1]), dt))
            out_specs.append(_row_spec(tm, wd.shape[1]))
    return pl.pallas_call(
        body, name=name, grid=(t // tm,),
        in_specs=[_row_spec(tm, r.shape[1]) for r in rows] + [_full_spec(p.shape) for p in params],
        out_specs=out_specs, out_shape=out_shape, compiler_params=_cparams(),
    )(*rows, *params)


def stage_bwd(name, fn, tm, rows, params, cts, drow_dtypes, ride=None):
    t = rows[0].shape[0]
    nr, npar = len(rows), len(params)
    flat_cts = [c for cl in cts if cl is not None for c in cl]
    nct = len(flat_cts)
    want = [i for i, dt in enumerate(drow_dtypes) if dt is not None]
    r_ins, r_in_specs, r_out_specs, r_out_shape, r_scratch = _ride_args(ride)

    def body(*refs):
        r = [x[...].astype(F32) for x in refs[:nr]]
        p = [x[...] for x in refs[nr:nr + npar]]
        res, vjp = jax.vjp(lambda pp, rr: fn(pp, *rr), p, r)
        k = nr + npar
        ct = []
        for o, cl in zip(res, cts):
            if cl is None:
                ct.append(jnp.zeros_like(o))
            else:
                acc = refs[k][...].astype(F32)
                for j in range(1, len(cl)):
                    acc = acc + refs[k + j][...].astype(F32)
                k += len(cl)
                ct.append(acc)
        dp, dr = vjp(tuple(ct))
        k += len(r_ins)
        for i in want:
            refs[k][...] = dr[i].astype(drow_dtypes[i])
            k += 1

        @pl.when(pl.program_id(0) == 0)
        def _():
            for j in range(npar):
                refs[k + j][...] = jnp.zeros_like(refs[k + j])

        for j in range(npar):
            refs[k + j][...] += dp[j]
        _ride(ride, nr + npar + nct, len(want) + npar, refs, t // tm)

    out_shape = [jax.ShapeDtypeStruct(rows[i].shape, drow_dtypes[i]) for i in want]
    out_specs = [_row_spec(tm, rows[i].shape[1]) for i in want]
    out_shape += [jax.ShapeDtypeStruct(p.shape, F32) for p in params]
    out_specs += [_full_spec(p.shape) for p in params]
    return pl.pallas_call(
        body, name=name, grid=(t // tm,),
        in_specs=[_row_spec(tm, r.shape[1]) for r in rows] + [_full_spec(p.shape) for p in params]
        + [_row_spec(tm, c.shape[1]) for c in flat_cts] + r_in_specs,
        out_specs=out_specs + r_out_specs, out_shape=out_shape + r_out_shape, scratch_shapes=r_scratch,
        compiler_params=_cparams(),
    )(*rows, *params, *flat_cts, *r_ins)


def _halo_spec(tm, halo, width, n_tiles=None):
    per = tm // halo
    if n_tiles is None:
        return pl.BlockSpec((halo, width), lambda i: (jnp.maximum(i * per - 1, 0), 0))
    return pl.BlockSpec((halo, width), lambda i: (jnp.maximum((n_tiles - 1 - i) * per - 1, 0), 0))


def halo_fwd(name, fn, tm, halo, src, params, outs, ride=None):
    t, width = src.shape
    npar = len(params)
    widths = jax.eval_shape(lambda p, h, r: fn(p, h, r), [jax.ShapeDtypeStruct(p.shape, F32) for p in params],
                            jax.ShapeDtypeStruct((halo, width), F32), jax.ShapeDtypeStruct((tm, width), F32))
    r_ins, r_in_specs, r_out_specs, r_out_shape, r_scratch = _ride_args(ride)
    n_out = sum(len(dts) for dts in outs)

    def body(h_ref, x_ref, *refs):
        hal = jnp.where(pl.program_id(0) == 0, 0.0, h_ref[...])
        res = fn([x[...] for x in refs[:npar]], hal, x_ref[...])
        k = npar + len(r_ins)
        for o, dts in zip(res, outs):
            for dt in dts:
                refs[k][...] = o.astype(dt)
                k += 1
        _ride(ride, 2 + npar, n_out, (h_ref, x_ref) + refs, t // tm)

    out_shape, out_specs = [], []
    for wd, dts in zip(widths, outs):
        for dt in dts:
            out_shape.append(jax.ShapeDtypeStruct((t, wd.shape[1]), dt))
            out_specs.append(_row_spec(tm, wd.shape[1]))
    return pl.pallas_call(
        body, name=name, grid=(t // tm,),
        in_specs=[_halo_spec(tm, halo, width), _row_spec(tm, width)] + [_full_spec(p.shape) for p in params] + r_in_specs,
        out_specs=out_specs + r_out_specs, out_shape=out_shape + r_out_shape, scratch_shapes=r_scratch,
        compiler_params=_cparams(),
    )(src, src, *params, *r_ins)


def halo_bwd(name, fn, tm, halo, src, params, cts, dsrc_dtype, ride=None):
    t, width = src.shape
    n_tiles = t // tm
    npar = len(params)
    n_in = 2 + npar + len(cts)
    r_ins, r_in_specs, r_out_specs, r_out_shape, r_scratch = _ride_args(ride)

    def body(h_ref, x_ref, *refs):
        carry = refs[n_in - 2 + len(r_ins) + 1 + npar + len(r_out_shape)]
        first_tile = pl.program_id(0) == n_tiles - 1
        hal = jnp.where(first_tile, 0.0, h_ref[...])
        p = [x[...] for x in refs[:npar]]
        res, vjp = jax.vjp(fn, p, hal, x_ref[...])
        k = npar
        ct = tuple(refs[k + j][...].astype(F32) for j in range(len(cts)))
        k += len(cts) + len(r_ins)
        dp, dh, dx = vjp(ct)

        @pl.when(pl.program_id(0) == 0)
        def _():
            carry[...] = jnp.zeros_like(carry)
            for j in range(npar):
                refs[k + 1 + j][...] = jnp.zeros_like(refs[k + 1 + j])

        dx = jnp.concatenate([dx[:tm - halo, :], dx[tm - halo:, :] + carry[...]], axis=0)
        refs[k][...] = dx.astype(dsrc_dtype)
        carry[...] = dh
        for j in range(npar):
            refs[k + 1 + j][...] += dp[j]
        _ride(ride, n_in, 1 + npar, (h_ref, x_ref) + refs, n_tiles)

    out_shape = [jax.ShapeDtypeStruct((t, width), dsrc_dtype)] + [jax.ShapeDtypeStruct(p.shape, F32) for p in params]
    out_specs = [_row_spec(tm, width, n_tiles)] + [_full_spec(p.shape) for p in params]
    return pl.pallas_call(
        body, name=name, grid=(n_tiles,),
        in_specs=[_halo_spec(tm, halo, width, n_tiles), _row_spec(tm, width, n_tiles)] + [_full_spec(p.shape) for p in params]
        + [_row_spec(tm, c.shape[1], n_tiles) for c in cts] + r_in_specs,
        out_specs=out_specs + r_out_specs, out_shape=out_shape + r_out_shape,
        scratch_shapes=[pltpu.VMEM((halo, width), F32)] + r_scratch, compiler_params=_cparams(),
    )(src, src, *params, *cts, *r_ins)


SCAN_TM = 256


def scan_fwd(name, norm_g, u, w, a, qd, kd, egl, z, ride=None):
    t = u.shape[0]
    n_tiles = t // SCAN_TM
    srows = GDN_HEADS * GDN_HD

    r_ins, r_in_specs, r_out_specs, r_out_shape, r_scratch = _ride_args(ride)

    def body(*refs):
        ng_ref, u_ref, w_ref, a_ref, qd_ref, kd_ref, e_ref, z_ref = refs[:8]
        o_ref, s_ref = refs[8 + len(r_ins):10 + len(r_ins)]
        state = refs[10 + len(r_ins) + len(r_out_shape)]

        @pl.when(pl.program_id(0) == 0)
        def _():
            state[...] = jnp.zeros_like(state)

        s_ref[0] = state[...]
        o, s_new = f_gdn_scan([ng_ref[...]], state[...], u_ref[...], w_ref[...], a_ref[...], qd_ref[...], kd_ref[...],
                              e_ref[...], z_ref[...])
        o_ref[...] = o.astype(BF16)
        state[...] = s_new
        _ride(ride, 8, 2, refs, n_tiles)

    rows = [u, w, a, qd, kd, egl, z]
    return pl.pallas_call(
        body, name=name, grid=(n_tiles,),
        in_specs=[_full_spec(norm_g.shape)] + [_row_spec(SCAN_TM, r.shape[1]) for r in rows] + r_in_specs,
        out_specs=[_row_spec(SCAN_TM, 512), pl.BlockSpec((1, srows, GDN_HD), lambda i: (i, 0, 0))] + r_out_specs,
        out_shape=[jax.ShapeDtypeStruct((t, 512), BF16), jax.ShapeDtypeStruct((n_tiles, srows, GDN_HD), F32)] + r_out_shape,
        scratch_shapes=[pltpu.VMEM((srows, GDN_HD), F32)] + r_scratch, compiler_params=_cparams(),
    )(norm_g, *rows, *r_ins)


def scan_bwd(name, norm_g, states, u, w, a, qd, kd, egl, z, dout, ride=None):
    t = u.shape[0]
    n_tiles = t // SCAN_TM
    srows = GDN_HEADS * GDN_HD
    r_ins, r_in_specs, r_out_specs, r_out_shape, r_scratch = _ride_args(ride)

    def body(*refs):
        ng_ref, s_ref, u_ref, w_ref, a_ref, qd_ref, kd_ref, e_ref, z_ref, do_ref = refs[:10]
        du_ref, dw_ref, da_ref, dqd_ref, dkd_ref, de_ref, dz_ref, dng_ref = refs[10 + len(r_ins):18 + len(r_ins)]
        dstate = refs[18 + len(r_ins) + len(r_out_shape)]

        @pl.when(pl.program_id(0) == 0)
        def _():
            dstate[...] = jnp.zeros_like(dstate)
            dng_ref[...] = jnp.zeros_like(dng_ref)

        args = ([ng_ref[...]], s_ref[0], u_ref[...], w_ref[...], a_ref[...], qd_ref[...], kd_ref[...], e_ref[...], z_ref[...])
        _, vjp = jax.vjp(f_gdn_scan, *args)
        dp, ds, du, dw, da, dqd, dkd, de, dz = vjp((do_ref[...].astype(F32), dstate[...]))
        du_ref[...] = du
        dw_ref[...] = dw
        da_ref[...] = da
        dqd_ref[...] = dqd
        dkd_ref[...] = dkd
        de_ref[...] = de
        dz_ref[...] = dz.astype(BF16)
        dng_ref[...] += dp[0]
        dstate[...] = ds
        _ride(ride, 10, 8, refs, n_tiles)

    rows = [u, w, a, qd, kd, egl, z, dout]
    return pl.pallas_call(
        body, name=name, grid=(n_tiles,),
        in_specs=[_full_spec(norm_g.shape), pl.BlockSpec((1, srows, GDN_HD), lambda i: (n_tiles - 1 - i, 0, 0))]
        + [_row_spec(SCAN_TM, r.shape[1], n_tiles) for r in rows] + r_in_specs,
        out_specs=[_row_spec(SCAN_TM, r.shape[1], n_tiles) for r in rows[:7]] + [_full_spec(norm_g.shape)] + r_out_specs,
        out_shape=[jax.ShapeDtypeStruct(r.shape, F32) for r in rows[:6]] + [jax.ShapeDtypeStruct(z.shape, BF16)]
        + [jax.ShapeDtypeStruct(norm_g.shape, F32)] + r_out_shape,
        scratch_shapes=[pltpu.VMEM((srows, GDN_HD), F32)] + r_scratch, compiler_params=_cparams(),
    )(norm_g, states, *rows, *r_ins)


def loss_stage(name, y, target, tm):
    t, d = y.shape

    def body(y_ref, t_ref, l_ref, dy_ref):
        @pl.when(pl.program_id(0) == 0)
        def _():
            l_ref[...] = jnp.zeros_like(l_ref)

        e = y_ref[...] - t_ref[...]
        dy_ref[...] = e * (1.0 / d)
        sq = e * e
        part = sq[:, 0:LANES]
        for j in range(1, d // LANES):
            part = part + sq[:, j * LANES:(j + 1) * LANES]
        acc = part[0:8, :]
        for j in range(1, tm // 8):
            acc = acc + part[j * 8:(j + 1) * 8, :]
        l_ref[...] += acc * (0.5 / d)

    return pl.pallas_call(
        body, name=name, grid=(t // tm,),
        in_specs=[_row_spec(tm, d), _row_spec(tm, d)],
        out_specs=[_full_spec((8, LANES)), _row_spec(tm, d)],
        out_shape=[jax.ShapeDtypeStruct((8, LANES), F32), jax.ShapeDtypeStruct((t, d), F32)],
        compiler_params=_cparams(),
    )(y, target)


def _pick(n, cands):
    for c in cands:
        if n % c == 0:
            return c
    return n


def matmul(name, a, b, form, out_dtype, acc=None, fuse=None):
    if form == "nn":
        (m, k), n = a.shape, b.shape[1]
    elif form == "nt":
        (m, k), n = a.shape, b.shape[0]
    else:
        (k, m), n = a.shape, b.shape[1]
    if form == "tn":
        tm = _pick(m, (1024, 512, 256, 128))
        tn = n if n <= 2048 else _pick(n, (1536, 1024, 512, 256, 128))
        tk = _pick(k, (2048, 1024, 512, 256, 128))
    else:
        tm = _pick(m, (512, 256, 128))
        tk = k if k <= 4096 else _pick(k, (2048, 1024, 512))
        cap = 2048 * 2048 if fuse is None else 2048 * 1024
        tn = n if n * tk <= cap else _pick(n, (2048, 1536, 1024, 512, 256, 128))
    nk = k // tk
    ca, cb = {"nn": (1, 0), "nt": (1, 1), "tn": (0, 0)}[form]
    a_spec = (pl.BlockSpec((tm, tk), lambda j, i, kk: (i, kk)) if form != "tn"
              else pl.BlockSpec((tk, tm), lambda j, i, kk: (kk, i)))
    b_spec = (pl.BlockSpec((tn, tk), lambda j, i, kk: (j, kk)) if form == "nt"
              else pl.BlockSpec((tk, tn), lambda j, i, kk: (kk, j)))
    o_spec = pl.BlockSpec((tm, tn), lambda j, i, kk: (i, j))
    col_spec = pl.BlockSpec((1, tn), lambda j, i, kk: (0, j))
    has_acc = acc is not None

    assert nk == 1 or (out_dtype == F32 and fuse is None)

    def body(*refs):
        a_ref, b_ref = refs[0], refs[1]
        prod = lax.dot_general(a_ref[...], b_ref[...], (((ca,), (cb,)), ((), ())), preferred_element_type=F32)
        if fuse is not None and fuse[0] == "relu2":
            pre = prod + refs[2][...]
            refs[3][...] = jnp.square(jnp.maximum(pre, 0.0)).astype(BF16)
            refs[4][...] = pre.astype(BF16)
            return
        if fuse is not None and fuse[0] == "relu2_bwd":
            d = prod * (2.0 * jnp.maximum(refs[2][...].astype(F32), 0.0))
            refs[3][...] = d.astype(BF16)

            @pl.when(pl.program_id(1) == 0)
            def _():
                refs[4][...] = jnp.zeros_like(refs[4])

            refs[4][...] += jnp.sum(d, axis=0, keepdims=True)
            return
        o_ref = refs[-1]
        if nk == 1:
            o_ref[...] = (prod + refs[2][...] if has_acc else prod).astype(out_dtype)
            return
        kk = pl.program_id(2)

        @pl.when(kk == 0)
        def _():
            o_ref[...] = prod + refs[2][...] if has_acc else prod

        @pl.when(kk > 0)
        def _():
            o_ref[...] += prod

    ins = [a.astype(BF16), b.astype(BF16)] + ([acc] if has_acc else [])
    in_specs = [a_spec, b_spec] + ([o_spec] if has_acc else [])
    out_specs, out_shape = o_spec, jax.ShapeDtypeStruct((m, n), out_dtype)
    if fuse is not None and fuse[0] == "relu2":
        ins, in_specs = ins + [fuse[1]], in_specs + [col_spec]
        out_specs, out_shape = [o_spec, o_spec], [jax.ShapeDtypeStruct((m, n), BF16)] * 2
    elif fuse is not None:
        ins, in_specs = ins + [fuse[1]], in_specs + [o_spec]
        out_specs = [o_spec, col_spec]
        out_shape = [jax.ShapeDtypeStruct((m, n), BF16), jax.ShapeDtypeStruct((1, n), F32)]
    return pl.pallas_call(
        body, name=name, grid=(n // tn, m // tm, nk),
        in_specs=in_specs, out_specs=out_specs, out_shape=out_shape,
        input_output_aliases={2: 0} if has_acc else {},
        compiler_params=_cparams(3),
    )(*ins)


MESH_ID = pl.DeviceIdType.MESH
ANY_SPEC = pl.BlockSpec(memory_space=pl.ANY)


def _place():
    return lax.axis_index("x"), lax.axis_index("y"), lax.axis_index("c")


def _chips():
    x, y, c = _place()
    return x, y, c, 2 * x + y, [(1 - x, y), (x, 1 - y), (1 - x, 1 - y)]


def scatter_rider(arrs):
    n = len(arrs)

    def remote(ins, outs, sems, i, j, landing_from_me):
        x, y, c, me, others = _chips()
        ox, oy = others[j]
        peer = 2 * ox + oy
        return pltpu.make_async_remote_copy(src_ref=ins[i].at[peer], dst_ref=outs[i].at[me if landing_from_me else peer],
                                            send_sem=sems[0].at[3 * i + j], recv_sem=sems[1].at[3 * i + j],
                                            device_id=(ox, oy, c), device_id_type=MESH_ID)

    def local(ins, outs, sems, i):
        me = _chips()[3]
        return pltpu.make_async_copy(ins[i].at[me], outs[i].at[me], sems[2].at[i])

    def start(ins, outs, sems):
        for i in range(n):
            local(ins, outs, sems, i).start()
            for j in range(3):
                remote(ins, outs, sems, i, j, True).start()

    def finish(ins, outs, sems):
        for i in range(n):
            for j in range(3):
                remote(ins, outs, sems, i, j, False).wait_recv()
        for i in range(n):
            for j in range(3):
                remote(ins, outs, sems, i, j, True).wait_send()
            local(ins, outs, sems, i).wait()

    return dict(ins=list(arrs), out_shape=[jax.ShapeDtypeStruct(a.shape, a.dtype) for a in arrs],
                scratch=[pltpu.SemaphoreType.DMA((3 * n,)), pltpu.SemaphoreType.DMA((3 * n,)), pltpu.SemaphoreType.DMA((n,))],
                start=start, middle=lambda ins, outs, sems: None, finish=finish)


def gather_rider(split, flat):
    ns, n = len(split), len(split) + len(flat)
    arrs = list(split) + list(flat)
    half = [a.shape[0] // 2 for a in split]

    def ici(ins, outs, sems, i, j, landing_from_me):
        x, y, c, me, others = _chips()
        ox, oy = others[j]
        slot = me if landing_from_me else 2 * ox + oy
        if i < ns:
            rows = pl.ds(c * half[i], half[i])
            src, dst = ins[i].at[rows], outs[i].at[slot, rows]
        else:
            src, dst = ins[i], outs[i].at[slot]
        return pltpu.make_async_remote_copy(src_ref=src, dst_ref=dst, send_sem=sems[0].at[3 * i + j],
                                            recv_sem=sems[1].at[3 * i + j], device_id=(ox, oy, c), device_id_type=MESH_ID)

    def d2d(outs, sems, i, j, mine):
        x, y, c, me, others = _chips()
        ox, oy = others[j]
        ref = outs[i].at[2 * ox + oy, pl.ds((c if mine else 1 - c) * half[i], half[i])]
        return pltpu.make_async_remote_copy(src_ref=ref, dst_ref=ref, send_sem=sems[2].at[3 * i + j],
                                            recv_sem=sems[3].at[3 * i + j], device_id=(x, y, 1 - c), device_id_type=MESH_ID)

    def local(ins, outs, sems, i):
        return pltpu.make_async_copy(ins[i], outs[i].at[_chips()[3]], sems[4].at[i])

    def start(ins, outs, sems):
        for i in range(n):
            local(ins, outs, sems, i).start()
            for j in range(3):
                ici(ins, outs, sems, i, j, True).start()

    def middle(ins, outs, sems):
        for i in range(n):
            for j in range(3):
                ici(ins, outs, sems, i, j, False).wait_recv()
                if i < ns:
                    d2d(outs, sems, i, j, True).start()

    def finish(ins, outs, sems):
        for i in range(ns):
            for j in range(3):
                d2d(outs, sems, i, j, False).wait_recv()
        for i in range(n):
            for j in range(3):
                ici(ins, outs, sems, i, j, True).wait_send()
                if i < ns:
                    d2d(outs, sems, i, j, True).wait_send()
            local(ins, outs, sems, i).wait()

    return dict(ins=arrs, out_shape=[jax.ShapeDtypeStruct((N_CHIPS,) + a.shape, a.dtype) for a in arrs],
                scratch=[pltpu.SemaphoreType.DMA((3 * n,)), pltpu.SemaphoreType.DMA((3 * n,)),
                         pltpu.SemaphoreType.DMA((3 * max(ns, 1),)), pltpu.SemaphoreType.DMA((3 * max(ns, 1),)),
                         pltpu.SemaphoreType.DMA((n,))],
                start=start, middle=middle, finish=finish)


def run_alone(name, ride):
    n_in, n_out = len(ride["ins"]), len(ride["out_shape"])

    def body(*refs):
        parts = refs[:n_in], refs[n_in:n_in + n_out], refs[n_in + n_out:]
        ride["start"](*parts)
        ride["middle"](*parts)
        ride["finish"](*parts)

    return pl.pallas_call(body, name=name, in_specs=[ANY_SPEC] * n_in, out_specs=[ANY_SPEC] * n_out,
                          out_shape=ride["out_shape"], scratch_shapes=ride["scratch"])(*ride["ins"])


def sibling_exchange(name, arrs):
    n = len(arrs)

    def body(*refs):
        ins, outs = refs[:n], refs[n:2 * n]
        send_sems, recv_sems = refs[2 * n:]
        x, y, c = _place()
        cps = [pltpu.make_async_remote_copy(src_ref=ins[i], dst_ref=outs[i], send_sem=send_sems.at[i], recv_sem=recv_sems.at[i],
                                            device_id=(x, y, 1 - c), device_id_type=MESH_ID) for i in range(n)]
        for cp in cps:
            cp.start()
        for cp in cps:
            cp.wait_recv()
        for cp in cps:
            cp.wait_send()

    return pl.pallas_call(
        body, name=name, in_specs=[ANY_SPEC] * n, out_specs=[ANY_SPEC] * n,
        out_shape=[jax.ShapeDtypeStruct(a.shape, a.dtype) for a in arrs],
        scratch_shapes=[pltpu.SemaphoreType.DMA((n,)), pltpu.SemaphoreType.DMA((n,))],
    )(*arrs)


def gather_all(name, vec):
    def body(in_ref, out_ref, send_sems, recv_sems, local_sem):
        x, y, c = _place()
        me = 4 * x + 2 * y + c

        def peer(mask):
            return (1 - x if mask & 4 else x, 1 - y if mask & 2 else y, 1 - c if mask & 1 else c)

        def remote(mask, landing_from_me):
            px, py, pc = peer(mask)
            slot = me if landing_from_me else 4 * px + 2 * py + pc
            return pltpu.make_async_remote_copy(src_ref=in_ref, dst_ref=out_ref.at[slot], send_sem=send_sems.at[mask - 1],
                                                recv_sem=recv_sems.at[mask - 1], device_id=(px, py, pc), device_id_type=MESH_ID)

        local = pltpu.make_async_copy(in_ref, out_ref.at[me], local_sem)
        local.start()
        sent = [remote(mask, True) for mask in range(1, N_DEV)]
        for cp in sent:
            cp.start()
        for mask in range(1, N_DEV):
            remote(mask, False).wait_recv()
        for cp in sent:
            cp.wait_send()
        local.wait()

    return pl.pallas_call(
        body, name=name, in_specs=[ANY_SPEC], out_specs=ANY_SPEC,
        out_shape=jax.ShapeDtypeStruct((N_DEV,) + vec.shape, vec.dtype),
        scratch_shapes=[pltpu.SemaphoreType.DMA((N_DEV - 1,)), pltpu.SemaphoreType.DMA((N_DEV - 1,)), pltpu.SemaphoreType.DMA],
    )(vec)


def sum_slots(name, arrs, tr):
    k, r, c = arrs[0].shape
    nb = r // tr

    def body(*refs):
        o_ref = refs[-1]
        for li, a_ref in enumerate(refs[:-1]):
            @pl.when(pl.program_id(0) == li)
            def _(a_ref=a_ref):
                acc = a_ref[0].astype(F32)
                for j in range(1, k):
                    acc = acc + a_ref[j].astype(F32)
                o_ref[...] = acc

    def in_spec(li):
        return pl.BlockSpec((k, tr, c), lambda l, i: (0, jnp.where(l == li, i, jnp.where(l < li, 0, nb - 1)), 0))

    return pl.pallas_call(
        body, name=name, grid=(len(arrs), nb),
        in_specs=[in_spec(li) for li in range(len(arrs))], out_specs=pl.BlockSpec((tr, c), lambda l, i: (l * nb + i, 0)),
        out_shape=jax.ShapeDtypeStruct((len(arrs) * r, c), F32), compiler_params=_cparams(2),
    )(*arrs)


def adamw(name, parts, w, m, v, tr):
    r, c = w.shape
    npart = len(parts)
    bc1 = 1.0 - ADAM_B1 ** ADAM_STEP
    bc2 = 1.0 - ADAM_B2 ** ADAM_STEP

    def body(*refs):
        g = refs[0][...]
        for j in range(1, npart):
            g = g + refs[j][...]
        w_ref, m_ref, v_ref, g_out, d_out, m_out, v_out = refs[npart:]
        m_new = ADAM_B1 * m_ref[...] + (1.0 - ADAM_B1) * g
        v_new = ADAM_B2 * v_ref[...] + (1.0 - ADAM_B2) * jnp.square(g)
        m_hat = m_new / bc1
        v_hat = v_new / bc2
        g_out[...] = g
        d_out[...] = -ADAM_LR * (m_hat / (jnp.sqrt(v_hat) + ADAM_EPS) + ADAM_WD * w_ref[...])
        m_out[...] = m_new
        v_out[...] = v_new

    spec = pl.BlockSpec((tr, c), lambda i: (i, 0))
    return pl.pallas_call(
        body, name=name, grid=(r // tr,), in_specs=[spec] * (npart + 3), out_specs=[spec] * 4,
        out_shape=[jax.ShapeDtypeStruct((r, c), F32)] * 4, compiler_params=_cparams(),
    )(*parts, w, m, v)


ROW_TM = 256


def _row(v):
    return v.reshape(1, -1)


def _lane_pad(vals, at):
    return jnp.concatenate([jnp.zeros((at,), F32), vals, jnp.zeros((LANES - at - vals.shape[0],), F32)]).reshape(1, LANES)


def _layer_params(sm, l):
    return dict(
        conf=[sm["conv_dw_w"][l], _row(sm["conv_dw_b"][l]), _row(sm["conv_ln_g"][l]), _row(sm["conv_ln_b"][l])],
        pre=[sm["gdn_conv_q"][l], sm["gdn_conv_k"][l], sm["gdn_conv_v"][l],
             _lane_pad(sm["gdn_a_log"][l], GDN_HEADS), _lane_pad(sm["gdn_dt_bias"][l], GDN_HEADS)],
        scan=_row(sm["gdn_norm_g"][l]),
        sgu=[_row(sm["sgu_ln_g"][l]), _row(sm["sgu_ln_b"][l]), sm["sgu_w_s"][l].reshape(SGU_GROUPS * SGU_CHUNK, SGU_CHUNK),
             sm["sgu_b_s"][l].T],
        mix=[_row(sm["b_gate"][l])],
        ln1=[_row(sm["ln1_g"][l]), _row(sm["ln1_b"][l])],
        ff=[_row(sm["b_ff1"][l])],
        ln2=[_row(sm["b_ff2"][l]), _row(sm["ln2_g"][l]), _row(sm["ln2_b"][l])],
    )


def _split_w_in(w):
    zpad = jnp.zeros((D_MODEL, LANES - 2 * GDN_HEADS), w.dtype)
    return dict(A=w[:, 0:1024], B=jnp.concatenate([w[:, 1024:2560], w[:, 3072:3080], zpad], axis=1), Z=w[:, 2560:3072],
                C=w[:, 3080:4104], G=w[:, 4104:7176])


def _join_w_in(d):
    return jnp.concatenate([d["A"], d["B"][:, 0:1536], d["Z"], d["B"][:, 1536:1544], d["C"], d["G"]], axis=1)


def _hosted(get, host):
    got = get(host) if get is not None else None
    return got if got is not None else (None, lambda results: None)


def layer_forward(l, xin, xin_b, big, lp, fetch=None):
    tag = f"l{l}_"
    wi = big["w_in"][l]
    sv = dict(xin=xin, xin_b=xin_b)
    for r in "GACBZ":
        sv["p" + r] = matmul(tag + "proj_" + r, xin_b, wi[r], "nn", F32)
    ride, done = _hosted(fetch, "conformer")
    sv["ya_in"], *rode = halo_fwd(tag + "conformer", f_conformer, ROW_TM, CONV_HALO, sv["pA"], lp["conf"], [[BF16]], ride)
    done(rode)
    ride, done = _hosted(fetch, "gdn_pre")
    sv["qn"], sv["kn"], sv["vv"], sv["gb"], *rode = halo_fwd(tag + "gdn_pre", f_gdn_pre, ROW_TM, GDN_HALO, sv["pB"], lp["pre"],
                                                             [[F32]] * 4, ride)
    done(rode)
    ride, done = _hosted(fetch, "gdn_prep")
    prep_out = prep_fwd(tag + "gdn_prep", sv["qn"], sv["kn"], sv["vv"], sv["gb"], ride)
    prep, sv["tinv"] = prep_out[:6], prep_out[6]
    done(prep_out[7:])
    sv["prep"] = prep
    ride, done = _hosted(fetch, "gdn_scan")
    sv["yb_in"], sv["states"], *rode = scan_fwd(tag + "gdn_scan", lp["scan"], *prep, sv["pZ"], ride)
    done(rode)
    (sv["yc_in"],) = stage_fwd(tag + "sgu", f_sgu, ROW_TM, [sv["pC"]], lp["sgu"], [[BF16]])
    sv["ya"] = matmul(tag + "out_a", sv["ya_in"], big["w_pa"][l], "nn", F32)
    sv["yb"] = matmul(tag + "out_b", sv["yb_in"], big["w_pb"][l], "nn", F32)
    sv["yc"] = matmul(tag + "out_c", sv["yc_in"], big["w_pc"][l], "nn", F32)
    (sv["mix"],) = stage_fwd(tag + "mix", f_mix, ROW_TM, [sv["ya"], sv["yb"], sv["yc"], sv["pG"]], lp["mix"], [[BF16]])
    sv["m"] = matmul(tag + "w_o", sv["mix"], big["w_o"][l], "nn", F32)
    sv["x1"], sv["x1_b"] = stage_fwd(tag + "ln1", f_ln1, ROW_TM, [xin, sv["m"]], lp["ln1"], [[F32, BF16]])
    sv["h"], sv["hpre"] = matmul(tag + "ff1", sv["x1_b"], big["w_ff1"][l], "nn", BF16, fuse=("relu2", lp["ff"][0]))
    sv["f2"] = matmul(tag + "ff2", sv["h"], big["w_ff2"][l], "nn", F32)
    x2, x2_b = stage_fwd(tag + "ln2", f_ln2, ROW_TM, [sv["x1"], sv["f2"]], lp["ln2"], [[F32, BF16]])
    return x2, x2_b, sv


def layer_backward(l, dx2, sv, big, lp, send=None):
    tag = f"l{l}_b_"
    g = {}
    get = (lambda host: send(host, g)) if send is not None else None
    dx1_a, df2, db2, dg2, dbe2 = stage_bwd(tag + "ln2", f_ln2, ROW_TM, [sv["x1"], sv["f2"]], lp["ln2"], [dx2], [F32, BF16])
    g["b_ff2"], g["ln2_g"], g["ln2_b"] = db2[0], dg2[0], dbe2[0]
    g["w_ff2"] = matmul(tag + "dw_ff2", sv["h"], df2, "tn", F32)
    dhpre, db1 = matmul(tag + "dx_ff2", df2, big["w_ff2"][l], "nt", BF16, fuse=("relu2_bwd", sv["hpre"]))
    g["b_ff1"] = db1[0]
    g["w_ff1"] = matmul(tag + "dw_ff1", sv["x1_b"], dhpre, "tn", F32)
    dx1_b = matmul(tag + "dx_ff1", dhpre, big["w_ff1"][l], "nt", F32)
    dxin_a, dm, dg1, dbe1 = stage_bwd(tag + "ln1", f_ln1, ROW_TM, [sv["xin"], sv["m"]], lp["ln1"], [[dx1_a, dx1_b]], [F32, BF16])
    g["ln1_g"], g["ln1_b"] = dg1[0], dbe1[0]
    g["w_o"] = matmul(tag + "dw_o", sv["mix"], dm, "tn", F32)
    dmix = matmul(tag + "dx_o", dm, big["w_o"][l], "nt", F32)
    dya, dyb, dyc, dp_g, dbg = stage_bwd(tag + "mix", f_mix, ROW_TM, [sv["ya"], sv["yb"], sv["yc"], sv["pG"]], lp["mix"],
                                         [[dmix]], [BF16] * 4)
    g["b_gate"] = dbg[0]
    g["w_pa"] = matmul(tag + "dw_pa", sv["ya_in"], dya, "tn", F32)
    g["w_pb"] = matmul(tag + "dw_pb", sv["yb_in"], dyb, "tn", F32)
    g["w_pc"] = matmul(tag + "dw_pc", sv["yc_in"], dyc, "tn", F32)
    dya_in = matmul(tag + "dx_pa", dya, big["w_pa"][l], "nt", F32)
    dyb_in = matmul(tag + "dx_pb", dyb, big["w_pb"][l], "nt", F32)
    dyc_in = matmul(tag + "dx_pc", dyc, big["w_pc"][l], "nt", F32)
    ride, done = _hosted(get, "conformer")
    dp_a, dcw, dcb, dcg, dcbe, *rode = halo_bwd(tag + "conformer", f_conformer, ROW_TM, CONV_HALO, sv["pA"], lp["conf"], [dya_in],
                                                BF16, ride)
    done(rode)
    g["conv_dw_w"], g["conv_dw_b"], g["conv_ln_g"], g["conv_ln_b"] = dcw, dcb[0], dcg[0], dcbe[0]
    dp_c, dsg, dsb, dsw, dsbs = stage_bwd(tag + "sgu", f_sgu, ROW_TM, [sv["pC"]], lp["sgu"], [[dyc_in]], [BF16])
    g["sgu_ln_g"], g["sgu_ln_b"] = dsg[0], dsb[0]
    g["sgu_w_s"] = dsw.reshape(SGU_GROUPS, SGU_CHUNK, SGU_CHUNK)
    g["sgu_b_s"] = dsbs.T
    ride, done = _hosted(get, "gdn_scan")
    scan_out = scan_bwd(tag + "gdn_scan", lp["scan"], sv["states"], *sv["prep"], sv["pZ"], dyb_in, ride)
    dprep, dp_z, dng = scan_out[:6], scan_out[6], scan_out[7]
    done(scan_out[8:])
    g["gdn_norm_g"] = dng[0]
    dqn, dkn, dvv, dgb = prep_bwd(tag + "gdn_prep", sv["qn"], sv["kn"], sv["vv"], sv["gb"], sv["tinv"], dprep)
    ride, done = _hosted(get, "gdn_pre")
    dp_b, dq, dk, dv, dal, ddt, *rode = halo_bwd(tag + "gdn_pre", f_gdn_pre, ROW_TM, GDN_HALO, sv["pB"], lp["pre"],
                                                 [dqn, dkn, dvv, dgb], BF16, ride)
    done(rode)
    g["gdn_conv_q"], g["gdn_conv_k"], g["gdn_conv_v"] = dq, dk, dv
    g["gdn_a_log"] = dal[0, GDN_HEADS:2 * GDN_HEADS]
    g["gdn_dt_bias"] = ddt[0, GDN_HEADS:2 * GDN_HEADS]
    dps = dict(G=dp_g, A=dp_a, C=dp_c, B=dp_b, Z=dp_z)
    wi = big["w_in"][l]
    g["w_in"] = _join_w_in({r: matmul(tag + "dw_in_" + r, sv["xin_b"], dps[r], "tn", F32) for r in "GACBZ"})
    dxin_b = None
    for r in "GACBZ":
        dxin_b = matmul(tag + "dx_in_" + r, dps[r], wi[r], "nt", F32, acc=dxin_b)
    return [dxin_a, dxin_b], g


SMALL_PER_LAYER = ("b_gate", "conv_dw_w", "conv_dw_b", "conv_ln_g", "conv_ln_b", "gdn_conv_q", "gdn_conv_k", "gdn_conv_v",
                   "gdn_a_log", "gdn_dt_bias", "gdn_norm_g", "sgu_ln_g", "sgu_ln_b", "sgu_w_s", "sgu_b_s", "ln1_g", "ln1_b",
                   "b_ff1", "b_ff2", "ln2_g", "ln2_b")
BIG = ("w_in", "w_pa", "w_pb", "w_pc", "w_o", "w_ff1", "w_ff2")


def local_step(x, target, big, sm, fetch=None, send=None):
    lps = [_layer_params(sm, l) for l in range(DEPTH)]
    p_in = [_row(sm["ln_in_g"]), _row(sm["ln_in_b"])]
    xc, xc_b = stage_fwd("ln_in", f_ln_in, ROW_TM, [x], p_in, [[F32, BF16]])
    saved = []
    for l in range(DEPTH):
        xc, xc_b, sv = layer_forward(l, xc, xc_b, big, lps[l], fetch if l == 0 else None)
        saved.append(sv)
    loss_part, dy = loss_stage("loss", xc, target, ROW_TM)
    dx = [dy]
    per_layer = [None] * DEPTH

    def send_from(l):
        if send is None or l != 0:
            return None
        return lambda host, g: send(host, per_layer[:l] + [g] + per_layer[l + 1:])

    for l in reversed(range(DEPTH)):
        dx, per_layer[l] = layer_backward(l, dx, saved[l], big, lps[l], send_from(l))
    ride, done = _hosted((lambda host: send(host, per_layer)) if send is not None else None, "ln_in")
    grad_x, dgi, dbi, *rode = stage_bwd("ln_in_b", f_ln_in, ROW_TM, [x], p_in, [dx], [F32], ride)
    done(rode)
    small = {n: jnp.stack([per_layer[l][n] for l in range(DEPTH)]) for n in SMALL_PER_LAYER}
    small["ln_in_g"], small["ln_in_b"] = dgi[0], dbi[0]
    return loss_part, grad_x, small, per_layer


PARAM_NAMES = ("ln_in_g", "ln_in_b", "w_in", "b_gate", "conv_dw_w", "conv_dw_b", "conv_ln_g", "conv_ln_b", "w_pa",
               "gdn_conv_q", "gdn_conv_k", "gdn_conv_v", "gdn_a_log", "gdn_dt_bias", "gdn_norm_g", "w_pb", "sgu_ln_g",
               "sgu_ln_b", "sgu_w_s", "sgu_b_s", "w_pc", "w_o", "ln1_g", "ln1_b", "w_ff1", "b_ff1", "w_ff2", "b_ff2",
               "ln2_g", "ln2_b")
SMALL = tuple(n for n in PARAM_NAMES if n not in BIG)
COL_SHARDED_SMALL = ("conv_dw_w", "gdn_conv_q", "gdn_conv_k", "gdn_conv_v")
ROW_SHARDED_BIG = ("w_o", "w_ff2")
ADAM_ROWS = 128
_OUT_PROJ = ("w_pa", "w_pb", "w_pc", "w_o")
FETCH_BEHIND = {"conformer": [("w_ff1", 0), ("w_ff2", 0)], "gdn_pre": [(n, l) for l in (0, 1) for n in _OUT_PROJ],
                "gdn_prep": [("w_in", 1), ("w_ff1", 1)], "gdn_scan": [("w_ff2", 1)]}
SEND_BEHIND = {"conformer": [(n, 1) for n in BIG], "gdn_scan": [("w_ff1", 0), ("w_ff2", 0)],
               "gdn_pre": [(n, 0) for n in _OUT_PROJ], "ln_in": [("w_in", 0)]}


def _unshard(g, rows):
    n, r, c = g.shape
    if rows:
        return g.reshape(n * r, c)
    return jnp.moveaxis(g, 0, 1).reshape(r, n * c)


def _to_shards(w, rows):
    r, c = w.shape
    if rows:
        return w.reshape(N_CHIPS, r // N_CHIPS, c)
    return jnp.moveaxis(w.reshape(r, N_CHIPS, c // N_CHIPS), 1, 0)


def _pack(vals, rows_multiple=8):
    flat = jnp.concatenate([v.reshape(-1) for v in vals])
    per = LANES * rows_multiple
    padded = -(-flat.shape[0] // per) * per
    return jnp.pad(flat, (0, padded - flat.shape[0])).reshape(-1, LANES)


def _unpack(packed, shapes):
    flat = packed.reshape(-1)
    out, at = [], 0
    for s in shapes:
        size = 1
        for d in s:
            size *= d
        out.append(flat[at:at + size].reshape(s))
        at += size
    return out


def kernel(x, ln_in_g, ln_in_b, w_in, b_gate, conv_dw_w, conv_dw_b, conv_ln_g, conv_ln_b, w_pa, gdn_conv_q, gdn_conv_k, gdn_conv_v, gdn_a_log, gdn_dt_bias, gdn_norm_g, w_pb, sgu_ln_g, sgu_ln_b, sgu_w_s, sgu_b_s, w_pc, w_o, ln1_g, ln1_b, w_ff1, b_ff1, w_ff2, b_ff2, ln2_g, ln2_b, loss_target, m_ln_in_g, m_ln_in_b, m_w_in, m_b_gate, m_conv_dw_w, m_conv_dw_b, m_conv_ln_g, m_conv_ln_b, m_w_pa, m_gdn_conv_q, m_gdn_conv_k, m_gdn_conv_v, m_gdn_a_log, m_gdn_dt_bias, m_gdn_norm_g, m_w_pb, m_sgu_ln_g, m_sgu_ln_b, m_sgu_w_s, m_sgu_b_s, m_w_pc, m_w_o, m_ln1_g, m_ln1_b, m_w_ff1, m_b_ff1, m_w_ff2, m_b_ff2, m_ln2_g, m_ln2_b, v_ln_in_g, v_ln_in_b, v_w_in, v_b_gate, v_conv_dw_w, v_conv_dw_b, v_conv_ln_g, v_conv_ln_b, v_w_pa, v_gdn_conv_q, v_gdn_conv_k, v_gdn_conv_v, v_gdn_a_log, v_gdn_dt_bias, v_gdn_norm_g, v_w_pb, v_sgu_ln_g, v_sgu_ln_b, v_sgu_w_s, v_sgu_b_s, v_w_pc, v_w_o, v_ln1_g, v_ln1_b, v_w_ff1, v_b_ff1, v_w_ff2, v_b_ff2, v_ln2_g, v_ln2_b):
    w = dict(zip(PARAM_NAMES, (ln_in_g, ln_in_b, w_in, b_gate, conv_dw_w, conv_dw_b, conv_ln_g, conv_ln_b, w_pa, gdn_conv_q, gdn_conv_k, gdn_conv_v, gdn_a_log, gdn_dt_bias, gdn_norm_g, w_pb, sgu_ln_g, sgu_ln_b, sgu_w_s, sgu_b_s, w_pc, w_o, ln1_g, ln1_b, w_ff1, b_ff1, w_ff2, b_ff2, ln2_g, ln2_b)))
    m = dict(zip(PARAM_NAMES, (m_ln_in_g, m_ln_in_b, m_w_in, m_b_gate, m_conv_dw_w, m_conv_dw_b, m_conv_ln_g, m_conv_ln_b, m_w_pa, m_gdn_conv_q, m_gdn_conv_k, m_gdn_conv_v, m_gdn_a_log, m_gdn_dt_bias, m_gdn_norm_g, m_w_pb, m_sgu_ln_g, m_sgu_ln_b, m_sgu_w_s, m_sgu_b_s, m_w_pc, m_w_o, m_ln1_g, m_ln1_b, m_w_ff1, m_b_ff1, m_w_ff2, m_b_ff2, m_ln2_g, m_ln2_b)))
    v = dict(zip(PARAM_NAMES, (v_ln_in_g, v_ln_in_b, v_w_in, v_b_gate, v_conv_dw_w, v_conv_dw_b, v_conv_ln_g, v_conv_ln_b, v_w_pa, v_gdn_conv_q, v_gdn_conv_k, v_gdn_conv_v, v_gdn_a_log, v_gdn_dt_bias, v_gdn_norm_g, v_w_pb, v_sgu_ln_g, v_sgu_ln_b, v_sgu_w_s, v_sgu_b_s, v_w_pc, v_w_o, v_ln1_g, v_ln1_b, v_w_ff1, v_b_ff1, v_w_ff2, v_b_ff2, v_ln2_g, v_ln2_b)))
    chip = 2 * lax.axis_index("x") + lax.axis_index("y")

    shards = {n: w[n].astype(BF16) for n in BIG}
    col_small_shapes = [w[n].shape for n in COL_SHARDED_SMALL]
    big = {n: [None] * DEPTH for n in BIG}

    def place(keys, gathered):
        for (n, l), g in zip(keys, gathered):
            whole = _unshard(g, n in ROW_SHARDED_BIG)
            big[n][l] = _split_w_in(whole) if n == "w_in" else whole

    def fetch(host):
        keys = FETCH_BEHIND[host]
        return gather_rider([shards[n][l] for n, l in keys], []), lambda gathered: place(keys, gathered)

    first = run_alone("gather_first", gather_rider([shards["w_in"][0]], [_pack([w[n] for n in COL_SHARDED_SMALL])]))
    place([("w_in", 0)], first[:1])
    sm = {n: w[n] for n in SMALL}
    per_chip = [_unpack(first[-1][s], col_small_shapes) for s in range(N_CHIPS)]
    for i, n in enumerate(COL_SHARDED_SMALL):
        sm[n] = jnp.concatenate([per_chip[s][i] for s in range(N_CHIPS)], axis=-1)

    received = {}

    def send(host, per_layer):
        keys = SEND_BEHIND[host]
        ride = scatter_rider([_to_shards(per_layer[l][n], n in ROW_SHARDED_BIG).astype(BF16) for n, l in keys])
        return ride, lambda results: received.update(zip(keys, results))

    loss_part, grad_x, grads, _ = local_step(x[0], loss_target[0], big, sm, fetch, send)
    partial = [sum_slots("sum_chips_" + n, [received[(n, l)] for l in range(DEPTH)], ADAM_ROWS) for n in BIG]
    sibling = sibling_exchange("swap_cores", partial)
    out = {}
    for n, mine, theirs in zip(BIG, partial, sibling):
        shape = w[n].shape
        res = adamw("adamw_" + n, [mine, theirs], w[n].reshape(-1, shape[-1]), m[n].reshape(-1, shape[-1]),
                    v[n].reshape(-1, shape[-1]), ADAM_ROWS)
        out[n] = [r.reshape(shape) for r in res]

    small_shapes = [grads[n].shape for n in SMALL]
    vec = _pack([grads[n] for n in SMALL] + [jnp.sum(loss_part).reshape(1)])
    total = sum_slots("sum_small", [gather_all("gather_small", vec)], vec.shape[0])
    whole = _unpack(total, small_shapes + [(1,)])
    loss = whole[-1][0]
    g_small = {}
    for n, g in zip(SMALL, whole[:-1]):
        if n in COL_SHARDED_SMALL:
            width = g.shape[-1] // N_CHIPS
            g = lax.dynamic_slice_in_dim(g, chip * width, width, axis=g.ndim - 1)
        g_small[n] = g
    local_shapes = [w[n].shape for n in SMALL]
    packed_g = _pack([g_small[n] for n in SMALL])
    res = adamw("adamw_small", [packed_g], _pack([w[n] for n in SMALL]), _pack([m[n] for n in SMALL]),
                _pack([v[n] for n in SMALL]), packed_g.shape[0])
    unpacked = [_unpack(r, local_shapes) for r in res]
    for i, n in enumerate(SMALL):
        out[n] = [unpacked[k][i] for k in range(4)]

    return (loss, grad_x[None], *[out[n][0] for n in PARAM_NAMES], *[out[n][1] for n in PARAM_NAMES],
            *[out[n][2] for n in PARAM_NAMES], *[out[n][3] for n in PARAM_NAMES])
```

```python
import functools

import jax
import jax.numpy as jnp
from jax import lax
from jax.experimental import pallas as pl
from jax.experimental.pallas import tpu as pltpu

F32 = jnp.float32
BF16 = jnp.bfloat16
ACT = BF16

D_MODEL = 1024
DEPTH = 2
CONV_DIM = 512
CONV_WIDTH = 31
CONV_HALO = 32
GDN_HEADS = 4
GDN_HD = 128
GDN_CHUNK = 64
GDN_PAIR = 2 * GDN_CHUNK
GDN_CONV = 4
GDN_HALO = 16
SGU_GROUPS = 4
SGU_GD = 128
SGU_CHUNK = 128
D_FF = 4096
ALPHA = (2 * DEPTH) ** 0.25
LN_EPS = 1e-5
RMS_EPS = 1e-6
ADAM_LR, ADAM_B1, ADAM_B2, ADAM_EPS, ADAM_WD, ADAM_STEP = 0.001, 0.9, 0.999, 1e-08, 0.01, 10
N_CHIPS = 4
N_DEV = 8
LANES = 128
VMEM_LIMIT = 52 * 1024 * 1024

W_GATE, W_A, W_C, W_B, W_Z = 3 * D_MODEL, 2 * CONV_DIM, 2 * 512, 3 * 512 + LANES, 512


def _dg(a, b, ca, cb):
    return lax.dot_general(a.astype(BF16), b.astype(BF16), (((ca,), (cb,)), ((), ())), preferred_element_type=F32)


@jax.custom_vjp
def mm_nn(a, b):
    return _dg(a, b, 1, 0)


mm_nn.defvjp(lambda a, b: (_dg(a, b, 1, 0), (a, b)), lambda r, g: (_dg(g, r[1], 1, 1), _dg(r[0], g, 0, 0)))


@jax.custom_vjp
def mm_nt(a, b):
    return _dg(a, b, 1, 1)


mm_nt.defvjp(lambda a, b: (_dg(a, b, 1, 1), (a, b)), lambda r, g: (_dg(g, r[1], 1, 0), _dg(g, r[0], 0, 0)))


@jax.custom_vjp
def mm_tn(a, b):
    return _dg(a, b, 0, 0)


mm_tn.defvjp(lambda a, b: (_dg(a, b, 0, 0), (a, b)), lambda r, g: (_dg(r[1], g, 1, 1), _dg(r[0], g, 1, 0)))


def _ln(x, g, b):
    mu = jnp.mean(x, -1, keepdims=True)
    xc = x - mu
    var = jnp.mean(xc * xc, -1, keepdims=True)
    return xc * lax.rsqrt(var + LN_EPS) * g + b


def _iota2(shape, dim):
    return lax.broadcasted_iota(jnp.int32, shape, dim)


def f_ln_in(params, x):
    g, b = params
    return (_ln(x, g, b),)


def _roll_rows(x, s):
    return x if s == 0 else pltpu.roll(x, s, 0)


@functools.partial(jax.custom_vjp, nondiff_argnums=(1,))
def shift_rows(x, s):
    return _roll_rows(x, s)


shift_rows.defvjp(lambda x, s: (_roll_rows(x, s), None),
                  lambda s, _, ct: (_roll_rows(ct, (ct.shape[0] - s) % ct.shape[0]),))


def _causal_conv(full, w, halo):
    width = w.shape[0]
    groups = []
    for g in range(full.shape[1] // LANES):
        ls = slice(g * LANES, (g + 1) * LANES)
        xg, wg = full[:, ls], w[:, ls]
        acc = wg[width - 1:width, :] * xg[halo:, :]
        for j in range(width - 1):
            acc = acc + wg[j:j + 1, :] * shift_rows(xg, width - 1 - j)[halo:, :]
        groups.append(acc)
    return jnp.concatenate(groups, axis=1)


def f_conformer(params, halo, pa):
    w, b, g, be = params
    full = jnp.concatenate([halo, pa], axis=0)
    h = full[:, :CONV_DIM] * jax.nn.sigmoid(full[:, CONV_DIM:])
    return (jax.nn.silu(_ln(_causal_conv(h, w, CONV_HALO) + b, g, be)),)


def f_gdn_pre(params, halo, pb):
    cq, ck, cv, alog, dtb = params
    full = jnp.concatenate([halo, pb], axis=0)

    def conv(cols, w):
        return jax.nn.silu(_causal_conv(cols, w, GDN_HALO))

    def l2n(t, scale):
        parts = []
        for h in range(GDN_HEADS):
            th = t[:, h * GDN_HD:(h + 1) * GDN_HD]
            parts.append(th * (lax.rsqrt(jnp.sum(th * th, -1, keepdims=True) + RMS_EPS) * scale))
        return jnp.concatenate(parts, axis=1)

    q = l2n(conv(full[:, 0:512], cq), GDN_HD ** -0.5)
    k = l2n(conv(full[:, 512:1024], ck), 1.0)
    v = conv(full[:, 1024:1536], cv)
    logits = pb[:, 1536:1536 + LANES]
    z = logits + dtb
    softplus = jnp.maximum(z, 0.0) + jnp.log(1.0 + jnp.exp(-jnp.abs(z)))
    g = -jnp.exp(alog) * softplus
    beta = jax.nn.sigmoid(logits)
    lane = _iota2(logits.shape, 1)
    gb = jnp.where(lane < GDN_HEADS, beta, jnp.where(lane < 2 * GDN_HEADS, g, 0.0))
    return q, k, v, gb


def _split2(x):
    hi = x.astype(BF16)
    return hi, (x - hi.astype(F32)).astype(BF16)


def _dg3(a, b, ca, cb):
    ah, al = _split2(a)
    bh, bl = _split2(b)
    d = lambda p, q: lax.dot_general(p, q, (((ca,), (cb,)), ((), ())), preferred_element_type=F32)
    return d(ah, bh) + (d(ah, bl) + d(al, bh))


def _inv_unit_lower(lows):
    n = lows[0].shape[0]
    eye = jnp.where(_iota2((n, n), 0) == _iota2((n, n), 1), 1.0, 0.0).astype(F32)
    ps = [-low for low in lows]
    xs = [eye + p for p in ps]
    for _ in range(5):
        ps = [_dg3(p, p, 1, 0) for p in ps]
        xs = [x + _dg3(x, p, 1, 0) for x, p in zip(xs, ps)]
    return xs


def _pair_masks():
    n = GDN_PAIR
    r, c = _iota2((n, n), 0), _iota2((n, n), 1)
    same = (r >= GDN_CHUNK) == (c >= GDN_CHUNK)
    return same, jnp.logical_and(same, r >= c), jnp.logical_and(same, r > c), jnp.logical_and(same, r <= c)


def _masked_sum(mask, g):
    mk = jnp.where(mask, 1.0, 0.0).astype(BF16)
    g1 = g.astype(BF16)
    r1 = g - g1.astype(F32)
    g2 = r1.astype(BF16)
    g3 = (r1 - g2.astype(F32)).astype(BF16)
    d = lambda q: jnp.dot(mk, q, preferred_element_type=F32)
    return d(g1) + (d(g2) + d(g3))


@jax.custom_vjp
def chunk_cumsum(g):
    return _masked_sum(_pair_masks()[1], g)


chunk_cumsum.defvjp(lambda g: (_masked_sum(_pair_masks()[1], g), None), lambda _, ct: (_masked_sum(_pair_masks()[3], ct),))


def _prep_gates(gb):
    gam_all = chunk_cumsum(gb)
    first = _iota2((GDN_PAIR, LANES), 0) < GDN_CHUNK
    glast = jnp.where(first, gam_all[GDN_CHUNK - 1:GDN_CHUNK, :], gam_all[GDN_PAIR - 1:GDN_PAIR, :])
    return gam_all, gam_all.T, glast


def _head_gates(gb, gam_all, gam_t, h, causal):
    gc = gam_all[:, GDN_HEADS + h:GDN_HEADS + h + 1]
    gr = gam_t[GDN_HEADS + h:GDN_HEADS + h + 1, :]
    return gc, gb[:, h:h + 1], jnp.exp(jnp.where(causal, gc - gr, -jnp.inf))


def f_prep_low(kn, gb):
    _, causal, strict, _ = _pair_masks()
    gam_all, gam_t, _ = _prep_gates(gb)
    lows = []
    for h in range(GDN_HEADS):
        k = kn[:, h * GDN_HD:(h + 1) * GDN_HD]
        _, beta, decay = _head_gates(gb, gam_all, gam_t, h, causal)
        lows.append(jnp.where(strict, beta * mm_nt(k, k) * decay, 0.0))
    return jnp.concatenate(lows, axis=1)


def f_prep_rest(qn, kn, vv, gb, tinv_all):
    _, causal, _, _ = _pair_masks()
    gam_all, gam_t, glast = _prep_gates(gb)
    us, ws, as_, qds, kds = [], [], [], [], []
    for h in range(GDN_HEADS):
        hs = slice(h * GDN_HD, (h + 1) * GDN_HD)
        q, k, v, tinv = qn[:, hs], kn[:, hs], vv[:, hs], tinv_all[:, hs]
        gc, beta, decay = _head_gates(gb, gam_all, gam_t, h, causal)
        eg = jnp.exp(gc)
        us.append(mm_nn(tinv, beta * v))
        ws.append(mm_nn(tinv, beta * k * eg))
        as_.append(mm_nt(q, k) * decay)
        qds.append(q * eg)
        kds.append(k * jnp.exp(glast[:, GDN_HEADS + h:GDN_HEADS + h + 1] - gc))
    cat = lambda xs: jnp.concatenate(xs, axis=1)
    return cat(us), cat(ws), cat(as_), cat(qds), cat(kds), jnp.exp(glast)


PREP_TM = 2 * GDN_PAIR


def _ride(ride, n_in, n_out, refs, n_steps):
    if ride is None:
        return
    nri, nro = len(ride["ins"]), len(ride["out_shape"])
    r_in = refs[n_in:n_in + nri]
    r_out = refs[n_in + nri + n_out:n_in + nri + n_out + nro]
    sems = refs[len(refs) - len(ride["scratch"]):]
    step = pl.program_id(0)

    @pl.when(step == 0)
    def _():
        ride["start"](r_in, r_out, sems)

    @pl.when(step == (2 * n_steps) // 3)
    def _():
        ride["middle"](r_in, r_out, sems)

    @pl.when(step == n_steps - 1)
    def _():
        ride["finish"](r_in, r_out, sems)


def _ride_args(ride):
    if ride is None:
        return [], [], [], [], []
    n = len(ride["ins"])
    return list(ride["ins"]), [ANY_SPEC] * n, [ANY_SPEC] * len(ride["out_shape"]), list(ride["out_shape"]), list(ride["scratch"])


def prep_fwd(name, qn, kn, vv, gb, ride=None):
    t = qn.shape[0]
    hw = GDN_HEADS * GDN_HD
    n_steps = t // PREP_TM
    r_ins, r_in_specs, r_out_specs, r_out_shape, r_scratch = _ride_args(ride)

    def body(*refs):
        q_ref, k_ref, v_ref, gb_ref = refs[:4]
        outs = refs[4 + len(r_ins):4 + len(r_ins) + 7]
        pairs = [pl.ds(pi * GDN_PAIR, GDN_PAIR) for pi in range(PREP_TM // GDN_PAIR)]
        lows = [f_prep_low(k_ref[rs, :], gb_ref[rs, :]) for rs in pairs]
        invs = _inv_unit_lower([low[:, h * GDN_HD:(h + 1) * GDN_HD] for low in lows for h in range(GDN_HEADS)])
        for pi, rs in enumerate(pairs):
            tinv = jnp.concatenate(invs[pi * GDN_HEADS:(pi + 1) * GDN_HEADS], axis=1)
            res = f_prep_rest(q_ref[rs, :], k_ref[rs, :], v_ref[rs, :], gb_ref[rs, :], tinv)
            for o_ref, val in zip(outs, res + (tinv,)):
                o_ref[rs, :] = val
        _ride(ride, 4, 7, refs, n_steps)

    widths = [hw, hw, hw, hw, hw, LANES, hw]
    return pl.pallas_call(
        body, name=name, grid=(n_steps,),
        in_specs=[_row_spec(PREP_TM, hw)] * 3 + [_row_spec(PREP_TM, LANES)] + r_in_specs,
        out_specs=[_row_spec(PREP_TM, wd) for wd in widths] + r_out_specs,
        out_shape=[jax.ShapeDtypeStruct((t, wd), F32) for wd in widths] + r_out_shape,
        scratch_shapes=r_scratch, compiler_params=_cparams(),
    )(qn, kn, vv, gb, *r_ins)


def prep_bwd(name, qn, kn, vv, gb, tinv, cts):
    t = qn.shape[0]
    hw = GDN_HEADS * GDN_HD

    def body(q_ref, k_ref, v_ref, gb_ref, x_ref, *refs):
        outs = refs[6:]
        for pi in range(PREP_TM // GDN_PAIR):
            rs = pl.ds(pi * GDN_PAIR, GDN_PAIR)
            ct = tuple(r[rs, :] for r in refs[:6])
            kn_v, gb_v, x_all = k_ref[rs, :], gb_ref[rs, :], x_ref[rs, :]
            _, vjp_low = jax.vjp(f_prep_low, kn_v, gb_v)
            _, vjp_rest = jax.vjp(f_prep_rest, q_ref[rs, :], kn_v, v_ref[rs, :], gb_v, x_all)
            dq, dk, dv, dgb, dx_all = vjp_rest(ct)
            dlows = []
            for h in range(GDN_HEADS):
                hs = slice(h * GDN_HD, (h + 1) * GDN_HD)
                dlows.append(-_dg3(_dg3(x_all[:, hs], dx_all[:, hs], 0, 0), x_all[:, hs], 1, 1))
            dk2, dgb2 = vjp_low(jnp.concatenate(dlows, axis=1))
            outs[0][rs, :] = dq
            outs[1][rs, :] = dk + dk2
            outs[2][rs, :] = dv
            outs[3][rs, :] = dgb + dgb2

    in_w = [hw, hw, hw, LANES, hw] + [hw, hw, hw, hw, hw, LANES]
    out_w = [hw, hw, hw, LANES]
    return pl.pallas_call(
        body, name=name, grid=(t // PREP_TM,),
        in_specs=[_row_spec(PREP_TM, wd) for wd in in_w], out_specs=[_row_spec(PREP_TM, wd) for wd in out_w],
        out_shape=[jax.ShapeDtypeStruct((t, wd), F32) for wd in out_w], compiler_params=_cparams(),
    )(qn, kn, vv, gb, tinv, *cts)


def f_gdn_scan(params, state, u, w, a, qd, kd, egl, z):
    ng = params[0]
    tm = u.shape[0]
    st = [state[h * GDN_HD:(h + 1) * GDN_HD, :] for h in range(GDN_HEADS)]
    zeros = jnp.zeros((GDN_CHUNK, GDN_HD), F32)
    rows = []
    hss = [slice(h * GDN_HD, (h + 1) * GDN_HD) for h in range(GDN_HEADS)]
    for ci in range(tm // GDN_CHUNK):
        rs = slice(ci * GDN_CHUNK, (ci + 1) * GDN_CHUNK)
        vns = [u[rs, hs] - mm_nn(w[rs, hs], s) for hs, s in zip(hss, st)]
        qs = [mm_nn(qd[rs, hs], s) for hs, s in zip(hss, st)]
        upd = [mm_tn(kd[rs, hs], vn) for hs, vn in zip(hss, vns)]
        es = [egl[ci * GDN_CHUNK:ci * GDN_CHUNK + 1, GDN_HEADS + h:GDN_HEADS + h + 1] for h in range(GDN_HEADS)]
        st = [s * e + d for s, e, d in zip(st, es, upd)]
        vpads = [jnp.concatenate([vn, zeros] if ci % 2 == 0 else [zeros, vn], axis=0) for vn in vns]
        os_ = [q + mm_nn(a[rs, hs], vp) for q, hs, vp in zip(qs, hss, vpads)]
        heads = [o * lax.rsqrt(jnp.mean(o * o, -1, keepdims=True) + RMS_EPS) * ng * jax.nn.silu(z[rs, hs])
                 for o, hs in zip(os_, hss)]
        rows.append(jnp.concatenate(heads, axis=1))
    return jnp.concatenate(rows, axis=0), jnp.concatenate(st, axis=0)


def f_sgu(params, puv):
    g, b, ws, bst = params
    tm = puv.shape[0]
    uv = jax.nn.gelu(puv)
    u = uv[:, :512]
    v = _ln(uv[:, 512:], g, b)
    r, c = _iota2((SGU_CHUNK, SGU_CHUNK), 0), _iota2((SGU_CHUNK, SGU_CHUNK), 1)
    rows = []
    for ci in range(tm // SGU_CHUNK):
        rs = slice(ci * SGU_CHUNK, (ci + 1) * SGU_CHUNK)
        groups = []
        for gi in range(SGU_GROUPS):
            gs = slice(gi * SGU_GD, (gi + 1) * SGU_GD)
            wt = jnp.where(r >= c, ws[gs, :], 0.0)
            groups.append(mm_nn(wt, v[rs, gs]) + bst[:, gi:gi + 1])
        rows.append(jnp.concatenate(groups, axis=1))
    return (u * jnp.concatenate(rows, axis=0),)


def f_mix(params, ya, yb, yc, pg):
    bg = params[0]
    s = jax.nn.sigmoid(pg + bg)
    return (s[:, :D_MODEL] * ya + s[:, D_MODEL:2 * D_MODEL] * yb + s[:, 2 * D_MODEL:] * yc,)


def f_ln1(params, xin, m):
    g, b = params
    return (_ln(ALPHA * xin + m, g, b),)


def f_ln2(params, x1, f2):
    b2, g, b = params
    return (_ln(ALPHA * x1 + (f2 + b2), g, b),)


def _cparams(n_axes=1):
    return pltpu.CompilerParams(dimension_semantics=("arbitrary",) * n_axes, vmem_limit_bytes=VMEM_LIMIT)


def _row_spec(tm, width, n_tiles=None):
    if n_tiles is None:
        return pl.BlockSpec((tm, width), lambda i: (i, 0))
    return pl.BlockSpec((tm, width), lambda i: (n_tiles - 1 - i, 0))


def _full_spec(shape):
    return pl.BlockSpec(shape, lambda i: (0,) * len(shape))


def stage_fwd(name, fn, tm, rows, params, outs):
    t = rows[0].shape[0]
    nr, npar = len(rows), len(params)
    widths = jax.eval_shape(lambda p, r: fn(p, *r), [jax.ShapeDtypeStruct(p.shape, F32) for p in params],
                            [jax.ShapeDtypeStruct((tm, r.shape[1]), F32) for r in rows])

    def body(*refs):
        r = [x[...].astype(F32) for x in refs[:nr]]
        p = [x[...] for x in refs[nr:nr + npar]]
        res = fn(p, *r)
        k = nr + npar
        for o, dts in zip(res, outs):
            for dt in dts:
                refs[k][...] = o.astype(dt)
                k += 1

    out_shape, out_specs = [], []
    for wd, dts in zip(widths, outs):
        for dt in dts:
            out_shape.append(jax.ShapeDtypeStruct((t, wd.shape[1]), dt))
            out_specs.append(_row_spec(tm, wd.shape[1]))
    return pl.pallas_call(
        body, name=name, grid=(t // tm,),
        in_specs=[_row_spec(tm, r.shape[1]) for r in rows] + [_full_spec(p.shape) for p in params],
        out_specs=out_specs, out_shape=out_shape, compiler_params=_cparams(),
    )(*rows, *params)


def stage_bwd(name, fn, tm, rows, params, cts, drow_dtypes, ride=None):
    t = rows[0].shape[0]
    nr, npar = len(rows), len(params)
    flat_cts = [c for cl in cts if cl is not None for c in cl]
    nct = len(flat_cts)
    want = [i for i, dt in enumerate(drow_dtypes) if dt is not None]
    r_ins, r_in_specs, r_out_specs, r_out_shape, r_scratch = _ride_args(ride)

    def body(*refs):
        r = [x[...].astype(F32) for x in refs[:nr]]
        p = [x[...] for x in refs[nr:nr + npar]]
        res, vjp = jax.vjp(lambda pp, rr: fn(pp, *rr), p, r)
        k = nr + npar
        ct = []
        for o, cl in zip(res, cts):
            if cl is None:
                ct.append(jnp.zeros_like(o))
            else:
                acc = refs[k][...].astype(F32)
                for j in range(1, len(cl)):
                    acc = acc + refs[k + j][...].astype(F32)
                k += len(cl)
                ct.append(acc)
        dp, dr = vjp(tuple(ct))
        k += len(r_ins)
        for i in want:
            refs[k][...] = dr[i].astype(drow_dtypes[i])
            k += 1

        @pl.when(pl.program_id(0) == 0)
        def _():
            for j in range(npar):
                refs[k + j][...] = jnp.zeros_like(refs[k + j])

        for j in range(npar):
            refs[k + j][...] += dp[j]
        _ride(ride, nr + npar + nct, len(want) + npar, refs, t // tm)

    out_shape = [jax.ShapeDtypeStruct(rows[i].shape, drow_dtypes[i]) for i in want]
    out_specs = [_row_spec(tm, rows[i].shape[1]) for i in want]
    out_shape += [jax.ShapeDtypeStruct(p.shape, F32) for p in params]
    out_specs += [_full_spec(p.shape) for p in params]
    return pl.pallas_call(
        body, name=name, grid=(t // tm,),
        in_specs=[_row_spec(tm, r.shape[1]) for r in rows] + [_full_spec(p.shape) for p in params]
        + [_row_spec(tm, c.shape[1]) for c in flat_cts] + r_in_specs,
        out_specs=out_specs + r_out_specs, out_shape=out_shape + r_out_shape, scratch_shapes=r_scratch,
        compiler_params=_cparams(),
    )(*rows, *params, *flat_cts, *r_ins)


def _halo_spec(tm, halo, width, n_tiles=None):
    per = tm // halo
    if n_tiles is None:
        return pl.BlockSpec((halo, width), lambda i: (jnp.maximum(i * per - 1, 0), 0))
    return pl.BlockSpec((halo, width), lambda i: (jnp.maximum((n_tiles - 1 - i) * per - 1, 0), 0))


def halo_fwd(name, fn, tm, halo, src, params, outs, ride=None):
    t, width = src.shape
    npar = len(params)
    widths = jax.eval_shape(lambda p, h, r: fn(p, h, r), [jax.ShapeDtypeStruct(p.shape, F32) for p in params],
                            jax.ShapeDtypeStruct((halo, width), F32), jax.ShapeDtypeStruct((tm, width), F32))
    r_ins, r_in_specs, r_out_specs, r_out_shape, r_scratch = _ride_args(ride)
    n_out = sum(len(dts) for dts in outs)

    def body(h_ref, x_ref, *refs):
        hal = jnp.where(pl.program_id(0) == 0, 0.0, h_ref[...].astype(F32))
        res = fn([x[...] for x in refs[:npar]], hal, x_ref[...].astype(F32))
        k = npar + len(r_ins)
        for o, dts in zip(res, outs):
            for dt in dts:
                refs[k][...] = o.astype(dt)
                k += 1
        _ride(ride, 2 + npar, n_out, (h_ref, x_ref) + refs, t // tm)

    out_shape, out_specs = [], []
    for wd, dts in zip(widths, outs):
        for dt in dts:
            out_shape.append(jax.ShapeDtypeStruct((t, wd.shape[1]), dt))
            out_specs.append(_row_spec(tm, wd.shape[1]))
    return pl.pallas_call(
        body, name=name, grid=(t // tm,),
        in_specs=[_halo_spec(tm, halo, width), _row_spec(tm, width)] + [_full_spec(p.shape) for p in params] + r_in_specs,
        out_specs=out_specs + r_out_specs, out_shape=out_shape + r_out_shape, scratch_shapes=r_scratch,
        compiler_params=_cparams(),
    )(src, src, *params, *r_ins)


def halo_bwd(name, fn, tm, halo, src, params, cts, dsrc_dtype, ride=None):
    t, width = src.shape
    n_tiles = t // tm
    npar = len(params)
    n_in = 2 + npar + len(cts)
    r_ins, r_in_specs, r_out_specs, r_out_shape, r_scratch = _ride_args(ride)

    def body(h_ref, x_ref, *refs):
        carry = refs[n_in - 2 + len(r_ins) + 1 + npar + len(r_out_shape)]
        first_tile = pl.program_id(0) == n_tiles - 1
        hal = jnp.where(first_tile, 0.0, h_ref[...].astype(F32))
        p = [x[...] for x in refs[:npar]]
        res, vjp = jax.vjp(fn, p, hal, x_ref[...].astype(F32))
        k = npar
        ct = tuple(refs[k + j][...].astype(F32) for j in range(len(cts)))
        k += len(cts) + len(r_ins)
        dp, dh, dx = vjp(ct)

        @pl.when(pl.program_id(0) == 0)
        def _():
            carry[...] = jnp.zeros_like(carry)
            for j in range(npar):
                refs[k + 1 + j][...] = jnp.zeros_like(refs[k + 1 + j])

        dx = jnp.concatenate([dx[:tm - halo, :], dx[tm - halo:, :] + carry[...]], axis=0)
        refs[k][...] = dx.astype(dsrc_dtype)
        carry[...] = dh
        for j in range(npar):
            refs[k + 1 + j][...] += dp[j]
        _ride(ride, n_in, 1 + npar, (h_ref, x_ref) + refs, n_tiles)

    out_shape = [jax.ShapeDtypeStruct((t, width), dsrc_dtype)] + [jax.ShapeDtypeStruct(p.shape, F32) for p in params]
    out_specs = [_row_spec(tm, width, n_tiles)] + [_full_spec(p.shape) for p in params]
    return pl.pallas_call(
        body, name=name, grid=(n_tiles,),
        in_specs=[_halo_spec(tm, halo, width, n_tiles), _row_spec(tm, width, n_tiles)] + [_full_spec(p.shape) for p in params]
        + [_row_spec(tm, c.shape[1], n_tiles) for c in cts] + r_in_specs,
        out_specs=out_specs + r_out_specs, out_shape=out_shape + r_out_shape,
        scratch_shapes=[pltpu.VMEM((halo, width), F32)] + r_scratch, compiler_params=_cparams(),
    )(src, src, *params, *cts, *r_ins)


SCAN_TM = 256


def scan_fwd(name, norm_g, u, w, a, qd, kd, egl, z, ride=None):
    t = u.shape[0]
    n_tiles = t // SCAN_TM
    srows = GDN_HEADS * GDN_HD

    r_ins, r_in_specs, r_out_specs, r_out_shape, r_scratch = _ride_args(ride)

    def body(*refs):
        ng_ref, u_ref, w_ref, a_ref, qd_ref, kd_ref, e_ref, z_ref = refs[:8]
        o_ref, s_ref = refs[8 + len(r_ins):10 + len(r_ins)]
        state = refs[10 + len(r_ins) + len(r_out_shape)]

        @pl.when(pl.program_id(0) == 0)
        def _():
            state[...] = jnp.zeros_like(state)

        s_ref[0] = state[...]
        o, s_new = f_gdn_scan([ng_ref[...]], state[...], u_ref[...], w_ref[...], a_ref[...], qd_ref[...], kd_ref[...],
                              e_ref[...], z_ref[...].astype(F32))
        o_ref[...] = o.astype(BF16)
        state[...] = s_new
        _ride(ride, 8, 2, refs, n_tiles)

    rows = [u, w, a, qd, kd, egl, z]
    return pl.pallas_call(
        body, name=name, grid=(n_tiles,),
        in_specs=[_full_spec(norm_g.shape)] + [_row_spec(SCAN_TM, r.shape[1]) for r in rows] + r_in_specs,
        out_specs=[_row_spec(SCAN_TM, 512), pl.BlockSpec((1, srows, GDN_HD), lambda i: (i, 0, 0))] + r_out_specs,
        out_shape=[jax.ShapeDtypeStruct((t, 512), BF16), jax.ShapeDtypeStruct((n_tiles, srows, GDN_HD), F32)] + r_out_shape,
        scratch_shapes=[pltpu.VMEM((srows, GDN_HD), F32)] + r_scratch, compiler_params=_cparams(),
    )(norm_g, *rows, *r_ins)


def scan_bwd(name, norm_g, states, u, w, a, qd, kd, egl, z, dout, ride=None):
    t = u.shape[0]
    n_tiles = t // SCAN_TM
    srows = GDN_HEADS * GDN_HD
    r_ins, r_in_specs, r_out_specs, r_out_shape, r_scratch = _ride_args(ride)

    def body(*refs):
        ng_ref, s_ref, u_ref, w_ref, a_ref, qd_ref, kd_ref, e_ref, z_ref, do_ref = refs[:10]
        du_ref, dw_ref, da_ref, dqd_ref, dkd_ref, de_ref, dz_ref, dng_ref = refs[10 + len(r_ins):18 + len(r_ins)]
        dstate = refs[18 + len(r_ins) + len(r_out_shape)]

        @pl.when(pl.program_id(0) == 0)
        def _():
            dstate[...] = jnp.zeros_like(dstate)
            dng_ref[...] = jnp.zeros_like(dng_ref)

        args = ([ng_ref[...]], s_ref[0], u_ref[...], w_ref[...], a_ref[...], qd_ref[...], kd_ref[...], e_ref[...],
                z_ref[...].astype(F32))
        _, vjp = jax.vjp(f_gdn_scan, *args)
        dp, ds, du, dw, da, dqd, dkd, de, dz = vjp((do_ref[...].astype(F32), dstate[...]))
        du_ref[...] = du
        dw_ref[...] = dw
        da_ref[...] = da
        dqd_ref[...] = dqd
        dkd_ref[...] = dkd
        de_ref[...] = de
        dz_ref[...] = dz.astype(BF16)
        dng_ref[...] += dp[0]
        dstate[...] = ds
        _ride(ride, 10, 8, refs, n_tiles)

    rows = [u, w, a, qd, kd, egl, z, dout]
    return pl.pallas_call(
        body, name=name, grid=(n_tiles,),
        in_specs=[_full_spec(norm_g.shape), pl.BlockSpec((1, srows, GDN_HD), lambda i: (n_tiles - 1 - i, 0, 0))]
        + [_row_spec(SCAN_TM, r.shape[1], n_tiles) for r in rows] + r_in_specs,
        out_specs=[_row_spec(SCAN_TM, r.shape[1], n_tiles) for r in rows[:7]] + [_full_spec(norm_g.shape)] + r_out_specs,
        out_shape=[jax.ShapeDtypeStruct(r.shape, F32) for r in rows[:6]] + [jax.ShapeDtypeStruct(z.shape, BF16)]
        + [jax.ShapeDtypeStruct(norm_g.shape, F32)] + r_out_shape,
        scratch_shapes=[pltpu.VMEM((srows, GDN_HD), F32)] + r_scratch, compiler_params=_cparams(),
    )(norm_g, states, *rows, *r_ins)


def loss_stage(name, y, target, tm):
    t, d = y.shape

    def body(y_ref, t_ref, l_ref, dy_ref):
        @pl.when(pl.program_id(0) == 0)
        def _():
            l_ref[...] = jnp.zeros_like(l_ref)

        e = y_ref[...] - t_ref[...]
        dy_ref[...] = e * (1.0 / d)
        sq = e * e
        part = sq[:, 0:LANES]
        for j in range(1, d // LANES):
            part = part + sq[:, j * LANES:(j + 1) * LANES]
        acc = part[0:8, :]
        for j in range(1, tm // 8):
            acc = acc + part[j * 8:(j + 1) * 8, :]
        l_ref[...] += acc * (0.5 / d)

    return pl.pallas_call(
        body, name=name, grid=(t // tm,),
        in_specs=[_row_spec(tm, d), _row_spec(tm, d)],
        out_specs=[_full_spec((8, LANES)), _row_spec(tm, d)],
        out_shape=[jax.ShapeDtypeStruct((8, LANES), F32), jax.ShapeDtypeStruct((t, d), F32)],
        compiler_params=_cparams(),
    )(y, target)


def _pick(n, cands):
    for c in cands:
        if n % c == 0:
            return c
    return n


def matmul(name, a, b, form, out_dtype, acc=None, fuse=None):
    if form == "nn":
        (m, k), n = a.shape, b.shape[1]
    elif form == "nt":
        (m, k), n = a.shape, b.shape[0]
    else:
        (k, m), n = a.shape, b.shape[1]
    if form == "tn":
        tm = _pick(m, (1024, 512, 256, 128))
        tn = n if n <= 2048 else _pick(n, (1536, 1024, 512, 256, 128))
        tk = _pick(k, (2048, 1024, 512, 256, 128))
    else:
        tm = _pick(m, (512, 256, 128))
        tk = k if k <= 4096 else _pick(k, (2048, 1024, 512))
        cap = 2048 * 2048 if fuse is None else 2048 * 1024
        tn = n if n * tk <= cap else _pick(n, (2048, 1536, 1024, 512, 256, 128))
    nk = k // tk
    ca, cb = {"nn": (1, 0), "nt": (1, 1), "tn": (0, 0)}[form]
    a_spec = (pl.BlockSpec((tm, tk), lambda j, i, kk: (i, kk)) if form != "tn"
              else pl.BlockSpec((tk, tm), lambda j, i, kk: (kk, i)))
    b_spec = (pl.BlockSpec((tn, tk), lambda j, i, kk: (j, kk)) if form == "nt"
              else pl.BlockSpec((tk, tn), lambda j, i, kk: (kk, j)))
    o_spec = pl.BlockSpec((tm, tn), lambda j, i, kk: (i, j))
    col_spec = pl.BlockSpec((1, tn), lambda j, i, kk: (0, j))
    has_acc = acc is not None

    assert nk == 1 or (out_dtype == F32 and fuse is None)

    def body(*refs):
        a_ref, b_ref = refs[0], refs[1]
        prod = lax.dot_general(a_ref[...], b_ref[...], (((ca,), (cb,)), ((), ())), preferred_element_type=F32)
        if fuse is not None and fuse[0] == "relu2":
            pre = prod + refs[2][...]
            refs[3][...] = jnp.square(jnp.maximum(pre, 0.0)).astype(BF16)
            refs[4][...] = pre.astype(BF16)
            return
        if fuse is not None and fuse[0] == "relu2_bwd":
            d = prod * (2.0 * jnp.maximum(refs[2][...].astype(F32), 0.0))
            refs[3][...] = d.astype(BF16)

            @pl.when(pl.program_id(1) == 0)
            def _():
                refs[4][...] = jnp.zeros_like(refs[4])

            refs[4][...] += jnp.sum(d, axis=0, keepdims=True)
            return
        o_ref = refs[-1]
        if nk == 1:
            o_ref[...] = (prod + refs[2][...] if has_acc else prod).astype(out_dtype)
            return
        kk = pl.program_id(2)

        @pl.when(kk == 0)
        def _():
            o_ref[...] = prod + refs[2][...] if has_acc else prod

        @pl.when(kk > 0)
        def _():
            o_ref[...] += prod

    ins = [a.astype(BF16), b.astype(BF16)] + ([acc] if has_acc else [])
    in_specs = [a_spec, b_spec] + ([o_spec] if has_acc else [])
    out_specs, out_shape = o_spec, jax.ShapeDtypeStruct((m, n), out_dtype)
    if fuse is not None and fuse[0] == "relu2":
        ins, in_specs = ins + [fuse[1]], in_specs + [col_spec]
        out_specs, out_shape = [o_spec, o_spec], [jax.ShapeDtypeStruct((m, n), BF16)] * 2
    elif fuse is not None:
        ins, in_specs = ins + [fuse[1]], in_specs + [o_spec]
        out_specs = [o_spec, col_spec]
        out_shape = [jax.ShapeDtypeStruct((m, n), BF16), jax.ShapeDtypeStruct((1, n), F32)]
    return pl.pallas_call(
        body, name=name, grid=(n // tn, m // tm, nk),
        in_specs=in_specs, out_specs=out_specs, out_shape=out_shape,
        input_output_aliases={2: 0} if has_acc else {},
        compiler_params=_cparams(3),
    )(*ins)


MESH_ID = pl.DeviceIdType.MESH
ANY_SPEC = pl.BlockSpec(memory_space=pl.ANY)


def _place():
    return lax.axis_index("x"), lax.axis_index("y"), lax.axis_index("c")


def _chips():
    x, y, c = _place()
    return x, y, c, 2 * x + y, [(1 - x, y), (x, 1 - y), (1 - x, 1 - y)]


def scatter_rider(arrs):
    n = len(arrs)

    def remote(ins, outs, sems, i, j, landing_from_me):
        x, y, c, me, others = _chips()
        ox, oy = others[j]
        peer = 2 * ox + oy
        return pltpu.make_async_remote_copy(src_ref=ins[i].at[peer], dst_ref=outs[i].at[me if landing_from_me else peer],
                                            send_sem=sems[0].at[3 * i + j], recv_sem=sems[1].at[3 * i + j],
                                            device_id=(ox, oy, c), device_id_type=MESH_ID)

    def local(ins, outs, sems, i):
        me = _chips()[3]
        return pltpu.make_async_copy(ins[i].at[me], outs[i].at[me], sems[2].at[i])

    def start(ins, outs, sems):
        for i in range(n):
            local(ins, outs, sems, i).start()
            for j in range(3):
                remote(ins, outs, sems, i, j, True).start()

    def finish(ins, outs, sems):
        for i in range(n):
            for j in range(3):
                remote(ins, outs, sems, i, j, False).wait_recv()
        for i in range(n):
            for j in range(3):
                remote(ins, outs, sems, i, j, True).wait_send()
            local(ins, outs, sems, i).wait()

    return dict(ins=list(arrs), out_shape=[jax.ShapeDtypeStruct(a.shape, a.dtype) for a in arrs],
                scratch=[pltpu.SemaphoreType.DMA((3 * n,)), pltpu.SemaphoreType.DMA((3 * n,)), pltpu.SemaphoreType.DMA((n,))],
                start=start, middle=lambda ins, outs, sems: None, finish=finish)


def gather_rider(split, flat):
    ns, n = len(split), len(split) + len(flat)
    arrs = list(split) + list(flat)
    half = [a.shape[0] // 2 for a in split]

    def ici(ins, outs, sems, i, j, landing_from_me):
        x, y, c, me, others = _chips()
        ox, oy = others[j]
        slot = me if landing_from_me else 2 * ox + oy
        if i < ns:
            rows = pl.ds(c * half[i], half[i])
            src, dst = ins[i].at[rows], outs[i].at[slot, rows]
        else:
            src, dst = ins[i], outs[i].at[slot]
        return pltpu.make_async_remote_copy(src_ref=src, dst_ref=dst, send_sem=sems[0].at[3 * i + j],
                                            recv_sem=sems[1].at[3 * i + j], device_id=(ox, oy, c), device_id_type=MESH_ID)

    def d2d(outs, sems, i, j, mine):
        x, y, c, me, others = _chips()
        ox, oy = others[j]
        ref = outs[i].at[2 * ox + oy, pl.ds((c if mine else 1 - c) * half[i], half[i])]
        return pltpu.make_async_remote_copy(src_ref=ref, dst_ref=ref, send_sem=sems[2].at[3 * i + j],
                                            recv_sem=sems[3].at[3 * i + j], device_id=(x, y, 1 - c), device_id_type=MESH_ID)

    def local(ins, outs, sems, i):
        return pltpu.make_async_copy(ins[i], outs[i].at[_chips()[3]], sems[4].at[i])

    def start(ins, outs, sems):
        for i in range(n):
            local(ins, outs, sems, i).start()
            for j in range(3):
                ici(ins, outs, sems, i, j, True).start()

    def middle(ins, outs, sems):
        for i in range(n):
            for j in range(3):
                ici(ins, outs, sems, i, j, False).wait_recv()
                if i < ns:
                    d2d(outs, sems, i, j, True).start()

    def finish(ins, outs, sems):
        for i in range(ns):
            for j in range(3):
                d2d(outs, sems, i, j, False).wait_recv()
        for i in range(n):
            for j in range(3):
                ici(ins, outs, sems, i, j, True).wait_send()
                if i < ns:
                    d2d(outs, sems, i, j, True).wait_send()
            local(ins, outs, sems, i).wait()

    return dict(ins=arrs, out_shape=[jax.ShapeDtypeStruct((N_CHIPS,) + a.shape, a.dtype) for a in arrs],
                scratch=[pltpu.SemaphoreType.DMA((3 * n,)), pltpu.SemaphoreType.DMA((3 * n,)),
                         pltpu.SemaphoreType.DMA((3 * max(ns, 1),)), pltpu.SemaphoreType.DMA((3 * max(ns, 1),)),
                         pltpu.SemaphoreType.DMA((n,))],
                start=start, middle=middle, finish=finish)


def run_alone(name, ride):
    n_in, n_out = len(ride["ins"]), len(ride["out_shape"])

    def body(*refs):
        parts = refs[:n_in], refs[n_in:n_in + n_out], refs[n_in + n_out:]
        ride["start"](*parts)
        ride["middle"](*parts)
        ride["finish"](*parts)

    return pl.pallas_call(body, name=name, in_specs=[ANY_SPEC] * n_in, out_specs=[ANY_SPEC] * n_out,
                          out_shape=ride["out_shape"], scratch_shapes=ride["scratch"])(*ride["ins"])


def sibling_exchange(name, arrs):
    n = len(arrs)

    def body(*refs):
        ins, outs = refs[:n], refs[n:2 * n]
        send_sems, recv_sems = refs[2 * n:]
        x, y, c = _place()
        cps = [pltpu.make_async_remote_copy(src_ref=ins[i], dst_ref=outs[i], send_sem=send_sems.at[i], recv_sem=recv_sems.at[i],
                                            device_id=(x, y, 1 - c), device_id_type=MESH_ID) for i in range(n)]
        for cp in cps:
            cp.start()
        for cp in cps:
            cp.wait_recv()
        for cp in cps:
            cp.wait_send()

    return pl.pallas_call(
        body, name=name, in_specs=[ANY_SPEC] * n, out_specs=[ANY_SPEC] * n,
        out_shape=[jax.ShapeDtypeStruct(a.shape, a.dtype) for a in arrs],
        scratch_shapes=[pltpu.SemaphoreType.DMA((n,)), pltpu.SemaphoreType.DMA((n,))],
    )(*arrs)


def gather_all(name, vec):
    def body(in_ref, out_ref, send_sems, recv_sems, local_sem):
        x, y, c = _place()
        me = 4 * x + 2 * y + c

        def peer(mask):
            return (1 - x if mask & 4 else x, 1 - y if mask & 2 else y, 1 - c if mask & 1 else c)

        def remote(mask, landing_from_me):
            px, py, pc = peer(mask)
            slot = me if landing_from_me else 4 * px + 2 * py + pc
            return pltpu.make_async_remote_copy(src_ref=in_ref, dst_ref=out_ref.at[slot], send_sem=send_sems.at[mask - 1],
                                                recv_sem=recv_sems.at[mask - 1], device_id=(px, py, pc), device_id_type=MESH_ID)

        local = pltpu.make_async_copy(in_ref, out_ref.at[me], local_sem)
        local.start()
        sent = [remote(mask, True) for mask in range(1, N_DEV)]
        for cp in sent:
            cp.start()
        for mask in range(1, N_DEV):
            remote(mask, False).wait_recv()
        for cp in sent:
            cp.wait_send()
        local.wait()

    return pl.pallas_call(
        body, name=name, in_specs=[ANY_SPEC], out_specs=ANY_SPEC,
        out_shape=jax.ShapeDtypeStruct((N_DEV,) + vec.shape, vec.dtype),
        scratch_shapes=[pltpu.SemaphoreType.DMA((N_DEV - 1,)), pltpu.SemaphoreType.DMA((N_DEV - 1,)), pltpu.SemaphoreType.DMA],
    )(vec)


def sum_slots(name, arrs, tr):
    k, r, c = arrs[0].shape
    nb = r // tr

    def body(*refs):
        o_ref = refs[-1]
        for li, a_ref in enumerate(refs[:-1]):
            @pl.when(pl.program_id(0) == li)
            def _(a_ref=a_ref):
                acc = a_ref[0].astype(F32)
                for j in range(1, k):
                    acc = acc + a_ref[j].astype(F32)
                o_ref[...] = acc

    def in_spec(li):
        return pl.BlockSpec((k, tr, c), lambda l, i: (0, jnp.where(l == li, i, jnp.where(l < li, 0, nb - 1)), 0))

    return pl.pallas_call(
        body, name=name, grid=(len(arrs), nb),
        in_specs=[in_spec(li) for li in range(len(arrs))], out_specs=pl.BlockSpec((tr, c), lambda l, i: (l * nb + i, 0)),
        out_shape=jax.ShapeDtypeStruct((len(arrs) * r, c), F32), compiler_params=_cparams(2),
    )(*arrs)


def adamw(name, parts, w, m, v, tr):
    r, c = w.shape
    npart = len(parts)
    bc1 = 1.0 - ADAM_B1 ** ADAM_STEP
    bc2 = 1.0 - ADAM_B2 ** ADAM_STEP

    def body(*refs):
        g = refs[0][...]
        for j in range(1, npart):
            g = g + refs[j][...]
        w_ref, m_ref, v_ref, g_out, d_out, m_out, v_out = refs[npart:]
        m_new = ADAM_B1 * m_ref[...] + (1.0 - ADAM_B1) * g
        v_new = ADAM_B2 * v_ref[...] + (1.0 - ADAM_B2) * jnp.square(g)
        m_hat = m_new / bc1
        v_hat = v_new / bc2
        g_out[...] = g
        d_out[...] = -ADAM_LR * (m_hat / (jnp.sqrt(v_hat) + ADAM_EPS) + ADAM_WD * w_ref[...])
        m_out[...] = m_new
        v_out[...] = v_new

    spec = pl.BlockSpec((tr, c), lambda i: (i, 0))
    return pl.pallas_call(
        body, name=name, grid=(r // tr,), in_specs=[spec] * (npart + 3), out_specs=[spec] * 4,
        out_shape=[jax.ShapeDtypeStruct((r, c), F32)] * 4, compiler_params=_cparams(),
    )(*parts, w, m, v)


ROW_TM = 256


def _row(v):
    return v.reshape(1, -1)


def _lane_pad(vals, at):
    return jnp.concatenate([jnp.zeros((at,), F32), vals, jnp.zeros((LANES - at - vals.shape[0],), F32)]).reshape(1, LANES)


def _layer_params(sm, l):
    return dict(
        conf=[sm["conv_dw_w"][l], _row(sm["conv_dw_b"][l]), _row(sm["conv_ln_g"][l]), _row(sm["conv_ln_b"][l])],
        pre=[sm["gdn_conv_q"][l], sm["gdn_conv_k"][l], sm["gdn_conv_v"][l],
             _lane_pad(sm["gdn_a_log"][l], GDN_HEADS), _lane_pad(sm["gdn_dt_bias"][l], GDN_HEADS)],
        scan=_row(sm["gdn_norm_g"][l]),
        sgu=[_row(sm["sgu_ln_g"][l]), _row(sm["sgu_ln_b"][l]), sm["sgu_w_s"][l].reshape(SGU_GROUPS * SGU_CHUNK, SGU_CHUNK),
             sm["sgu_b_s"][l].T],
        mix=[_row(sm["b_gate"][l])],
        ln1=[_row(sm["ln1_g"][l]), _row(sm["ln1_b"][l])],
        ff=[_row(sm["b_ff1"][l])],
        ln2=[_row(sm["b_ff2"][l]), _row(sm["ln2_g"][l]), _row(sm["ln2_b"][l])],
    )


def _split_w_in(w):
    zpad = jnp.zeros((D_MODEL, LANES - 2 * GDN_HEADS), w.dtype)
    return dict(A=w[:, 0:1024], B=jnp.concatenate([w[:, 1024:2560], w[:, 3072:3080], zpad], axis=1), Z=w[:, 2560:3072],
                C=w[:, 3080:4104], G=w[:, 4104:7176])


def _join_w_in(d):
    return jnp.concatenate([d["A"], d["B"][:, 0:1536], d["Z"], d["B"][:, 1536:1544], d["C"], d["G"]], axis=1)


def _hosted(get, host):
    got = get(host) if get is not None else None
    return got if got is not None else (None, lambda results: None)


def layer_forward(l, xin, xin_b, big, lp, fetch=None):
    tag = f"l{l}_"
    wi = big["w_in"][l]
    sv = dict(xin=xin, xin_b=xin_b)
    for r in "GACBZ":
        sv["p" + r] = matmul(tag + "proj_" + r, xin_b, wi[r], "nn", ACT)
    ride, done = _hosted(fetch, "conformer")
    sv["ya_in"], *rode = halo_fwd(tag + "conformer", f_conformer, ROW_TM, CONV_HALO, sv["pA"], lp["conf"], [[BF16]], ride)
    done(rode)
    ride, done = _hosted(fetch, "gdn_pre")
    sv["qn"], sv["kn"], sv["vv"], sv["gb"], *rode = halo_fwd(tag + "gdn_pre", f_gdn_pre, ROW_TM, GDN_HALO, sv["pB"], lp["pre"],
                                                             [[F32]] * 4, ride)
    done(rode)
    ride, done = _hosted(fetch, "gdn_prep")
    prep_out = prep_fwd(tag + "gdn_prep", sv["qn"], sv["kn"], sv["vv"], sv["gb"], ride)
    prep, sv["tinv"] = prep_out[:6], prep_out[6]
    done(prep_out[7:])
    sv["prep"] = prep
    ride, done = _hosted(fetch, "gdn_scan")
    sv["yb_in"], sv["states"], *rode = scan_fwd(tag + "gdn_scan", lp["scan"], *prep, sv["pZ"], ride)
    done(rode)
    (sv["yc_in"],) = stage_fwd(tag + "sgu", f_sgu, ROW_TM, [sv["pC"]], lp["sgu"], [[BF16]])
    sv["ya"] = matmul(tag + "out_a", sv["ya_in"], big["w_pa"][l], "nn", ACT)
    sv["yb"] = matmul(tag + "out_b", sv["yb_in"], big["w_pb"][l], "nn", ACT)
    sv["yc"] = matmul(tag + "out_c", sv["yc_in"], big["w_pc"][l], "nn", ACT)
    (sv["mix"],) = stage_fwd(tag + "mix", f_mix, ROW_TM, [sv["ya"], sv["yb"], sv["yc"], sv["pG"]], lp["mix"], [[BF16]])
    sv["m"] = matmul(tag + "w_o", sv["mix"], big["w_o"][l], "nn", F32)
    sv["x1"], sv["x1_b"] = stage_fwd(tag + "ln1", f_ln1, ROW_TM, [xin, sv["m"]], lp["ln1"], [[F32, BF16]])
    sv["h"], sv["hpre"] = matmul(tag + "ff1", sv["x1_b"], big["w_ff1"][l], "nn", BF16, fuse=("relu2", lp["ff"][0]))
    sv["f2"] = matmul(tag + "ff2", sv["h"], big["w_ff2"][l], "nn", F32)
    x2, x2_b = stage_fwd(tag + "ln2", f_ln2, ROW_TM, [sv["x1"], sv["f2"]], lp["ln2"], [[F32, BF16]])
    return x2, x2_b, sv


def layer_backward(l, dx2, sv, big, lp, send=None):
    tag = f"l{l}_b_"
    g = {}
    get = (lambda host: send(host, g)) if send is not None else None
    dx1_a, df2, db2, dg2, dbe2 = stage_bwd(tag + "ln2", f_ln2, ROW_TM, [sv["x1"], sv["f2"]], lp["ln2"], [dx2], [F32, BF16])
    g["b_ff2"], g["ln2_g"], g["ln2_b"] = db2[0], dg2[0], dbe2[0]
    g["w_ff2"] = matmul(tag + "dw_ff2", sv["h"], df2, "tn", F32)
    dhpre, db1 = matmul(tag + "dx_ff2", df2, big["w_ff2"][l], "nt", BF16, fuse=("relu2_bwd", sv["hpre"]))
    g["b_ff1"] = db1[0]
    g["w_ff1"] = matmul(tag + "dw_ff1", sv["x1_b"], dhpre, "tn", F32)
    dx1_b = matmul(tag + "dx_ff1", dhpre, big["w_ff1"][l], "nt", F32)
    dxin_a, dm, dg1, dbe1 = stage_bwd(tag + "ln1", f_ln1, ROW_TM, [sv["xin"], sv["m"]], lp["ln1"], [[dx1_a, dx1_b]], [F32, BF16])
    g["ln1_g"], g["ln1_b"] = dg1[0], dbe1[0]
    g["w_o"] = matmul(tag + "dw_o", sv["mix"], dm, "tn", F32)
    dmix = matmul(tag + "dx_o", dm, big["w_o"][l], "nt", ACT)
    dya, dyb, dyc, dp_g, dbg = stage_bwd(tag + "mix", f_mix, ROW_TM, [sv["ya"], sv["yb"], sv["yc"], sv["pG"]], lp["mix"],
                                         [[dmix]], [BF16] * 4)
    g["b_gate"] = dbg[0]
    g["w_pa"] = matmul(tag + "dw_pa", sv["ya_in"], dya, "tn", F32)
    g["w_pb"] = matmul(tag + "dw_pb", sv["yb_in"], dyb, "tn", F32)
    g["w_pc"] = matmul(tag + "dw_pc", sv["yc_in"], dyc, "tn", F32)
    dya_in = matmul(tag + "dx_pa", dya, big["w_pa"][l], "nt", ACT)
    dyb_in = matmul(tag + "dx_pb", dyb, big["w_pb"][l], "nt", ACT)
    dyc_in = matmul(tag + "dx_pc", dyc, big["w_pc"][l], "nt", ACT)
    ride, done = _hosted(get, "conformer")
    dp_a, dcw, dcb, dcg, dcbe, *rode = halo_bwd(tag + "conformer", f_conformer, ROW_TM, CONV_HALO, sv["pA"], lp["conf"], [dya_in],
                                                BF16, ride)
    done(rode)
    g["conv_dw_w"], g["conv_dw_b"], g["conv_ln_g"], g["conv_ln_b"] = dcw, dcb[0], dcg[0], dcbe[0]
    dp_c, dsg, dsb, dsw, dsbs = stage_bwd(tag + "sgu", f_sgu, ROW_TM, [sv["pC"]], lp["sgu"], [[dyc_in]], [BF16])
    g["sgu_ln_g"], g["sgu_ln_b"] = dsg[0], dsb[0]
    g["sgu_w_s"] = dsw.reshape(SGU_GROUPS, SGU_CHUNK, SGU_CHUNK)
    g["sgu_b_s"] = dsbs.T
    ride, done = _hosted(get, "gdn_scan")
    scan_out = scan_bwd(tag + "gdn_scan", lp["scan"], sv["states"], *sv["prep"], sv["pZ"], dyb_in, ride)
    dprep, dp_z, dng = scan_out[:6], scan_out[6], scan_out[7]
    done(scan_out[8:])
    g["gdn_norm_g"] = dng[0]
    dqn, dkn, dvv, dgb = prep_bwd(tag + "gdn_prep", sv["qn"], sv["kn"], sv["vv"], sv["gb"], sv["tinv"], dprep)
    ride, done = _hosted(get, "gdn_pre")
    dp_b, dq, dk, dv, dal, ddt, *rode = halo_bwd(tag + "gdn_pre", f_gdn_pre, ROW_TM, GDN_HALO, sv["pB"], lp["pre"],
                                                 [dqn, dkn, dvv, dgb], BF16, ride)
    done(rode)
    g["gdn_conv_q"], g["gdn_conv_k"], g["gdn_conv_v"] = dq, dk, dv
    g["gdn_a_log"] = dal[0, GDN_HEADS:2 * GDN_HEADS]
    g["gdn_dt_bias"] = ddt[0, GDN_HEADS:2 * GDN_HEADS]
    dps = dict(G=dp_g, A=dp_a, C=dp_c, B=dp_b, Z=dp_z)
    wi = big["w_in"][l]
    g["w_in"] = _join_w_in({r: matmul(tag + "dw_in_" + r, sv["xin_b"], dps[r], "tn", F32) for r in "GACBZ"})
    dxin_b = None
    for r in "GACBZ":
        dxin_b = matmul(tag + "dx_in_" + r, dps[r], wi[r], "nt", F32, acc=dxin_b)
    return [dxin_a, dxin_b], g


SMALL_PER_LAYER = ("b_gate", "conv_dw_w", "conv_dw_b", "conv_ln_g", "conv_ln_b", "gdn_conv_q", "gdn_conv_k", "gdn_conv_v",
                   "gdn_a_log", "gdn_dt_bias", "gdn_norm_g", "sgu_ln_g", "sgu_ln_b", "sgu_w_s", "sgu_b_s", "ln1_g", "ln1_b",
                   "b_ff1", "b_ff2", "ln2_g", "ln2_b")
BIG = ("w_in", "w_pa", "w_pb", "w_pc", "w_o", "w_ff1", "w_ff2")


def local_step(x, target, big, sm, fetch=None, send=None):
    lps = [_layer_params(sm, l) for l in range(DEPTH)]
    p_in = [_row(sm["ln_in_g"]), _row(sm["ln_in_b"])]
    xc, xc_b = stage_fwd("ln_in", f_ln_in, ROW_TM, [x], p_in, [[F32, BF16]])
    saved = []
    for l in range(DEPTH):
        xc, xc_b, sv = layer_forward(l, xc, xc_b, big, lps[l], fetch if l == 0 else None)
        saved.append(sv)
    loss_part, dy = loss_stage("loss", xc, target, ROW_TM)
    dx = [dy]
    per_layer = [None] * DEPTH

    def send_from(l):
        if send is None or l != 0:
            return None
        return lambda host, g: send(host, per_layer[:l] + [g] + per_layer[l + 1:])

    for l in reversed(range(DEPTH)):
        dx, per_layer[l] = layer_backward(l, dx, saved[l], big, lps[l], send_from(l))
    ride, done = _hosted((lambda host: send(host, per_layer)) if send is not None else None, "ln_in")
    grad_x, dgi, dbi, *rode = stage_bwd("ln_in_b", f_ln_in, ROW_TM, [x], p_in, [dx], [F32], ride)
    done(rode)
    small = {n: jnp.stack([per_layer[l][n] for l in range(DEPTH)]) for n in SMALL_PER_LAYER}
    small["ln_in_g"], small["ln_in_b"] = dgi[0], dbi[0]
    return loss_part, grad_x, small, per_layer


PARAM_NAMES = ("ln_in_g", "ln_in_b", "w_in", "b_gate", "conv_dw_w", "conv_dw_b", "conv_ln_g", "conv_ln_b", "w_pa",
               "gdn_conv_q", "gdn_conv_k", "gdn_conv_v", "gdn_a_log", "gdn_dt_bias", "gdn_norm_g", "w_pb", "sgu_ln_g",
               "sgu_ln_b", "sgu_w_s", "sgu_b_s", "w_pc", "w_o", "ln1_g", "ln1_b", "w_ff1", "b_ff1", "w_ff2", "b_ff2",
               "ln2_g", "ln2_b")
SMALL = tuple(n for n in PARAM_NAMES if n not in BIG)
COL_SHARDED_SMALL = ("conv_dw_w", "gdn_conv_q", "gdn_conv_k", "gdn_conv_v")
ROW_SHARDED_BIG = ("w_o", "w_ff2")
ADAM_ROWS = 128
_OUT_PROJ = ("w_pa", "w_pb", "w_pc", "w_o")
FETCH_BEHIND = {"conformer": [("w_ff1", 0), ("w_ff2", 0)], "gdn_pre": [(n, l) for l in (0, 1) for n in _OUT_PROJ],
                "gdn_prep": [("w_in", 1), ("w_ff1", 1)], "gdn_scan": [("w_ff2", 1)]}
SEND_BEHIND = {"conformer": [(n, 1) for n in BIG], "gdn_scan": [("w_ff1", 0), ("w_ff2", 0)],
               "gdn_pre": [(n, 0) for n in _OUT_PROJ], "ln_in": [("w_in", 0)]}


def _unshard(g, rows):
    n, r, c = g.shape
    if rows:
        return g.reshape(n * r, c)
    return jnp.moveaxis(g, 0, 1).reshape(r, n * c)


def _to_shards(w, rows):
    r, c = w.shape
    if rows:
        return w.reshape(N_CHIPS, r // N_CHIPS, c)
    return jnp.moveaxis(w.reshape(r, N_CHIPS, c // N_CHIPS), 1, 0)


def _pack(vals, rows_multiple=8):
    flat = jnp.concatenate([v.reshape(-1) for v in vals])
    per = LANES * rows_multiple
    padded = -(-flat.shape[0] // per) * per
    return jnp.pad(flat, (0, padded - flat.shape[0])).reshape(-1, LANES)


def _unpack(packed, shapes):
    flat = packed.reshape(-1)
    out, at = [], 0
    for s in shapes:
        size = 1
        for d in s:
            size *= d
        out.append(flat[at:at + size].reshape(s))
        at += size
    return out


def kernel(x, ln_in_g, ln_in_b, w_in, b_gate, conv_dw_w, conv_dw_b, conv_ln_g, conv_ln_b, w_pa, gdn_conv_q, gdn_conv_k, gdn_conv_v, gdn_a_log, gdn_dt_bias, gdn_norm_g, w_pb, sgu_ln_g, sgu_ln_b, sgu_w_s, sgu_b_s, w_pc, w_o, ln1_g, ln1_b, w_ff1, b_ff1, w_ff2, b_ff2, ln2_g, ln2_b, loss_target, m_ln_in_g, m_ln_in_b, m_w_in, m_b_gate, m_conv_dw_w, m_conv_dw_b, m_conv_ln_g, m_conv_ln_b, m_w_pa, m_gdn_conv_q, m_gdn_conv_k, m_gdn_conv_v, m_gdn_a_log, m_gdn_dt_bias, m_gdn_norm_g, m_w_pb, m_sgu_ln_g, m_sgu_ln_b, m_sgu_w_s, m_sgu_b_s, m_w_pc, m_w_o, m_ln1_g, m_ln1_b, m_w_ff1, m_b_ff1, m_w_ff2, m_b_ff2, m_ln2_g, m_ln2_b, v_ln_in_g, v_ln_in_b, v_w_in, v_b_gate, v_conv_dw_w, v_conv_dw_b, v_conv_ln_g, v_conv_ln_b, v_w_pa, v_gdn_conv_q, v_gdn_conv_k, v_gdn_conv_v, v_gdn_a_log, v_gdn_dt_bias, v_gdn_norm_g, v_w_pb, v_sgu_ln_g, v_sgu_ln_b, v_sgu_w_s, v_sgu_b_s, v_w_pc, v_w_o, v_ln1_g, v_ln1_b, v_w_ff1, v_b_ff1, v_w_ff2, v_b_ff2, v_ln2_g, v_ln2_b):
    w = dict(zip(PARAM_NAMES, (ln_in_g, ln_in_b, w_in, b_gate, conv_dw_w, conv_dw_b, conv_ln_g, conv_ln_b, w_pa, gdn_conv_q, gdn_conv_k, gdn_conv_v, gdn_a_log, gdn_dt_bias, gdn_norm_g, w_pb, sgu_ln_g, sgu_ln_b, sgu_w_s, sgu_b_s, w_pc, w_o, ln1_g, ln1_b, w_ff1, b_ff1, w_ff2, b_ff2, ln2_g, ln2_b)))
    m = dict(zip(PARAM_NAMES, (m_ln_in_g, m_ln_in_b, m_w_in, m_b_gate, m_conv_dw_w, m_conv_dw_b, m_conv_ln_g, m_conv_ln_b, m_w_pa, m_gdn_conv_q, m_gdn_conv_k, m_gdn_conv_v, m_gdn_a_log, m_gdn_dt_bias, m_gdn_norm_g, m_w_pb, m_sgu_ln_g, m_sgu_ln_b, m_sgu_w_s, m_sgu_b_s, m_w_pc, m_w_o, m_ln1_g, m_ln1_b, m_w_ff1, m_b_ff1, m_w_ff2, m_b_ff2, m_ln2_g, m_ln2_b)))
    v = dict(zip(PARAM_NAMES, (v_ln_in_g, v_ln_in_b, v_w_in, v_b_gate, v_conv_dw_w, v_conv_dw_b, v_conv_ln_g, v_conv_ln_b, v_w_pa, v_gdn_conv_q, v_gdn_conv_k, v_gdn_conv_v, v_gdn_a_log, v_gdn_dt_bias, v_gdn_norm_g, v_w_pb, v_sgu_ln_g, v_sgu_ln_b, v_sgu_w_s, v_sgu_b_s, v_w_pc, v_w_o, v_ln1_g, v_ln1_b, v_w_ff1, v_b_ff1, v_w_ff2, v_b_ff2, v_ln2_g, v_ln2_b)))
    chip = 2 * lax.axis_index("x") + lax.axis_index("y")

    shards = {n: w[n].astype(BF16) for n in BIG}
    col_small_shapes = [w[n].shape for n in COL_SHARDED_SMALL]
    big = {n: [None] * DEPTH for n in BIG}

    def place(keys, gathered):
        for (n, l), g in zip(keys, gathered):
            whole = _unshard(g, n in ROW_SHARDED_BIG)
            big[n][l] = _split_w_in(whole) if n == "w_in" else whole

    def fetch(host):
        keys = FETCH_BEHIND[host]
        return gather_rider([shards[n][l] for n, l in keys], []), lambda gathered: place(keys, gathered)

    first = run_alone("gather_first", gather_rider([shards["w_in"][0]], [_pack([w[n] for n in COL_SHARDED_SMALL])]))
    place([("w_in", 0)], first[:1])
    sm = {n: w[n] for n in SMALL}
    per_chip = [_unpack(first[-1][s], col_small_shapes) for s in range(N_CHIPS)]
    for i, n in enumerate(COL_SHARDED_SMALL):
        sm[n] = jnp.concatenate([per_chip[s][i] for s in range(N_CHIPS)], axis=-1)

    received = {}

    def send(host, per_layer):
        keys = SEND_BEHIND[host]
        ride = scatter_rider([_to_shards(per_layer[l][n], n in ROW_SHARDED_BIG).astype(BF16) for n, l in keys])
        return ride, lambda results: received.update(zip(keys, results))

    loss_part, grad_x, grads, _ = local_step(x[0], loss_target[0], big, sm, fetch, send)
    partial = [sum_slots("sum_chips_" + n, [received[(n, l)] for l in range(DEPTH)], ADAM_ROWS) for n in BIG]
    sibling = sibling_exchange("swap_cores", partial)
    out = {}
    for n, mine, theirs in zip(BIG, partial, sibling):
        shape = w[n].shape
        res = adamw("adamw_" + n, [mine, theirs], w[n].reshape(-1, shape[-1]), m[n].reshape(-1, shape[-1]),
                    v[n].reshape(-1, shape[-1]), ADAM_ROWS)
        out[n] = [r.reshape(shape) for r in res]

    small_shapes = [grads[n].shape for n in SMALL]
    vec = _pack([grads[n] for n in SMALL] + [jnp.sum(loss_part).reshape(1)])
    total = sum_slots("sum_small", [gather_all("gather_small", vec)], vec.shape[0])
    whole = _unpack(total, small_shapes + [(1,)])
    loss = whole[-1][0]
    g_small = {}
    for n, g in zip(SMALL, whole[:-1]):
        if n in COL_SHARDED_SMALL:
            width = g.shape[-1] // N_CHIPS
            g = lax.dynamic_slice_in_dim(g, chip * width, width, axis=g.ndim - 1)
        g_small[n] = g
    local_shapes = [w[n].shape for n in SMALL]
    packed_g = _pack([g_small[n] for n in SMALL])
    res = adamw("adamw_small", [packed_g], _pack([w[n] for n in SMALL]), _pack([m[n] for n in SMALL]),
                _pack([v[n] for n in SMALL]), packed_g.shape[0])
    unpacked = [_unpack(r, local_shapes) for r in res]
    for i, n in enumerate(SMALL):
        out[n] = [unpacked[k][i] for k in range(4)]

    return (loss, grad_x[None], *[out[n][0] for n in PARAM_NAMES], *[out[n][1] for n in PARAM_NAMES],
            *[out[n][2] for n in PARAM_NAMES], *[out[n][3] for n in PARAM_NAMES])
```

```python
import functools

import jax
import jax.numpy as jnp
from jax import lax
from jax.experimental import pallas as pl
from jax.experimental.pallas import tpu as pltpu

F32 = jnp.float32
BF16 = jnp.bfloat16
ACT = BF16

D_MODEL = 1024
DEPTH = 2
CONV_DIM = 512
CONV_WIDTH = 31
CONV_HALO = 32
GDN_HEADS = 4
GDN_HD = 128
GDN_CHUNK = 64
GDN_PAIR = 2 * GDN_CHUNK
GDN_CONV = 4
GDN_HALO = 16
SGU_GROUPS = 4
SGU_GD = 128
SGU_CHUNK = 128
D_FF = 4096
ALPHA = (2 * DEPTH) ** 0.25
LN_EPS = 1e-5
RMS_EPS = 1e-6
ADAM_LR, ADAM_B1, ADAM_B2, ADAM_EPS, ADAM_WD, ADAM_STEP = 0.001, 0.9, 0.999, 1e-08, 0.01, 10
N_CHIPS = 4
N_DEV = 8
LANES = 128
VMEM_LIMIT = 52 * 1024 * 1024

W_GATE, W_A, W_C, W_B, W_Z = 3 * D_MODEL, 2 * CONV_DIM, 2 * 512, 3 * 512 + LANES, 512


def _dg(a, b, ca, cb):
    return lax.dot_general(a.astype(BF16), b.astype(BF16), (((ca,), (cb,)), ((), ())), preferred_element_type=F32)


@jax.custom_vjp
def mm_nn(a, b):
    return _dg(a, b, 1, 0)


mm_nn.defvjp(lambda a, b: (_dg(a, b, 1, 0), (a, b)), lambda r, g: (_dg(g, r[1], 1, 1), _dg(r[0], g, 0, 0)))


@jax.custom_vjp
def mm_nt(a, b):
    return _dg(a, b, 1, 1)


mm_nt.defvjp(lambda a, b: (_dg(a, b, 1, 1), (a, b)), lambda r, g: (_dg(g, r[1], 1, 0), _dg(g, r[0], 0, 0)))


@jax.custom_vjp
def mm_tn(a, b):
    return _dg(a, b, 0, 0)


mm_tn.defvjp(lambda a, b: (_dg(a, b, 0, 0), (a, b)), lambda r, g: (_dg(r[1], g, 1, 1), _dg(r[0], g, 1, 0)))


def _ln(x, g, b):
    mu = jnp.mean(x, -1, keepdims=True)
    xc = x - mu
    var = jnp.mean(xc * xc, -1, keepdims=True)
    return xc * lax.rsqrt(var + LN_EPS) * g + b


def _iota2(shape, dim):
    return lax.broadcasted_iota(jnp.int32, shape, dim)


def f_ln_in(params, x):
    g, b = params
    return (_ln(x, g, b),)


def _roll_rows(x, s):
    return x if s == 0 else pltpu.roll(x, s, 0)


@functools.partial(jax.custom_vjp, nondiff_argnums=(1,))
def shift_rows(x, s):
    return _roll_rows(x, s)


shift_rows.defvjp(lambda x, s: (_roll_rows(x, s), None),
                  lambda s, _, ct: (_roll_rows(ct, (ct.shape[0] - s) % ct.shape[0]),))


def _causal_conv(full, w, halo):
    width = w.shape[0]
    groups = []
    for g in range(full.shape[1] // LANES):
        ls = slice(g * LANES, (g + 1) * LANES)
        xg, wg = full[:, ls], w[:, ls]
        acc = wg[width - 1:width, :] * xg[halo:, :]
        for j in range(width - 1):
            acc = acc + wg[j:j + 1, :] * shift_rows(xg, width - 1 - j)[halo:, :]
        groups.append(acc)
    return jnp.concatenate(groups, axis=1)


def f_conformer(params, halo, pa):
    w, b, g, be = params
    full = jnp.concatenate([halo, pa], axis=0)
    h = full[:, :CONV_DIM] * jax.nn.sigmoid(full[:, CONV_DIM:])
    return (jax.nn.silu(_ln(_causal_conv(h, w, CONV_HALO) + b, g, be)),)


def f_gdn_pre(params, halo, pb):
    cq, ck, cv, alog, dtb = params
    full = jnp.concatenate([halo, pb], axis=0)

    def conv(cols, w):
        return jax.nn.silu(_causal_conv(cols, w, GDN_HALO))

    def l2n(t, scale):
        parts = []
        for h in range(GDN_HEADS):
            th = t[:, h * GDN_HD:(h + 1) * GDN_HD]
            parts.append(th * (lax.rsqrt(jnp.sum(th * th, -1, keepdims=True) + RMS_EPS) * scale))
        return jnp.concatenate(parts, axis=1)

    q = l2n(conv(full[:, 0:512], cq), GDN_HD ** -0.5)
    k = l2n(conv(full[:, 512:1024], ck), 1.0)
    v = conv(full[:, 1024:1536], cv)
    logits = pb[:, 1536:1536 + LANES]
    z = logits + dtb
    softplus = jnp.maximum(z, 0.0) + jnp.log(1.0 + jnp.exp(-jnp.abs(z)))
    g = -jnp.exp(alog) * softplus
    beta = jax.nn.sigmoid(logits)
    lane = _iota2(logits.shape, 1)
    gb = jnp.where(lane < GDN_HEADS, beta, jnp.where(lane < 2 * GDN_HEADS, g, 0.0))
    return q, k, v, gb


def _split2(x):
    hi = x.astype(BF16)
    return hi, (x - hi.astype(F32)).astype(BF16)


def _dg3(a, b, ca, cb):
    ah, al = _split2(a)
    bh, bl = _split2(b)
    d = lambda p, q: lax.dot_general(p, q, (((ca,), (cb,)), ((), ())), preferred_element_type=F32)
    return d(ah, bh) + (d(ah, bl) + d(al, bh))


def _inv_unit_lower(lows):
    n = lows[0].shape[0]
    eye = jnp.where(_iota2((n, n), 0) == _iota2((n, n), 1), 1.0, 0.0).astype(F32)
    ps = [-low for low in lows]
    xs = [eye + p for p in ps]
    for _ in range(5):
        ps = [_dg3(p, p, 1, 0) for p in ps]
        xs = [x + _dg3(x, p, 1, 0) for x, p in zip(xs, ps)]
    return xs


def _pair_masks():
    n = GDN_PAIR
    r, c = _iota2((n, n), 0), _iota2((n, n), 1)
    same = (r >= GDN_CHUNK) == (c >= GDN_CHUNK)
    return same, jnp.logical_and(same, r >= c), jnp.logical_and(same, r > c), jnp.logical_and(same, r <= c)


def _masked_sum(mask, g):
    mk = jnp.where(mask, 1.0, 0.0).astype(BF16)
    g1 = g.astype(BF16)
    r1 = g - g1.astype(F32)
    g2 = r1.astype(BF16)
    g3 = (r1 - g2.astype(F32)).astype(BF16)
    d = lambda q: jnp.dot(mk, q, preferred_element_type=F32)
    return d(g1) + (d(g2) + d(g3))


@jax.custom_vjp
def chunk_cumsum(g):
    return _masked_sum(_pair_masks()[1], g)


chunk_cumsum.defvjp(lambda g: (_masked_sum(_pair_masks()[1], g), None), lambda _, ct: (_masked_sum(_pair_masks()[3], ct),))


def _prep_gates(gb):
    gam_all = chunk_cumsum(gb)
    first = _iota2((GDN_PAIR, LANES), 0) < GDN_CHUNK
    glast = jnp.where(first, gam_all[GDN_CHUNK - 1:GDN_CHUNK, :], gam_all[GDN_PAIR - 1:GDN_PAIR, :])
    return gam_all, gam_all.T, glast


def _head_gates(gb, gam_all, gam_t, h, causal):
    gc = gam_all[:, GDN_HEADS + h:GDN_HEADS + h + 1]
    gr = gam_t[GDN_HEADS + h:GDN_HEADS + h + 1, :]
    return gc, gb[:, h:h + 1], jnp.exp(jnp.where(causal, gc - gr, -jnp.inf))


def f_prep_low(kn, gb):
    _, causal, strict, _ = _pair_masks()
    gam_all, gam_t, _ = _prep_gates(gb)
    lows = []
    for h in range(GDN_HEADS):
        k = kn[:, h * GDN_HD:(h + 1) * GDN_HD]
        _, beta, decay = _head_gates(gb, gam_all, gam_t, h, causal)
        lows.append(jnp.where(strict, beta * mm_nt(k, k) * decay, 0.0))
    return jnp.concatenate(lows, axis=1)


def f_prep_rest(qn, kn, vv, gb, tinv_all):
    _, causal, _, _ = _pair_masks()
    gam_all, gam_t, glast = _prep_gates(gb)
    us, ws, as_, qds, kds = [], [], [], [], []
    for h in range(GDN_HEADS):
        hs = slice(h * GDN_HD, (h + 1) * GDN_HD)
        q, k, v, tinv = qn[:, hs], kn[:, hs], vv[:, hs], tinv_all[:, hs]
        gc, beta, decay = _head_gates(gb, gam_all, gam_t, h, causal)
        eg = jnp.exp(gc)
        us.append(mm_nn(tinv, beta * v))
        ws.append(mm_nn(tinv, beta * k * eg))
        as_.append(mm_nt(q, k) * decay)
        qds.append(q * eg)
        kds.append(k * jnp.exp(glast[:, GDN_HEADS + h:GDN_HEADS + h + 1] - gc))
    cat = lambda xs: jnp.concatenate(xs, axis=1)
    return cat(us), cat(ws), cat(as_), cat(qds), cat(kds), jnp.exp(glast)


PREP_TM = 2 * GDN_PAIR


def _ride(ride, n_in, n_out, refs, n_steps):
    if ride is None:
        return
    nri, nro = len(ride["ins"]), len(ride["out_shape"])
    r_in = refs[n_in:n_in + nri]
    r_out = refs[n_in + nri + n_out:n_in + nri + n_out + nro]
    sems = refs[len(refs) - len(ride["scratch"]):]
    step = pl.program_id(0)

    @pl.when(step == 0)
    def _():
        ride["start"](r_in, r_out, sems)

    @pl.when(step == (2 * n_steps) // 3)
    def _():
        ride["middle"](r_in, r_out, sems)

    @pl.when(step == n_steps - 1)
    def _():
        ride["finish"](r_in, r_out, sems)


def _ride_args(ride):
    if ride is None:
        return [], [], [], [], []
    n = len(ride["ins"])
    return list(ride["ins"]), [ANY_SPEC] * n, [ANY_SPEC] * len(ride["out_shape"]), list(ride["out_shape"]), list(ride["scratch"])


def prep_fwd(name, qn, kn, vv, gb, ride=None):
    t = qn.shape[0]
    hw = GDN_HEADS * GDN_HD
    n_steps = t // PREP_TM
    r_ins, r_in_specs, r_out_specs, r_out_shape, r_scratch = _ride_args(ride)

    def body(*refs):
        q_ref, k_ref, v_ref, gb_ref = refs[:4]
        outs = refs[4 + len(r_ins):4 + len(r_ins) + 7]
        pairs = [pl.ds(pi * GDN_PAIR, GDN_PAIR) for pi in range(PREP_TM // GDN_PAIR)]
        lows = [f_prep_low(k_ref[rs, :], gb_ref[rs, :]) for rs in pairs]
        invs = _inv_unit_lower([low[:, h * GDN_HD:(h + 1) * GDN_HD] for low in lows for h in range(GDN_HEADS)])
        for pi, rs in enumerate(pairs):
            tinv = jnp.concatenate(invs[pi * GDN_HEADS:(pi + 1) * GDN_HEADS], axis=1)
            res = f_prep_rest(q_ref[rs, :], k_ref[rs, :], v_ref[rs, :], gb_ref[rs, :], tinv)
            for o_ref, val in zip(outs, res + (tinv,)):
                o_ref[rs, :] = val
        _ride(ride, 4, 7, refs, n_steps)

    widths = [hw, hw, hw, hw, hw, LANES, hw]
    return pl.pallas_call(
        body, name=name, grid=(n_steps,),
        in_specs=[_row_spec(PREP_TM, hw)] * 3 + [_row_spec(PREP_TM, LANES)] + r_in_specs,
        out_specs=[_row_spec(PREP_TM, wd) for wd in widths] + r_out_specs,
        out_shape=[jax.ShapeDtypeStruct((t, wd), F32) for wd in widths] + r_out_shape,
        scratch_shapes=r_scratch, compiler_params=_cparams(),
    )(qn, kn, vv, gb, *r_ins)


def prep_bwd(name, qn, kn, vv, gb, tinv, cts):
    t = qn.shape[0]
    hw = GDN_HEADS * GDN_HD

    def body(q_ref, k_ref, v_ref, gb_ref, x_ref, *refs):
        outs = refs[6:]
        for pi in range(PREP_TM // GDN_PAIR):
            rs = pl.ds(pi * GDN_PAIR, GDN_PAIR)
            ct = tuple(r[rs, :] for r in refs[:6])
            kn_v, gb_v, x_all = k_ref[rs, :], gb_ref[rs, :], x_ref[rs, :]
            _, vjp_low = jax.vjp(f_prep_low, kn_v, gb_v)
            _, vjp_rest = jax.vjp(f_prep_rest, q_ref[rs, :], kn_v, v_ref[rs, :], gb_v, x_all)
            dq, dk, dv, dgb, dx_all = vjp_rest(ct)
            dlows = []
            for h in range(GDN_HEADS):
                hs = slice(h * GDN_HD, (h + 1) * GDN_HD)
                dlows.append(-_dg3(_dg3(x_all[:, hs], dx_all[:, hs], 0, 0), x_all[:, hs], 1, 1))
            dk2, dgb2 = vjp_low(jnp.concatenate(dlows, axis=1))
            outs[0][rs, :] = dq
            outs[1][rs, :] = dk + dk2
            outs[2][rs, :] = dv
            outs[3][rs, :] = dgb + dgb2

    in_w = [hw, hw, hw, LANES, hw] + [hw, hw, hw, hw, hw, LANES]
    out_w = [hw, hw, hw, LANES]
    return pl.pallas_call(
        body, name=name, grid=(t // PREP_TM,),
        in_specs=[_row_spec(PREP_TM, wd) for wd in in_w], out_specs=[_row_spec(PREP_TM, wd) for wd in out_w],
        out_shape=[jax.ShapeDtypeStruct((t, wd), F32) for wd in out_w], compiler_params=_cparams(),
    )(qn, kn, vv, gb, tinv, *cts)


def f_gdn_scan(params, state, u, w, a, qd, kd, egl, z):
    ng = params[0]
    tm = u.shape[0]
    st = [state[h * GDN_HD:(h + 1) * GDN_HD, :] for h in range(GDN_HEADS)]
    zeros = jnp.zeros((GDN_CHUNK, GDN_HD), F32)
    rows = []
    hss = [slice(h * GDN_HD, (h + 1) * GDN_HD) for h in range(GDN_HEADS)]
    for ci in range(tm // GDN_CHUNK):
        rs = slice(ci * GDN_CHUNK, (ci + 1) * GDN_CHUNK)
        vns = [u[rs, hs] - mm_nn(w[rs, hs], s) for hs, s in zip(hss, st)]
        qs = [mm_nn(qd[rs, hs], s) for hs, s in zip(hss, st)]
        upd = [mm_tn(kd[rs, hs], vn) for hs, vn in zip(hss, vns)]
        es = [egl[ci * GDN_CHUNK:ci * GDN_CHUNK + 1, GDN_HEADS + h:GDN_HEADS + h + 1] for h in range(GDN_HEADS)]
        st = [s * e + d for s, e, d in zip(st, es, upd)]
        vpads = [jnp.concatenate([vn, zeros] if ci % 2 == 0 else [zeros, vn], axis=0) for vn in vns]
        os_ = [q + mm_nn(a[rs, hs], vp) for q, hs, vp in zip(qs, hss, vpads)]
        heads = [o * lax.rsqrt(jnp.mean(o * o, -1, keepdims=True) + RMS_EPS) * ng * jax.nn.silu(z[rs, hs])
                 for o, hs in zip(os_, hss)]
        rows.append(jnp.concatenate(heads, axis=1))
    return jnp.concatenate(rows, axis=0), jnp.concatenate(st, axis=0)


def f_sgu(params, puv):
    g, b, ws, bst = params
    tm = puv.shape[0]
    uv = jax.nn.gelu(puv)
    u = uv[:, :512]
    v = _ln(uv[:, 512:], g, b)
    r, c = _iota2((SGU_CHUNK, SGU_CHUNK), 0), _iota2((SGU_CHUNK, SGU_CHUNK), 1)
    rows = []
    for ci in range(tm // SGU_CHUNK):
        rs = slice(ci * SGU_CHUNK, (ci + 1) * SGU_CHUNK)
        groups = []
        for gi in range(SGU_GROUPS):
            gs = slice(gi * SGU_GD, (gi + 1) * SGU_GD)
            wt = jnp.where(r >= c, ws[gs, :], 0.0)
            groups.append(mm_nn(wt, v[rs, gs]) + bst[:, gi:gi + 1])
        rows.append(jnp.concatenate(groups, axis=1))
    return (u * jnp.concatenate(rows, axis=0),)


def f_mix(params, ya, yb, yc, pg):
    bg = params[0]
    s = jax.nn.sigmoid(pg + bg)
    return (s[:, :D_MODEL] * ya + s[:, D_MODEL:2 * D_MODEL] * yb + s[:, 2 * D_MODEL:] * yc,)


def f_ln1(params, xin, m):
    g, b = params
    return (_ln(ALPHA * xin + m, g, b),)


def f_ln2(params, x1, f2):
    b2, g, b = params
    return (_ln(ALPHA * x1 + (f2 + b2), g, b),)


def _cparams(n_axes=1):
    return pltpu.CompilerParams(dimension_semantics=("arbitrary",) * n_axes, vmem_limit_bytes=VMEM_LIMIT)


def _row_spec(tm, width, n_tiles=None):
    if n_tiles is None:
        return pl.BlockSpec((tm, width), lambda i: (i, 0))
    return pl.BlockSpec((tm, width), lambda i: (n_tiles - 1 - i, 0))


def _full_spec(shape):
    return pl.BlockSpec(shape, lambda i: (0,) * len(shape))


def stage_fwd(name, fn, tm, rows, params, outs, ride=None):
    t = rows[0].shape[0]
    nr, npar = len(rows), len(params)
    widths = jax.eval_shape(lambda p, r: fn(p, *r), [jax.ShapeDtypeStruct(p.shape, F32) for p in params],
                            [jax.ShapeDtypeStruct((tm, r.shape[1]), F32) for r in rows])
    r_ins, r_in_specs, r_out_specs, r_out_shape, r_scratch = _ride_args(ride)
    n_out = sum(len(dts) for dts in outs)

    def body(*refs):
        r = [x[...].astype(F32) for x in refs[:nr]]
        p = [x[...] for x in refs[nr:nr + npar]]
        res = fn(p, *r)
        k = nr + npar + len(r_ins)
        for o, dts in zip(res, outs):
            for dt in dts:
                refs[k][...] = o.astype(dt)
                k += 1
        _ride(ride, nr + npar, n_out, refs, t // tm)

    out_shape, out_specs = [], []
    for wd, dts in zip(widths, outs):
        for dt in dts:
            out_shape.append(jax.ShapeDtypeStruct((t, wd.shape[1]), dt))
            out_specs.append(_row_spec(tm, wd.shape[1]))
    return pl.pallas_call(
        body, name=name, grid=(t // tm,),
        in_specs=[_row_spec(tm, r.shape[1]) for r in rows] + [_full_spec(p.shape) for p in params] + r_in_specs,
        out_specs=out_specs + r_out_specs, out_shape=out_shape + r_out_shape, scratch_shapes=r_scratch,
        compiler_params=_cparams(),
    )(*rows, *params, *r_ins)


def stage_bwd(name, fn, tm, rows, params, cts, drow_dtypes, ride=None):
    t = rows[0].shape[0]
    nr, npar = len(rows), len(params)
    flat_cts = [c for cl in cts if cl is not None for c in cl]
    nct = len(flat_cts)
    want = [i for i, dt in enumerate(drow_dtypes) if dt is not None]
    r_ins, r_in_specs, r_out_specs, r_out_shape, r_scratch = _ride_args(ride)

    def body(*refs):
        r = [x[...].astype(F32) for x in refs[:nr]]
        p = [x[...] for x in refs[nr:nr + npar]]
        res, vjp = jax.vjp(lambda pp, rr: fn(pp, *rr), p, r)
        k = nr + npar
        ct = []
        for o, cl in zip(res, cts):
            if cl is None:
                ct.append(jnp.zeros_like(o))
            else:
                acc = refs[k][...].astype(F32)
                for j in range(1, len(cl)):
                    acc = acc + refs[k + j][...].astype(F32)
                k += len(cl)
                ct.append(acc)
        dp, dr = vjp(tuple(ct))
        k += len(r_ins)
        for i in want:
            refs[k][...] = dr[i].astype(drow_dtypes[i])
            k += 1

        @pl.when(pl.program_id(0) == 0)
        def _():
            for j in range(npar):
                refs[k + j][...] = jnp.zeros_like(refs[k + j])

        for j in range(npar):
            refs[k + j][...] += dp[j]
        _ride(ride, nr + npar + nct, len(want) + npar, refs, t // tm)

    out_shape = [jax.ShapeDtypeStruct(rows[i].shape, drow_dtypes[i]) for i in want]
    out_specs = [_row_spec(tm, rows[i].shape[1]) for i in want]
    out_shape += [jax.ShapeDtypeStruct(p.shape, F32) for p in params]
    out_specs += [_full_spec(p.shape) for p in params]
    return pl.pallas_call(
        body, name=name, grid=(t // tm,),
        in_specs=[_row_spec(tm, r.shape[1]) for r in rows] + [_full_spec(p.shape) for p in params]
        + [_row_spec(tm, c.shape[1]) for c in flat_cts] + r_in_specs,
        out_specs=out_specs + r_out_specs, out_shape=out_shape + r_out_shape, scratch_shapes=r_scratch,
        compiler_params=_cparams(),
    )(*rows, *params, *flat_cts, *r_ins)


def _halo_spec(tm, halo, width, n_tiles=None):
    per = tm // halo
    if n_tiles is None:
        return pl.BlockSpec((halo, width), lambda i: (jnp.maximum(i * per - 1, 0), 0))
    return pl.BlockSpec((halo, width), lambda i: (jnp.maximum((n_tiles - 1 - i) * per - 1, 0), 0))


def halo_fwd(name, fn, tm, halo, src, params, outs, ride=None):
    t, width = src.shape
    npar = len(params)
    widths = jax.eval_shape(lambda p, h, r: fn(p, h, r), [jax.ShapeDtypeStruct(p.shape, F32) for p in params],
                            jax.ShapeDtypeStruct((halo, width), F32), jax.ShapeDtypeStruct((tm, width), F32))
    r_ins, r_in_specs, r_out_specs, r_out_shape, r_scratch = _ride_args(ride)
    n_out = sum(len(dts) for dts in outs)

    def body(h_ref, x_ref, *refs):
        hal = jnp.where(pl.program_id(0) == 0, 0.0, h_ref[...].astype(F32))
        res = fn([x[...] for x in refs[:npar]], hal, x_ref[...].astype(F32))
        k = npar + len(r_ins)
        for o, dts in zip(res, outs):
            for dt in dts:
                refs[k][...] = o.astype(dt)
                k += 1
        _ride(ride, 2 + npar, n_out, (h_ref, x_ref) + refs, t // tm)

    out_shape, out_specs = [], []
    for wd, dts in zip(widths, outs):
        for dt in dts:
            out_shape.append(jax.ShapeDtypeStruct((t, wd.shape[1]), dt))
            out_specs.append(_row_spec(tm, wd.shape[1]))
    return pl.pallas_call(
        body, name=name, grid=(t // tm,),
        in_specs=[_halo_spec(tm, halo, width), _row_spec(tm, width)] + [_full_spec(p.shape) for p in params] + r_in_specs,
        out_specs=out_specs + r_out_specs, out_shape=out_shape + r_out_shape, scratch_shapes=r_scratch,
        compiler_params=_cparams(),
    )(src, src, *params, *r_ins)


def halo_bwd(name, fn, tm, halo, src, params, cts, dsrc_dtype, ride=None):
    t, width = src.shape
    n_tiles = t // tm
    npar = len(params)
    n_in = 2 + npar + len(cts)
    r_ins, r_in_specs, r_out_specs, r_out_shape, r_scratch = _ride_args(ride)

    def body(h_ref, x_ref, *refs):
        carry = refs[n_in - 2 + len(r_ins) + 1 + npar + len(r_out_shape)]
        first_tile = pl.program_id(0) == n_tiles - 1
        hal = jnp.where(first_tile, 0.0, h_ref[...].astype(F32))
        p = [x[...] for x in refs[:npar]]
        res, vjp = jax.vjp(fn, p, hal, x_ref[...].astype(F32))
        k = npar
        ct = tuple(refs[k + j][...].astype(F32) for j in range(len(cts)))
        k += len(cts) + len(r_ins)
        dp, dh, dx = vjp(ct)

        @pl.when(pl.program_id(0) == 0)
        def _():
            carry[...] = jnp.zeros_like(carry)
            for j in range(npar):
                refs[k + 1 + j][...] = jnp.zeros_like(refs[k + 1 + j])

        dx = jnp.concatenate([dx[:tm - halo, :], dx[tm - halo:, :] + carry[...]], axis=0)
        refs[k][...] = dx.astype(dsrc_dtype)
        carry[...] = dh
        for j in range(npar):
            refs[k + 1 + j][...] += dp[j]
        _ride(ride, n_in, 1 + npar, (h_ref, x_ref) + refs, n_tiles)

    out_shape = [jax.ShapeDtypeStruct((t, width), dsrc_dtype)] + [jax.ShapeDtypeStruct(p.shape, F32) for p in params]
    out_specs = [_row_spec(tm, width, n_tiles)] + [_full_spec(p.shape) for p in params]
    return pl.pallas_call(
        body, name=name, grid=(n_tiles,),
        in_specs=[_halo_spec(tm, halo, width, n_tiles), _row_spec(tm, width, n_tiles)] + [_full_spec(p.shape) for p in params]
        + [_row_spec(tm, c.shape[1], n_tiles) for c in cts] + r_in_specs,
        out_specs=out_specs + r_out_specs, out_shape=out_shape + r_out_shape,
        scratch_shapes=[pltpu.VMEM((halo, width), F32)] + r_scratch, compiler_params=_cparams(),
    )(src, src, *params, *cts, *r_ins)


SCAN_TM = 256


def scan_fwd(name, norm_g, u, w, a, qd, kd, egl, z, ride=None):
    t = u.shape[0]
    n_tiles = t // SCAN_TM
    srows = GDN_HEADS * GDN_HD

    r_ins, r_in_specs, r_out_specs, r_out_shape, r_scratch = _ride_args(ride)

    def body(*refs):
        ng_ref, u_ref, w_ref, a_ref, qd_ref, kd_ref, e_ref, z_ref = refs[:8]
        o_ref, s_ref = refs[8 + len(r_ins):10 + len(r_ins)]
        state = refs[10 + len(r_ins) + len(r_out_shape)]

        @pl.when(pl.program_id(0) == 0)
        def _():
            state[...] = jnp.zeros_like(state)

        s_ref[0] = state[...]
        o, s_new = f_gdn_scan([ng_ref[...]], state[...], u_ref[...], w_ref[...], a_ref[...], qd_ref[...], kd_ref[...],
                              e_ref[...], z_ref[...].astype(F32))
        o_ref[...] = o.astype(BF16)
        state[...] = s_new
        _ride(ride, 8, 2, refs, n_tiles)

    rows = [u, w, a, qd, kd, egl, z]
    return pl.pallas_call(
        body, name=name, grid=(n_tiles,),
        in_specs=[_full_spec(norm_g.shape)] + [_row_spec(SCAN_TM, r.shape[1]) for r in rows] + r_in_specs,
        out_specs=[_row_spec(SCAN_TM, 512), pl.BlockSpec((1, srows, GDN_HD), lambda i: (i, 0, 0))] + r_out_specs,
        out_shape=[jax.ShapeDtypeStruct((t, 512), BF16), jax.ShapeDtypeStruct((n_tiles, srows, GDN_HD), F32)] + r_out_shape,
        scratch_shapes=[pltpu.VMEM((srows, GDN_HD), F32)] + r_scratch, compiler_params=_cparams(),
    )(norm_g, *rows, *r_ins)


def scan_bwd(name, norm_g, states, u, w, a, qd, kd, egl, z, dout, ride=None):
    t = u.shape[0]
    n_tiles = t // SCAN_TM
    srows = GDN_HEADS * GDN_HD
    r_ins, r_in_specs, r_out_specs, r_out_shape, r_scratch = _ride_args(ride)

    def body(*refs):
        ng_ref, s_ref, u_ref, w_ref, a_ref, qd_ref, kd_ref, e_ref, z_ref, do_ref = refs[:10]
        du_ref, dw_ref, da_ref, dqd_ref, dkd_ref, de_ref, dz_ref, dng_ref = refs[10 + len(r_ins):18 + len(r_ins)]
        dstate = refs[18 + len(r_ins) + len(r_out_shape)]

        @pl.when(pl.program_id(0) == 0)
        def _():
            dstate[...] = jnp.zeros_like(dstate)
            dng_ref[...] = jnp.zeros_like(dng_ref)

        args = ([ng_ref[...]], s_ref[0], u_ref[...], w_ref[...], a_ref[...], qd_ref[...], kd_ref[...], e_ref[...],
                z_ref[...].astype(F32))
        _, vjp = jax.vjp(f_gdn_scan, *args)
        dp, ds, du, dw, da, dqd, dkd, de, dz = vjp((do_ref[...].astype(F32), dstate[...]))
        du_ref[...] = du
        dw_ref[...] = dw
        da_ref[...] = da
        dqd_ref[...] = dqd
        dkd_ref[...] = dkd
        de_ref[...] = de
        dz_ref[...] = dz.astype(BF16)
        dng_ref[...] += dp[0]
        dstate[...] = ds
        _ride(ride, 10, 8, refs, n_tiles)

    rows = [u, w, a, qd, kd, egl, z, dout]
    return pl.pallas_call(
        body, name=name, grid=(n_tiles,),
        in_specs=[_full_spec(norm_g.shape), pl.BlockSpec((1, srows, GDN_HD), lambda i: (n_tiles - 1 - i, 0, 0))]
        + [_row_spec(SCAN_TM, r.shape[1], n_tiles) for r in rows] + r_in_specs,
        out_specs=[_row_spec(SCAN_TM, r.shape[1], n_tiles) for r in rows[:7]] + [_full_spec(norm_g.shape)] + r_out_specs,
        out_shape=[jax.ShapeDtypeStruct(r.shape, F32) for r in rows[:6]] + [jax.ShapeDtypeStruct(z.shape, BF16)]
        + [jax.ShapeDtypeStruct(norm_g.shape, F32)] + r_out_shape,
        scratch_shapes=[pltpu.VMEM((srows, GDN_HD), F32)] + r_scratch, compiler_params=_cparams(),
    )(norm_g, states, *rows, *r_ins)


def loss_stage(name, y, target, tm):
    t, d = y.shape

    def body(y_ref, t_ref, l_ref, dy_ref):
        @pl.when(pl.program_id(0) == 0)
        def _():
            l_ref[...] = jnp.zeros_like(l_ref)

        e = y_ref[...] - t_ref[...]
        dy_ref[...] = e * (1.0 / d)
        sq = e * e
        part = sq[:, 0:LANES]
        for j in range(1, d // LANES):
            part = part + sq[:, j * LANES:(j + 1) * LANES]
        acc = part[0:8, :]
        for j in range(1, tm // 8):
            acc = acc + part[j * 8:(j + 1) * 8, :]
        l_ref[...] += acc * (0.5 / d)

    return pl.pallas_call(
        body, name=name, grid=(t // tm,),
        in_specs=[_row_spec(tm, d), _row_spec(tm, d)],
        out_specs=[_full_spec((8, LANES)), _row_spec(tm, d)],
        out_shape=[jax.ShapeDtypeStruct((8, LANES), F32), jax.ShapeDtypeStruct((t, d), F32)],
        compiler_params=_cparams(),
    )(y, target)


def _pick(n, cands):
    for c in cands:
        if n % c == 0:
            return c
    return n


def matmul(name, a, b, form, out_dtype, acc=None, fuse=None):
    if form == "nn":
        (m, k), n = a.shape, b.shape[1]
    elif form == "nt":
        (m, k), n = a.shape, b.shape[0]
    else:
        (k, m), n = a.shape, b.shape[1]
    if form == "tn":
        tm = _pick(m, (1024, 512, 256, 128))
        tn = n if n <= 2048 else _pick(n, (1536, 1024, 512, 256, 128))
        tk = _pick(k, (2048, 1024, 512, 256, 128))
    else:
        tm = _pick(m, (512, 256, 128))
        tk = k if k <= 4096 else _pick(k, (2048, 1024, 512))
        cap = 2048 * 2048 if fuse is None else 2048 * 1024
        tn = n if n * tk <= cap else _pick(n, (2048, 1536, 1024, 512, 256, 128))
    nk = k // tk
    ca, cb = {"nn": (1, 0), "nt": (1, 1), "tn": (0, 0)}[form]
    a_spec = (pl.BlockSpec((tm, tk), lambda j, i, kk: (i, kk)) if form != "tn"
              else pl.BlockSpec((tk, tm), lambda j, i, kk: (kk, i)))
    b_spec = (pl.BlockSpec((tn, tk), lambda j, i, kk: (j, kk)) if form == "nt"
              else pl.BlockSpec((tk, tn), lambda j, i, kk: (kk, j)))
    o_spec = pl.BlockSpec((tm, tn), lambda j, i, kk: (i, j))
    col_spec = pl.BlockSpec((1, tn), lambda j, i, kk: (0, j))
    has_acc = acc is not None

    assert nk == 1 or (out_dtype == F32 and fuse is None)

    def body(*refs):
        a_ref, b_ref = refs[0], refs[1]
        prod = lax.dot_general(a_ref[...], b_ref[...], (((ca,), (cb,)), ((), ())), preferred_element_type=F32)
        if fuse is not None and fuse[0] == "relu2":
            pre = prod + refs[2][...]
            refs[3][...] = jnp.square(jnp.maximum(pre, 0.0)).astype(BF16)
            refs[4][...] = pre.astype(BF16)
            return
        if fuse is not None and fuse[0] == "relu2_bwd":
            d = prod * (2.0 * jnp.maximum(refs[2][...].astype(F32), 0.0))
            refs[3][...] = d.astype(BF16)

            @pl.when(pl.program_id(1) == 0)
            def _():
                refs[4][...] = jnp.zeros_like(refs[4])

            refs[4][...] += jnp.sum(d, axis=0, keepdims=True)
            return
        o_ref = refs[-1]
        if nk == 1:
            o_ref[...] = (prod + refs[2][...] if has_acc else prod).astype(out_dtype)
            return
        kk = pl.program_id(2)

        @pl.when(kk == 0)
        def _():
            o_ref[...] = prod + refs[2][...] if has_acc else prod

        @pl.when(kk > 0)
        def _():
            o_ref[...] += prod

    ins = [a.astype(BF16), b.astype(BF16)] + ([acc] if has_acc else [])
    in_specs = [a_spec, b_spec] + ([o_spec] if has_acc else [])
    out_specs, out_shape = o_spec, jax.ShapeDtypeStruct((m, n), out_dtype)
    if fuse is not None and fuse[0] == "relu2":
        ins, in_specs = ins + [fuse[1]], in_specs + [col_spec]
        out_specs, out_shape = [o_spec, o_spec], [jax.ShapeDtypeStruct((m, n), BF16)] * 2
    elif fuse is not None:
        ins, in_specs = ins + [fuse[1]], in_specs + [o_spec]
        out_specs = [o_spec, col_spec]
        out_shape = [jax.ShapeDtypeStruct((m, n), BF16), jax.ShapeDtypeStruct((1, n), F32)]
    return pl.pallas_call(
        body, name=name, grid=(n // tn, m // tm, nk),
        in_specs=in_specs, out_specs=out_specs, out_shape=out_shape,
        input_output_aliases={2: 0} if has_acc else {},
        compiler_params=_cparams(3),
    )(*ins)


SUM_NT_TM = 256


def matmul_sum_nt(name, pairs, out_dtype):
    m, n = pairs[0][0].shape[0], pairs[0][1].shape[0]
    npair = len(pairs)

    def body(*refs):
        acc = None
        for r in range(npair):
            prod = lax.dot_general(refs[2 * r][...], refs[2 * r + 1][...], (((1,), (1,)), ((), ())), preferred_element_type=F32)
            acc = prod if acc is None else acc + prod
        refs[-1][...] = acc.astype(out_dtype)

    ins, in_specs = [], []
    for a, b in pairs:
        ins += [a.astype(BF16), b.astype(BF16)]
        in_specs += [pl.BlockSpec((SUM_NT_TM, a.shape[1]), lambda i: (i, 0)), _full_spec(b.shape)]
    return pl.pallas_call(
        body, name=name, grid=(m // SUM_NT_TM,), in_specs=in_specs, out_specs=pl.BlockSpec((SUM_NT_TM, n), lambda i: (i, 0)),
        out_shape=jax.ShapeDtypeStruct((m, n), out_dtype), compiler_params=_cparams(),
    )(*ins)


MESH_ID = pl.DeviceIdType.MESH
ANY_SPEC = pl.BlockSpec(memory_space=pl.ANY)


def _place():
    return lax.axis_index("x"), lax.axis_index("y"), lax.axis_index("c")


def _chips():
    x, y, c = _place()
    return x, y, c, 2 * x + y, [(1 - x, y), (x, 1 - y), (1 - x, 1 - y)]


def scatter_rider(arrs):
    n = len(arrs)

    def remote(ins, outs, sems, i, j, landing_from_me):
        x, y, c, me, others = _chips()
        ox, oy = others[j]
        peer = 2 * ox + oy
        return pltpu.make_async_remote_copy(src_ref=ins[i].at[peer], dst_ref=outs[i].at[me if landing_from_me else peer],
                                            send_sem=sems[0].at[3 * i + j], recv_sem=sems[1].at[3 * i + j],
                                            device_id=(ox, oy, c), device_id_type=MESH_ID)

    def local(ins, outs, sems, i):
        me = _chips()[3]
        return pltpu.make_async_copy(ins[i].at[me], outs[i].at[me], sems[2].at[i])

    def start(ins, outs, sems):
        for i in range(n):
            local(ins, outs, sems, i).start()
            for j in range(3):
                remote(ins, outs, sems, i, j, True).start()

    def finish(ins, outs, sems):
        for i in range(n):
            for j in range(3):
                remote(ins, outs, sems, i, j, False).wait_recv()
        for i in range(n):
            for j in range(3):
                remote(ins, outs, sems, i, j, True).wait_send()
            local(ins, outs, sems, i).wait()

    return dict(ins=list(arrs), out_shape=[jax.ShapeDtypeStruct(a.shape, a.dtype) for a in arrs],
                scratch=[pltpu.SemaphoreType.DMA((3 * n,)), pltpu.SemaphoreType.DMA((3 * n,)), pltpu.SemaphoreType.DMA((n,))],
                start=start, middle=lambda ins, outs, sems: None, finish=finish)


def gather_rider(split, flat):
    ns, n = len(split), len(split) + len(flat)
    arrs = list(split) + list(flat)
    half = [a.shape[0] // 2 for a in split]

    def ici(ins, outs, sems, i, j, landing_from_me):
        x, y, c, me, others = _chips()
        ox, oy = others[j]
        slot = me if landing_from_me else 2 * ox + oy
        if i < ns:
            rows = pl.ds(c * half[i], half[i])
            src, dst = ins[i].at[rows], outs[i].at[slot, rows]
        else:
            src, dst = ins[i], outs[i].at[slot]
        return pltpu.make_async_remote_copy(src_ref=src, dst_ref=dst, send_sem=sems[0].at[3 * i + j],
                                            recv_sem=sems[1].at[3 * i + j], device_id=(ox, oy, c), device_id_type=MESH_ID)

    def d2d(outs, sems, i, j, mine):
        x, y, c, me, others = _chips()
        ox, oy = others[j]
        ref = outs[i].at[2 * ox + oy, pl.ds((c if mine else 1 - c) * half[i], half[i])]
        return pltpu.make_async_remote_copy(src_ref=ref, dst_ref=ref, send_sem=sems[2].at[3 * i + j],
                                            recv_sem=sems[3].at[3 * i + j], device_id=(x, y, 1 - c), device_id_type=MESH_ID)

    def local(ins, outs, sems, i):
        return pltpu.make_async_copy(ins[i], outs[i].at[_chips()[3]], sems[4].at[i])

    def start(ins, outs, sems):
        for i in range(n):
            local(ins, outs, sems, i).start()
            for j in range(3):
                ici(ins, outs, sems, i, j, True).start()

    def middle(ins, outs, sems):
        for i in range(n):
            for j in range(3):
                ici(ins, outs, sems, i, j, False).wait_recv()
                if i < ns:
                    d2d(outs, sems, i, j, True).start()

    def finish(ins, outs, sems):
        for i in range(ns):
            for j in range(3):
                d2d(outs, sems, i, j, False).wait_recv()
        for i in range(n):
            for j in range(3):
                ici(ins, outs, sems, i, j, True).wait_send()
                if i < ns:
                    d2d(outs, sems, i, j, True).wait_send()
            local(ins, outs, sems, i).wait()

    return dict(ins=arrs, out_shape=[jax.ShapeDtypeStruct((N_CHIPS,) + a.shape, a.dtype) for a in arrs],
                scratch=[pltpu.SemaphoreType.DMA((3 * n,)), pltpu.SemaphoreType.DMA((3 * n,)),
                         pltpu.SemaphoreType.DMA((3 * max(ns, 1),)), pltpu.SemaphoreType.DMA((3 * max(ns, 1),)),
                         pltpu.SemaphoreType.DMA((n,))],
                start=start, middle=middle, finish=finish)


def run_alone(name, ride):
    n_in, n_out = len(ride["ins"]), len(ride["out_shape"])

    def body(*refs):
        parts = refs[:n_in], refs[n_in:n_in + n_out], refs[n_in + n_out:]
        ride["start"](*parts)
        ride["middle"](*parts)
        ride["finish"](*parts)

    return pl.pallas_call(body, name=name, in_specs=[ANY_SPEC] * n_in, out_specs=[ANY_SPEC] * n_out,
                          out_shape=ride["out_shape"], scratch_shapes=ride["scratch"])(*ride["ins"])


def sibling_exchange(name, arrs):
    n = len(arrs)

    def body(*refs):
        ins, outs = refs[:n], refs[n:2 * n]
        send_sems, recv_sems = refs[2 * n:]
        x, y, c = _place()
        cps = [pltpu.make_async_remote_copy(src_ref=ins[i], dst_ref=outs[i], send_sem=send_sems.at[i], recv_sem=recv_sems.at[i],
                                            device_id=(x, y, 1 - c), device_id_type=MESH_ID) for i in range(n)]
        for cp in cps:
            cp.start()
        for cp in cps:
            cp.wait_recv()
        for cp in cps:
            cp.wait_send()

    return pl.pallas_call(
        body, name=name, in_specs=[ANY_SPEC] * n, out_specs=[ANY_SPEC] * n,
        out_shape=[jax.ShapeDtypeStruct(a.shape, a.dtype) for a in arrs],
        scratch_shapes=[pltpu.SemaphoreType.DMA((n,)), pltpu.SemaphoreType.DMA((n,))],
    )(*arrs)


def gather_all(name, vec):
    def body(in_ref, out_ref, send_sems, recv_sems, local_sem):
        x, y, c = _place()
        me = 4 * x + 2 * y + c

        def peer(mask):
            return (1 - x if mask & 4 else x, 1 - y if mask & 2 else y, 1 - c if mask & 1 else c)

        def remote(mask, landing_from_me):
            px, py, pc = peer(mask)
            slot = me if landing_from_me else 4 * px + 2 * py + pc
            return pltpu.make_async_remote_copy(src_ref=in_ref, dst_ref=out_ref.at[slot], send_sem=send_sems.at[mask - 1],
                                                recv_sem=recv_sems.at[mask - 1], device_id=(px, py, pc), device_id_type=MESH_ID)

        local = pltpu.make_async_copy(in_ref, out_ref.at[me], local_sem)
        local.start()
        sent = [remote(mask, True) for mask in range(1, N_DEV)]
        for cp in sent:
            cp.start()
        for mask in range(1, N_DEV):
            remote(mask, False).wait_recv()
        for cp in sent:
            cp.wait_send()
        local.wait()

    return pl.pallas_call(
        body, name=name, in_specs=[ANY_SPEC], out_specs=ANY_SPEC,
        out_shape=jax.ShapeDtypeStruct((N_DEV,) + vec.shape, vec.dtype),
        scratch_shapes=[pltpu.SemaphoreType.DMA((N_DEV - 1,)), pltpu.SemaphoreType.DMA((N_DEV - 1,)), pltpu.SemaphoreType.DMA],
    )(vec)


def sum_slots(name, arrs, tr):
    k, r, c = arrs[0].shape
    nb = r // tr

    def body(*refs):
        o_ref = refs[-1]
        for li, a_ref in enumerate(refs[:-1]):
            @pl.when(pl.program_id(0) == li)
            def _(a_ref=a_ref):
                acc = a_ref[0].astype(F32)
                for j in range(1, k):
                    acc = acc + a_ref[j].astype(F32)
                o_ref[...] = acc

    def in_spec(li):
        return pl.BlockSpec((k, tr, c), lambda l, i: (0, jnp.where(l == li, i, jnp.where(l < li, 0, nb - 1)), 0))

    return pl.pallas_call(
        body, name=name, grid=(len(arrs), nb),
        in_specs=[in_spec(li) for li in range(len(arrs))], out_specs=pl.BlockSpec((tr, c), lambda l, i: (l * nb + i, 0)),
        out_shape=jax.ShapeDtypeStruct((len(arrs) * r, c), F32), compiler_params=_cparams(2),
    )(*arrs)


def adamw(name, parts, w, m, v, tr):
    r, c = w.shape
    npart = len(parts)
    bc1 = 1.0 - ADAM_B1 ** ADAM_STEP
    bc2 = 1.0 - ADAM_B2 ** ADAM_STEP

    def body(*refs):
        g = refs[0][...]
        for j in range(1, npart):
            g = g + refs[j][...]
        w_ref, m_ref, v_ref, g_out, d_out, m_out, v_out = refs[npart:]
        m_new = ADAM_B1 * m_ref[...] + (1.0 - ADAM_B1) * g
        v_new = ADAM_B2 * v_ref[...] + (1.0 - ADAM_B2) * jnp.square(g)
        m_hat = m_new / bc1
        v_hat = v_new / bc2
        g_out[...] = g
        d_out[...] = -ADAM_LR * (m_hat / (jnp.sqrt(v_hat) + ADAM_EPS) + ADAM_WD * w_ref[...])
        m_out[...] = m_new
        v_out[...] = v_new

    spec = pl.BlockSpec((tr, c), lambda i: (i, 0))
    return pl.pallas_call(
        body, name=name, grid=(r // tr,), in_specs=[spec] * (npart + 3), out_specs=[spec] * 4,
        out_shape=[jax.ShapeDtypeStruct((r, c), F32)] * 4, compiler_params=_cparams(),
    )(*parts, w, m, v)


ROW_TM = 256
POINT_TM = 512


def _row(v):
    return v.reshape(1, -1)


def _lane_pad(vals, at):
    return jnp.concatenate([jnp.zeros((at,), F32), vals, jnp.zeros((LANES - at - vals.shape[0],), F32)]).reshape(1, LANES)


def _layer_params(sm, l):
    return dict(
        conf=[sm["conv_dw_w"][l], _row(sm["conv_dw_b"][l]), _row(sm["conv_ln_g"][l]), _row(sm["conv_ln_b"][l])],
        pre=[sm["gdn_conv_q"][l], sm["gdn_conv_k"][l], sm["gdn_conv_v"][l],
             _lane_pad(sm["gdn_a_log"][l], GDN_HEADS), _lane_pad(sm["gdn_dt_bias"][l], GDN_HEADS)],
        scan=_row(sm["gdn_norm_g"][l]),
        sgu=[_row(sm["sgu_ln_g"][l]), _row(sm["sgu_ln_b"][l]), sm["sgu_w_s"][l].reshape(SGU_GROUPS * SGU_CHUNK, SGU_CHUNK),
             sm["sgu_b_s"][l].T],
        mix=[_row(sm["b_gate"][l])],
        ln1=[_row(sm["ln1_g"][l]), _row(sm["ln1_b"][l])],
        ff=[_row(sm["b_ff1"][l])],
        ln2=[_row(sm["b_ff2"][l]), _row(sm["ln2_g"][l]), _row(sm["ln2_b"][l])],
    )


def _split_w_in(w):
    zpad = jnp.zeros((D_MODEL, LANES - 2 * GDN_HEADS), w.dtype)
    return dict(A=w[:, 0:1024], B=jnp.concatenate([w[:, 1024:2560], w[:, 3072:3080], zpad], axis=1), Z=w[:, 2560:3072],
                C=w[:, 3080:4104], G=w[:, 4104:7176])


def _join_w_in(d):
    return jnp.concatenate([d["A"], d["B"][:, 0:1536], d["Z"], d["B"][:, 1536:1544], d["C"], d["G"]], axis=1)


def _hosted(get, host):
    got = get(host) if get is not None else None
    return got if got is not None else (None, lambda results: None)


def layer_forward(l, xin, xin_b, big, lp, fetch=None):
    tag = f"l{l}_"
    wi = big["w_in"][l]
    sv = dict(xin=xin, xin_b=xin_b)
    for r in "GACBZ":
        sv["p" + r] = matmul(tag + "proj_" + r, xin_b, wi[r], "nn", ACT)
    ride, done = _hosted(fetch, "conformer")
    sv["ya_in"], *rode = halo_fwd(tag + "conformer", f_conformer, ROW_TM, CONV_HALO, sv["pA"], lp["conf"], [[BF16]], ride)
    done(rode)
    ride, done = _hosted(fetch, "gdn_pre")
    sv["qn"], sv["kn"], sv["vv"], sv["gb"], *rode = halo_fwd(tag + "gdn_pre", f_gdn_pre, ROW_TM, GDN_HALO, sv["pB"], lp["pre"],
                                                             [[F32]] * 4, ride)
    done(rode)
    ride, done = _hosted(fetch, "gdn_prep")
    prep_out = prep_fwd(tag + "gdn_prep", sv["qn"], sv["kn"], sv["vv"], sv["gb"], ride)
    prep, sv["tinv"] = prep_out[:6], prep_out[6]
    done(prep_out[7:])
    sv["prep"] = prep
    ride, done = _hosted(fetch, "gdn_scan")
    sv["yb_in"], sv["states"], *rode = scan_fwd(tag + "gdn_scan", lp["scan"], *prep, sv["pZ"], ride)
    done(rode)
    (sv["yc_in"],) = stage_fwd(tag + "sgu", f_sgu, ROW_TM, [sv["pC"]], lp["sgu"], [[BF16]])
    sv["ya"] = matmul(tag + "out_a", sv["ya_in"], big["w_pa"][l], "nn", ACT)
    sv["yb"] = matmul(tag + "out_b", sv["yb_in"], big["w_pb"][l], "nn", ACT)
    sv["yc"] = matmul(tag + "out_c", sv["yc_in"], big["w_pc"][l], "nn", ACT)
    (sv["mix"],) = stage_fwd(tag + "mix", f_mix, POINT_TM, [sv["ya"], sv["yb"], sv["yc"], sv["pG"]], lp["mix"], [[BF16]])
    sv["m"] = matmul(tag + "w_o", sv["mix"], big["w_o"][l], "nn", F32)
    sv["x1"], sv["x1_b"] = stage_fwd(tag + "ln1", f_ln1, POINT_TM, [xin, sv["m"]], lp["ln1"], [[F32, BF16]])
    sv["h"], sv["hpre"] = matmul(tag + "ff1", sv["x1_b"], big["w_ff1"][l], "nn", BF16, fuse=("relu2", lp["ff"][0]))
    sv["f2"] = matmul(tag + "ff2", sv["h"], big["w_ff2"][l], "nn", F32)
    x2, x2_b = stage_fwd(tag + "ln2", f_ln2, POINT_TM, [sv["x1"], sv["f2"]], lp["ln2"], [[F32, BF16]])
    return x2, x2_b, sv


def layer_backward(l, dx2, sv, big, lp, send=None):
    tag = f"l{l}_b_"
    g = {}
    get = (lambda host: send(host, g)) if send is not None else None
    dx1_a, df2, db2, dg2, dbe2 = stage_bwd(tag + "ln2", f_ln2, POINT_TM, [sv["x1"], sv["f2"]], lp["ln2"], [dx2], [F32, BF16])
    g["b_ff2"], g["ln2_g"], g["ln2_b"] = db2[0], dg2[0], dbe2[0]
    g["w_ff2"] = matmul(tag + "dw_ff2", sv["h"], df2, "tn", F32)
    dhpre, db1 = matmul(tag + "dx_ff2", df2, big["w_ff2"][l], "nt", BF16, fuse=("relu2_bwd", sv["hpre"]))
    g["b_ff1"] = db1[0]
    g["w_ff1"] = matmul(tag + "dw_ff1", sv["x1_b"], dhpre, "tn", F32)
    dx1_b = matmul(tag + "dx_ff1", dhpre, big["w_ff1"][l], "nt", F32)
    dxin_a, dm, dg1, dbe1 = stage_bwd(tag + "ln1", f_ln1, POINT_TM, [sv["xin"], sv["m"]], lp["ln1"], [[dx1_a, dx1_b]], [F32, BF16])
    g["ln1_g"], g["ln1_b"] = dg1[0], dbe1[0]
    g["w_o"] = matmul(tag + "dw_o", sv["mix"], dm, "tn", F32)
    dmix = matmul(tag + "dx_o", dm, big["w_o"][l], "nt", ACT)
    dya, dyb, dyc, dp_g, dbg = stage_bwd(tag + "mix", f_mix, POINT_TM, [sv["ya"], sv["yb"], sv["yc"], sv["pG"]], lp["mix"],
                                         [[dmix]], [BF16] * 4)
    g["b_gate"] = dbg[0]
    g["w_pa"] = matmul(tag + "dw_pa", sv["ya_in"], dya, "tn", F32)
    g["w_pb"] = matmul(tag + "dw_pb", sv["yb_in"], dyb, "tn", F32)
    g["w_pc"] = matmul(tag + "dw_pc", sv["yc_in"], dyc, "tn", F32)
    dya_in = matmul(tag + "dx_pa", dya, big["w_pa"][l], "nt", ACT)
    dyb_in = matmul(tag + "dx_pb", dyb, big["w_pb"][l], "nt", ACT)
    dyc_in = matmul(tag + "dx_pc", dyc, big["w_pc"][l], "nt", ACT)
    ride, done = _hosted(get, "conformer")
    dp_a, dcw, dcb, dcg, dcbe, *rode = halo_bwd(tag + "conformer", f_conformer, ROW_TM, CONV_HALO, sv["pA"], lp["conf"], [dya_in],
                                                BF16, ride)
    done(rode)
    g["conv_dw_w"], g["conv_dw_b"], g["conv_ln_g"], g["conv_ln_b"] = dcw, dcb[0], dcg[0], dcbe[0]
    dp_c, dsg, dsb, dsw, dsbs = stage_bwd(tag + "sgu", f_sgu, ROW_TM, [sv["pC"]], lp["sgu"], [[dyc_in]], [BF16])
    g["sgu_ln_g"], g["sgu_ln_b"] = dsg[0], dsb[0]
    g["sgu_w_s"] = dsw.reshape(SGU_GROUPS, SGU_CHUNK, SGU_CHUNK)
    g["sgu_b_s"] = dsbs.T
    ride, done = _hosted(get, "gdn_scan")
    scan_out = scan_bwd(tag + "gdn_scan", lp["scan"], sv["states"], *sv["prep"], sv["pZ"], dyb_in, ride)
    dprep, dp_z, dng = scan_out[:6], scan_out[6], scan_out[7]
    done(scan_out[8:])
    g["gdn_norm_g"] = dng[0]
    dqn, dkn, dvv, dgb = prep_bwd(tag + "gdn_prep", sv["qn"], sv["kn"], sv["vv"], sv["gb"], sv["tinv"], dprep)
    ride, done = _hosted(get, "gdn_pre")
    dp_b, dq, dk, dv, dal, ddt, *rode = halo_bwd(tag + "gdn_pre", f_gdn_pre, ROW_TM, GDN_HALO, sv["pB"], lp["pre"],
                                                 [dqn, dkn, dvv, dgb], BF16, ride)
    done(rode)
    g["gdn_conv_q"], g["gdn_conv_k"], g["gdn_conv_v"] = dq, dk, dv
    g["gdn_a_log"] = dal[0, GDN_HEADS:2 * GDN_HEADS]
    g["gdn_dt_bias"] = ddt[0, GDN_HEADS:2 * GDN_HEADS]
    dps = dict(G=dp_g, A=dp_a, C=dp_c, B=dp_b, Z=dp_z)
    wi = big["w_in"][l]
    g["w_in"] = _join_w_in({r: matmul(tag + "dw_in_" + r, sv["xin_b"], dps[r], "tn", F32) for r in "GACBZ"})
    dxin_b = matmul_sum_nt(tag + "dx_in", [(dps[r], wi[r]) for r in "GACBZ"], F32)
    return [dxin_a, dxin_b], g


SMALL_PER_LAYER = ("b_gate", "conv_dw_w", "conv_dw_b", "conv_ln_g", "conv_ln_b", "gdn_conv_q", "gdn_conv_k", "gdn_conv_v",
                   "gdn_a_log", "gdn_dt_bias", "gdn_norm_g", "sgu_ln_g", "sgu_ln_b", "sgu_w_s", "sgu_b_s", "ln1_g", "ln1_b",
                   "b_ff1", "b_ff2", "ln2_g", "ln2_b")
BIG = ("w_in", "w_pa", "w_pb", "w_pc", "w_o", "w_ff1", "w_ff2")


def local_step(x, target, big, sm, fetch=None, send=None):
    p_in = [_row(sm["ln_in_g"]), _row(sm["ln_in_b"])]
    ride, done = _hosted(fetch, "ln_in")
    xc, xc_b, *rode = stage_fwd("ln_in", f_ln_in, POINT_TM, [x], p_in, [[F32, BF16]], ride)
    done(rode)
    lps = [_layer_params(sm, l) for l in range(DEPTH)]
    saved = []
    for l in range(DEPTH):
        xc, xc_b, sv = layer_forward(l, xc, xc_b, big, lps[l], fetch if l == 0 else None)
        saved.append(sv)
    loss_part, dy = loss_stage("loss", xc, target, POINT_TM)
    dx = [dy]
    per_layer = [None] * DEPTH

    def send_from(l):
        if send is None or l != 0:
            return None
        return lambda host, g: send(host, per_layer[:l] + [g] + per_layer[l + 1:])

    for l in reversed(range(DEPTH)):
        dx, per_layer[l] = layer_backward(l, dx, saved[l], big, lps[l], send_from(l))
    ride, done = _hosted((lambda host: send(host, per_layer)) if send is not None else None, "ln_in")
    grad_x, dgi, dbi, *rode = stage_bwd("ln_in_b", f_ln_in, POINT_TM, [x], p_in, [dx], [F32], ride)
    done(rode)
    small = {n: jnp.stack([per_layer[l][n] for l in range(DEPTH)]) for n in SMALL_PER_LAYER}
    small["ln_in_g"], small["ln_in_b"] = dgi[0], dbi[0]
    return loss_part, grad_x, small, per_layer


PARAM_NAMES = ("ln_in_g", "ln_in_b", "w_in", "b_gate", "conv_dw_w", "conv_dw_b", "conv_ln_g", "conv_ln_b", "w_pa",
               "gdn_conv_q", "gdn_conv_k", "gdn_conv_v", "gdn_a_log", "gdn_dt_bias", "gdn_norm_g", "w_pb", "sgu_ln_g",
               "sgu_ln_b", "sgu_w_s", "sgu_b_s", "w_pc", "w_o", "ln1_g", "ln1_b", "w_ff1", "b_ff1", "w_ff2", "b_ff2",
               "ln2_g", "ln2_b")
SMALL = tuple(n for n in PARAM_NAMES if n not in BIG)
COL_SHARDED_SMALL = ("conv_dw_w", "gdn_conv_q", "gdn_conv_k", "gdn_conv_v")
ROW_SHARDED_BIG = ("w_o", "w_ff2")
ADAM_ROWS = 128
_OUT_PROJ = ("w_pa", "w_pb", "w_pc", "w_o")
FETCH_BEHIND = {"conformer": [("w_ff1", 0), ("w_ff2", 0)], "gdn_pre": [(n, l) for l in (0, 1) for n in _OUT_PROJ],
                "gdn_prep": [("w_in", 1), ("w_ff1", 1)], "gdn_scan": [("w_ff2", 1)]}
SEND_BEHIND = {"conformer": [(n, 1) for n in BIG], "gdn_scan": [("w_ff1", 0), ("w_ff2", 0)],
               "gdn_pre": [(n, 0) for n in _OUT_PROJ], "ln_in": [("w_in", 0)]}


def _unshard(g, rows):
    n, r, c = g.shape
    if rows:
        return g.reshape(n * r, c)
    return jnp.moveaxis(g, 0, 1).reshape(r, n * c)


def _to_shards(w, rows):
    r, c = w.shape
    if rows:
        return w.reshape(N_CHIPS, r // N_CHIPS, c)
    return jnp.moveaxis(w.reshape(r, N_CHIPS, c // N_CHIPS), 1, 0)


def _pack(vals, rows_multiple=8):
    flat = jnp.concatenate([v.reshape(-1) for v in vals])
    per = LANES * rows_multiple
    padded = -(-flat.shape[0] // per) * per
    return jnp.pad(flat, (0, padded - flat.shape[0])).reshape(-1, LANES)


def _unpack(packed, shapes):
    flat = packed.reshape(-1)
    out, at = [], 0
    for s in shapes:
        size = 1
        for d in s:
            size *= d
        out.append(flat[at:at + size].reshape(s))
        at += size
    return out


def kernel(x, ln_in_g, ln_in_b, w_in, b_gate, conv_dw_w, conv_dw_b, conv_ln_g, conv_ln_b, w_pa, gdn_conv_q, gdn_conv_k, gdn_conv_v, gdn_a_log, gdn_dt_bias, gdn_norm_g, w_pb, sgu_ln_g, sgu_ln_b, sgu_w_s, sgu_b_s, w_pc, w_o, ln1_g, ln1_b, w_ff1, b_ff1, w_ff2, b_ff2, ln2_g, ln2_b, loss_target, m_ln_in_g, m_ln_in_b, m_w_in, m_b_gate, m_conv_dw_w, m_conv_dw_b, m_conv_ln_g, m_conv_ln_b, m_w_pa, m_gdn_conv_q, m_gdn_conv_k, m_gdn_conv_v, m_gdn_a_log, m_gdn_dt_bias, m_gdn_norm_g, m_w_pb, m_sgu_ln_g, m_sgu_ln_b, m_sgu_w_s, m_sgu_b_s, m_w_pc, m_w_o, m_ln1_g, m_ln1_b, m_w_ff1, m_b_ff1, m_w_ff2, m_b_ff2, m_ln2_g, m_ln2_b, v_ln_in_g, v_ln_in_b, v_w_in, v_b_gate, v_conv_dw_w, v_conv_dw_b, v_conv_ln_g, v_conv_ln_b, v_w_pa, v_gdn_conv_q, v_gdn_conv_k, v_gdn_conv_v, v_gdn_a_log, v_gdn_dt_bias, v_gdn_norm_g, v_w_pb, v_sgu_ln_g, v_sgu_ln_b, v_sgu_w_s, v_sgu_b_s, v_w_pc, v_w_o, v_ln1_g, v_ln1_b, v_w_ff1, v_b_ff1, v_w_ff2, v_b_ff2, v_ln2_g, v_ln2_b):
    w = dict(zip(PARAM_NAMES, (ln_in_g, ln_in_b, w_in, b_gate, conv_dw_w, conv_dw_b, conv_ln_g, conv_ln_b, w_pa, gdn_conv_q, gdn_conv_k, gdn_conv_v, gdn_a_log, gdn_dt_bias, gdn_norm_g, w_pb, sgu_ln_g, sgu_ln_b, sgu_w_s, sgu_b_s, w_pc, w_o, ln1_g, ln1_b, w_ff1, b_ff1, w_ff2, b_ff2, ln2_g, ln2_b)))
    m = dict(zip(PARAM_NAMES, (m_ln_in_g, m_ln_in_b, m_w_in, m_b_gate, m_conv_dw_w, m_conv_dw_b, m_conv_ln_g, m_conv_ln_b, m_w_pa, m_gdn_conv_q, m_gdn_conv_k, m_gdn_conv_v, m_gdn_a_log, m_gdn_dt_bias, m_gdn_norm_g, m_w_pb, m_sgu_ln_g, m_sgu_ln_b, m_sgu_w_s, m_sgu_b_s, m_w_pc, m_w_o, m_ln1_g, m_ln1_b, m_w_ff1, m_b_ff1, m_w_ff2, m_b_ff2, m_ln2_g, m_ln2_b)))
    v = dict(zip(PARAM_NAMES, (v_ln_in_g, v_ln_in_b, v_w_in, v_b_gate, v_conv_dw_w, v_conv_dw_b, v_conv_ln_g, v_conv_ln_b, v_w_pa, v_gdn_conv_q, v_gdn_conv_k, v_gdn_conv_v, v_gdn_a_log, v_gdn_dt_bias, v_gdn_norm_g, v_w_pb, v_sgu_ln_g, v_sgu_ln_b, v_sgu_w_s, v_sgu_b_s, v_w_pc, v_w_o, v_ln1_g, v_ln1_b, v_w_ff1, v_b_ff1, v_w_ff2, v_b_ff2, v_ln2_g, v_ln2_b)))
    chip = 2 * lax.axis_index("x") + lax.axis_index("y")

    shards = {n: w[n].astype(BF16) for n in BIG}
    col_small_shapes = [w[n].shape for n in COL_SHARDED_SMALL]
    big = {n: [None] * DEPTH for n in BIG}

    def place(keys, gathered):
        for (n, l), g in zip(keys, gathered):
            whole = _unshard(g, n in ROW_SHARDED_BIG)
            big[n][l] = _split_w_in(whole) if n == "w_in" else whole

    sm = {n: w[n] for n in SMALL if n not in COL_SHARDED_SMALL}

    def place_first(gathered):
        place([("w_in", 0)], gathered[:1])
        per_chip = [_unpack(gathered[-1][s], col_small_shapes) for s in range(N_CHIPS)]
        for i, n in enumerate(COL_SHARDED_SMALL):
            sm[n] = jnp.concatenate([per_chip[s][i] for s in range(N_CHIPS)], axis=-1)

    def fetch(host):
        if host == "ln_in":
            return gather_rider([shards["w_in"][0]], [_pack([w[n] for n in COL_SHARDED_SMALL])]), place_first
        keys = FETCH_BEHIND[host]
        return gather_rider([shards[n][l] for n, l in keys], []), lambda gathered: place(keys, gathered)

    received = {}

    def send(host, per_layer):
        keys = SEND_BEHIND[host]
        ride = scatter_rider([_to_shards(per_layer[l][n], n in ROW_SHARDED_BIG).astype(BF16) for n, l in keys])
        return ride, lambda results: received.update(zip(keys, results))

    loss_part, grad_x, grads, _ = local_step(x[0], loss_target[0], big, sm, fetch, send)
    partial = [sum_slots("sum_chips_" + n, [received[(n, l)] for l in range(DEPTH)], ADAM_ROWS) for n in BIG]
    sibling = sibling_exchange("swap_cores", partial)
    out = {}
    for n, mine, theirs in zip(BIG, partial, sibling):
        shape = w[n].shape
        res = adamw("adamw_" + n, [mine, theirs], w[n].reshape(-1, shape[-1]), m[n].reshape(-1, shape[-1]),
                    v[n].reshape(-1, shape[-1]), ADAM_ROWS)
        out[n] = [r.reshape(shape) for r in res]

    small_shapes = [grads[n].shape for n in SMALL]
    vec = _pack([grads[n] for n in SMALL] + [jnp.sum(loss_part).reshape(1)])
    total = sum_slots("sum_small", [gather_all("gather_small", vec)], vec.shape[0])
    whole = _unpack(total, small_shapes + [(1,)])
    loss = whole[-1][0]
    g_small = {}
    for n, g in zip(SMALL, whole[:-1]):
        if n in COL_SHARDED_SMALL:
            width = g.shape[-1] // N_CHIPS
            g = lax.dynamic_slice_in_dim(g, chip * width, width, axis=g.ndim - 1)
        g_small[n] = g
    local_shapes = [w[n].shape for n in SMALL]
    packed_g = _pack([g_small[n] for n in SMALL])
    res = adamw("adamw_small", [packed_g], _pack([w[n] for n in SMALL]), _pack([m[n] for n in SMALL]),
                _pack([v[n] for n in SMALL]), packed_g.shape[0])
    unpacked = [_unpack(r, local_shapes) for r in res]
    for i, n in enumerate(SMALL):
        out[n] = [unpacked[k][i] for k in range(4)]

    return (loss, grad_x[None], *[out[n][0] for n in PARAM_NAMES], *[out[n][1] for n in PARAM_NAMES],
            *[out[n][2] for n in PARAM_NAMES], *[out[n][3] for n in PARAM_NAMES])
```

```python
import functools

import jax
import jax.numpy as jnp
from jax import lax
from jax.experimental import pallas as pl
from jax.experimental.pallas import tpu as pltpu

F32 = jnp.float32
BF16 = jnp.bfloat16
ACT = BF16

D_MODEL = 1024
DEPTH = 2
CONV_DIM = 512
CONV_WIDTH = 31
CONV_HALO = 32
CONV_WINDOW = 128
GDN_HEADS = 4
GDN_HD = 128
GDN_CHUNK = 64
GDN_PAIR = 2 * GDN_CHUNK
GDN_CONV = 4
GDN_HALO = 16
SGU_GROUPS = 4
SGU_GD = 128
SGU_CHUNK = 128
D_FF = 4096
ALPHA = (2 * DEPTH) ** 0.25
LN_EPS = 1e-5
RMS_EPS = 1e-6
ADAM_LR, ADAM_B1, ADAM_B2, ADAM_EPS, ADAM_WD, ADAM_STEP = 0.001, 0.9, 0.999, 1e-08, 0.01, 10
N_CHIPS = 4
N_DEV = 8
LANES = 128
VMEM_LIMIT = 52 * 1024 * 1024

W_GATE, W_A, W_C, W_B, W_Z = 3 * D_MODEL, 2 * CONV_DIM, 2 * 512, 3 * 512 + LANES, 512


def _dg(a, b, ca, cb):
    return lax.dot_general(a.astype(BF16), b.astype(BF16), (((ca,), (cb,)), ((), ())), preferred_element_type=F32)


@jax.custom_vjp
def mm_nn(a, b):
    return _dg(a, b, 1, 0)


mm_nn.defvjp(lambda a, b: (_dg(a, b, 1, 0), (a, b)), lambda r, g: (_dg(g, r[1], 1, 1), _dg(r[0], g, 0, 0)))


@jax.custom_vjp
def mm_nt(a, b):
    return _dg(a, b, 1, 1)


mm_nt.defvjp(lambda a, b: (_dg(a, b, 1, 1), (a, b)), lambda r, g: (_dg(g, r[1], 1, 0), _dg(g, r[0], 0, 0)))


@jax.custom_vjp
def mm_tn(a, b):
    return _dg(a, b, 0, 0)


mm_tn.defvjp(lambda a, b: (_dg(a, b, 0, 0), (a, b)), lambda r, g: (_dg(r[1], g, 1, 1), _dg(r[0], g, 1, 0)))


def _ln(x, g, b):
    mu = jnp.mean(x, -1, keepdims=True)
    xc = x - mu
    var = jnp.mean(xc * xc, -1, keepdims=True)
    return xc * lax.rsqrt(var + LN_EPS) * g + b


def _iota2(shape, dim):
    return lax.broadcasted_iota(jnp.int32, shape, dim)


def f_ln_in(params, x):
    g, b = params
    return (_ln(x, g, b),)


def _roll_rows(x, s):
    return x if s == 0 else pltpu.roll(x, s, 0)


@functools.partial(jax.custom_vjp, nondiff_argnums=(1,))
def shift_rows(x, s):
    return _roll_rows(x, s)


shift_rows.defvjp(lambda x, s: (_roll_rows(x, s), None),
                  lambda s, _, ct: (_roll_rows(ct, (ct.shape[0] - s) % ct.shape[0]),))


def _causal_conv(full, w, halo):
    width = w.shape[0]
    n_out = full.shape[0] - halo
    win = min(CONV_WINDOW, n_out)
    groups = []
    for g in range(full.shape[1] // LANES):
        ls = slice(g * LANES, (g + 1) * LANES)
        wg = w[:, ls]
        parts = []
        for r0 in range(0, n_out, win):
            xw = full[r0:r0 + halo + win, ls]
            acc = wg[width - 1:width, :] * xw[halo:, :]
            for j in range(width - 1):
                acc = acc + wg[j:j + 1, :] * shift_rows(xw, width - 1 - j)[halo:, :]
            parts.append(acc)
        groups.append(jnp.concatenate(parts, axis=0))
    return jnp.concatenate(groups, axis=1)


def f_conformer(params, halo, pa):
    w, b, g, be = params
    full = jnp.concatenate([halo, pa], axis=0)
    h = full[:, :CONV_DIM] * jax.nn.sigmoid(full[:, CONV_DIM:])
    return (jax.nn.silu(_ln(_causal_conv(h, w, CONV_HALO) + b, g, be)),)


def f_gdn_pre(params, halo, pb):
    cq, ck, cv, alog, dtb = params
    full = jnp.concatenate([halo, pb], axis=0)

    def conv(cols, w):
        return jax.nn.silu(_causal_conv(cols, w, GDN_HALO))

    def l2n(t, scale):
        parts = []
        for h in range(GDN_HEADS):
            th = t[:, h * GDN_HD:(h + 1) * GDN_HD]
            parts.append(th * (lax.rsqrt(jnp.sum(th * th, -1, keepdims=True) + RMS_EPS) * scale))
        return jnp.concatenate(parts, axis=1)

    q = l2n(conv(full[:, 0:512], cq), GDN_HD ** -0.5)
    k = l2n(conv(full[:, 512:1024], ck), 1.0)
    v = conv(full[:, 1024:1536], cv)
    logits = pb[:, 1536:1536 + LANES]
    z = logits + dtb
    softplus = jnp.maximum(z, 0.0) + jnp.log(1.0 + jnp.exp(-jnp.abs(z)))
    g = -jnp.exp(alog) * softplus
    beta = jax.nn.sigmoid(logits)
    lane = _iota2(logits.shape, 1)
    gb = jnp.where(lane < GDN_HEADS, beta, jnp.where(lane < 2 * GDN_HEADS, g, 0.0))
    return q, k, v, gb


def _split2(x):
    hi = x.astype(BF16)
    return hi, (x - hi.astype(F32)).astype(BF16)


def _dg3(a, b, ca, cb):
    ah, al = _split2(a)
    bh, bl = _split2(b)
    d = lambda p, q: lax.dot_general(p, q, (((ca,), (cb,)), ((), ())), preferred_element_type=F32)
    return d(ah, bh) + (d(ah, bl) + d(al, bh))


def _inv_unit_lower(lows):
    n = lows[0].shape[0]
    eye = jnp.where(_iota2((n, n), 0) == _iota2((n, n), 1), 1.0, 0.0).astype(F32)
    ps = [-low for low in lows]
    xs = [eye + p for p in ps]
    for _ in range(5):
        ps = [_dg3(p, p, 1, 0) for p in ps]
        xs = [x + _dg3(x, p, 1, 0) for x, p in zip(xs, ps)]
    return xs


def _pair_masks():
    n = GDN_PAIR
    r, c = _iota2((n, n), 0), _iota2((n, n), 1)
    same = (r >= GDN_CHUNK) == (c >= GDN_CHUNK)
    return same, jnp.logical_and(same, r >= c), jnp.logical_and(same, r > c), jnp.logical_and(same, r <= c)


def _masked_sum(mask, g):
    mk = jnp.where(mask, 1.0, 0.0).astype(BF16)
    g1 = g.astype(BF16)
    r1 = g - g1.astype(F32)
    g2 = r1.astype(BF16)
    g3 = (r1 - g2.astype(F32)).astype(BF16)
    d = lambda q: jnp.dot(mk, q, preferred_element_type=F32)
    return d(g1) + (d(g2) + d(g3))


@jax.custom_vjp
def chunk_cumsum(g):
    return _masked_sum(_pair_masks()[1], g)


chunk_cumsum.defvjp(lambda g: (_masked_sum(_pair_masks()[1], g), None), lambda _, ct: (_masked_sum(_pair_masks()[3], ct),))


def _prep_gates(gb):
    gam_all = chunk_cumsum(gb)
    first = _iota2((GDN_PAIR, LANES), 0) < GDN_CHUNK
    glast = jnp.where(first, gam_all[GDN_CHUNK - 1:GDN_CHUNK, :], gam_all[GDN_PAIR - 1:GDN_PAIR, :])
    return gam_all, gam_all.T, glast


def _head_gates(gb, gam_all, gam_t, h, causal):
    gc = gam_all[:, GDN_HEADS + h:GDN_HEADS + h + 1]
    gr = gam_t[GDN_HEADS + h:GDN_HEADS + h + 1, :]
    return gc, gb[:, h:h + 1], jnp.exp(jnp.where(causal, gc - gr, -jnp.inf))


def f_prep_low(kn, gb):
    _, causal, strict, _ = _pair_masks()
    gam_all, gam_t, _ = _prep_gates(gb)
    lows = []
    for h in range(GDN_HEADS):
        k = kn[:, h * GDN_HD:(h + 1) * GDN_HD]
        _, beta, decay = _head_gates(gb, gam_all, gam_t, h, causal)
        lows.append(jnp.where(strict, beta * mm_nt(k, k) * decay, 0.0))
    return jnp.concatenate(lows, axis=1)


def f_prep_rest(qn, kn, vv, gb, tinv_all):
    _, causal, _, _ = _pair_masks()
    gam_all, gam_t, glast = _prep_gates(gb)
    us, ws, as_, qds, kds = [], [], [], [], []
    for h in range(GDN_HEADS):
        hs = slice(h * GDN_HD, (h + 1) * GDN_HD)
        q, k, v, tinv = qn[:, hs], kn[:, hs], vv[:, hs], tinv_all[:, hs]
        gc, beta, decay = _head_gates(gb, gam_all, gam_t, h, causal)
        eg = jnp.exp(gc)
        us.append(mm_nn(tinv, beta * v))
        ws.append(mm_nn(tinv, beta * k * eg))
        as_.append(mm_nt(q, k) * decay)
        qds.append(q * eg)
        kds.append(k * jnp.exp(glast[:, GDN_HEADS + h:GDN_HEADS + h + 1] - gc))
    cat = lambda xs: jnp.concatenate(xs, axis=1)
    return cat(us), cat(ws), cat(as_), cat(qds), cat(kds), jnp.exp(glast)


PREP_TM = 2 * GDN_PAIR


def _ride(ride, n_in, n_out, refs, n_steps):
    if ride is None:
        return
    nri, nro = len(ride["ins"]), len(ride["out_shape"])
    r_in = refs[n_in:n_in + nri]
    r_out = refs[n_in + nri + n_out:n_in + nri + n_out + nro]
    sems = refs[len(refs) - len(ride["scratch"]):]
    step = pl.program_id(0)

    @pl.when(step == 0)
    def _():
        ride["start"](r_in, r_out, sems)

    @pl.when(step == (2 * n_steps) // 3)
    def _():
        ride["middle"](r_in, r_out, sems)

    @pl.when(step == n_steps - 1)
    def _():
        ride["finish"](r_in, r_out, sems)


def _ride_args(ride):
    if ride is None:
        return [], [], [], [], []
    n = len(ride["ins"])
    return list(ride["ins"]), [ANY_SPEC] * n, [ANY_SPEC] * len(ride["out_shape"]), list(ride["out_shape"]), list(ride["scratch"])


def prep_fwd(name, qn, kn, vv, gb, ride=None):
    t = qn.shape[0]
    hw = GDN_HEADS * GDN_HD
    n_steps = t // PREP_TM
    r_ins, r_in_specs, r_out_specs, r_out_shape, r_scratch = _ride_args(ride)

    def body(*refs):
        q_ref, k_ref, v_ref, gb_ref = refs[:4]
        outs = refs[4 + len(r_ins):4 + len(r_ins) + 7]
        pairs = [pl.ds(pi * GDN_PAIR, GDN_PAIR) for pi in range(PREP_TM // GDN_PAIR)]
        lows = [f_prep_low(k_ref[rs, :], gb_ref[rs, :]) for rs in pairs]
        invs = _inv_unit_lower([low[:, h * GDN_HD:(h + 1) * GDN_HD] for low in lows for h in range(GDN_HEADS)])
        for pi, rs in enumerate(pairs):
            tinv = jnp.concatenate(invs[pi * GDN_HEADS:(pi + 1) * GDN_HEADS], axis=1)
            res = f_prep_rest(q_ref[rs, :], k_ref[rs, :], v_ref[rs, :], gb_ref[rs, :], tinv)
            for o_ref, val in zip(outs, res + (tinv,)):
                o_ref[rs, :] = val
        _ride(ride, 4, 7, refs, n_steps)

    widths = [hw, hw, hw, hw, hw, LANES, hw]
    return pl.pallas_call(
        body, name=name, grid=(n_steps,),
        in_specs=[_row_spec(PREP_TM, hw)] * 3 + [_row_spec(PREP_TM, LANES)] + r_in_specs,
        out_specs=[_row_spec(PREP_TM, wd) for wd in widths] + r_out_specs,
        out_shape=[jax.ShapeDtypeStruct((t, wd), F32) for wd in widths] + r_out_shape,
        scratch_shapes=r_scratch, compiler_params=_cparams(),
    )(qn, kn, vv, gb, *r_ins)


def prep_bwd(name, qn, kn, vv, gb, tinv, cts):
    t = qn.shape[0]
    hw = GDN_HEADS * GDN_HD

    def body(q_ref, k_ref, v_ref, gb_ref, x_ref, *refs):
        outs = refs[6:]
        for pi in range(PREP_TM // GDN_PAIR):
            rs = pl.ds(pi * GDN_PAIR, GDN_PAIR)
            ct = tuple(r[rs, :] for r in refs[:6])
            kn_v, gb_v, x_all = k_ref[rs, :], gb_ref[rs, :], x_ref[rs, :]
            _, vjp_low = jax.vjp(f_prep_low, kn_v, gb_v)
            _, vjp_rest = jax.vjp(f_prep_rest, q_ref[rs, :], kn_v, v_ref[rs, :], gb_v, x_all)
            dq, dk, dv, dgb, dx_all = vjp_rest(ct)
            dlows = []
            for h in range(GDN_HEADS):
                hs = slice(h * GDN_HD, (h + 1) * GDN_HD)
                dlows.append(-_dg3(_dg3(x_all[:, hs], dx_all[:, hs], 0, 0), x_all[:, hs], 1, 1))
            dk2, dgb2 = vjp_low(jnp.concatenate(dlows, axis=1))
            outs[0][rs, :] = dq
            outs[1][rs, :] = dk + dk2
            outs[2][rs, :] = dv
            outs[3][rs, :] = dgb + dgb2

    in_w = [hw, hw, hw, LANES, hw] + [hw, hw, hw, hw, hw, LANES]
    out_w = [hw, hw, hw, LANES]
    return pl.pallas_call(
        body, name=name, grid=(t // PREP_TM,),
        in_specs=[_row_spec(PREP_TM, wd) for wd in in_w], out_specs=[_row_spec(PREP_TM, wd) for wd in out_w],
        out_shape=[jax.ShapeDtypeStruct((t, wd), F32) for wd in out_w], compiler_params=_cparams(),
    )(qn, kn, vv, gb, tinv, *cts)


def f_gdn_scan(params, state, u, w, a, qd, kd, egl, z):
    ng = params[0]
    tm = u.shape[0]
    st = [state[h * GDN_HD:(h + 1) * GDN_HD, :] for h in range(GDN_HEADS)]
    zeros = jnp.zeros((GDN_CHUNK, GDN_HD), F32)
    rows = []
    hss = [slice(h * GDN_HD, (h + 1) * GDN_HD) for h in range(GDN_HEADS)]
    for ci in range(tm // GDN_CHUNK):
        rs = slice(ci * GDN_CHUNK, (ci + 1) * GDN_CHUNK)
        vns = [u[rs, hs] - mm_nn(w[rs, hs], s) for hs, s in zip(hss, st)]
        qs = [mm_nn(qd[rs, hs], s) for hs, s in zip(hss, st)]
        upd = [mm_tn(kd[rs, hs], vn) for hs, vn in zip(hss, vns)]
        es = [egl[ci * GDN_CHUNK:ci * GDN_CHUNK + 1, GDN_HEADS + h:GDN_HEADS + h + 1] for h in range(GDN_HEADS)]
        st = [s * e + d for s, e, d in zip(st, es, upd)]
        vpads = [jnp.concatenate([vn, zeros] if ci % 2 == 0 else [zeros, vn], axis=0) for vn in vns]
        os_ = [q + mm_nn(a[rs, hs], vp) for q, hs, vp in zip(qs, hss, vpads)]
        heads = [o * lax.rsqrt(jnp.mean(o * o, -1, keepdims=True) + RMS_EPS) * ng * jax.nn.silu(z[rs, hs])
                 for o, hs in zip(os_, hss)]
        rows.append(jnp.concatenate(heads, axis=1))
    return jnp.concatenate(rows, axis=0), jnp.concatenate(st, axis=0)


def f_sgu(params, puv):
    g, b, ws, bst = params
    tm = puv.shape[0]
    uv = jax.nn.gelu(puv)
    u = uv[:, :512]
    v = _ln(uv[:, 512:], g, b)
    r, c = _iota2((SGU_CHUNK, SGU_CHUNK), 0), _iota2((SGU_CHUNK, SGU_CHUNK), 1)
    rows = []
    for ci in range(tm // SGU_CHUNK):
        rs = slice(ci * SGU_CHUNK, (ci + 1) * SGU_CHUNK)
        groups = []
        for gi in range(SGU_GROUPS):
            gs = slice(gi * SGU_GD, (gi + 1) * SGU_GD)
            wt = jnp.where(r >= c, ws[gs, :], 0.0)
            groups.append(mm_nn(wt, v[rs, gs]) + bst[:, gi:gi + 1])
        rows.append(jnp.concatenate(groups, axis=1))
    return (u * jnp.concatenate(rows, axis=0),)


def f_mix(params, ya, yb, yc, pg):
    bg = params[0]
    s = jax.nn.sigmoid(pg + bg)
    return (s[:, :D_MODEL] * ya + s[:, D_MODEL:2 * D_MODEL] * yb + s[:, 2 * D_MODEL:] * yc,)


def f_ln1(params, xin, m):
    g, b = params
    return (_ln(ALPHA * xin + m, g, b),)


def f_ln2(params, x1, f2):
    b2, g, b = params
    return (_ln(ALPHA * x1 + (f2 + b2), g, b),)


def _cparams(n_axes=1):
    return pltpu.CompilerParams(dimension_semantics=("arbitrary",) * n_axes, vmem_limit_bytes=VMEM_LIMIT)


def _row_spec(tm, width, n_tiles=None):
    if n_tiles is None:
        return pl.BlockSpec((tm, width), lambda i: (i, 0))
    return pl.BlockSpec((tm, width), lambda i: (n_tiles - 1 - i, 0))


def _full_spec(shape):
    return pl.BlockSpec(shape, lambda i: (0,) * len(shape))


def stage_fwd(name, fn, tm, rows, params, outs, ride=None):
    t = rows[0].shape[0]
    nr, npar = len(rows), len(params)
    widths = jax.eval_shape(lambda p, r: fn(p, *r), [jax.ShapeDtypeStruct(p.shape, F32) for p in params],
                            [jax.ShapeDtypeStruct((tm, r.shape[1]), F32) for r in rows])
    r_ins, r_in_specs, r_out_specs, r_out_shape, r_scratch = _ride_args(ride)
    n_out = sum(len(dts) for dts in outs)

    def body(*refs):
        r = [x[...].astype(F32) for x in refs[:nr]]
        p = [x[...] for x in refs[nr:nr + npar]]
        res = fn(p, *r)
        k = nr + npar + len(r_ins)
        for o, dts in zip(res, outs):
            for dt in dts:
                refs[k][...] = o.astype(dt)
                k += 1
        _ride(ride, nr + npar, n_out, refs, t // tm)

    out_shape, out_specs = [], []
    for wd, dts in zip(widths, outs):
        for dt in dts:
            out_shape.append(jax.ShapeDtypeStruct((t, wd.shape[1]), dt))
            out_specs.append(_row_spec(tm, wd.shape[1]))
    return pl.pallas_call(
        body, name=name, grid=(t // tm,),
        in_specs=[_row_spec(tm, r.shape[1]) for r in rows] + [_full_spec(p.shape) for p in params] + r_in_specs,
        out_specs=out_specs + r_out_specs, out_shape=out_shape + r_out_shape, scratch_shapes=r_scratch,
        compiler_params=_cparams(),
    )(*rows, *params, *r_ins)


def stage_bwd(name, fn, tm, rows, params, cts, drow_dtypes, ride=None):
    t = rows[0].shape[0]
    nr, npar = len(rows), len(params)
    flat_cts = [c for cl in cts if cl is not None for c in cl]
    nct = len(flat_cts)
    want = [i for i, dt in enumerate(drow_dtypes) if dt is not None]
    r_ins, r_in_specs, r_out_specs, r_out_shape, r_scratch = _ride_args(ride)

    def body(*refs):
        r = [x[...].astype(F32) for x in refs[:nr]]
        p = [x[...] for x in refs[nr:nr + npar]]
        res, vjp = jax.vjp(lambda pp, rr: fn(pp, *rr), p, r)
        k = nr + npar
        ct = []
        for o, cl in zip(res, cts):
            if cl is None:
                ct.append(jnp.zeros_like(o))
            else:
                acc = refs[k][...].astype(F32)
                for j in range(1, len(cl)):
                    acc = acc + refs[k + j][...].astype(F32)
                k += len(cl)
                ct.append(acc)
        dp, dr = vjp(tuple(ct))
        k += len(r_ins)
        for i in want:
            refs[k][...] = dr[i].astype(drow_dtypes[i])
            k += 1

        @pl.when(pl.program_id(0) == 0)
        def _():
            for j in range(npar):
                refs[k + j][...] = jnp.zeros_like(refs[k + j])

        for j in range(npar):
            refs[k + j][...] += dp[j]
        _ride(ride, nr + npar + nct, len(want) + npar, refs, t // tm)

    out_shape = [jax.ShapeDtypeStruct(rows[i].shape, drow_dtypes[i]) for i in want]
    out_specs = [_row_spec(tm, rows[i].shape[1]) for i in want]
    out_shape += [jax.ShapeDtypeStruct(p.shape, F32) for p in params]
    out_specs += [_full_spec(p.shape) for p in params]
    return pl.pallas_call(
        body, name=name, grid=(t // tm,),
        in_specs=[_row_spec(tm, r.shape[1]) for r in rows] + [_full_spec(p.shape) for p in params]
        + [_row_spec(tm, c.shape[1]) for c in flat_cts] + r_in_specs,
        out_specs=out_specs + r_out_specs, out_shape=out_shape + r_out_shape, scratch_shapes=r_scratch,
        compiler_params=_cparams(),
    )(*rows, *params, *flat_cts, *r_ins)


def _halo_spec(tm, halo, width, n_tiles=None):
    per = tm // halo
    if n_tiles is None:
        return pl.BlockSpec((halo, width), lambda i: (jnp.maximum(i * per - 1, 0), 0))
    return pl.BlockSpec((halo, width), lambda i: (jnp.maximum((n_tiles - 1 - i) * per - 1, 0), 0))


def halo_fwd(name, fn, tm, halo, src, params, outs, ride=None):
    t, width = src.shape
    npar = len(params)
    widths = jax.eval_shape(lambda p, h, r: fn(p, h, r), [jax.ShapeDtypeStruct(p.shape, F32) for p in params],
                            jax.ShapeDtypeStruct((halo, width), F32), jax.ShapeDtypeStruct((tm, width), F32))
    r_ins, r_in_specs, r_out_specs, r_out_shape, r_scratch = _ride_args(ride)
    n_out = sum(len(dts) for dts in outs)

    def body(h_ref, x_ref, *refs):
        hal = jnp.where(pl.program_id(0) == 0, 0.0, h_ref[...].astype(F32))
        res = fn([x[...] for x in refs[:npar]], hal, x_ref[...].astype(F32))
        k = npar + len(r_ins)
        for o, dts in zip(res, outs):
            for dt in dts:
                refs[k][...] = o.astype(dt)
                k += 1
        _ride(ride, 2 + npar, n_out, (h_ref, x_ref) + refs, t // tm)

    out_shape, out_specs = [], []
    for wd, dts in zip(widths, outs):
        for dt in dts:
            out_shape.append(jax.ShapeDtypeStruct((t, wd.shape[1]), dt))
            out_specs.append(_row_spec(tm, wd.shape[1]))
    return pl.pallas_call(
        body, name=name, grid=(t // tm,),
        in_specs=[_halo_spec(tm, halo, width), _row_spec(tm, width)] + [_full_spec(p.shape) for p in params] + r_in_specs,
        out_specs=out_specs + r_out_specs, out_shape=out_shape + r_out_shape, scratch_shapes=r_scratch,
        compiler_params=_cparams(),
    )(src, src, *params, *r_ins)


def halo_bwd(name, fn, tm, halo, src, params, cts, dsrc_dtype, ride=None):
    t, width = src.shape
    n_tiles = t // tm
    npar = len(params)
    n_in = 2 + npar + len(cts)
    r_ins, r_in_specs, r_out_specs, r_out_shape, r_scratch = _ride_args(ride)

    def body(h_ref, x_ref, *refs):
        carry = refs[n_in - 2 + len(r_ins) + 1 + npar + len(r_out_shape)]
        first_tile = pl.program_id(0) == n_tiles - 1
        hal = jnp.where(first_tile, 0.0, h_ref[...].astype(F32))
        p = [x[...] for x in refs[:npar]]
        res, vjp = jax.vjp(fn, p, hal, x_ref[...].astype(F32))
        k = npar
        ct = tuple(refs[k + j][...].astype(F32) for j in range(len(cts)))
        k += len(cts) + len(r_ins)
        dp, dh, dx = vjp(ct)

        @pl.when(pl.program_id(0) == 0)
        def _():
            carry[...] = jnp.zeros_like(carry)
            for j in range(npar):
                refs[k + 1 + j][...] = jnp.zeros_like(refs[k + 1 + j])

        dx = jnp.concatenate([dx[:tm - halo, :], dx[tm - halo:, :] + carry[...]], axis=0)
        refs[k][...] = dx.astype(dsrc_dtype)
        carry[...] = dh
        for j in range(npar):
            refs[k + 1 + j][...] += dp[j]
        _ride(ride, n_in, 1 + npar, (h_ref, x_ref) + refs, n_tiles)

    out_shape = [jax.ShapeDtypeStruct((t, width), dsrc_dtype)] + [jax.ShapeDtypeStruct(p.shape, F32) for p in params]
    out_specs = [_row_spec(tm, width, n_tiles)] + [_full_spec(p.shape) for p in params]
    return pl.pallas_call(
        body, name=name, grid=(n_tiles,),
        in_specs=[_halo_spec(tm, halo, width, n_tiles), _row_spec(tm, width, n_tiles)] + [_full_spec(p.shape) for p in params]
        + [_row_spec(tm, c.shape[1], n_tiles) for c in cts] + r_in_specs,
        out_specs=out_specs + r_out_specs, out_shape=out_shape + r_out_shape,
        scratch_shapes=[pltpu.VMEM((halo, width), F32)] + r_scratch, compiler_params=_cparams(),
    )(src, src, *params, *cts, *r_ins)


SCAN_TM = 256


def scan_fwd(name, norm_g, u, w, a, qd, kd, egl, z, ride=None):
    t = u.shape[0]
    n_tiles = t // SCAN_TM
    srows = GDN_HEADS * GDN_HD

    r_ins, r_in_specs, r_out_specs, r_out_shape, r_scratch = _ride_args(ride)

    def body(*refs):
        ng_ref, u_ref, w_ref, a_ref, qd_ref, kd_ref, e_ref, z_ref = refs[:8]
        o_ref, s_ref = refs[8 + len(r_ins):10 + len(r_ins)]
        state = refs[10 + len(r_ins) + len(r_out_shape)]

        @pl.when(pl.program_id(0) == 0)
        def _():
            state[...] = jnp.zeros_like(state)

        s_ref[0] = state[...]
        o, s_new = f_gdn_scan([ng_ref[...]], state[...], u_ref[...], w_ref[...], a_ref[...], qd_ref[...], kd_ref[...],
                              e_ref[...], z_ref[...].astype(F32))
        o_ref[...] = o.astype(BF16)
        state[...] = s_new
        _ride(ride, 8, 2, refs, n_tiles)

    rows = [u, w, a, qd, kd, egl, z]
    return pl.pallas_call(
        body, name=name, grid=(n_tiles,),
        in_specs=[_full_spec(norm_g.shape)] + [_row_spec(SCAN_TM, r.shape[1]) for r in rows] + r_in_specs,
        out_specs=[_row_spec(SCAN_TM, 512), pl.BlockSpec((1, srows, GDN_HD), lambda i: (i, 0, 0))] + r_out_specs,
        out_shape=[jax.ShapeDtypeStruct((t, 512), BF16), jax.ShapeDtypeStruct((n_tiles, srows, GDN_HD), F32)] + r_out_shape,
        scratch_shapes=[pltpu.VMEM((srows, GDN_HD), F32)] + r_scratch, compiler_params=_cparams(),
    )(norm_g, *rows, *r_ins)


def scan_bwd(name, norm_g, states, u, w, a, qd, kd, egl, z, dout, ride=None):
    t = u.shape[0]
    n_tiles = t // SCAN_TM
    srows = GDN_HEADS * GDN_HD
    r_ins, r_in_specs, r_out_specs, r_out_shape, r_scratch = _ride_args(ride)

    def body(*refs):
        ng_ref, s_ref, u_ref, w_ref, a_ref, qd_ref, kd_ref, e_ref, z_ref, do_ref = refs[:10]
        du_ref, dw_ref, da_ref, dqd_ref, dkd_ref, de_ref, dz_ref, dng_ref = refs[10 + len(r_ins):18 + len(r_ins)]
        dstate = refs[18 + len(r_ins) + len(r_out_shape)]

        @pl.when(pl.program_id(0) == 0)
        def _():
            dstate[...] = jnp.zeros_like(dstate)
            dng_ref[...] = jnp.zeros_like(dng_ref)

        args = ([ng_ref[...]], s_ref[0], u_ref[...], w_ref[...], a_ref[...], qd_ref[...], kd_ref[...], e_ref[...],
                z_ref[...].astype(F32))
        _, vjp = jax.vjp(f_gdn_scan, *args)
        dp, ds, du, dw, da, dqd, dkd, de, dz = vjp((do_ref[...].astype(F32), dstate[...]))
        du_ref[...] = du
        dw_ref[...] = dw
        da_ref[...] = da
        dqd_ref[...] = dqd
        dkd_ref[...] = dkd
        de_ref[...] = de
        dz_ref[...] = dz.astype(BF16)
        dng_ref[...] += dp[0]
        dstate[...] = ds
        _ride(ride, 10, 8, refs, n_tiles)

    rows = [u, w, a, qd, kd, egl, z, dout]
    return pl.pallas_call(
        body, name=name, grid=(n_tiles,),
        in_specs=[_full_spec(norm_g.shape), pl.BlockSpec((1, srows, GDN_HD), lambda i: (n_tiles - 1 - i, 0, 0))]
        + [_row_spec(SCAN_TM, r.shape[1], n_tiles) for r in rows] + r_in_specs,
        out_specs=[_row_spec(SCAN_TM, r.shape[1], n_tiles) for r in rows[:7]] + [_full_spec(norm_g.shape)] + r_out_specs,
        out_shape=[jax.ShapeDtypeStruct(r.shape, F32) for r in rows[:6]] + [jax.ShapeDtypeStruct(z.shape, BF16)]
        + [jax.ShapeDtypeStruct(norm_g.shape, F32)] + r_out_shape,
        scratch_shapes=[pltpu.VMEM((srows, GDN_HD), F32)] + r_scratch, compiler_params=_cparams(),
    )(norm_g, states, *rows, *r_ins)


def loss_stage(name, y, target, tm):
    t, d = y.shape

    def body(y_ref, t_ref, l_ref, dy_ref):
        @pl.when(pl.program_id(0) == 0)
        def _():
            l_ref[...] = jnp.zeros_like(l_ref)

        e = y_ref[...] - t_ref[...]
        dy_ref[...] = e * (1.0 / d)
        sq = e * e
        part = sq[:, 0:LANES]
        for j in range(1, d // LANES):
            part = part + sq[:, j * LANES:(j + 1) * LANES]
        acc = part[0:8, :]
        for j in range(1, tm // 8):
            acc = acc + part[j * 8:(j + 1) * 8, :]
        l_ref[...] += acc * (0.5 / d)

    return pl.pallas_call(
        body, name=name, grid=(t // tm,),
        in_specs=[_row_spec(tm, d), _row_spec(tm, d)],
        out_specs=[_full_spec((8, LANES)), _row_spec(tm, d)],
        out_shape=[jax.ShapeDtypeStruct((8, LANES), F32), jax.ShapeDtypeStruct((t, d), F32)],
        compiler_params=_cparams(),
    )(y, target)


def _pick(n, cands):
    for c in cands:
        if n % c == 0:
            return c
    return n


def matmul(name, a, b, form, out_dtype, acc=None, fuse=None):
    if form == "nn":
        (m, k), n = a.shape, b.shape[1]
    elif form == "nt":
        (m, k), n = a.shape, b.shape[0]
    else:
        (k, m), n = a.shape, b.shape[1]
    if form == "tn":
        tm = _pick(m, (1024, 512, 256, 128))
        tn = n if n <= 2048 else _pick(n, (1536, 1024, 512, 256, 128))
        tk = _pick(k, (2048, 1024, 512, 256, 128))
    else:
        tm = _pick(m, (512, 256, 128))
        tk = k if k <= 4096 else _pick(k, (2048, 1024, 512))
        cap = 2048 * 2048 if fuse is None else 2048 * 1024
        tn = n if n * tk <= cap else _pick(n, (2048, 1536, 1024, 512, 256, 128))
    nk = k // tk
    ca, cb = {"nn": (1, 0), "nt": (1, 1), "tn": (0, 0)}[form]
    a_spec = (pl.BlockSpec((tm, tk), lambda j, i, kk: (i, kk)) if form != "tn"
              else pl.BlockSpec((tk, tm), lambda j, i, kk: (kk, i)))
    b_spec = (pl.BlockSpec((tn, tk), lambda j, i, kk: (j, kk)) if form == "nt"
              else pl.BlockSpec((tk, tn), lambda j, i, kk: (kk, j)))
    o_spec = pl.BlockSpec((tm, tn), lambda j, i, kk: (i, j))
    col_spec = pl.BlockSpec((1, tn), lambda j, i, kk: (0, j))
    has_acc = acc is not None

    assert nk == 1 or (out_dtype == F32 and fuse is None)

    def body(*refs):
        a_ref, b_ref = refs[0], refs[1]
        prod = lax.dot_general(a_ref[...], b_ref[...], (((ca,), (cb,)), ((), ())), preferred_element_type=F32)
        if fuse is not None and fuse[0] == "relu2":
            pre = prod + refs[2][...]
            refs[3][...] = jnp.square(jnp.maximum(pre, 0.0)).astype(BF16)
            refs[4][...] = pre.astype(BF16)
            return
        if fuse is not None and fuse[0] == "relu2_bwd":
            d = prod * (2.0 * jnp.maximum(refs[2][...].astype(F32), 0.0))
            refs[3][...] = d.astype(BF16)

            @pl.when(pl.program_id(1) == 0)
            def _():
                refs[4][...] = jnp.zeros_like(refs[4])

            refs[4][...] += jnp.sum(d, axis=0, keepdims=True)
            return
        o_ref = refs[-1]
        if nk == 1:
            o_ref[...] = (prod + refs[2][...] if has_acc else prod).astype(out_dtype)
            return
        kk = pl.program_id(2)

        @pl.when(kk == 0)
        def _():
            o_ref[...] = prod + refs[2][...] if has_acc else prod

        @pl.when(kk > 0)
        def _():
            o_ref[...] += prod

    ins = [a.astype(BF16), b.astype(BF16)] + ([acc] if has_acc else [])
    in_specs = [a_spec, b_spec] + ([o_spec] if has_acc else [])
    out_specs, out_shape = o_spec, jax.ShapeDtypeStruct((m, n), out_dtype)
    if fuse is not None and fuse[0] == "relu2":
        ins, in_specs = ins + [fuse[1]], in_specs + [col_spec]
        out_specs, out_shape = [o_spec, o_spec], [jax.ShapeDtypeStruct((m, n), BF16)] * 2
    elif fuse is not None:
        ins, in_specs = ins + [fuse[1]], in_specs + [o_spec]
        out_specs = [o_spec, col_spec]
        out_shape = [jax.ShapeDtypeStruct((m, n), BF16), jax.ShapeDtypeStruct((1, n), F32)]
    return pl.pallas_call(
        body, name=name, grid=(n // tn, m // tm, nk),
        in_specs=in_specs, out_specs=out_specs, out_shape=out_shape,
        input_output_aliases={2: 0} if has_acc else {},
        compiler_params=_cparams(3),
    )(*ins)


SUM_NT_TM = 256


def matmul_sum_nt(name, pairs, out_dtype, ride=None):
    m, n = pairs[0][0].shape[0], pairs[0][1].shape[0]
    npair = len(pairs)
    r_ins, r_in_specs, r_out_specs, r_out_shape, r_scratch = _ride_args(ride)

    def body(*refs):
        acc = None
        for r in range(npair):
            prod = lax.dot_general(refs[2 * r][...], refs[2 * r + 1][...], (((1,), (1,)), ((), ())), preferred_element_type=F32)
            acc = prod if acc is None else acc + prod
        refs[2 * npair + len(r_ins)][...] = acc.astype(out_dtype)
        _ride(ride, 2 * npair, 1, refs, m // SUM_NT_TM)

    ins, in_specs = [], []
    for a, b in pairs:
        ins += [a.astype(BF16), b.astype(BF16)]
        in_specs += [pl.BlockSpec((SUM_NT_TM, a.shape[1]), lambda i: (i, 0)), _full_spec(b.shape)]
    return pl.pallas_call(
        body, name=name, grid=(m // SUM_NT_TM,), in_specs=in_specs + r_in_specs,
        out_specs=[pl.BlockSpec((SUM_NT_TM, n), lambda i: (i, 0))] + r_out_specs,
        out_shape=[jax.ShapeDtypeStruct((m, n), out_dtype)] + r_out_shape, scratch_shapes=r_scratch,
        compiler_params=_cparams(),
    )(*ins, *r_ins)


MESH_ID = pl.DeviceIdType.MESH
ANY_SPEC = pl.BlockSpec(memory_space=pl.ANY)


def _place():
    return lax.axis_index("x"), lax.axis_index("y"), lax.axis_index("c")


def _chips():
    x, y, c = _place()
    return x, y, c, 2 * x + y, [(1 - x, y), (x, 1 - y), (1 - x, 1 - y)]


def scatter_rider(arrs):
    n = len(arrs)

    def remote(ins, outs, sems, i, j, landing_from_me):
        x, y, c, me, others = _chips()
        ox, oy = others[j]
        peer = 2 * ox + oy
        return pltpu.make_async_remote_copy(src_ref=ins[i].at[peer], dst_ref=outs[i].at[me if landing_from_me else peer],
                                            send_sem=sems[0].at[3 * i + j], recv_sem=sems[1].at[3 * i + j],
                                            device_id=(ox, oy, c), device_id_type=MESH_ID)

    def local(ins, outs, sems, i):
        me = _chips()[3]
        return pltpu.make_async_copy(ins[i].at[me], outs[i].at[me], sems[2].at[i])

    def start(ins, outs, sems):
        for i in range(n):
            local(ins, outs, sems, i).start()
            for j in range(3):
                remote(ins, outs, sems, i, j, True).start()

    def finish(ins, outs, sems):
        for i in range(n):
            for j in range(3):
                remote(ins, outs, sems, i, j, False).wait_recv()
        for i in range(n):
            for j in range(3):
                remote(ins, outs, sems, i, j, True).wait_send()
            local(ins, outs, sems, i).wait()

    return dict(ins=list(arrs), out_shape=[jax.ShapeDtypeStruct(a.shape, a.dtype) for a in arrs],
                scratch=[pltpu.SemaphoreType.DMA((3 * n,)), pltpu.SemaphoreType.DMA((3 * n,)), pltpu.SemaphoreType.DMA((n,))],
                start=start, middle=lambda ins, outs, sems: None, finish=finish)


def gather_rider(split, flat):
    ns, n = len(split), len(split) + len(flat)
    arrs = list(split) + list(flat)
    half = [a.shape[0] // 2 for a in split]

    def ici(ins, outs, sems, i, j, landing_from_me):
        x, y, c, me, others = _chips()
        ox, oy = others[j]
        slot = me if landing_from_me else 2 * ox + oy
        if i < ns:
            rows = pl.ds(c * half[i], half[i])
            src, dst = ins[i].at[rows], outs[i].at[slot, rows]
        else:
            src, dst = ins[i], outs[i].at[slot]
        return pltpu.make_async_remote_copy(src_ref=src, dst_ref=dst, send_sem=sems[0].at[3 * i + j],
                                            recv_sem=sems[1].at[3 * i + j], device_id=(ox, oy, c), device_id_type=MESH_ID)

    def d2d(outs, sems, i, j, mine):
        x, y, c, me, others = _chips()
        ox, oy = others[j]
        ref = outs[i].at[2 * ox + oy, pl.ds((c if mine else 1 - c) * half[i], half[i])]
        return pltpu.make_async_remote_copy(src_ref=ref, dst_ref=ref, send_sem=sems[2].at[3 * i + j],
                                            recv_sem=sems[3].at[3 * i + j], device_id=(x, y, 1 - c), device_id_type=MESH_ID)

    def local(ins, outs, sems, i):
        return pltpu.make_async_copy(ins[i], outs[i].at[_chips()[3]], sems[4].at[i])

    def start(ins, outs, sems):
        for i in range(n):
            local(ins, outs, sems, i).start()
            for j in range(3):
                ici(ins, outs, sems, i, j, True).start()

    def middle(ins, outs, sems):
        for i in range(n):
            for j in range(3):
                ici(ins, outs, sems, i, j, False).wait_recv()
                if i < ns:
                    d2d(outs, sems, i, j, True).start()

    def finish(ins, outs, sems):
        for i in range(ns):
            for j in range(3):
                d2d(outs, sems, i, j, False).wait_recv()
        for i in range(n):
            for j in range(3):
                ici(ins, outs, sems, i, j, True).wait_send()
                if i < ns:
                    d2d(outs, sems, i, j, True).wait_send()
            local(ins, outs, sems, i).wait()

    return dict(ins=arrs, out_shape=[jax.ShapeDtypeStruct((N_CHIPS,) + a.shape, a.dtype) for a in arrs],
                scratch=[pltpu.SemaphoreType.DMA((3 * n,)), pltpu.SemaphoreType.DMA((3 * n,)),
                         pltpu.SemaphoreType.DMA((3 * max(ns, 1),)), pltpu.SemaphoreType.DMA((3 * max(ns, 1),)),
                         pltpu.SemaphoreType.DMA((n,))],
                start=start, middle=middle, finish=finish)


def run_alone(name, ride):
    n_in, n_out = len(ride["ins"]), len(ride["out_shape"])

    def body(*refs):
        parts = refs[:n_in], refs[n_in:n_in + n_out], refs[n_in + n_out:]
        ride["start"](*parts)
        ride["middle"](*parts)
        ride["finish"](*parts)

    return pl.pallas_call(body, name=name, in_specs=[ANY_SPEC] * n_in, out_specs=[ANY_SPEC] * n_out,
                          out_shape=ride["out_shape"], scratch_shapes=ride["scratch"])(*ride["ins"])


def sibling_exchange(name, arrs):
    n = len(arrs)

    def body(*refs):
        ins, outs = refs[:n], refs[n:2 * n]
        send_sems, recv_sems = refs[2 * n:]
        x, y, c = _place()
        cps = [pltpu.make_async_remote_copy(src_ref=ins[i], dst_ref=outs[i], send_sem=send_sems.at[i], recv_sem=recv_sems.at[i],
                                            device_id=(x, y, 1 - c), device_id_type=MESH_ID) for i in range(n)]
        for cp in cps:
            cp.start()
        for cp in cps:
            cp.wait_recv()
        for cp in cps:
            cp.wait_send()

    return pl.pallas_call(
        body, name=name, in_specs=[ANY_SPEC] * n, out_specs=[ANY_SPEC] * n,
        out_shape=[jax.ShapeDtypeStruct(a.shape, a.dtype) for a in arrs],
        scratch_shapes=[pltpu.SemaphoreType.DMA((n,)), pltpu.SemaphoreType.DMA((n,))],
    )(*arrs)


def gather_all(name, vec):
    def body(in_ref, out_ref, send_sems, recv_sems, local_sem):
        x, y, c = _place()
        me = 4 * x + 2 * y + c

        def peer(mask):
            return (1 - x if mask & 4 else x, 1 - y if mask & 2 else y, 1 - c if mask & 1 else c)

        def remote(mask, landing_from_me):
            px, py, pc = peer(mask)
            slot = me if landing_from_me else 4 * px + 2 * py + pc
            return pltpu.make_async_remote_copy(src_ref=in_ref, dst_ref=out_ref.at[slot], send_sem=send_sems.at[mask - 1],
                                                recv_sem=recv_sems.at[mask - 1], device_id=(px, py, pc), device_id_type=MESH_ID)

        local = pltpu.make_async_copy(in_ref, out_ref.at[me], local_sem)
        local.start()
        sent = [remote(mask, True) for mask in range(1, N_DEV)]
        for cp in sent:
            cp.start()
        for mask in range(1, N_DEV):
            remote(mask, False).wait_recv()
        for cp in sent:
            cp.wait_send()
        local.wait()

    return pl.pallas_call(
        body, name=name, in_specs=[ANY_SPEC], out_specs=ANY_SPEC,
        out_shape=jax.ShapeDtypeStruct((N_DEV,) + vec.shape, vec.dtype),
        scratch_shapes=[pltpu.SemaphoreType.DMA((N_DEV - 1,)), pltpu.SemaphoreType.DMA((N_DEV - 1,)), pltpu.SemaphoreType.DMA],
    )(vec)


def sum_slots(name, arrs, tr):
    k, r, c = arrs[0].shape
    nb = r // tr

    def body(*refs):
        o_ref = refs[-1]
        for li, a_ref in enumerate(refs[:-1]):
            @pl.when(pl.program_id(0) == li)
            def _(a_ref=a_ref):
                acc = a_ref[0].astype(F32)
                for j in range(1, k):
                    acc = acc + a_ref[j].astype(F32)
                o_ref[...] = acc

    def in_spec(li):
        return pl.BlockSpec((k, tr, c), lambda l, i: (0, jnp.where(l == li, i, jnp.where(l < li, 0, nb - 1)), 0))

    return pl.pallas_call(
        body, name=name, grid=(len(arrs), nb),
        in_specs=[in_spec(li) for li in range(len(arrs))], out_specs=pl.BlockSpec((tr, c), lambda l, i: (l * nb + i, 0)),
        out_shape=jax.ShapeDtypeStruct((len(arrs) * r, c), F32), compiler_params=_cparams(2),
    )(*arrs)


def adamw(name, parts, w, m, v, tr):
    r, c = w.shape
    npart = len(parts)
    bc1 = 1.0 - ADAM_B1 ** ADAM_STEP
    bc2 = 1.0 - ADAM_B2 ** ADAM_STEP

    def body(*refs):
        g = refs[0][...]
        for j in range(1, npart):
            g = g + refs[j][...]
        w_ref, m_ref, v_ref, g_out, d_out, m_out, v_out = refs[npart:]
        m_new = ADAM_B1 * m_ref[...] + (1.0 - ADAM_B1) * g
        v_new = ADAM_B2 * v_ref[...] + (1.0 - ADAM_B2) * jnp.square(g)
        m_hat = m_new / bc1
        v_hat = v_new / bc2
        g_out[...] = g
        d_out[...] = -ADAM_LR * (m_hat / (jnp.sqrt(v_hat) + ADAM_EPS) + ADAM_WD * w_ref[...])
        m_out[...] = m_new
        v_out[...] = v_new

    spec = pl.BlockSpec((tr, c), lambda i: (i, 0))
    return pl.pallas_call(
        body, name=name, grid=(r // tr,), in_specs=[spec] * (npart + 3), out_specs=[spec] * 4,
        out_shape=[jax.ShapeDtypeStruct((r, c), F32)] * 4, compiler_params=_cparams(),
    )(*parts, w, m, v)


ROW_TM = 256
POINT_TM = 512


def _row(v):
    return v.reshape(1, -1)


def _lane_pad(vals, at):
    return jnp.concatenate([jnp.zeros((at,), F32), vals, jnp.zeros((LANES - at - vals.shape[0],), F32)]).reshape(1, LANES)


def _layer_params(sm, l):
    return dict(
        conf=[sm["conv_dw_w"][l], _row(sm["conv_dw_b"][l]), _row(sm["conv_ln_g"][l]), _row(sm["conv_ln_b"][l])],
        pre=[sm["gdn_conv_q"][l], sm["gdn_conv_k"][l], sm["gdn_conv_v"][l],
             _lane_pad(sm["gdn_a_log"][l], GDN_HEADS), _lane_pad(sm["gdn_dt_bias"][l], GDN_HEADS)],
        scan=_row(sm["gdn_norm_g"][l]),
        sgu=[_row(sm["sgu_ln_g"][l]), _row(sm["sgu_ln_b"][l]), sm["sgu_w_s"][l].reshape(SGU_GROUPS * SGU_CHUNK, SGU_CHUNK),
             sm["sgu_b_s"][l].T],
        mix=[_row(sm["b_gate"][l])],
        ln1=[_row(sm["ln1_g"][l]), _row(sm["ln1_b"][l])],
        ff=[_row(sm["b_ff1"][l])],
        ln2=[_row(sm["b_ff2"][l]), _row(sm["ln2_g"][l]), _row(sm["ln2_b"][l])],
    )


def _split_w_in(w):
    zpad = jnp.zeros((D_MODEL, LANES - 2 * GDN_HEADS), w.dtype)
    return dict(A=w[:, 0:1024], B=jnp.concatenate([w[:, 1024:2560], w[:, 3072:3080], zpad], axis=1), Z=w[:, 2560:3072],
                C=w[:, 3080:4104], G=w[:, 4104:7176])


def _join_w_in(d):
    return jnp.concatenate([d["A"], d["B"][:, 0:1536], d["Z"], d["B"][:, 1536:1544], d["C"], d["G"]], axis=1)


def _hosted(get, host):
    got = get(host) if get is not None else None
    return got if got is not None else (None, lambda results: None)


def layer_forward(l, xin, xin_b, big, lp, fetch=None):
    tag = f"l{l}_"
    wi = big["w_in"][l]
    sv = dict(xin=xin, xin_b=xin_b)
    for r in "GACBZ":
        sv["p" + r] = matmul(tag + "proj_" + r, xin_b, wi[r], "nn", ACT)
    ride, done = _hosted(fetch, "conformer")
    sv["ya_in"], *rode = halo_fwd(tag + "conformer", f_conformer, ROW_TM, CONV_HALO, sv["pA"], lp["conf"], [[BF16]], ride)
    done(rode)
    ride, done = _hosted(fetch, "gdn_pre")
    sv["qn"], sv["kn"], sv["vv"], sv["gb"], *rode = halo_fwd(tag + "gdn_pre", f_gdn_pre, ROW_TM, GDN_HALO, sv["pB"], lp["pre"],
                                                             [[F32]] * 4, ride)
    done(rode)
    ride, done = _hosted(fetch, "gdn_prep")
    prep_out = prep_fwd(tag + "gdn_prep", sv["qn"], sv["kn"], sv["vv"], sv["gb"], ride)
    prep, sv["tinv"] = prep_out[:6], prep_out[6]
    done(prep_out[7:])
    sv["prep"] = prep
    ride, done = _hosted(fetch, "gdn_scan")
    sv["yb_in"], sv["states"], *rode = scan_fwd(tag + "gdn_scan", lp["scan"], *prep, sv["pZ"], ride)
    done(rode)
    (sv["yc_in"],) = stage_fwd(tag + "sgu", f_sgu, ROW_TM, [sv["pC"]], lp["sgu"], [[BF16]])
    sv["ya"] = matmul(tag + "out_a", sv["ya_in"], big["w_pa"][l], "nn", ACT)
    sv["yb"] = matmul(tag + "out_b", sv["yb_in"], big["w_pb"][l], "nn", ACT)
    sv["yc"] = matmul(tag + "out_c", sv["yc_in"], big["w_pc"][l], "nn", ACT)
    (sv["mix"],) = stage_fwd(tag + "mix", f_mix, POINT_TM, [sv["ya"], sv["yb"], sv["yc"], sv["pG"]], lp["mix"], [[BF16]])
    sv["m"] = matmul(tag + "w_o", sv["mix"], big["w_o"][l], "nn", F32)
    sv["x1"], sv["x1_b"] = stage_fwd(tag + "ln1", f_ln1, POINT_TM, [xin, sv["m"]], lp["ln1"], [[F32, BF16]])
    sv["h"], sv["hpre"] = matmul(tag + "ff1", sv["x1_b"], big["w_ff1"][l], "nn", BF16, fuse=("relu2", lp["ff"][0]))
    sv["f2"] = matmul(tag + "ff2", sv["h"], big["w_ff2"][l], "nn", F32)
    x2, x2_b = stage_fwd(tag + "ln2", f_ln2, POINT_TM, [sv["x1"], sv["f2"]], lp["ln2"], [[F32, BF16]])
    return x2, x2_b, sv


def layer_backward(l, dx2, sv, big, lp, send=None):
    tag = f"l{l}_b_"
    g = {}
    get = (lambda host: send(host, g)) if send is not None else None
    dx1_a, df2, db2, dg2, dbe2 = stage_bwd(tag + "ln2", f_ln2, POINT_TM, [sv["x1"], sv["f2"]], lp["ln2"], [dx2], [F32, BF16])
    g["b_ff2"], g["ln2_g"], g["ln2_b"] = db2[0], dg2[0], dbe2[0]
    g["w_ff2"] = matmul(tag + "dw_ff2", sv["h"], df2, "tn", F32)
    dhpre, db1 = matmul(tag + "dx_ff2", df2, big["w_ff2"][l], "nt", BF16, fuse=("relu2_bwd", sv["hpre"]))
    g["b_ff1"] = db1[0]
    g["w_ff1"] = matmul(tag + "dw_ff1", sv["x1_b"], dhpre, "tn", F32)
    dx1_b = matmul(tag + "dx_ff1", dhpre, big["w_ff1"][l], "nt", F32)
    dxin_a, dm, dg1, dbe1 = stage_bwd(tag + "ln1", f_ln1, POINT_TM, [sv["xin"], sv["m"]], lp["ln1"], [[dx1_a, dx1_b]], [F32, BF16])
    g["ln1_g"], g["ln1_b"] = dg1[0], dbe1[0]
    g["w_o"] = matmul(tag + "dw_o", sv["mix"], dm, "tn", F32)
    dmix = matmul(tag + "dx_o", dm, big["w_o"][l], "nt", ACT)
    dya, dyb, dyc, dp_g, dbg = stage_bwd(tag + "mix", f_mix, POINT_TM, [sv["ya"], sv["yb"], sv["yc"], sv["pG"]], lp["mix"],
                                         [[dmix]], [BF16] * 4)
    g["b_gate"] = dbg[0]
    g["w_pa"] = matmul(tag + "dw_pa", sv["ya_in"], dya, "tn", F32)
    g["w_pb"] = matmul(tag + "dw_pb", sv["yb_in"], dyb, "tn", F32)
    g["w_pc"] = matmul(tag + "dw_pc", sv["yc_in"], dyc, "tn", F32)
    dya_in = matmul(tag + "dx_pa", dya, big["w_pa"][l], "nt", ACT)
    dyb_in = matmul(tag + "dx_pb", dyb, big["w_pb"][l], "nt", ACT)
    dyc_in = matmul(tag + "dx_pc", dyc, big["w_pc"][l], "nt", ACT)
    ride, done = _hosted(get, "conformer")
    dp_a, dcw, dcb, dcg, dcbe, *rode = halo_bwd(tag + "conformer", f_conformer, ROW_TM, CONV_HALO, sv["pA"], lp["conf"], [dya_in],
                                                BF16, ride)
    done(rode)
    g["conv_dw_w"], g["conv_dw_b"], g["conv_ln_g"], g["conv_ln_b"] = dcw, dcb[0], dcg[0], dcbe[0]
    dp_c, dsg, dsb, dsw, dsbs = stage_bwd(tag + "sgu", f_sgu, ROW_TM, [sv["pC"]], lp["sgu"], [[dyc_in]], [BF16])
    g["sgu_ln_g"], g["sgu_ln_b"] = dsg[0], dsb[0]
    g["sgu_w_s"] = dsw.reshape(SGU_GROUPS, SGU_CHUNK, SGU_CHUNK)
    g["sgu_b_s"] = dsbs.T
    ride, done = _hosted(get, "gdn_scan")
    scan_out = scan_bwd(tag + "gdn_scan", lp["scan"], sv["states"], *sv["prep"], sv["pZ"], dyb_in, ride)
    dprep, dp_z, dng = scan_out[:6], scan_out[6], scan_out[7]
    done(scan_out[8:])
    g["gdn_norm_g"] = dng[0]
    dqn, dkn, dvv, dgb = prep_bwd(tag + "gdn_prep", sv["qn"], sv["kn"], sv["vv"], sv["gb"], sv["tinv"], dprep)
    ride, done = _hosted(get, "gdn_pre")
    dp_b, dq, dk, dv, dal, ddt, *rode = halo_bwd(tag + "gdn_pre", f_gdn_pre, ROW_TM, GDN_HALO, sv["pB"], lp["pre"],
                                                 [dqn, dkn, dvv, dgb], BF16, ride)
    done(rode)
    g["gdn_conv_q"], g["gdn_conv_k"], g["gdn_conv_v"] = dq, dk, dv
    g["gdn_a_log"] = dal[0, GDN_HEADS:2 * GDN_HEADS]
    g["gdn_dt_bias"] = ddt[0, GDN_HEADS:2 * GDN_HEADS]
    dps = dict(G=dp_g, A=dp_a, C=dp_c, B=dp_b, Z=dp_z)
    wi = big["w_in"][l]
    g["w_in"] = _join_w_in({r: matmul(tag + "dw_in_" + r, sv["xin_b"], dps[r], "tn", F32) for r in "GACBZ"})
    ride, done = _hosted(get, "dx_in")
    dxin_b, *rode = matmul_sum_nt(tag + "dx_in", [(dps[r], wi[r]) for r in "GACBZ"], F32, ride)
    done(rode)
    return [dxin_a, dxin_b], g


SMALL_PER_LAYER = ("b_gate", "conv_dw_w", "conv_dw_b", "conv_ln_g", "conv_ln_b", "gdn_conv_q", "gdn_conv_k", "gdn_conv_v",
                   "gdn_a_log", "gdn_dt_bias", "gdn_norm_g", "sgu_ln_g", "sgu_ln_b", "sgu_w_s", "sgu_b_s", "ln1_g", "ln1_b",
                   "b_ff1", "b_ff2", "ln2_g", "ln2_b")
BIG = ("w_in", "w_pa", "w_pb", "w_pc", "w_o", "w_ff1", "w_ff2")


def local_step(x, target, big, sm, fetch=None, send=None):
    p_in = [_row(sm["ln_in_g"]), _row(sm["ln_in_b"])]
    ride, done = _hosted(fetch, "ln_in")
    xc, xc_b, *rode = stage_fwd("ln_in", f_ln_in, POINT_TM, [x], p_in, [[F32, BF16]], ride)
    done(rode)
    lps = [_layer_params(sm, l) for l in range(DEPTH)]
    saved = []
    for l in range(DEPTH):
        xc, xc_b, sv = layer_forward(l, xc, xc_b, big, lps[l], fetch if l == 0 else None)
        saved.append(sv)
    loss_part, dy = loss_stage("loss", xc, target, POINT_TM)
    dx = [dy]
    per_layer = [None] * DEPTH

    def send_from(l):
        if send is None or l != 0:
            return None
        return lambda host, g: send(host, per_layer[:l] + [g] + per_layer[l + 1:])

    for l in reversed(range(DEPTH)):
        dx, per_layer[l] = layer_backward(l, dx, saved[l], big, lps[l], send_from(l))
    grad_x, dgi, dbi = stage_bwd("ln_in_b", f_ln_in, POINT_TM, [x], p_in, [dx], [F32])
    small = {n: jnp.stack([per_layer[l][n] for l in range(DEPTH)]) for n in SMALL_PER_LAYER}
    small["ln_in_g"], small["ln_in_b"] = dgi[0], dbi[0]
    return loss_part, grad_x, small, per_layer


PARAM_NAMES = ("ln_in_g", "ln_in_b", "w_in", "b_gate", "conv_dw_w", "conv_dw_b", "conv_ln_g", "conv_ln_b", "w_pa",
               "gdn_conv_q", "gdn_conv_k", "gdn_conv_v", "gdn_a_log", "gdn_dt_bias", "gdn_norm_g", "w_pb", "sgu_ln_g",
               "sgu_ln_b", "sgu_w_s", "sgu_b_s", "w_pc", "w_o", "ln1_g", "ln1_b", "w_ff1", "b_ff1", "w_ff2", "b_ff2",
               "ln2_g", "ln2_b")
SMALL = tuple(n for n in PARAM_NAMES if n not in BIG)
COL_SHARDED_SMALL = ("conv_dw_w", "gdn_conv_q", "gdn_conv_k", "gdn_conv_v")
ROW_SHARDED_BIG = ("w_o", "w_ff2")
ADAM_ROWS = 128
_OUT_PROJ = ("w_pa", "w_pb", "w_pc", "w_o")
FETCH_BEHIND = {"conformer": [("w_ff1", 0), ("w_ff2", 0)], "gdn_pre": [(n, l) for l in (0, 1) for n in _OUT_PROJ],
                "gdn_prep": [("w_in", 1), ("w_ff1", 1)], "gdn_scan": [("w_ff2", 1)]}
SEND_BEHIND = {"conformer": [(n, 1) for n in BIG], "gdn_scan": [("w_ff1", 0), ("w_ff2", 0)],
               "gdn_pre": [(n, 0) for n in _OUT_PROJ], "dx_in": [("w_in", 0)]}


def _unshard(g, rows):
    n, r, c = g.shape
    if rows:
        return g.reshape(n * r, c)
    return jnp.moveaxis(g, 0, 1).reshape(r, n * c)


def _to_shards(w, rows):
    r, c = w.shape
    if rows:
        return w.reshape(N_CHIPS, r // N_CHIPS, c)
    return jnp.moveaxis(w.reshape(r, N_CHIPS, c // N_CHIPS), 1, 0)


def _pack(vals, rows_multiple=8):
    flat = jnp.concatenate([v.reshape(-1) for v in vals])
    per = LANES * rows_multiple
    padded = -(-flat.shape[0] // per) * per
    return jnp.pad(flat, (0, padded - flat.shape[0])).reshape(-1, LANES)


def _unpack(packed, shapes):
    flat = packed.reshape(-1)
    out, at = [], 0
    for s in shapes:
        size = 1
        for d in s:
            size *= d
        out.append(flat[at:at + size].reshape(s))
        at += size
    return out


def kernel(x, ln_in_g, ln_in_b, w_in, b_gate, conv_dw_w, conv_dw_b, conv_ln_g, conv_ln_b, w_pa, gdn_conv_q, gdn_conv_k, gdn_conv_v, gdn_a_log, gdn_dt_bias, gdn_norm_g, w_pb, sgu_ln_g, sgu_ln_b, sgu_w_s, sgu_b_s, w_pc, w_o, ln1_g, ln1_b, w_ff1, b_ff1, w_ff2, b_ff2, ln2_g, ln2_b, loss_target, m_ln_in_g, m_ln_in_b, m_w_in, m_b_gate, m_conv_dw_w, m_conv_dw_b, m_conv_ln_g, m_conv_ln_b, m_w_pa, m_gdn_conv_q, m_gdn_conv_k, m_gdn_conv_v, m_gdn_a_log, m_gdn_dt_bias, m_gdn_norm_g, m_w_pb, m_sgu_ln_g, m_sgu_ln_b, m_sgu_w_s, m_sgu_b_s, m_w_pc, m_w_o, m_ln1_g, m_ln1_b, m_w_ff1, m_b_ff1, m_w_ff2, m_b_ff2, m_ln2_g, m_ln2_b, v_ln_in_g, v_ln_in_b, v_w_in, v_b_gate, v_conv_dw_w, v_conv_dw_b, v_conv_ln_g, v_conv_ln_b, v_w_pa, v_gdn_conv_q, v_gdn_conv_k, v_gdn_conv_v, v_gdn_a_log, v_gdn_dt_bias, v_gdn_norm_g, v_w_pb, v_sgu_ln_g, v_sgu_ln_b, v_sgu_w_s, v_sgu_b_s, v_w_pc, v_w_o, v_ln1_g, v_ln1_b, v_w_ff1, v_b_ff1, v_w_ff2, v_b_ff2, v_ln2_g, v_ln2_b):
    w = dict(zip(PARAM_NAMES, (ln_in_g, ln_in_b, w_in, b_gate, conv_dw_w, conv_dw_b, conv_ln_g, conv_ln_b, w_pa, gdn_conv_q, gdn_conv_k, gdn_conv_v, gdn_a_log, gdn_dt_bias, gdn_norm_g, w_pb, sgu_ln_g, sgu_ln_b, sgu_w_s, sgu_b_s, w_pc, w_o, ln1_g, ln1_b, w_ff1, b_ff1, w_ff2, b_ff2, ln2_g, ln2_b)))
    m = dict(zip(PARAM_NAMES, (m_ln_in_g, m_ln_in_b, m_w_in, m_b_gate, m_conv_dw_w, m_conv_dw_b, m_conv_ln_g, m_conv_ln_b, m_w_pa, m_gdn_conv_q, m_gdn_conv_k, m_gdn_conv_v, m_gdn_a_log, m_gdn_dt_bias, m_gdn_norm_g, m_w_pb, m_sgu_ln_g, m_sgu_ln_b, m_sgu_w_s, m_sgu_b_s, m_w_pc, m_w_o, m_ln1_g, m_ln1_b, m_w_ff1, m_b_ff1, m_w_ff2, m_b_ff2, m_ln2_g, m_ln2_b)))
    v = dict(zip(PARAM_NAMES, (v_ln_in_g, v_ln_in_b, v_w_in, v_b_gate, v_conv_dw_w, v_conv_dw_b, v_conv_ln_g, v_conv_ln_b, v_w_pa, v_gdn_conv_q, v_gdn_conv_k, v_gdn_conv_v, v_gdn_a_log, v_gdn_dt_bias, v_gdn_norm_g, v_w_pb, v_sgu_ln_g, v_sgu_ln_b, v_sgu_w_s, v_sgu_b_s, v_w_pc, v_w_o, v_ln1_g, v_ln1_b, v_w_ff1, v_b_ff1, v_w_ff2, v_b_ff2, v_ln2_g, v_ln2_b)))
    chip = 2 * lax.axis_index("x") + lax.axis_index("y")

    shards = {n: w[n].astype(BF16) for n in BIG}
    col_small_shapes = [w[n].shape for n in COL_SHARDED_SMALL]
    big = {n: [None] * DEPTH for n in BIG}

    def place(keys, gathered):
        for (n, l), g in zip(keys, gathered):
            whole = _unshard(g, n in ROW_SHARDED_BIG)
            big[n][l] = _split_w_in(whole) if n == "w_in" else whole

    sm = {n: w[n] for n in SMALL if n not in COL_SHARDED_SMALL}

    def place_first(gathered):
        place([("w_in", 0)], gathered[:1])
        per_chip = [_unpack(gathered[-1][s], col_small_shapes) for s in range(N_CHIPS)]
        for i, n in enumerate(COL_SHARDED_SMALL):
            sm[n] = jnp.concatenate([per_chip[s][i] for s in range(N_CHIPS)], axis=-1)

    def fetch(host):
        if host == "ln_in":
            return gather_rider([shards["w_in"][0]], [_pack([w[n] for n in COL_SHARDED_SMALL])]), place_first
        keys = FETCH_BEHIND[host]
        return gather_rider([shards[n][l] for n, l in keys], []), lambda gathered: place(keys, gathered)

    received = {}

    def send(host, per_layer):
        keys = SEND_BEHIND[host]
        ride = scatter_rider([_to_shards(per_layer[l][n], n in ROW_SHARDED_BIG).astype(BF16) for n, l in keys])
        return ride, lambda results: received.update(zip(keys, results))

    loss_part, grad_x, grads, _ = local_step(x[0], loss_target[0], big, sm, fetch, send)
    partial = [sum_slots("sum_chips_" + n, [received[(n, l)] for l in range(DEPTH)], ADAM_ROWS) for n in BIG]
    sibling = sibling_exchange("swap_cores", partial)
    out = {}
    for n, mine, theirs in zip(BIG, partial, sibling):
        shape = w[n].shape
        res = adamw("adamw_" + n, [mine, theirs], w[n].reshape(-1, shape[-1]), m[n].reshape(-1, shape[-1]),
                    v[n].reshape(-1, shape[-1]), ADAM_ROWS)
        out[n] = [r.reshape(shape) for r in res]

    small_shapes = [grads[n].shape for n in SMALL]
    vec = _pack([grads[n] for n in SMALL] + [jnp.sum(loss_part).reshape(1)])
    total = sum_slots("sum_small", [gather_all("gather_small", vec)], vec.shape[0])
    whole = _unpack(total, small_shapes + [(1,)])
    loss = whole[-1][0]
    g_small = {}
    for n, g in zip(SMALL, whole[:-1]):
        if n in COL_SHARDED_SMALL:
            width = g.shape[-1] // N_CHIPS
            g = lax.dynamic_slice_in_dim(g, chip * width, width, axis=g.ndim - 1)
        g_small[n] = g
    local_shapes = [w[n].shape for n in SMALL]
    packed_g = _pack([g_small[n] for n in SMALL])
    res = adamw("adamw_small", [packed_g], _pack([w[n] for n in SMALL]), _pack([m[n] for n in SMALL]),
                _pack([v[n] for n in SMALL]), packed_g.shape[0])
    unpacked = [_unpack(r, local_shapes) for r in res]
    for i, n in enumerate(SMALL):
        out[n] = [unpacked[k][i] for k in range(4)]

    return (loss, grad_x[None], *[out[n][0] for n in PARAM_NAMES], *[out[n][1] for n in PARAM_NAMES],
            *[out[n][2] for n in PARAM_NAMES], *[out[n][3] for n in PARAM_NAMES])
```

```python
import functools

import jax
import jax.numpy as jnp
from jax import lax
from jax.experimental import pallas as pl
from jax.experimental.pallas import tpu as pltpu

F32 = jnp.float32
BF16 = jnp.bfloat16
ACT = BF16

D_MODEL = 1024
DEPTH = 2
CONV_DIM = 512
CONV_WIDTH = 31
CONV_HALO = 32
CONV_WINDOW = 128
GDN_HEADS = 4
GDN_HD = 128
GDN_CHUNK = 64
GDN_PAIR = 2 * GDN_CHUNK
GDN_CONV = 4
GDN_HALO = 16
SGU_GROUPS = 4
SGU_GD = 128
SGU_CHUNK = 128
D_FF = 4096
ALPHA = (2 * DEPTH) ** 0.25
LN_EPS = 1e-5
RMS_EPS = 1e-6
ADAM_LR, ADAM_B1, ADAM_B2, ADAM_EPS, ADAM_WD, ADAM_STEP = 0.001, 0.9, 0.999, 1e-08, 0.01, 10
N_CHIPS = 4
N_DEV = 8
LANES = 128
VMEM_LIMIT = 52 * 1024 * 1024

W_GATE, W_A, W_C, W_B, W_Z = 3 * D_MODEL, 2 * CONV_DIM, 2 * 512, 3 * 512 + LANES, 512


def _dg(a, b, ca, cb):
    return lax.dot_general(a.astype(BF16), b.astype(BF16), (((ca,), (cb,)), ((), ())), preferred_element_type=F32)


@jax.custom_vjp
def mm_nn(a, b):
    return _dg(a, b, 1, 0)


mm_nn.defvjp(lambda a, b: (_dg(a, b, 1, 0), (a, b)), lambda r, g: (_dg(g, r[1], 1, 1), _dg(r[0], g, 0, 0)))


@jax.custom_vjp
def mm_nt(a, b):
    return _dg(a, b, 1, 1)


mm_nt.defvjp(lambda a, b: (_dg(a, b, 1, 1), (a, b)), lambda r, g: (_dg(g, r[1], 1, 0), _dg(g, r[0], 0, 0)))


@jax.custom_vjp
def mm_tn(a, b):
    return _dg(a, b, 0, 0)


mm_tn.defvjp(lambda a, b: (_dg(a, b, 0, 0), (a, b)), lambda r, g: (_dg(r[1], g, 1, 1), _dg(r[0], g, 1, 0)))


def _ln(x, g, b):
    mu = jnp.mean(x, -1, keepdims=True)
    xc = x - mu
    var = jnp.mean(xc * xc, -1, keepdims=True)
    return xc * lax.rsqrt(var + LN_EPS) * g + b


def _iota2(shape, dim):
    return lax.broadcasted_iota(jnp.int32, shape, dim)


def f_ln_in(params, x):
    g, b = params
    return (_ln(x, g, b),)


def _roll_rows(x, s):
    return x if s == 0 else pltpu.roll(x, s, 0)


@functools.partial(jax.custom_vjp, nondiff_argnums=(1,))
def shift_rows(x, s):
    return _roll_rows(x, s)


shift_rows.defvjp(lambda x, s: (_roll_rows(x, s), None),
                  lambda s, _, ct: (_roll_rows(ct, (ct.shape[0] - s) % ct.shape[0]),))


def _causal_conv(full, w, halo):
    width = w.shape[0]
    n_out = full.shape[0] - halo
    win = min(CONV_WINDOW, n_out)
    groups = []
    for g in range(full.shape[1] // LANES):
        ls = slice(g * LANES, (g + 1) * LANES)
        wg = w[:, ls]
        parts = []
        for r0 in range(0, n_out, win):
            xw = full[r0:r0 + halo + win, ls]
            acc = wg[width - 1:width, :] * xw[halo:, :]
            for j in range(width - 1):
                acc = acc + wg[j:j + 1, :] * shift_rows(xw, width - 1 - j)[halo:, :]
            parts.append(acc)
        groups.append(jnp.concatenate(parts, axis=0))
    return jnp.concatenate(groups, axis=1)


def f_conformer(params, halo, pa):
    w, b, g, be = params
    full = jnp.concatenate([halo, pa], axis=0)
    h = full[:, :CONV_DIM] * jax.nn.sigmoid(full[:, CONV_DIM:])
    return (jax.nn.silu(_ln(_causal_conv(h, w, CONV_HALO) + b, g, be)),)


def f_gdn_pre(params, halo, pb):
    cq, ck, cv, alog, dtb = params
    full = jnp.concatenate([halo, pb], axis=0)

    def conv(cols, w):
        return jax.nn.silu(_causal_conv(cols, w, GDN_HALO))

    def l2n(t, scale):
        parts = []
        for h in range(GDN_HEADS):
            th = t[:, h * GDN_HD:(h + 1) * GDN_HD]
            parts.append(th * (lax.rsqrt(jnp.sum(th * th, -1, keepdims=True) + RMS_EPS) * scale))
        return jnp.concatenate(parts, axis=1)

    q = l2n(conv(full[:, 0:512], cq), GDN_HD ** -0.5)
    k = l2n(conv(full[:, 512:1024], ck), 1.0)
    v = conv(full[:, 1024:1536], cv)
    logits = pb[:, 1536:1536 + LANES]
    z = logits + dtb
    softplus = jnp.maximum(z, 0.0) + jnp.log(1.0 + jnp.exp(-jnp.abs(z)))
    g = -jnp.exp(alog) * softplus
    beta = jax.nn.sigmoid(logits)
    lane = _iota2(logits.shape, 1)
    gb = jnp.where(lane < GDN_HEADS, beta, jnp.where(lane < 2 * GDN_HEADS, g, 0.0))
    return q, k, v, gb


def _split2(x):
    hi = x.astype(BF16)
    return hi, (x - hi.astype(F32)).astype(BF16)


def _dg3(a, b, ca, cb):
    ah, al = _split2(a)
    bh, bl = _split2(b)
    d = lambda p, q: lax.dot_general(p, q, (((ca,), (cb,)), ((), ())), preferred_element_type=F32)
    return d(ah, bh) + (d(ah, bl) + d(al, bh))


def _inv_unit_lower(lows):
    n = lows[0].shape[0]
    eye = jnp.where(_iota2((n, n), 0) == _iota2((n, n), 1), 1.0, 0.0).astype(F32)
    ps = [-low for low in lows]
    xs = [eye + p for p in ps]
    for _ in range(5):
        ps = [_dg3(p, p, 1, 0) for p in ps]
        xs = [x + _dg3(x, p, 1, 0) for x, p in zip(xs, ps)]
    return xs


def _pair_masks():
    n = GDN_PAIR
    r, c = _iota2((n, n), 0), _iota2((n, n), 1)
    same = (r >= GDN_CHUNK) == (c >= GDN_CHUNK)
    return same, jnp.logical_and(same, r >= c), jnp.logical_and(same, r > c), jnp.logical_and(same, r <= c)


def _masked_sum(mask, g):
    mk = jnp.where(mask, 1.0, 0.0).astype(BF16)
    g1 = g.astype(BF16)
    r1 = g - g1.astype(F32)
    g2 = r1.astype(BF16)
    g3 = (r1 - g2.astype(F32)).astype(BF16)
    d = lambda q: jnp.dot(mk, q, preferred_element_type=F32)
    return d(g1) + (d(g2) + d(g3))


@jax.custom_vjp
def chunk_cumsum(g):
    return _masked_sum(_pair_masks()[1], g)


chunk_cumsum.defvjp(lambda g: (_masked_sum(_pair_masks()[1], g), None), lambda _, ct: (_masked_sum(_pair_masks()[3], ct),))


def _prep_gates(gb):
    gam_all = chunk_cumsum(gb)
    first = _iota2((GDN_PAIR, LANES), 0) < GDN_CHUNK
    glast = jnp.where(first, gam_all[GDN_CHUNK - 1:GDN_CHUNK, :], gam_all[GDN_PAIR - 1:GDN_PAIR, :])
    return gam_all, gam_all.T, glast


def _head_gates(gb, gam_all, gam_t, h, causal):
    gc = gam_all[:, GDN_HEADS + h:GDN_HEADS + h + 1]
    gr = gam_t[GDN_HEADS + h:GDN_HEADS + h + 1, :]
    return gc, gb[:, h:h + 1], jnp.exp(jnp.where(causal, gc - gr, -jnp.inf))


def f_prep_low(kn, gb):
    _, causal, strict, _ = _pair_masks()
    gam_all, gam_t, _ = _prep_gates(gb)
    lows = []
    for h in range(GDN_HEADS):
        k = kn[:, h * GDN_HD:(h + 1) * GDN_HD]
        _, beta, decay = _head_gates(gb, gam_all, gam_t, h, causal)
        lows.append(jnp.where(strict, beta * mm_nt(k, k) * decay, 0.0))
    return jnp.concatenate(lows, axis=1)


def f_prep_rest(qn, kn, vv, gb, tinv_all):
    _, causal, _, _ = _pair_masks()
    gam_all, gam_t, glast = _prep_gates(gb)
    us, ws, as_, qds, kds = [], [], [], [], []
    for h in range(GDN_HEADS):
        hs = slice(h * GDN_HD, (h + 1) * GDN_HD)
        q, k, v, tinv = qn[:, hs], kn[:, hs], vv[:, hs], tinv_all[:, hs]
        gc, beta, decay = _head_gates(gb, gam_all, gam_t, h, causal)
        eg = jnp.exp(gc)
        us.append(mm_nn(tinv, beta * v))
        ws.append(mm_nn(tinv, beta * k * eg))
        as_.append(mm_nt(q, k) * decay)
        qds.append(q * eg)
        kds.append(k * jnp.exp(glast[:, GDN_HEADS + h:GDN_HEADS + h + 1] - gc))
    cat = lambda xs: jnp.concatenate(xs, axis=1)
    return cat(us), cat(ws), cat(as_), cat(qds), cat(kds), jnp.exp(glast)


PREP_TM = 2 * GDN_PAIR


def _ride(ride, n_in, n_out, refs, n_steps):
    if ride is None:
        return
    nri, nro = len(ride["ins"]), len(ride["out_shape"])
    r_in = refs[n_in:n_in + nri]
    r_out = refs[n_in + nri + n_out:n_in + nri + n_out + nro]
    sems = refs[len(refs) - len(ride["scratch"]):]
    step = pl.program_id(0)

    @pl.when(step == 0)
    def _():
        ride["start"](r_in, r_out, sems)

    @pl.when(step == (2 * n_steps) // 3)
    def _():
        ride["middle"](r_in, r_out, sems)

    @pl.when(step == n_steps - 1)
    def _():
        ride["finish"](r_in, r_out, sems)


def _ride_args(ride):
    if ride is None:
        return [], [], [], [], []
    n = len(ride["ins"])
    return list(ride["ins"]), [ANY_SPEC] * n, [ANY_SPEC] * len(ride["out_shape"]), list(ride["out_shape"]), list(ride["scratch"])


def prep_fwd(name, qn, kn, vv, gb, ride=None):
    t = qn.shape[0]
    hw = GDN_HEADS * GDN_HD
    n_steps = t // PREP_TM
    r_ins, r_in_specs, r_out_specs, r_out_shape, r_scratch = _ride_args(ride)

    def body(*refs):
        q_ref, k_ref, v_ref, gb_ref = refs[:4]
        outs = refs[4 + len(r_ins):4 + len(r_ins) + 7]
        pairs = [pl.ds(pi * GDN_PAIR, GDN_PAIR) for pi in range(PREP_TM // GDN_PAIR)]
        lows = [f_prep_low(k_ref[rs, :], gb_ref[rs, :]) for rs in pairs]
        invs = _inv_unit_lower([low[:, h * GDN_HD:(h + 1) * GDN_HD] for low in lows for h in range(GDN_HEADS)])
        for pi, rs in enumerate(pairs):
            tinv = jnp.concatenate(invs[pi * GDN_HEADS:(pi + 1) * GDN_HEADS], axis=1)
            res = f_prep_rest(q_ref[rs, :], k_ref[rs, :], v_ref[rs, :], gb_ref[rs, :], tinv)
            for o_ref, val in zip(outs, res + (tinv,)):
                o_ref[rs, :] = val
        _ride(ride, 4, 7, refs, n_steps)

    widths = [hw, hw, hw, hw, hw, LANES, hw]
    return pl.pallas_call(
        body, name=name, grid=(n_steps,),
        in_specs=[_row_spec(PREP_TM, hw)] * 3 + [_row_spec(PREP_TM, LANES)] + r_in_specs,
        out_specs=[_row_spec(PREP_TM, wd) for wd in widths] + r_out_specs,
        out_shape=[jax.ShapeDtypeStruct((t, wd), F32) for wd in widths] + r_out_shape,
        scratch_shapes=r_scratch, compiler_params=_cparams(),
    )(qn, kn, vv, gb, *r_ins)


def prep_bwd(name, qn, kn, vv, gb, tinv, cts):
    t = qn.shape[0]
    hw = GDN_HEADS * GDN_HD

    def body(q_ref, k_ref, v_ref, gb_ref, x_ref, *refs):
        outs = refs[6:]
        for pi in range(PREP_TM // GDN_PAIR):
            rs = pl.ds(pi * GDN_PAIR, GDN_PAIR)
            ct = tuple(r[rs, :] for r in refs[:6])
            kn_v, gb_v, x_all = k_ref[rs, :], gb_ref[rs, :], x_ref[rs, :]
            _, vjp_low = jax.vjp(f_prep_low, kn_v, gb_v)
            _, vjp_rest = jax.vjp(f_prep_rest, q_ref[rs, :], kn_v, v_ref[rs, :], gb_v, x_all)
            dq, dk, dv, dgb, dx_all = vjp_rest(ct)
            dlows = []
            for h in range(GDN_HEADS):
                hs = slice(h * GDN_HD, (h + 1) * GDN_HD)
                dlows.append(-_dg3(_dg3(x_all[:, hs], dx_all[:, hs], 0, 0), x_all[:, hs], 1, 1))
            dk2, dgb2 = vjp_low(jnp.concatenate(dlows, axis=1))
            outs[0][rs, :] = dq
            outs[1][rs, :] = dk + dk2
            outs[2][rs, :] = dv
            outs[3][rs, :] = dgb + dgb2

    in_w = [hw, hw, hw, LANES, hw] + [hw, hw, hw, hw, hw, LANES]
    out_w = [hw, hw, hw, LANES]
    return pl.pallas_call(
        body, name=name, grid=(t // PREP_TM,),
        in_specs=[_row_spec(PREP_TM, wd) for wd in in_w], out_specs=[_row_spec(PREP_TM, wd) for wd in out_w],
        out_shape=[jax.ShapeDtypeStruct((t, wd), F32) for wd in out_w], compiler_params=_cparams(),
    )(qn, kn, vv, gb, tinv, *cts)


def f_gdn_scan(params, state, u, w, a, qd, kd, egl, z):
    ng = params[0]
    tm = u.shape[0]
    st = [state[h * GDN_HD:(h + 1) * GDN_HD, :] for h in range(GDN_HEADS)]
    zeros = jnp.zeros((GDN_CHUNK, GDN_HD), F32)
    rows = []
    hss = [slice(h * GDN_HD, (h + 1) * GDN_HD) for h in range(GDN_HEADS)]
    for ci in range(tm // GDN_CHUNK):
        rs = slice(ci * GDN_CHUNK, (ci + 1) * GDN_CHUNK)
        vns = [u[rs, hs] - mm_nn(w[rs, hs], s) for hs, s in zip(hss, st)]
        qs = [mm_nn(qd[rs, hs], s) for hs, s in zip(hss, st)]
        upd = [mm_tn(kd[rs, hs], vn) for hs, vn in zip(hss, vns)]
        es = [egl[ci * GDN_CHUNK:ci * GDN_CHUNK + 1, GDN_HEADS + h:GDN_HEADS + h + 1] for h in range(GDN_HEADS)]
        st = [s * e + d for s, e, d in zip(st, es, upd)]
        vpads = [jnp.concatenate([vn, zeros] if ci % 2 == 0 else [zeros, vn], axis=0) for vn in vns]
        os_ = [q + mm_nn(a[rs, hs], vp) for q, hs, vp in zip(qs, hss, vpads)]
        heads = [o * lax.rsqrt(jnp.mean(o * o, -1, keepdims=True) + RMS_EPS) * ng * jax.nn.silu(z[rs, hs])
                 for o, hs in zip(os_, hss)]
        rows.append(jnp.concatenate(heads, axis=1))
    return jnp.concatenate(rows, axis=0), jnp.concatenate(st, axis=0)


def f_sgu(params, puv):
    g, b, ws, bst = params
    tm = puv.shape[0]
    uv = jax.nn.gelu(puv)
    u = uv[:, :512]
    v = _ln(uv[:, 512:], g, b)
    r, c = _iota2((SGU_CHUNK, SGU_CHUNK), 0), _iota2((SGU_CHUNK, SGU_CHUNK), 1)
    rows = []
    for ci in range(tm // SGU_CHUNK):
        rs = slice(ci * SGU_CHUNK, (ci + 1) * SGU_CHUNK)
        groups = []
        for gi in range(SGU_GROUPS):
            gs = slice(gi * SGU_GD, (gi + 1) * SGU_GD)
            wt = jnp.where(r >= c, ws[gs, :], 0.0)
            groups.append(mm_nn(wt, v[rs, gs]) + bst[:, gi:gi + 1])
        rows.append(jnp.concatenate(groups, axis=1))
    return (u * jnp.concatenate(rows, axis=0),)


def f_mix(params, ya, yb, yc, pg):
    bg = params[0]
    s = jax.nn.sigmoid(pg + bg)
    return (s[:, :D_MODEL] * ya + s[:, D_MODEL:2 * D_MODEL] * yb + s[:, 2 * D_MODEL:] * yc,)


def f_ln1(params, xin, m):
    g, b = params
    return (_ln(ALPHA * xin + m, g, b),)


def f_ln2(params, x1, f2):
    b2, g, b = params
    return (_ln(ALPHA * x1 + (f2 + b2), g, b),)


def _cparams(n_axes=1):
    return pltpu.CompilerParams(dimension_semantics=("arbitrary",) * n_axes, vmem_limit_bytes=VMEM_LIMIT)


def _row_spec(tm, width, n_tiles=None):
    if n_tiles is None:
        return pl.BlockSpec((tm, width), lambda i: (i, 0))
    return pl.BlockSpec((tm, width), lambda i: (n_tiles - 1 - i, 0))


def _full_spec(shape):
    return pl.BlockSpec(shape, lambda i: (0,) * len(shape))


def stage_fwd(name, fn, tm, rows, params, outs, ride=None):
    t = rows[0].shape[0]
    nr, npar = len(rows), len(params)
    widths = jax.eval_shape(lambda p, r: fn(p, *r), [jax.ShapeDtypeStruct(p.shape, F32) for p in params],
                            [jax.ShapeDtypeStruct((tm, r.shape[1]), F32) for r in rows])
    r_ins, r_in_specs, r_out_specs, r_out_shape, r_scratch = _ride_args(ride)
    n_out = sum(len(dts) for dts in outs)

    def body(*refs):
        r = [x[...].astype(F32) for x in refs[:nr]]
        p = [x[...] for x in refs[nr:nr + npar]]
        res = fn(p, *r)
        k = nr + npar + len(r_ins)
        for o, dts in zip(res, outs):
            for dt in dts:
                refs[k][...] = o.astype(dt)
                k += 1
        _ride(ride, nr + npar, n_out, refs, t // tm)

    out_shape, out_specs = [], []
    for wd, dts in zip(widths, outs):
        for dt in dts:
            out_shape.append(jax.ShapeDtypeStruct((t, wd.shape[1]), dt))
            out_specs.append(_row_spec(tm, wd.shape[1]))
    return pl.pallas_call(
        body, name=name, grid=(t // tm,),
        in_specs=[_row_spec(tm, r.shape[1]) for r in rows] + [_full_spec(p.shape) for p in params] + r_in_specs,
        out_specs=out_specs + r_out_specs, out_shape=out_shape + r_out_shape, scratch_shapes=r_scratch,
        compiler_params=_cparams(),
    )(*rows, *params, *r_ins)


def stage_bwd(name, fn, tm, rows, params, cts, drow_dtypes, ride=None):
    t = rows[0].shape[0]
    nr, npar = len(rows), len(params)
    flat_cts = [c for cl in cts if cl is not None for c in cl]
    nct = len(flat_cts)
    want = [i for i, dt in enumerate(drow_dtypes) if dt is not None]
    r_ins, r_in_specs, r_out_specs, r_out_shape, r_scratch = _ride_args(ride)

    def body(*refs):
        r = [x[...].astype(F32) for x in refs[:nr]]
        p = [x[...] for x in refs[nr:nr + npar]]
        res, vjp = jax.vjp(lambda pp, rr: fn(pp, *rr), p, r)
        k = nr + npar
        ct = []
        for o, cl in zip(res, cts):
            if cl is None:
                ct.append(jnp.zeros_like(o))
            else:
                acc = refs[k][...].astype(F32)
                for j in range(1, len(cl)):
                    acc = acc + refs[k + j][...].astype(F32)
                k += len(cl)
                ct.append(acc)
        dp, dr = vjp(tuple(ct))
        k += len(r_ins)
        for i in want:
            refs[k][...] = dr[i].astype(drow_dtypes[i])
            k += 1

        @pl.when(pl.program_id(0) == 0)
        def _():
            for j in range(npar):
                refs[k + j][...] = jnp.zeros_like(refs[k + j])

        for j in range(npar):
            refs[k + j][...] += dp[j]
        _ride(ride, nr + npar + nct, len(want) + npar, refs, t // tm)

    out_shape = [jax.ShapeDtypeStruct(rows[i].shape, drow_dtypes[i]) for i in want]
    out_specs = [_row_spec(tm, rows[i].shape[1]) for i in want]
    out_shape += [jax.ShapeDtypeStruct(p.shape, F32) for p in params]
    out_specs += [_full_spec(p.shape) for p in params]
    return pl.pallas_call(
        body, name=name, grid=(t // tm,),
        in_specs=[_row_spec(tm, r.shape[1]) for r in rows] + [_full_spec(p.shape) for p in params]
        + [_row_spec(tm, c.shape[1]) for c in flat_cts] + r_in_specs,
        out_specs=out_specs + r_out_specs, out_shape=out_shape + r_out_shape, scratch_shapes=r_scratch,
        compiler_params=_cparams(),
    )(*rows, *params, *flat_cts, *r_ins)


def _halo_spec(tm, halo, width, n_tiles=None):
    per = tm // halo
    if n_tiles is None:
        return pl.BlockSpec((halo, width), lambda i: (jnp.maximum(i * per - 1, 0), 0))
    return pl.BlockSpec((halo, width), lambda i: (jnp.maximum((n_tiles - 1 - i) * per - 1, 0), 0))


def halo_fwd(name, fn, tm, halo, src, params, outs, ride=None):
    t, width = src.shape
    npar = len(params)
    widths = jax.eval_shape(lambda p, h, r: fn(p, h, r), [jax.ShapeDtypeStruct(p.shape, F32) for p in params],
                            jax.ShapeDtypeStruct((halo, width), F32), jax.ShapeDtypeStruct((tm, width), F32))
    r_ins, r_in_specs, r_out_specs, r_out_shape, r_scratch = _ride_args(ride)
    n_out = sum(len(dts) for dts in outs)

    def body(h_ref, x_ref, *refs):
        hal = jnp.where(pl.program_id(0) == 0, 0.0, h_ref[...].astype(F32))
        res = fn([x[...] for x in refs[:npar]], hal, x_ref[...].astype(F32))
        k = npar + len(r_ins)
        for o, dts in zip(res, outs):
            for dt in dts:
                refs[k][...] = o.astype(dt)
                k += 1
        _ride(ride, 2 + npar, n_out, (h_ref, x_ref) + refs, t // tm)

    out_shape, out_specs = [], []
    for wd, dts in zip(widths, outs):
        for dt in dts:
            out_shape.append(jax.ShapeDtypeStruct((t, wd.shape[1]), dt))
            out_specs.append(_row_spec(tm, wd.shape[1]))
    return pl.pallas_call(
        body, name=name, grid=(t // tm,),
        in_specs=[_halo_spec(tm, halo, width), _row_spec(tm, width)] + [_full_spec(p.shape) for p in params] + r_in_specs,
        out_specs=out_specs + r_out_specs, out_shape=out_shape + r_out_shape, scratch_shapes=r_scratch,
        compiler_params=_cparams(),
    )(src, src, *params, *r_ins)


def halo_bwd(name, fn, tm, halo, src, params, cts, dsrc_dtype, ride=None):
    t, width = src.shape
    n_tiles = t // tm
    npar = len(params)
    n_in = 2 + npar + len(cts)
    r_ins, r_in_specs, r_out_specs, r_out_shape, r_scratch = _ride_args(ride)

    def body(h_ref, x_ref, *refs):
        carry = refs[n_in - 2 + len(r_ins) + 1 + npar + len(r_out_shape)]
        first_tile = pl.program_id(0) == n_tiles - 1
        hal = jnp.where(first_tile, 0.0, h_ref[...].astype(F32))
        p = [x[...] for x in refs[:npar]]
        res, vjp = jax.vjp(fn, p, hal, x_ref[...].astype(F32))
        k = npar
        ct = tuple(refs[k + j][...].astype(F32) for j in range(len(cts)))
        k += len(cts) + len(r_ins)
        dp, dh, dx = vjp(ct)

        @pl.when(pl.program_id(0) == 0)
        def _():
            carry[...] = jnp.zeros_like(carry)
            for j in range(npar):
                refs[k + 1 + j][...] = jnp.zeros_like(refs[k + 1 + j])

        dx = jnp.concatenate([dx[:tm - halo, :], dx[tm - halo:, :] + carry[...]], axis=0)
        refs[k][...] = dx.astype(dsrc_dtype)
        carry[...] = dh
        for j in range(npar):
            refs[k + 1 + j][...] += dp[j]
        _ride(ride, n_in, 1 + npar, (h_ref, x_ref) + refs, n_tiles)

    out_shape = [jax.ShapeDtypeStruct((t, width), dsrc_dtype)] + [jax.ShapeDtypeStruct(p.shape, F32) for p in params]
    out_specs = [_row_spec(tm, width, n_tiles)] + [_full_spec(p.shape) for p in params]
    return pl.pallas_call(
        body, name=name, grid=(n_tiles,),
        in_specs=[_halo_spec(tm, halo, width, n_tiles), _row_spec(tm, width, n_tiles)] + [_full_spec(p.shape) for p in params]
        + [_row_spec(tm, c.shape[1], n_tiles) for c in cts] + r_in_specs,
        out_specs=out_specs + r_out_specs, out_shape=out_shape + r_out_shape,
        scratch_shapes=[pltpu.VMEM((halo, width), F32)] + r_scratch, compiler_params=_cparams(),
    )(src, src, *params, *cts, *r_ins)


SCAN_TM = 256


def scan_fwd(name, norm_g, u, w, a, qd, kd, egl, z, ride=None):
    t = u.shape[0]
    n_tiles = t // SCAN_TM
    srows = GDN_HEADS * GDN_HD

    r_ins, r_in_specs, r_out_specs, r_out_shape, r_scratch = _ride_args(ride)

    def body(*refs):
        ng_ref, u_ref, w_ref, a_ref, qd_ref, kd_ref, e_ref, z_ref = refs[:8]
        o_ref, s_ref = refs[8 + len(r_ins):10 + len(r_ins)]
        state = refs[10 + len(r_ins) + len(r_out_shape)]

        @pl.when(pl.program_id(0) == 0)
        def _():
            state[...] = jnp.zeros_like(state)

        s_ref[0] = state[...]
        o, s_new = f_gdn_scan([ng_ref[...]], state[...], u_ref[...], w_ref[...], a_ref[...], qd_ref[...], kd_ref[...],
                              e_ref[...], z_ref[...].astype(F32))
        o_ref[...] = o.astype(BF16)
        state[...] = s_new
        _ride(ride, 8, 2, refs, n_tiles)

    rows = [u, w, a, qd, kd, egl, z]
    return pl.pallas_call(
        body, name=name, grid=(n_tiles,),
        in_specs=[_full_spec(norm_g.shape)] + [_row_spec(SCAN_TM, r.shape[1]) for r in rows] + r_in_specs,
        out_specs=[_row_spec(SCAN_TM, 512), pl.BlockSpec((1, srows, GDN_HD), lambda i: (i, 0, 0))] + r_out_specs,
        out_shape=[jax.ShapeDtypeStruct((t, 512), BF16), jax.ShapeDtypeStruct((n_tiles, srows, GDN_HD), F32)] + r_out_shape,
        scratch_shapes=[pltpu.VMEM((srows, GDN_HD), F32)] + r_scratch, compiler_params=_cparams(),
    )(norm_g, *rows, *r_ins)


def scan_bwd(name, norm_g, states, u, w, a, qd, kd, egl, z, dout, ride=None):
    t = u.shape[0]
    n_tiles = t // SCAN_TM
    srows = GDN_HEADS * GDN_HD
    r_ins, r_in_specs, r_out_specs, r_out_shape, r_scratch = _ride_args(ride)

    def body(*refs):
        ng_ref, s_ref, u_ref, w_ref, a_ref, qd_ref, kd_ref, e_ref, z_ref, do_ref = refs[:10]
        du_ref, dw_ref, da_ref, dqd_ref, dkd_ref, de_ref, dz_ref, dng_ref = refs[10 + len(r_ins):18 + len(r_ins)]
        dstate = refs[18 + len(r_ins) + len(r_out_shape)]

        @pl.when(pl.program_id(0) == 0)
        def _():
            dstate[...] = jnp.zeros_like(dstate)
            dng_ref[...] = jnp.zeros_like(dng_ref)

        args = ([ng_ref[...]], s_ref[0], u_ref[...], w_ref[...], a_ref[...], qd_ref[...], kd_ref[...], e_ref[...],
                z_ref[...].astype(F32))
        _, vjp = jax.vjp(f_gdn_scan, *args)
        dp, ds, du, dw, da, dqd, dkd, de, dz = vjp((do_ref[...].astype(F32), dstate[...]))
        du_ref[...] = du
        dw_ref[...] = dw
        da_ref[...] = da
        dqd_ref[...] = dqd
        dkd_ref[...] = dkd
        de_ref[...] = de
        dz_ref[...] = dz.astype(BF16)
        dng_ref[...] += dp[0]
        dstate[...] = ds
        _ride(ride, 10, 8, refs, n_tiles)

    rows = [u, w, a, qd, kd, egl, z, dout]
    return pl.pallas_call(
        body, name=name, grid=(n_tiles,),
        in_specs=[_full_spec(norm_g.shape), pl.BlockSpec((1, srows, GDN_HD), lambda i: (n_tiles - 1 - i, 0, 0))]
        + [_row_spec(SCAN_TM, r.shape[1], n_tiles) for r in rows] + r_in_specs,
        out_specs=[_row_spec(SCAN_TM, r.shape[1], n_tiles) for r in rows[:7]] + [_full_spec(norm_g.shape)] + r_out_specs,
        out_shape=[jax.ShapeDtypeStruct(r.shape, F32) for r in rows[:6]] + [jax.ShapeDtypeStruct(z.shape, BF16)]
        + [jax.ShapeDtypeStruct(norm_g.shape, F32)] + r_out_shape,
        scratch_shapes=[pltpu.VMEM((srows, GDN_HD), F32)] + r_scratch, compiler_params=_cparams(),
    )(norm_g, states, *rows, *r_ins)


def loss_stage(name, y, target, tm):
    t, d = y.shape

    def body(y_ref, t_ref, l_ref, dy_ref):
        @pl.when(pl.program_id(0) == 0)
        def _():
            l_ref[...] = jnp.zeros_like(l_ref)

        e = y_ref[...] - t_ref[...]
        dy_ref[...] = e * (1.0 / d)
        sq = e * e
        part = sq[:, 0:LANES]
        for j in range(1, d // LANES):
            part = part + sq[:, j * LANES:(j + 1) * LANES]
        acc = part[0:8, :]
        for j in range(1, tm // 8):
            acc = acc + part[j * 8:(j + 1) * 8, :]
        l_ref[...] += acc * (0.5 / d)

    return pl.pallas_call(
        body, name=name, grid=(t // tm,),
        in_specs=[_row_spec(tm, d), _row_spec(tm, d)],
        out_specs=[_full_spec((8, LANES)), _row_spec(tm, d)],
        out_shape=[jax.ShapeDtypeStruct((8, LANES), F32), jax.ShapeDtypeStruct((t, d), F32)],
        compiler_params=_cparams(),
    )(y, target)


def _pick(n, cands):
    for c in cands:
        if n % c == 0:
            return c
    return n


def matmul(name, a, b, form, out_dtype, acc=None, fuse=None):
    if form == "nn":
        (m, k), n = a.shape, b.shape[1]
    elif form == "nt":
        (m, k), n = a.shape, b.shape[0]
    else:
        (k, m), n = a.shape, b.shape[1]
    if form == "tn":
        tm = _pick(m, (1024, 512, 256, 128))
        tn = n if n <= 2048 else _pick(n, (1536, 1024, 512, 256, 128))
        tk = _pick(k, (2048, 1024, 512, 256, 128))
    else:
        tm = _pick(m, (512, 256, 128))
        tk = k if k <= 4096 else _pick(k, (2048, 1024, 512))
        cap = 2048 * 2048 if fuse is None else 2048 * 1024
        tn = n if n * tk <= cap else _pick(n, (2048, 1536, 1024, 512, 256, 128))
    nk = k // tk
    ca, cb = {"nn": (1, 0), "nt": (1, 1), "tn": (0, 0)}[form]
    a_spec = (pl.BlockSpec((tm, tk), lambda j, i, kk: (i, kk)) if form != "tn"
              else pl.BlockSpec((tk, tm), lambda j, i, kk: (kk, i)))
    b_spec = (pl.BlockSpec((tn, tk), lambda j, i, kk: (j, kk)) if form == "nt"
              else pl.BlockSpec((tk, tn), lambda j, i, kk: (kk, j)))
    o_spec = pl.BlockSpec((tm, tn), lambda j, i, kk: (i, j))
    col_spec = pl.BlockSpec((1, tn), lambda j, i, kk: (0, j))
    has_acc = acc is not None

    assert nk == 1 or (out_dtype == F32 and fuse is None)

    def body(*refs):
        a_ref, b_ref = refs[0], refs[1]
        prod = lax.dot_general(a_ref[...], b_ref[...], (((ca,), (cb,)), ((), ())), preferred_element_type=F32)
        if fuse is not None and fuse[0] == "relu2":
            pre = prod + refs[2][...]
            refs[3][...] = jnp.square(jnp.maximum(pre, 0.0)).astype(BF16)
            refs[4][...] = pre.astype(BF16)
            return
        if fuse is not None and fuse[0] == "relu2_bwd":
            d = prod * (2.0 * jnp.maximum(refs[2][...].astype(F32), 0.0))
            refs[3][...] = d.astype(BF16)

            @pl.when(pl.program_id(1) == 0)
            def _():
                refs[4][...] = jnp.zeros_like(refs[4])

            refs[4][...] += jnp.sum(d, axis=0, keepdims=True)
            return
        o_ref = refs[-1]
        if nk == 1:
            o_ref[...] = (prod + refs[2][...] if has_acc else prod).astype(out_dtype)
            return
        kk = pl.program_id(2)

        @pl.when(kk == 0)
        def _():
            o_ref[...] = prod + refs[2][...] if has_acc else prod

        @pl.when(kk > 0)
        def _():
            o_ref[...] += prod

    ins = [a.astype(BF16), b.astype(BF16)] + ([acc] if has_acc else [])
    in_specs = [a_spec, b_spec] + ([o_spec] if has_acc else [])
    out_specs, out_shape = o_spec, jax.ShapeDtypeStruct((m, n), out_dtype)
    if fuse is not None and fuse[0] == "relu2":
        ins, in_specs = ins + [fuse[1]], in_specs + [col_spec]
        out_specs, out_shape = [o_spec, o_spec], [jax.ShapeDtypeStruct((m, n), BF16)] * 2
    elif fuse is not None:
        ins, in_specs = ins + [fuse[1]], in_specs + [o_spec]
        out_specs = [o_spec, col_spec]
        out_shape = [jax.ShapeDtypeStruct((m, n), BF16), jax.ShapeDtypeStruct((1, n), F32)]
    return pl.pallas_call(
        body, name=name, grid=(n // tn, m // tm, nk),
        in_specs=in_specs, out_specs=out_specs, out_shape=out_shape,
        input_output_aliases={2: 0} if has_acc else {},
        compiler_params=_cparams(3),
    )(*ins)


SUM_NT_TM = 256


def matmul_multi_nn(name, a, bs, out_dtype):
    m, k = a.shape
    nb = len(bs)

    def body(*refs):
        x = refs[0][...]
        for r in range(nb):
            refs[1 + nb + r][...] = jnp.dot(x, refs[1 + r][...], preferred_element_type=F32).astype(out_dtype)

    return pl.pallas_call(
        body, name=name, grid=(m // SUM_NT_TM,),
        in_specs=[pl.BlockSpec((SUM_NT_TM, k), lambda i: (i, 0))] + [_full_spec(b.shape) for b in bs],
        out_specs=[pl.BlockSpec((SUM_NT_TM, b.shape[1]), lambda i: (i, 0)) for b in bs],
        out_shape=[jax.ShapeDtypeStruct((m, b.shape[1]), out_dtype) for b in bs], compiler_params=_cparams(),
    )(a.astype(BF16), *[b.astype(BF16) for b in bs])


def matmul_sum_nt(name, pairs, out_dtype, ride=None):
    m, n = pairs[0][0].shape[0], pairs[0][1].shape[0]
    npair = len(pairs)
    r_ins, r_in_specs, r_out_specs, r_out_shape, r_scratch = _ride_args(ride)

    def body(*refs):
        acc = None
        for r in range(npair):
            prod = lax.dot_general(refs[2 * r][...], refs[2 * r + 1][...], (((1,), (1,)), ((), ())), preferred_element_type=F32)
            acc = prod if acc is None else acc + prod
        refs[2 * npair + len(r_ins)][...] = acc.astype(out_dtype)
        _ride(ride, 2 * npair, 1, refs, m // SUM_NT_TM)

    ins, in_specs = [], []
    for a, b in pairs:
        ins += [a.astype(BF16), b.astype(BF16)]
        in_specs += [pl.BlockSpec((SUM_NT_TM, a.shape[1]), lambda i: (i, 0)), _full_spec(b.shape)]
    return pl.pallas_call(
        body, name=name, grid=(m // SUM_NT_TM,), in_specs=in_specs + r_in_specs,
        out_specs=[pl.BlockSpec((SUM_NT_TM, n), lambda i: (i, 0))] + r_out_specs,
        out_shape=[jax.ShapeDtypeStruct((m, n), out_dtype)] + r_out_shape, scratch_shapes=r_scratch,
        compiler_params=_cparams(),
    )(*ins, *r_ins)


MESH_ID = pl.DeviceIdType.MESH
ANY_SPEC = pl.BlockSpec(memory_space=pl.ANY)


def _place():
    return lax.axis_index("x"), lax.axis_index("y"), lax.axis_index("c")


def _chips():
    x, y, c = _place()
    return x, y, c, 2 * x + y, [(1 - x, y), (x, 1 - y), (1 - x, 1 - y)]


def scatter_rider(arrs):
    n = len(arrs)

    def remote(ins, outs, sems, i, j, landing_from_me):
        x, y, c, me, others = _chips()
        ox, oy = others[j]
        peer = 2 * ox + oy
        return pltpu.make_async_remote_copy(src_ref=ins[i].at[peer], dst_ref=outs[i].at[me if landing_from_me else peer],
                                            send_sem=sems[0].at[3 * i + j], recv_sem=sems[1].at[3 * i + j],
                                            device_id=(ox, oy, c), device_id_type=MESH_ID)

    def local(ins, outs, sems, i):
        me = _chips()[3]
        return pltpu.make_async_copy(ins[i].at[me], outs[i].at[me], sems[2].at[i])

    def start(ins, outs, sems):
        for i in range(n):
            local(ins, outs, sems, i).start()
            for j in range(3):
                remote(ins, outs, sems, i, j, True).start()

    def finish(ins, outs, sems):
        for i in range(n):
            for j in range(3):
                remote(ins, outs, sems, i, j, False).wait_recv()
        for i in range(n):
            for j in range(3):
                remote(ins, outs, sems, i, j, True).wait_send()
            local(ins, outs, sems, i).wait()

    return dict(ins=list(arrs), out_shape=[jax.ShapeDtypeStruct(a.shape, a.dtype) for a in arrs],
                scratch=[pltpu.SemaphoreType.DMA((3 * n,)), pltpu.SemaphoreType.DMA((3 * n,)), pltpu.SemaphoreType.DMA((n,))],
                start=start, middle=lambda ins, outs, sems: None, finish=finish)


def gather_rider(split, flat):
    ns, n = len(split), len(split) + len(flat)
    arrs = list(split) + list(flat)
    half = [a.shape[0] // 2 for a in split]

    def ici(ins, outs, sems, i, j, landing_from_me):
        x, y, c, me, others = _chips()
        ox, oy = others[j]
        slot = me if landing_from_me else 2 * ox + oy
        if i < ns:
            rows = pl.ds(c * half[i], half[i])
            src, dst = ins[i].at[rows], outs[i].at[slot, rows]
        else:
            src, dst = ins[i], outs[i].at[slot]
        return pltpu.make_async_remote_copy(src_ref=src, dst_ref=dst, send_sem=sems[0].at[3 * i + j],
                                            recv_sem=sems[1].at[3 * i + j], device_id=(ox, oy, c), device_id_type=MESH_ID)

    def d2d(outs, sems, i, j, mine):
        x, y, c, me, others = _chips()
        ox, oy = others[j]
        ref = outs[i].at[2 * ox + oy, pl.ds((c if mine else 1 - c) * half[i], half[i])]
        return pltpu.make_async_remote_copy(src_ref=ref, dst_ref=ref, send_sem=sems[2].at[3 * i + j],
                                            recv_sem=sems[3].at[3 * i + j], device_id=(x, y, 1 - c), device_id_type=MESH_ID)

    def local(ins, outs, sems, i):
        return pltpu.make_async_copy(ins[i], outs[i].at[_chips()[3]], sems[4].at[i])

    def start(ins, outs, sems):
        for i in range(n):
            local(ins, outs, sems, i).start()
            for j in range(3):
                ici(ins, outs, sems, i, j, True).start()

    def middle(ins, outs, sems):
        for i in range(n):
            for j in range(3):
                ici(ins, outs, sems, i, j, False).wait_recv()
                if i < ns:
                    d2d(outs, sems, i, j, True).start()

    def finish(ins, outs, sems):
        for i in range(ns):
            for j in range(3):
                d2d(outs, sems, i, j, False).wait_recv()
        for i in range(n):
            for j in range(3):
                ici(ins, outs, sems, i, j, True).wait_send()
                if i < ns:
                    d2d(outs, sems, i, j, True).wait_send()
            local(ins, outs, sems, i).wait()

    return dict(ins=arrs, out_shape=[jax.ShapeDtypeStruct((N_CHIPS,) + a.shape, a.dtype) for a in arrs],
                scratch=[pltpu.SemaphoreType.DMA((3 * n,)), pltpu.SemaphoreType.DMA((3 * n,)),
                         pltpu.SemaphoreType.DMA((3 * max(ns, 1),)), pltpu.SemaphoreType.DMA((3 * max(ns, 1),)),
                         pltpu.SemaphoreType.DMA((n,))],
                start=start, middle=middle, finish=finish)


def run_alone(name, ride):
    n_in, n_out = len(ride["ins"]), len(ride["out_shape"])

    def body(*refs):
        parts = refs[:n_in], refs[n_in:n_in + n_out], refs[n_in + n_out:]
        ride["start"](*parts)
        ride["middle"](*parts)
        ride["finish"](*parts)

    return pl.pallas_call(body, name=name, in_specs=[ANY_SPEC] * n_in, out_specs=[ANY_SPEC] * n_out,
                          out_shape=ride["out_shape"], scratch_shapes=ride["scratch"])(*ride["ins"])


def sibling_exchange(name, arrs):
    n = len(arrs)

    def body(*refs):
        ins, outs = refs[:n], refs[n:2 * n]
        send_sems, recv_sems = refs[2 * n:]
        x, y, c = _place()
        cps = [pltpu.make_async_remote_copy(src_ref=ins[i], dst_ref=outs[i], send_sem=send_sems.at[i], recv_sem=recv_sems.at[i],
                                            device_id=(x, y, 1 - c), device_id_type=MESH_ID) for i in range(n)]
        for cp in cps:
            cp.start()
        for cp in cps:
            cp.wait_recv()
        for cp in cps:
            cp.wait_send()

    return pl.pallas_call(
        body, name=name, in_specs=[ANY_SPEC] * n, out_specs=[ANY_SPEC] * n,
        out_shape=[jax.ShapeDtypeStruct(a.shape, a.dtype) for a in arrs],
        scratch_shapes=[pltpu.SemaphoreType.DMA((n,)), pltpu.SemaphoreType.DMA((n,))],
    )(*arrs)


def gather_all_rider(vec):
    def remote(ins, outs, sems, mask, landing_from_me):
        x, y, c = _place()
        px, py, pc = (1 - x if mask & 4 else x, 1 - y if mask & 2 else y, 1 - c if mask & 1 else c)
        slot = 4 * x + 2 * y + c if landing_from_me else 4 * px + 2 * py + pc
        return pltpu.make_async_remote_copy(src_ref=ins[0], dst_ref=outs[0].at[slot], send_sem=sems[0].at[mask - 1],
                                            recv_sem=sems[1].at[mask - 1], device_id=(px, py, pc), device_id_type=MESH_ID)

    def local(ins, outs, sems):
        x, y, c = _place()
        return pltpu.make_async_copy(ins[0], outs[0].at[4 * x + 2 * y + c], sems[2].at[0])

    def start(ins, outs, sems):
        local(ins, outs, sems).start()
        for mask in range(1, N_DEV):
            remote(ins, outs, sems, mask, True).start()

    def finish(ins, outs, sems):
        for mask in range(1, N_DEV):
            remote(ins, outs, sems, mask, False).wait_recv()
        for mask in range(1, N_DEV):
            remote(ins, outs, sems, mask, True).wait_send()
        local(ins, outs, sems).wait()

    return dict(ins=[vec], out_shape=[jax.ShapeDtypeStruct((N_DEV,) + vec.shape, vec.dtype)],
                scratch=[pltpu.SemaphoreType.DMA((N_DEV - 1,)), pltpu.SemaphoreType.DMA((N_DEV - 1,)), pltpu.SemaphoreType.DMA((1,))],
                start=start, middle=lambda ins, outs, sems: None, finish=finish)


def sum_slots(name, arrs, tr):
    k, r, c = arrs[0].shape
    nb = r // tr

    def body(*refs):
        o_ref = refs[-1]
        for li, a_ref in enumerate(refs[:-1]):
            @pl.when(pl.program_id(0) == li)
            def _(a_ref=a_ref):
                acc = a_ref[0].astype(F32)
                for j in range(1, k):
                    acc = acc + a_ref[j].astype(F32)
                o_ref[...] = acc

    def in_spec(li):
        return pl.BlockSpec((k, tr, c), lambda l, i: (0, jnp.where(l == li, i, jnp.where(l < li, 0, nb - 1)), 0))

    return pl.pallas_call(
        body, name=name, grid=(len(arrs), nb),
        in_specs=[in_spec(li) for li in range(len(arrs))], out_specs=pl.BlockSpec((tr, c), lambda l, i: (l * nb + i, 0)),
        out_shape=jax.ShapeDtypeStruct((len(arrs) * r, c), F32), compiler_params=_cparams(2),
    )(*arrs)


def adamw(name, parts, w, m, v, tr, ride=None):
    r, c = w.shape
    npart = len(parts)
    bc1 = 1.0 - ADAM_B1 ** ADAM_STEP
    bc2 = 1.0 - ADAM_B2 ** ADAM_STEP
    r_ins, r_in_specs, r_out_specs, r_out_shape, r_scratch = _ride_args(ride)

    def body(*refs):
        _ride(ride, npart + 3, 4, refs, r // tr)
        g = refs[0][...]
        for j in range(1, npart):
            g = g + refs[j][...]
        w_ref, m_ref, v_ref = refs[npart:npart + 3]
        g_out, d_out, m_out, v_out = refs[npart + 3 + len(r_ins):npart + 7 + len(r_ins)]
        m_new = ADAM_B1 * m_ref[...] + (1.0 - ADAM_B1) * g
        v_new = ADAM_B2 * v_ref[...] + (1.0 - ADAM_B2) * jnp.square(g)
        m_hat = m_new / bc1
        v_hat = v_new / bc2
        g_out[...] = g
        d_out[...] = -ADAM_LR * (m_hat / (jnp.sqrt(v_hat) + ADAM_EPS) + ADAM_WD * w_ref[...])
        m_out[...] = m_new
        v_out[...] = v_new

    spec = pl.BlockSpec((tr, c), lambda i: (i, 0))
    return pl.pallas_call(
        body, name=name, grid=(r // tr,), in_specs=[spec] * (npart + 3) + r_in_specs, out_specs=[spec] * 4 + r_out_specs,
        out_shape=[jax.ShapeDtypeStruct((r, c), F32)] * 4 + r_out_shape, scratch_shapes=r_scratch,
        compiler_params=_cparams(),
    )(*parts, w, m, v, *r_ins)


ROW_TM = 256
POINT_TM = 512


def _row(v):
    return v.reshape(1, -1)


def _lane_pad(vals, at):
    return jnp.concatenate([jnp.zeros((at,), F32), vals, jnp.zeros((LANES - at - vals.shape[0],), F32)]).reshape(1, LANES)


def _layer_params(sm, l):
    return dict(
        conf=[sm["conv_dw_w"][l], _row(sm["conv_dw_b"][l]), _row(sm["conv_ln_g"][l]), _row(sm["conv_ln_b"][l])],
        pre=[sm["gdn_conv_q"][l], sm["gdn_conv_k"][l], sm["gdn_conv_v"][l],
             _lane_pad(sm["gdn_a_log"][l], GDN_HEADS), _lane_pad(sm["gdn_dt_bias"][l], GDN_HEADS)],
        scan=_row(sm["gdn_norm_g"][l]),
        sgu=[_row(sm["sgu_ln_g"][l]), _row(sm["sgu_ln_b"][l]), sm["sgu_w_s"][l].reshape(SGU_GROUPS * SGU_CHUNK, SGU_CHUNK),
             sm["sgu_b_s"][l].T],
        mix=[_row(sm["b_gate"][l])],
        ln1=[_row(sm["ln1_g"][l]), _row(sm["ln1_b"][l])],
        ff=[_row(sm["b_ff1"][l])],
        ln2=[_row(sm["b_ff2"][l]), _row(sm["ln2_g"][l]), _row(sm["ln2_b"][l])],
    )


def _split_w_in(w):
    zpad = jnp.zeros((D_MODEL, LANES - 2 * GDN_HEADS), w.dtype)
    return dict(A=w[:, 0:1024], B=jnp.concatenate([w[:, 1024:2560], w[:, 3072:3080], zpad], axis=1), Z=w[:, 2560:3072],
                C=w[:, 3080:4104], G=w[:, 4104:7176])


def _join_w_in(d):
    return jnp.concatenate([d["A"], d["B"][:, 0:1536], d["Z"], d["B"][:, 1536:1544], d["C"], d["G"]], axis=1)


def _hosted(get, host):
    got = get(host) if get is not None else None
    return got if got is not None else (None, lambda results: None)


def layer_forward(l, xin, xin_b, big, lp, fetch=None):
    tag = f"l{l}_"
    wi = big["w_in"][l]
    sv = dict(xin=xin, xin_b=xin_b)
    for r, p in zip("GACBZ", matmul_multi_nn(tag + "proj", xin_b, [wi[r] for r in "GACBZ"], ACT)):
        sv["p" + r] = p
    ride, done = _hosted(fetch, "conformer")
    sv["ya_in"], *rode = halo_fwd(tag + "conformer", f_conformer, ROW_TM, CONV_HALO, sv["pA"], lp["conf"], [[BF16]], ride)
    done(rode)
    ride, done = _hosted(fetch, "gdn_pre")
    sv["qn"], sv["kn"], sv["vv"], sv["gb"], *rode = halo_fwd(tag + "gdn_pre", f_gdn_pre, ROW_TM, GDN_HALO, sv["pB"], lp["pre"],
                                                             [[F32]] * 4, ride)
    done(rode)
    ride, done = _hosted(fetch, "gdn_prep")
    prep_out = prep_fwd(tag + "gdn_prep", sv["qn"], sv["kn"], sv["vv"], sv["gb"], ride)
    prep, sv["tinv"] = prep_out[:6], prep_out[6]
    done(prep_out[7:])
    sv["prep"] = prep
    ride, done = _hosted(fetch, "gdn_scan")
    sv["yb_in"], sv["states"], *rode = scan_fwd(tag + "gdn_scan", lp["scan"], *prep, sv["pZ"], ride)
    done(rode)
    (sv["yc_in"],) = stage_fwd(tag + "sgu", f_sgu, ROW_TM, [sv["pC"]], lp["sgu"], [[BF16]])
    sv["ya"] = matmul(tag + "out_a", sv["ya_in"], big["w_pa"][l], "nn", ACT)
    sv["yb"] = matmul(tag + "out_b", sv["yb_in"], big["w_pb"][l], "nn", ACT)
    sv["yc"] = matmul(tag + "out_c", sv["yc_in"], big["w_pc"][l], "nn", ACT)
    (sv["mix"],) = stage_fwd(tag + "mix", f_mix, POINT_TM, [sv["ya"], sv["yb"], sv["yc"], sv["pG"]], lp["mix"], [[BF16]])
    sv["m"] = matmul(tag + "w_o", sv["mix"], big["w_o"][l], "nn", F32)
    sv["x1"], sv["x1_b"] = stage_fwd(tag + "ln1", f_ln1, POINT_TM, [xin, sv["m"]], lp["ln1"], [[F32, BF16]])
    sv["h"], sv["hpre"] = matmul(tag + "ff1", sv["x1_b"], big["w_ff1"][l], "nn", BF16, fuse=("relu2", lp["ff"][0]))
    sv["f2"] = matmul(tag + "ff2", sv["h"], big["w_ff2"][l], "nn", F32)
    x2, x2_b = stage_fwd(tag + "ln2", f_ln2, POINT_TM, [sv["x1"], sv["f2"]], lp["ln2"], [[F32, BF16]])
    return x2, x2_b, sv


def layer_backward(l, dx2, sv, big, lp, send=None):
    tag = f"l{l}_b_"
    g = {}
    get = (lambda host: send(host, g)) if send is not None else None
    dx1_a, df2, db2, dg2, dbe2 = stage_bwd(tag + "ln2", f_ln2, POINT_TM, [sv["x1"], sv["f2"]], lp["ln2"], [dx2], [F32, BF16])
    g["b_ff2"], g["ln2_g"], g["ln2_b"] = db2[0], dg2[0], dbe2[0]
    g["w_ff2"] = matmul(tag + "dw_ff2", sv["h"], df2, "tn", F32)
    dhpre, db1 = matmul(tag + "dx_ff2", df2, big["w_ff2"][l], "nt", BF16, fuse=("relu2_bwd", sv["hpre"]))
    g["b_ff1"] = db1[0]
    g["w_ff1"] = matmul(tag + "dw_ff1", sv["x1_b"], dhpre, "tn", F32)
    dx1_b = matmul(tag + "dx_ff1", dhpre, big["w_ff1"][l], "nt", F32)
    dxin_a, dm, dg1, dbe1 = stage_bwd(tag + "ln1", f_ln1, POINT_TM, [sv["xin"], sv["m"]], lp["ln1"], [[dx1_a, dx1_b]], [F32, BF16])
    g["ln1_g"], g["ln1_b"] = dg1[0], dbe1[0]
    g["w_o"] = matmul(tag + "dw_o", sv["mix"], dm, "tn", F32)
    dmix = matmul(tag + "dx_o", dm, big["w_o"][l], "nt", ACT)
    dya, dyb, dyc, dp_g, dbg = stage_bwd(tag + "mix", f_mix, POINT_TM, [sv["ya"], sv["yb"], sv["yc"], sv["pG"]], lp["mix"],
                                         [[dmix]], [BF16] * 4)
    g["b_gate"] = dbg[0]
    g["w_pa"] = matmul(tag + "dw_pa", sv["ya_in"], dya, "tn", F32)
    g["w_pb"] = matmul(tag + "dw_pb", sv["yb_in"], dyb, "tn", F32)
    g["w_pc"] = matmul(tag + "dw_pc", sv["yc_in"], dyc, "tn", F32)
    dya_in = matmul(tag + "dx_pa", dya, big["w_pa"][l], "nt", ACT)
    dyb_in = matmul(tag + "dx_pb", dyb, big["w_pb"][l], "nt", ACT)
    dyc_in = matmul(tag + "dx_pc", dyc, big["w_pc"][l], "nt", ACT)
    ride, done = _hosted(get, "conformer")
    dp_a, dcw, dcb, dcg, dcbe, *rode = halo_bwd(tag + "conformer", f_conformer, ROW_TM, CONV_HALO, sv["pA"], lp["conf"], [dya_in],
                                                BF16, ride)
    done(rode)
    g["conv_dw_w"], g["conv_dw_b"], g["conv_ln_g"], g["conv_ln_b"] = dcw, dcb[0], dcg[0], dcbe[0]
    dp_c, dsg, dsb, dsw, dsbs = stage_bwd(tag + "sgu", f_sgu, ROW_TM, [sv["pC"]], lp["sgu"], [[dyc_in]], [BF16])
    g["sgu_ln_g"], g["sgu_ln_b"] = dsg[0], dsb[0]
    g["sgu_w_s"] = dsw.reshape(SGU_GROUPS, SGU_CHUNK, SGU_CHUNK)
    g["sgu_b_s"] = dsbs.T
    ride, done = _hosted(get, "gdn_scan")
    scan_out = scan_bwd(tag + "gdn_scan", lp["scan"], sv["states"], *sv["prep"], sv["pZ"], dyb_in, ride)
    dprep, dp_z, dng = scan_out[:6], scan_out[6], scan_out[7]
    done(scan_out[8:])
    g["gdn_norm_g"] = dng[0]
    dqn, dkn, dvv, dgb = prep_bwd(tag + "gdn_prep", sv["qn"], sv["kn"], sv["vv"], sv["gb"], sv["tinv"], dprep)
    ride, done = _hosted(get, "gdn_pre")
    dp_b, dq, dk, dv, dal, ddt, *rode = halo_bwd(tag + "gdn_pre", f_gdn_pre, ROW_TM, GDN_HALO, sv["pB"], lp["pre"],
                                                 [dqn, dkn, dvv, dgb], BF16, ride)
    done(rode)
    g["gdn_conv_q"], g["gdn_conv_k"], g["gdn_conv_v"] = dq, dk, dv
    g["gdn_a_log"] = dal[0, GDN_HEADS:2 * GDN_HEADS]
    g["gdn_dt_bias"] = ddt[0, GDN_HEADS:2 * GDN_HEADS]
    dps = dict(G=dp_g, A=dp_a, C=dp_c, B=dp_b, Z=dp_z)
    wi = big["w_in"][l]
    g["w_in"] = _join_w_in({r: matmul(tag + "dw_in_" + r, sv["xin_b"], dps[r], "tn", F32) for r in "GACBZ"})
    ride, done = _hosted(get, "dx_in")
    dxin_b, *rode = matmul_sum_nt(tag + "dx_in", [(dps[r], wi[r]) for r in "GACBZ"], F32, ride)
    done(rode)
    return [dxin_a, dxin_b], g


SMALL_PER_LAYER = ("b_gate", "conv_dw_w", "conv_dw_b", "conv_ln_g", "conv_ln_b", "gdn_conv_q", "gdn_conv_k", "gdn_conv_v",
                   "gdn_a_log", "gdn_dt_bias", "gdn_norm_g", "sgu_ln_g", "sgu_ln_b", "sgu_w_s", "sgu_b_s", "ln1_g", "ln1_b",
                   "b_ff1", "b_ff2", "ln2_g", "ln2_b")
BIG = ("w_in", "w_pa", "w_pb", "w_pc", "w_o", "w_ff1", "w_ff2")


def local_step(x, target, big, sm, fetch=None, send=None):
    p_in = [_row(sm["ln_in_g"]), _row(sm["ln_in_b"])]
    ride, done = _hosted(fetch, "ln_in")
    xc, xc_b, *rode = stage_fwd("ln_in", f_ln_in, POINT_TM, [x], p_in, [[F32, BF16]], ride)
    done(rode)
    lps = [_layer_params(sm, l) for l in range(DEPTH)]
    saved = []
    for l in range(DEPTH):
        xc, xc_b, sv = layer_forward(l, xc, xc_b, big, lps[l], fetch if l == 0 else None)
        saved.append(sv)
    loss_part, dy = loss_stage("loss", xc, target, POINT_TM)
    dx = [dy]
    per_layer = [None] * DEPTH

    def send_from(l):
        if send is None or l != 0:
            return None
        return lambda host, g: send(host, per_layer[:l] + [g] + per_layer[l + 1:])

    for l in reversed(range(DEPTH)):
        dx, per_layer[l] = layer_backward(l, dx, saved[l], big, lps[l], send_from(l))
    grad_x, dgi, dbi = stage_bwd("ln_in_b", f_ln_in, POINT_TM, [x], p_in, [dx], [F32])
    small = {n: jnp.stack([per_layer[l][n] for l in range(DEPTH)]) for n in SMALL_PER_LAYER}
    small["ln_in_g"], small["ln_in_b"] = dgi[0], dbi[0]
    return loss_part, grad_x, small, per_layer


PARAM_NAMES = ("ln_in_g", "ln_in_b", "w_in", "b_gate", "conv_dw_w", "conv_dw_b", "conv_ln_g", "conv_ln_b", "w_pa",
               "gdn_conv_q", "gdn_conv_k", "gdn_conv_v", "gdn_a_log", "gdn_dt_bias", "gdn_norm_g", "w_pb", "sgu_ln_g",
               "sgu_ln_b", "sgu_w_s", "sgu_b_s", "w_pc", "w_o", "ln1_g", "ln1_b", "w_ff1", "b_ff1", "w_ff2", "b_ff2",
               "ln2_g", "ln2_b")
SMALL = tuple(n for n in PARAM_NAMES if n not in BIG)
COL_SHARDED_SMALL = ("conv_dw_w", "gdn_conv_q", "gdn_conv_k", "gdn_conv_v")
ROW_SHARDED_BIG = ("w_o", "w_ff2")
ADAM_ROWS = 128
_OUT_PROJ = ("w_pa", "w_pb", "w_pc", "w_o")
FETCH_BEHIND = {"conformer": [("w_ff1", 0), ("w_ff2", 0)], "gdn_pre": [(n, l) for l in (0, 1) for n in _OUT_PROJ],
                "gdn_prep": [("w_in", 1), ("w_ff1", 1)], "gdn_scan": [("w_ff2", 1)]}
SEND_BEHIND = {"conformer": [(n, 1) for n in BIG], "gdn_scan": [("w_ff1", 0), ("w_ff2", 0)],
               "gdn_pre": [(n, 0) for n in _OUT_PROJ], "dx_in": [("w_in", 0)]}


def _unshard(g, rows):
    n, r, c = g.shape
    if rows:
        return g.reshape(n * r, c)
    return jnp.moveaxis(g, 0, 1).reshape(r, n * c)


def _to_shards(w, rows):
    r, c = w.shape
    if rows:
        return w.reshape(N_CHIPS, r // N_CHIPS, c)
    return jnp.moveaxis(w.reshape(r, N_CHIPS, c // N_CHIPS), 1, 0)


def _pack(vals, rows_multiple=8):
    flat = jnp.concatenate([v.reshape(-1) for v in vals])
    per = LANES * rows_multiple
    padded = -(-flat.shape[0] // per) * per
    return jnp.pad(flat, (0, padded - flat.shape[0])).reshape(-1, LANES)


def _unpack(packed, shapes):
    flat = packed.reshape(-1)
    out, at = [], 0
    for s in shapes:
        size = 1
        for d in s:
            size *= d
        out.append(flat[at:at + size].reshape(s))
        at += size
    return out


def kernel(x, ln_in_g, ln_in_b, w_in, b_gate, conv_dw_w, conv_dw_b, conv_ln_g, conv_ln_b, w_pa, gdn_conv_q, gdn_conv_k, gdn_conv_v, gdn_a_log, gdn_dt_bias, gdn_norm_g, w_pb, sgu_ln_g, sgu_ln_b, sgu_w_s, sgu_b_s, w_pc, w_o, ln1_g, ln1_b, w_ff1, b_ff1, w_ff2, b_ff2, ln2_g, ln2_b, loss_target, m_ln_in_g, m_ln_in_b, m_w_in, m_b_gate, m_conv_dw_w, m_conv_dw_b, m_conv_ln_g, m_conv_ln_b, m_w_pa, m_gdn_conv_q, m_gdn_conv_k, m_gdn_conv_v, m_gdn_a_log, m_gdn_dt_bias, m_gdn_norm_g, m_w_pb, m_sgu_ln_g, m_sgu_ln_b, m_sgu_w_s, m_sgu_b_s, m_w_pc, m_w_o, m_ln1_g, m_ln1_b, m_w_ff1, m_b_ff1, m_w_ff2, m_b_ff2, m_ln2_g, m_ln2_b, v_ln_in_g, v_ln_in_b, v_w_in, v_b_gate, v_conv_dw_w, v_conv_dw_b, v_conv_ln_g, v_conv_ln_b, v_w_pa, v_gdn_conv_q, v_gdn_conv_k, v_gdn_conv_v, v_gdn_a_log, v_gdn_dt_bias, v_gdn_norm_g, v_w_pb, v_sgu_ln_g, v_sgu_ln_b, v_sgu_w_s, v_sgu_b_s, v_w_pc, v_w_o, v_ln1_g, v_ln1_b, v_w_ff1, v_b_ff1, v_w_ff2, v_b_ff2, v_ln2_g, v_ln2_b):
    w = dict(zip(PARAM_NAMES, (ln_in_g, ln_in_b, w_in, b_gate, conv_dw_w, conv_dw_b, conv_ln_g, conv_ln_b, w_pa, gdn_conv_q, gdn_conv_k, gdn_conv_v, gdn_a_log, gdn_dt_bias, gdn_norm_g, w_pb, sgu_ln_g, sgu_ln_b, sgu_w_s, sgu_b_s, w_pc, w_o, ln1_g, ln1_b, w_ff1, b_ff1, w_ff2, b_ff2, ln2_g, ln2_b)))
    m = dict(zip(PARAM_NAMES, (m_ln_in_g, m_ln_in_b, m_w_in, m_b_gate, m_conv_dw_w, m_conv_dw_b, m_conv_ln_g, m_conv_ln_b, m_w_pa, m_gdn_conv_q, m_gdn_conv_k, m_gdn_conv_v, m_gdn_a_log, m_gdn_dt_bias, m_gdn_norm_g, m_w_pb, m_sgu_ln_g, m_sgu_ln_b, m_sgu_w_s, m_sgu_b_s, m_w_pc, m_w_o, m_ln1_g, m_ln1_b, m_w_ff1, m_b_ff1, m_w_ff2, m_b_ff2, m_ln2_g, m_ln2_b)))
    v = dict(zip(PARAM_NAMES, (v_ln_in_g, v_ln_in_b, v_w_in, v_b_gate, v_conv_dw_w, v_conv_dw_b, v_conv_ln_g, v_conv_ln_b, v_w_pa, v_gdn_conv_q, v_gdn_conv_k, v_gdn_conv_v, v_gdn_a_log, v_gdn_dt_bias, v_gdn_norm_g, v_w_pb, v_sgu_ln_g, v_sgu_ln_b, v_sgu_w_s, v_sgu_b_s, v_w_pc, v_w_o, v_ln1_g, v_ln1_b, v_w_ff1, v_b_ff1, v_w_ff2, v_b_ff2, v_ln2_g, v_ln2_b)))
    chip = 2 * lax.axis_index("x") + lax.axis_index("y")

    shards = {n: w[n].astype(BF16) for n in BIG}
    col_small_shapes = [w[n].shape for n in COL_SHARDED_SMALL]
    big = {n: [None] * DEPTH for n in BIG}

    def place(keys, gathered):
        for (n, l), g in zip(keys, gathered):
            whole = _unshard(g, n in ROW_SHARDED_BIG)
            big[n][l] = _split_w_in(whole) if n == "w_in" else whole

    sm = {n: w[n] for n in SMALL if n not in COL_SHARDED_SMALL}

    def place_first(gathered):
        place([("w_in", 0)], gathered[:1])
        per_chip = [_unpack(gathered[-1][s], col_small_shapes) for s in range(N_CHIPS)]
        for i, n in enumerate(COL_SHARDED_SMALL):
            sm[n] = jnp.concatenate([per_chip[s][i] for s in range(N_CHIPS)], axis=-1)

    def fetch(host):
        if host == "ln_in":
            return gather_rider([shards["w_in"][0]], [_pack([w[n] for n in COL_SHARDED_SMALL])]), place_first
        keys = FETCH_BEHIND[host]
        return gather_rider([shards[n][l] for n, l in keys], []), lambda gathered: place(keys, gathered)

    received = {}

    def send(host, per_layer):
        keys = SEND_BEHIND[host]
        ride = scatter_rider([_to_shards(per_layer[l][n], n in ROW_SHARDED_BIG).astype(BF16) for n, l in keys])
        return ride, lambda results: received.update(zip(keys, results))

    loss_part, grad_x, grads, _ = local_step(x[0], loss_target[0], big, sm, fetch, send)
    partial = [sum_slots("sum_chips_" + n, [received[(n, l)] for l in range(DEPTH)], ADAM_ROWS) for n in BIG]
    sibling = sibling_exchange("swap_cores", partial)
    small_shapes = [grads[n].shape for n in SMALL]
    vec = _pack([grads[n] for n in SMALL] + [jnp.sum(loss_part).reshape(1)])
    out = {}
    for n, mine, theirs in zip(BIG, partial, sibling):
        shape = w[n].shape
        res = adamw("adamw_" + n, [mine, theirs], w[n].reshape(-1, shape[-1]), m[n].reshape(-1, shape[-1]),
                    v[n].reshape(-1, shape[-1]), ADAM_ROWS, gather_all_rider(vec) if n == "w_in" else None)
        out[n] = [r.reshape(shape) for r in res[:4]]
        if n == "w_in":
            all_small = res[4]
    total = sum_slots("sum_small", [all_small], vec.shape[0])
    whole = _unpack(total, small_shapes + [(1,)])
    loss = whole[-1][0]
    g_small = {}
    for n, g in zip(SMALL, whole[:-1]):
        if n in COL_SHARDED_SMALL:
            width = g.shape[-1] // N_CHIPS
            g = lax.dynamic_slice_in_dim(g, chip * width, width, axis=g.ndim - 1)
        g_small[n] = g
    local_shapes = [w[n].shape for n in SMALL]
    packed_g = _pack([g_small[n] for n in SMALL])
    res = adamw("adamw_small", [packed_g], _pack([w[n] for n in SMALL]), _pack([m[n] for n in SMALL]),
                _pack([v[n] for n in SMALL]), packed_g.shape[0])
    unpacked = [_unpack(r, local_shapes) for r in res]
    for i, n in enumerate(SMALL):
        out[n] = [unpacked[k][i] for k in range(4)]

    return (loss, grad_x[None], *[out[n][0] for n in PARAM_NAMES], *[out[n][1] for n in PARAM_NAMES],
            *[out[n][2] for n in PARAM_NAMES], *[out[n][3] for n in PARAM_NAMES])
```

```python
import functools

import jax
import jax.numpy as jnp
from jax import lax
from jax.experimental import pallas as pl
from jax.experimental.pallas import tpu as pltpu

F32 = jnp.float32
BF16 = jnp.bfloat16
ACT = BF16

D_MODEL = 1024
DEPTH = 2
CONV_DIM = 512
CONV_WIDTH = 31
CONV_HALO = 32
CONV_WINDOW = 128
GDN_HEADS = 4
GDN_HD = 128
GDN_CHUNK = 64
GDN_PAIR = 2 * GDN_CHUNK
GDN_CONV = 4
GDN_HALO = 16
SGU_GROUPS = 4
SGU_GD = 128
SGU_CHUNK = 128
D_FF = 4096
ALPHA = (2 * DEPTH) ** 0.25
LN_EPS = 1e-5
RMS_EPS = 1e-6
ADAM_LR, ADAM_B1, ADAM_B2, ADAM_EPS, ADAM_WD, ADAM_STEP = 0.001, 0.9, 0.999, 1e-08, 0.01, 10
N_CHIPS = 4
N_DEV = 8
LANES = 128
VMEM_LIMIT = 52 * 1024 * 1024


def _dg(a, b, ca, cb):
    return lax.dot_general(a.astype(BF16), b.astype(BF16), (((ca,), (cb,)), ((), ())), preferred_element_type=F32)


@jax.custom_vjp
def mm_nn(a, b):
    return _dg(a, b, 1, 0)


mm_nn.defvjp(lambda a, b: (_dg(a, b, 1, 0), (a, b)), lambda r, g: (_dg(g, r[1], 1, 1), _dg(r[0], g, 0, 0)))


@jax.custom_vjp
def mm_nt(a, b):
    return _dg(a, b, 1, 1)


mm_nt.defvjp(lambda a, b: (_dg(a, b, 1, 1), (a, b)), lambda r, g: (_dg(g, r[1], 1, 0), _dg(g, r[0], 0, 0)))


@jax.custom_vjp
def mm_tn(a, b):
    return _dg(a, b, 0, 0)


mm_tn.defvjp(lambda a, b: (_dg(a, b, 0, 0), (a, b)), lambda r, g: (_dg(r[1], g, 1, 1), _dg(r[0], g, 1, 0)))


def _ln(x, g, b):
    mu = jnp.mean(x, -1, keepdims=True)
    xc = x - mu
    var = jnp.mean(xc * xc, -1, keepdims=True)
    return xc * lax.rsqrt(var + LN_EPS) * g + b


def _iota2(shape, dim):
    return lax.broadcasted_iota(jnp.int32, shape, dim)


def f_ln_in(params, x):
    g, b = params
    return (_ln(x, g, b),)


def _roll_rows(x, s):
    return x if s == 0 else pltpu.roll(x, s, 0)


@functools.partial(jax.custom_vjp, nondiff_argnums=(1,))
def shift_rows(x, s):
    return _roll_rows(x, s)


shift_rows.defvjp(lambda x, s: (_roll_rows(x, s), None),
                  lambda s, _, ct: (_roll_rows(ct, (ct.shape[0] - s) % ct.shape[0]),))


def _causal_conv(full, w, halo):
    width = w.shape[0]
    n_out = full.shape[0] - halo
    win = min(CONV_WINDOW, n_out)
    groups = []
    for g in range(full.shape[1] // LANES):
        ls = slice(g * LANES, (g + 1) * LANES)
        wg = w[:, ls]
        parts = []
        for r0 in range(0, n_out, win):
            xw = full[r0:r0 + halo + win, ls]
            acc = wg[width - 1:width, :] * xw[halo:, :]
            for j in range(width - 1):
                acc = acc + wg[j:j + 1, :] * shift_rows(xw, width - 1 - j)[halo:, :]
            parts.append(acc)
        groups.append(jnp.concatenate(parts, axis=0))
    return jnp.concatenate(groups, axis=1)


def f_conformer(params, halo, pa):
    w, b, g, be = params
    full = jnp.concatenate([halo, pa], axis=0)
    h = full[:, :CONV_DIM] * jax.nn.sigmoid(full[:, CONV_DIM:])
    return (jax.nn.silu(_ln(_causal_conv(h, w, CONV_HALO) + b, g, be)),)


def f_gdn_pre(params, halo, pb):
    cq, ck, cv, alog, dtb = params
    full = jnp.concatenate([halo, pb], axis=0)

    def conv(cols, w):
        return jax.nn.silu(_causal_conv(cols, w, GDN_HALO))

    def l2n(t, scale):
        parts = []
        for h in range(GDN_HEADS):
            th = t[:, h * GDN_HD:(h + 1) * GDN_HD]
            parts.append(th * (lax.rsqrt(jnp.sum(th * th, -1, keepdims=True) + RMS_EPS) * scale))
        return jnp.concatenate(parts, axis=1)

    q = l2n(conv(full[:, 0:512], cq), GDN_HD ** -0.5)
    k = l2n(conv(full[:, 512:1024], ck), 1.0)
    v = conv(full[:, 1024:1536], cv)
    logits = pb[:, 1536:1536 + LANES]
    z = logits + dtb
    softplus = jnp.maximum(z, 0.0) + jnp.log(1.0 + jnp.exp(-jnp.abs(z)))
    g = -jnp.exp(alog) * softplus
    beta = jax.nn.sigmoid(logits)
    lane = _iota2(logits.shape, 1)
    gb = jnp.where(lane < GDN_HEADS, beta, jnp.where(lane < 2 * GDN_HEADS, g, 0.0))
    return q, k, v, gb


def _split2(x):
    hi = x.astype(BF16)
    return hi, (x - hi.astype(F32)).astype(BF16)


def _dg3(a, b, ca, cb):
    ah, al = _split2(a)
    bh, bl = _split2(b)
    d = lambda p, q: lax.dot_general(p, q, (((ca,), (cb,)), ((), ())), preferred_element_type=F32)
    return d(ah, bh) + (d(ah, bl) + d(al, bh))


def _inv_unit_lower(lows):
    n = lows[0].shape[0]
    eye = jnp.where(_iota2((n, n), 0) == _iota2((n, n), 1), 1.0, 0.0).astype(F32)
    ps = [-low for low in lows]
    xs = [eye + p for p in ps]
    for _ in range(5):
        ps = [_dg3(p, p, 1, 0) for p in ps]
        xs = [x + _dg3(x, p, 1, 0) for x, p in zip(xs, ps)]
    return xs


def _pair_masks():
    n = GDN_PAIR
    r, c = _iota2((n, n), 0), _iota2((n, n), 1)
    same = (r >= GDN_CHUNK) == (c >= GDN_CHUNK)
    return same, jnp.logical_and(same, r >= c), jnp.logical_and(same, r > c), jnp.logical_and(same, r <= c)


def _masked_sum(mask, g):
    mk = jnp.where(mask, 1.0, 0.0).astype(BF16)
    g1 = g.astype(BF16)
    r1 = g - g1.astype(F32)
    g2 = r1.astype(BF16)
    g3 = (r1 - g2.astype(F32)).astype(BF16)
    d = lambda q: jnp.dot(mk, q, preferred_element_type=F32)
    return d(g1) + (d(g2) + d(g3))


@jax.custom_vjp
def chunk_cumsum(g):
    return _masked_sum(_pair_masks()[1], g)


chunk_cumsum.defvjp(lambda g: (_masked_sum(_pair_masks()[1], g), None), lambda _, ct: (_masked_sum(_pair_masks()[3], ct),))


def _prep_gates(gb):
    gam_all = chunk_cumsum(gb)
    first = _iota2((GDN_PAIR, LANES), 0) < GDN_CHUNK
    glast = jnp.where(first, gam_all[GDN_CHUNK - 1:GDN_CHUNK, :], gam_all[GDN_PAIR - 1:GDN_PAIR, :])
    return gam_all, gam_all.T, glast


def _head_gates(gb, gam_all, gam_t, h, causal):
    gc = gam_all[:, GDN_HEADS + h:GDN_HEADS + h + 1]
    gr = gam_t[GDN_HEADS + h:GDN_HEADS + h + 1, :]
    return gc, gb[:, h:h + 1], jnp.exp(jnp.where(causal, gc - gr, -jnp.inf))


def f_prep_low(kn, gb):
    _, causal, strict, _ = _pair_masks()
    gam_all, gam_t, _ = _prep_gates(gb)
    lows = []
    for h in range(GDN_HEADS):
        k = kn[:, h * GDN_HD:(h + 1) * GDN_HD]
        _, beta, decay = _head_gates(gb, gam_all, gam_t, h, causal)
        lows.append(jnp.where(strict, beta * mm_nt(k, k) * decay, 0.0))
    return jnp.concatenate(lows, axis=1)


def f_prep_rest(qn, kn, vv, gb, tinv_all):
    _, causal, _, _ = _pair_masks()
    gam_all, gam_t, glast = _prep_gates(gb)
    us, ws, as_, qds, kds = [], [], [], [], []
    for h in range(GDN_HEADS):
        hs = slice(h * GDN_HD, (h + 1) * GDN_HD)
        q, k, v, tinv = qn[:, hs], kn[:, hs], vv[:, hs], tinv_all[:, hs]
        gc, beta, decay = _head_gates(gb, gam_all, gam_t, h, causal)
        eg = jnp.exp(gc)
        us.append(mm_nn(tinv, beta * v))
        ws.append(mm_nn(tinv, beta * k * eg))
        as_.append(mm_nt(q, k) * decay)
        qds.append(q * eg)
        kds.append(k * jnp.exp(glast[:, GDN_HEADS + h:GDN_HEADS + h + 1] - gc))
    cat = lambda xs: jnp.concatenate(xs, axis=1)
    return cat(us), cat(ws), cat(as_), cat(qds), cat(kds), jnp.exp(glast)


PREP_TM = 2 * GDN_PAIR


def _ride(ride, n_in, n_out, refs, n_steps):
    if ride is None:
        return
    nri, nro = len(ride["ins"]), len(ride["out_shape"])
    r_in = refs[n_in:n_in + nri]
    r_out = refs[n_in + nri + n_out:n_in + nri + n_out + nro]
    sems = refs[len(refs) - len(ride["scratch"]):]
    step = pl.program_id(0)

    @pl.when(step == 0)
    def _():
        ride["start"](r_in, r_out, sems)

    @pl.when(step == (2 * n_steps) // 3)
    def _():
        ride["middle"](r_in, r_out, sems)

    @pl.when(step == n_steps - 1)
    def _():
        ride["finish"](r_in, r_out, sems)


def _ride_args(ride):
    if ride is None:
        return [], [], [], [], []
    n = len(ride["ins"])
    return list(ride["ins"]), [ANY_SPEC] * n, [ANY_SPEC] * len(ride["out_shape"]), list(ride["out_shape"]), list(ride["scratch"])


def prep_fwd(name, qn, kn, vv, gb, ride=None):
    t = qn.shape[0]
    hw = GDN_HEADS * GDN_HD
    n_steps = t // PREP_TM
    r_ins, r_in_specs, r_out_specs, r_out_shape, r_scratch = _ride_args(ride)

    def body(*refs):
        q_ref, k_ref, v_ref, gb_ref = refs[:4]
        outs = refs[4 + len(r_ins):4 + len(r_ins) + 7]
        pairs = [pl.ds(pi * GDN_PAIR, GDN_PAIR) for pi in range(PREP_TM // GDN_PAIR)]
        lows = [f_prep_low(k_ref[rs, :], gb_ref[rs, :]) for rs in pairs]
        invs = _inv_unit_lower([low[:, h * GDN_HD:(h + 1) * GDN_HD] for low in lows for h in range(GDN_HEADS)])
        for pi, rs in enumerate(pairs):
            tinv = jnp.concatenate(invs[pi * GDN_HEADS:(pi + 1) * GDN_HEADS], axis=1)
            res = f_prep_rest(q_ref[rs, :], k_ref[rs, :], v_ref[rs, :], gb_ref[rs, :], tinv)
            for o_ref, val in zip(outs, res + (tinv,)):
                o_ref[rs, :] = val
        _ride(ride, 4, 7, refs, n_steps)

    widths = [hw, hw, hw, hw, hw, LANES, hw]
    return pl.pallas_call(
        body, name=name, grid=(n_steps,),
        in_specs=[_row_spec(PREP_TM, hw)] * 3 + [_row_spec(PREP_TM, LANES)] + r_in_specs,
        out_specs=[_row_spec(PREP_TM, wd) for wd in widths] + r_out_specs,
        out_shape=[jax.ShapeDtypeStruct((t, wd), F32) for wd in widths] + r_out_shape,
        scratch_shapes=r_scratch, compiler_params=_cparams(),
    )(qn, kn, vv, gb, *r_ins)


def prep_bwd(name, qn, kn, vv, gb, tinv, cts):
    t = qn.shape[0]
    hw = GDN_HEADS * GDN_HD

    def body(q_ref, k_ref, v_ref, gb_ref, x_ref, *refs):
        outs = refs[6:]
        for pi in range(PREP_TM // GDN_PAIR):
            rs = pl.ds(pi * GDN_PAIR, GDN_PAIR)
            ct = tuple(r[rs, :] for r in refs[:6])
            kn_v, gb_v, x_all = k_ref[rs, :], gb_ref[rs, :], x_ref[rs, :]
            _, vjp_low = jax.vjp(f_prep_low, kn_v, gb_v)
            _, vjp_rest = jax.vjp(f_prep_rest, q_ref[rs, :], kn_v, v_ref[rs, :], gb_v, x_all)
            dq, dk, dv, dgb, dx_all = vjp_rest(ct)
            dlows = []
            for h in range(GDN_HEADS):
                hs = slice(h * GDN_HD, (h + 1) * GDN_HD)
                dlows.append(-_dg3(_dg3(x_all[:, hs], dx_all[:, hs], 0, 0), x_all[:, hs], 1, 1))
            dk2, dgb2 = vjp_low(jnp.concatenate(dlows, axis=1))
            outs[0][rs, :] = dq
            outs[1][rs, :] = dk + dk2
            outs[2][rs, :] = dv
            outs[3][rs, :] = dgb + dgb2

    in_w = [hw, hw, hw, LANES, hw] + [hw, hw, hw, hw, hw, LANES]
    out_w = [hw, hw, hw, LANES]
    return pl.pallas_call(
        body, name=name, grid=(t // PREP_TM,),
        in_specs=[_row_spec(PREP_TM, wd) for wd in in_w], out_specs=[_row_spec(PREP_TM, wd) for wd in out_w],
        out_shape=[jax.ShapeDtypeStruct((t, wd), F32) for wd in out_w], compiler_params=_cparams(),
    )(qn, kn, vv, gb, tinv, *cts)


def f_gdn_scan(params, state, u, w, a, qd, kd, egl, z):
    ng = params[0]
    tm = u.shape[0]
    st = [state[h * GDN_HD:(h + 1) * GDN_HD, :] for h in range(GDN_HEADS)]
    zeros = jnp.zeros((GDN_CHUNK, GDN_HD), F32)
    rows = []
    hss = [slice(h * GDN_HD, (h + 1) * GDN_HD) for h in range(GDN_HEADS)]
    for ci in range(tm // GDN_CHUNK):
        rs = slice(ci * GDN_CHUNK, (ci + 1) * GDN_CHUNK)
        vns = [u[rs, hs] - mm_nn(w[rs, hs], s) for hs, s in zip(hss, st)]
        qs = [mm_nn(qd[rs, hs], s) for hs, s in zip(hss, st)]
        upd = [mm_tn(kd[rs, hs], vn) for hs, vn in zip(hss, vns)]
        es = [egl[ci * GDN_CHUNK:ci * GDN_CHUNK + 1, GDN_HEADS + h:GDN_HEADS + h + 1] for h in range(GDN_HEADS)]
        st = [s * e + d for s, e, d in zip(st, es, upd)]
        vpads = [jnp.concatenate([vn, zeros] if ci % 2 == 0 else [zeros, vn], axis=0) for vn in vns]
        os_ = [q + mm_nn(a[rs, hs], vp) for q, hs, vp in zip(qs, hss, vpads)]
        heads = [o * lax.rsqrt(jnp.mean(o * o, -1, keepdims=True) + RMS_EPS) * ng * jax.nn.silu(z[rs, hs])
                 for o, hs in zip(os_, hss)]
        rows.append(jnp.concatenate(heads, axis=1))
    return jnp.concatenate(rows, axis=0), jnp.concatenate(st, axis=0)


def f_sgu(params, puv):
    g, b, ws, bst = params
    tm = puv.shape[0]
    uv = jax.nn.gelu(puv)
    u = uv[:, :512]
    v = _ln(uv[:, 512:], g, b)
    r, c = _iota2((SGU_CHUNK, SGU_CHUNK), 0), _iota2((SGU_CHUNK, SGU_CHUNK), 1)
    rows = []
    for ci in range(tm // SGU_CHUNK):
        rs = slice(ci * SGU_CHUNK, (ci + 1) * SGU_CHUNK)
        groups = []
        for gi in range(SGU_GROUPS):
            gs = slice(gi * SGU_GD, (gi + 1) * SGU_GD)
            wt = jnp.where(r >= c, ws[gs, :], 0.0)
            groups.append(mm_nn(wt, v[rs, gs]) + bst[:, gi:gi + 1])
        rows.append(jnp.concatenate(groups, axis=1))
    return (u * jnp.concatenate(rows, axis=0),)


def f_mix(params, ya, yb, yc, pg):
    bg = params[0]
    s = jax.nn.sigmoid(pg + bg)
    return (s[:, :D_MODEL] * ya + s[:, D_MODEL:2 * D_MODEL] * yb + s[:, 2 * D_MODEL:] * yc,)


def f_ln1(params, xin, m):
    g, b = params
    return (_ln(ALPHA * xin + m, g, b),)


def f_ln2(params, x1, f2):
    b2, g, b = params
    return (_ln(ALPHA * x1 + (f2 + b2), g, b),)


def _cparams(n_axes=1):
    return pltpu.CompilerParams(dimension_semantics=("arbitrary",) * n_axes, vmem_limit_bytes=VMEM_LIMIT)


def _row_spec(tm, width, n_tiles=None):
    if n_tiles is None:
        return pl.BlockSpec((tm, width), lambda i: (i, 0))
    return pl.BlockSpec((tm, width), lambda i: (n_tiles - 1 - i, 0))


def _full_spec(shape):
    return pl.BlockSpec(shape, lambda i: (0,) * len(shape))


def stage_fwd(name, fn, tm, rows, params, outs, ride=None):
    t = rows[0].shape[0]
    nr, npar = len(rows), len(params)
    widths = jax.eval_shape(lambda p, r: fn(p, *r), [jax.ShapeDtypeStruct(p.shape, F32) for p in params],
                            [jax.ShapeDtypeStruct((tm, r.shape[1]), F32) for r in rows])
    r_ins, r_in_specs, r_out_specs, r_out_shape, r_scratch = _ride_args(ride)
    n_out = sum(len(dts) for dts in outs)

    def body(*refs):
        r = [x[...].astype(F32) for x in refs[:nr]]
        p = [x[...] for x in refs[nr:nr + npar]]
        res = fn(p, *r)
        k = nr + npar + len(r_ins)
        for o, dts in zip(res, outs):
            for dt in dts:
                refs[k][...] = o.astype(dt)
                k += 1
        _ride(ride, nr + npar, n_out, refs, t // tm)

    out_shape, out_specs = [], []
    for wd, dts in zip(widths, outs):
        for dt in dts:
            out_shape.append(jax.ShapeDtypeStruct((t, wd.shape[1]), dt))
            out_specs.append(_row_spec(tm, wd.shape[1]))
    return pl.pallas_call(
        body, name=name, grid=(t // tm,),
        in_specs=[_row_spec(tm, r.shape[1]) for r in rows] + [_full_spec(p.shape) for p in params] + r_in_specs,
        out_specs=out_specs + r_out_specs, out_shape=out_shape + r_out_shape, scratch_shapes=r_scratch,
        compiler_params=_cparams(),
    )(*rows, *params, *r_ins)


def stage_bwd(name, fn, tm, rows, params, cts, drow_dtypes, ride=None):
    t = rows[0].shape[0]
    nr, npar = len(rows), len(params)
    flat_cts = [c for cl in cts if cl is not None for c in cl]
    nct = len(flat_cts)
    want = [i for i, dt in enumerate(drow_dtypes) if dt is not None]
    r_ins, r_in_specs, r_out_specs, r_out_shape, r_scratch = _ride_args(ride)

    def body(*refs):
        r = [x[...].astype(F32) for x in refs[:nr]]
        p = [x[...] for x in refs[nr:nr + npar]]
        res, vjp = jax.vjp(lambda pp, rr: fn(pp, *rr), p, r)
        k = nr + npar
        ct = []
        for o, cl in zip(res, cts):
            if cl is None:
                ct.append(jnp.zeros_like(o))
            else:
                acc = refs[k][...].astype(F32)
                for j in range(1, len(cl)):
                    acc = acc + refs[k + j][...].astype(F32)
                k += len(cl)
                ct.append(acc)
        dp, dr = vjp(tuple(ct))
        k += len(r_ins)
        for i in want:
            refs[k][...] = dr[i].astype(drow_dtypes[i])
            k += 1

        @pl.when(pl.program_id(0) == 0)
        def _():
            for j in range(npar):
                refs[k + j][...] = jnp.zeros_like(refs[k + j])

        for j in range(npar):
            refs[k + j][...] += dp[j]
        _ride(ride, nr + npar + nct, len(want) + npar, refs, t // tm)

    out_shape = [jax.ShapeDtypeStruct(rows[i].shape, drow_dtypes[i]) for i in want]
    out_specs = [_row_spec(tm, rows[i].shape[1]) for i in want]
    out_shape += [jax.ShapeDtypeStruct(p.shape, F32) for p in params]
    out_specs += [_full_spec(p.shape) for p in params]
    return pl.pallas_call(
        body, name=name, grid=(t // tm,),
        in_specs=[_row_spec(tm, r.shape[1]) for r in rows] + [_full_spec(p.shape) for p in params]
        + [_row_spec(tm, c.shape[1]) for c in flat_cts] + r_in_specs,
        out_specs=out_specs + r_out_specs, out_shape=out_shape + r_out_shape, scratch_shapes=r_scratch,
        compiler_params=_cparams(),
    )(*rows, *params, *flat_cts, *r_ins)


def _halo_spec(tm, halo, width, n_tiles=None):
    per = tm // halo
    if n_tiles is None:
        return pl.BlockSpec((halo, width), lambda i: (jnp.maximum(i * per - 1, 0), 0))
    return pl.BlockSpec((halo, width), lambda i: (jnp.maximum((n_tiles - 1 - i) * per - 1, 0), 0))


def halo_fwd(name, fn, tm, halo, src, params, outs, ride=None):
    t, width = src.shape
    npar = len(params)
    widths = jax.eval_shape(lambda p, h, r: fn(p, h, r), [jax.ShapeDtypeStruct(p.shape, F32) for p in params],
                            jax.ShapeDtypeStruct((halo, width), F32), jax.ShapeDtypeStruct((tm, width), F32))
    r_ins, r_in_specs, r_out_specs, r_out_shape, r_scratch = _ride_args(ride)
    n_out = sum(len(dts) for dts in outs)

    def body(h_ref, x_ref, *refs):
        hal = jnp.where(pl.program_id(0) == 0, 0.0, h_ref[...].astype(F32))
        res = fn([x[...] for x in refs[:npar]], hal, x_ref[...].astype(F32))
        k = npar + len(r_ins)
        for o, dts in zip(res, outs):
            for dt in dts:
                refs[k][...] = o.astype(dt)
                k += 1
        _ride(ride, 2 + npar, n_out, (h_ref, x_ref) + refs, t // tm)

    out_shape, out_specs = [], []
    for wd, dts in zip(widths, outs):
        for dt in dts:
            out_shape.append(jax.ShapeDtypeStruct((t, wd.shape[1]), dt))
            out_specs.append(_row_spec(tm, wd.shape[1]))
    return pl.pallas_call(
        body, name=name, grid=(t // tm,),
        in_specs=[_halo_spec(tm, halo, width), _row_spec(tm, width)] + [_full_spec(p.shape) for p in params] + r_in_specs,
        out_specs=out_specs + r_out_specs, out_shape=out_shape + r_out_shape, scratch_shapes=r_scratch,
        compiler_params=_cparams(),
    )(src, src, *params, *r_ins)


def halo_bwd(name, fn, tm, halo, src, params, cts, dsrc_dtype, ride=None):
    t, width = src.shape
    n_tiles = t // tm
    npar = len(params)
    n_in = 2 + npar + len(cts)
    r_ins, r_in_specs, r_out_specs, r_out_shape, r_scratch = _ride_args(ride)

    def body(h_ref, x_ref, *refs):
        carry = refs[n_in - 2 + len(r_ins) + 1 + npar + len(r_out_shape)]
        first_tile = pl.program_id(0) == n_tiles - 1
        hal = jnp.where(first_tile, 0.0, h_ref[...].astype(F32))
        p = [x[...] for x in refs[:npar]]
        res, vjp = jax.vjp(fn, p, hal, x_ref[...].astype(F32))
        k = npar
        ct = tuple(refs[k + j][...].astype(F32) for j in range(len(cts)))
        k += len(cts) + len(r_ins)
        dp, dh, dx = vjp(ct)

        @pl.when(pl.program_id(0) == 0)
        def _():
            carry[...] = jnp.zeros_like(carry)
            for j in range(npar):
                refs[k + 1 + j][...] = jnp.zeros_like(refs[k + 1 + j])

        dx = jnp.concatenate([dx[:tm - halo, :], dx[tm - halo:, :] + carry[...]], axis=0)
        refs[k][...] = dx.astype(dsrc_dtype)
        carry[...] = dh
        for j in range(npar):
            refs[k + 1 + j][...] += dp[j]
        _ride(ride, n_in, 1 + npar, (h_ref, x_ref) + refs, n_tiles)

    out_shape = [jax.ShapeDtypeStruct((t, width), dsrc_dtype)] + [jax.ShapeDtypeStruct(p.shape, F32) for p in params]
    out_specs = [_row_spec(tm, width, n_tiles)] + [_full_spec(p.shape) for p in params]
    return pl.pallas_call(
        body, name=name, grid=(n_tiles,),
        in_specs=[_halo_spec(tm, halo, width, n_tiles), _row_spec(tm, width, n_tiles)] + [_full_spec(p.shape) for p in params]
        + [_row_spec(tm, c.shape[1], n_tiles) for c in cts] + r_in_specs,
        out_specs=out_specs + r_out_specs, out_shape=out_shape + r_out_shape,
        scratch_shapes=[pltpu.VMEM((halo, width), F32)] + r_scratch, compiler_params=_cparams(),
    )(src, src, *params, *cts, *r_ins)


SCAN_TM = 512


def scan_fwd(name, norm_g, u, w, a, qd, kd, egl, z, ride=None):
    t = u.shape[0]
    n_tiles = t // SCAN_TM
    srows = GDN_HEADS * GDN_HD

    r_ins, r_in_specs, r_out_specs, r_out_shape, r_scratch = _ride_args(ride)

    def body(*refs):
        ng_ref, u_ref, w_ref, a_ref, qd_ref, kd_ref, e_ref, z_ref = refs[:8]
        o_ref, s_ref = refs[8 + len(r_ins):10 + len(r_ins)]
        state = refs[10 + len(r_ins) + len(r_out_shape)]

        @pl.when(pl.program_id(0) == 0)
        def _():
            state[...] = jnp.zeros_like(state)

        s_ref[0] = state[...]
        o, s_new = f_gdn_scan([ng_ref[...]], state[...], u_ref[...], w_ref[...], a_ref[...], qd_ref[...], kd_ref[...],
                              e_ref[...], z_ref[...].astype(F32))
        o_ref[...] = o.astype(BF16)
        state[...] = s_new
        _ride(ride, 8, 2, refs, n_tiles)

    rows = [u, w, a, qd, kd, egl, z]
    return pl.pallas_call(
        body, name=name, grid=(n_tiles,),
        in_specs=[_full_spec(norm_g.shape)] + [_row_spec(SCAN_TM, r.shape[1]) for r in rows] + r_in_specs,
        out_specs=[_row_spec(SCAN_TM, 512), pl.BlockSpec((1, srows, GDN_HD), lambda i: (i, 0, 0))] + r_out_specs,
        out_shape=[jax.ShapeDtypeStruct((t, 512), BF16), jax.ShapeDtypeStruct((n_tiles, srows, GDN_HD), F32)] + r_out_shape,
        scratch_shapes=[pltpu.VMEM((srows, GDN_HD), F32)] + r_scratch, compiler_params=_cparams(),
    )(norm_g, *rows, *r_ins)


def scan_bwd(name, norm_g, states, u, w, a, qd, kd, egl, z, dout, ride=None):
    t = u.shape[0]
    n_tiles = t // SCAN_TM
    srows = GDN_HEADS * GDN_HD
    r_ins, r_in_specs, r_out_specs, r_out_shape, r_scratch = _ride_args(ride)

    def body(*refs):
        ng_ref, s_ref, u_ref, w_ref, a_ref, qd_ref, kd_ref, e_ref, z_ref, do_ref = refs[:10]
        du_ref, dw_ref, da_ref, dqd_ref, dkd_ref, de_ref, dz_ref, dng_ref = refs[10 + len(r_ins):18 + len(r_ins)]
        dstate = refs[18 + len(r_ins) + len(r_out_shape)]

        @pl.when(pl.program_id(0) == 0)
        def _():
            dstate[...] = jnp.zeros_like(dstate)
            dng_ref[...] = jnp.zeros_like(dng_ref)

        args = ([ng_ref[...]], s_ref[0], u_ref[...], w_ref[...], a_ref[...], qd_ref[...], kd_ref[...], e_ref[...],
                z_ref[...].astype(F32))
        _, vjp = jax.vjp(f_gdn_scan, *args)
        dp, ds, du, dw, da, dqd, dkd, de, dz = vjp((do_ref[...].astype(F32), dstate[...]))
        du_ref[...] = du
        dw_ref[...] = dw
        da_ref[...] = da
        dqd_ref[...] = dqd
        dkd_ref[...] = dkd
        de_ref[...] = de
        dz_ref[...] = dz.astype(BF16)
        dng_ref[...] += dp[0]
        dstate[...] = ds
        _ride(ride, 10, 8, refs, n_tiles)

    rows = [u, w, a, qd, kd, egl, z, dout]
    return pl.pallas_call(
        body, name=name, grid=(n_tiles,),
        in_specs=[_full_spec(norm_g.shape), pl.BlockSpec((1, srows, GDN_HD), lambda i: (n_tiles - 1 - i, 0, 0))]
        + [_row_spec(SCAN_TM, r.shape[1], n_tiles) for r in rows] + r_in_specs,
        out_specs=[_row_spec(SCAN_TM, r.shape[1], n_tiles) for r in rows[:7]] + [_full_spec(norm_g.shape)] + r_out_specs,
        out_shape=[jax.ShapeDtypeStruct(r.shape, F32) for r in rows[:6]] + [jax.ShapeDtypeStruct(z.shape, BF16)]
        + [jax.ShapeDtypeStruct(norm_g.shape, F32)] + r_out_shape,
        scratch_shapes=[pltpu.VMEM((srows, GDN_HD), F32)] + r_scratch, compiler_params=_cparams(),
    )(norm_g, states, *rows, *r_ins)


def loss_stage(name, y, target, tm):
    t, d = y.shape

    def body(y_ref, t_ref, l_ref, dy_ref):
        @pl.when(pl.program_id(0) == 0)
        def _():
            l_ref[...] = jnp.zeros_like(l_ref)

        e = y_ref[...] - t_ref[...]
        dy_ref[...] = e * (1.0 / d)
        sq = e * e
        part = sq[:, 0:LANES]
        for j in range(1, d // LANES):
            part = part + sq[:, j * LANES:(j + 1) * LANES]
        acc = part[0:8, :]
        for j in range(1, tm // 8):
            acc = acc + part[j * 8:(j + 1) * 8, :]
        l_ref[...] += acc * (0.5 / d)

    return pl.pallas_call(
        body, name=name, grid=(t // tm,),
        in_specs=[_row_spec(tm, d), _row_spec(tm, d)],
        out_specs=[_full_spec((8, LANES)), _row_spec(tm, d)],
        out_shape=[jax.ShapeDtypeStruct((8, LANES), F32), jax.ShapeDtypeStruct((t, d), F32)],
        compiler_params=_cparams(),
    )(y, target)


def _pick(n, cands):
    for c in cands:
        if n % c == 0:
            return c
    return n


def matmul(name, a, b, form, out_dtype, fuse=None):
    if form == "nn":
        (m, k), n = a.shape, b.shape[1]
    elif form == "nt":
        (m, k), n = a.shape, b.shape[0]
    else:
        (k, m), n = a.shape, b.shape[1]
    if form == "tn":
        tm = _pick(m, (1024, 512, 256, 128))
        tn = n if n <= 2048 else _pick(n, (1536, 1024, 512, 256, 128))
        tk = _pick(k, (2048, 1024, 512, 256, 128))
    else:
        tm = _pick(m, (512, 256, 128))
        tk = k if k <= 4096 else _pick(k, (2048, 1024, 512))
        cap = 2048 * 2048 if fuse is None else 2048 * 1024
        tn = n if n * tk <= cap else _pick(n, (2048, 1536, 1024, 512, 256, 128))
    nk = k // tk
    ca, cb = {"nn": (1, 0), "nt": (1, 1), "tn": (0, 0)}[form]
    a_spec = (pl.BlockSpec((tm, tk), lambda j, i, kk: (i, kk)) if form != "tn"
              else pl.BlockSpec((tk, tm), lambda j, i, kk: (kk, i)))
    b_spec = (pl.BlockSpec((tn, tk), lambda j, i, kk: (j, kk)) if form == "nt"
              else pl.BlockSpec((tk, tn), lambda j, i, kk: (kk, j)))
    o_spec = pl.BlockSpec((tm, tn), lambda j, i, kk: (i, j))
    col_spec = pl.BlockSpec((1, tn), lambda j, i, kk: (0, j))

    assert nk == 1 or (out_dtype == F32 and fuse is None)

    def body(*refs):
        a_ref, b_ref = refs[0], refs[1]
        prod = lax.dot_general(a_ref[...], b_ref[...], (((ca,), (cb,)), ((), ())), preferred_element_type=F32)
        if fuse is not None and fuse[0] == "relu2":
            pre = prod + refs[2][...]
            refs[3][...] = jnp.square(jnp.maximum(pre, 0.0)).astype(BF16)
            refs[4][...] = pre.astype(BF16)
            return
        if fuse is not None and fuse[0] == "relu2_bwd":
            d = prod * (2.0 * jnp.maximum(refs[2][...].astype(F32), 0.0))
            refs[3][...] = d.astype(BF16)

            @pl.when(pl.program_id(1) == 0)
            def _():
                refs[4][...] = jnp.zeros_like(refs[4])

            refs[4][...] += jnp.sum(d, axis=0, keepdims=True)
            return
        o_ref = refs[-1]
        if nk == 1:
            o_ref[...] = prod.astype(out_dtype)
            return
        kk = pl.program_id(2)

        @pl.when(kk == 0)
        def _():
            o_ref[...] = prod

        @pl.when(kk > 0)
        def _():
            o_ref[...] += prod

    ins = [a.astype(BF16), b.astype(BF16)]
    in_specs = [a_spec, b_spec]
    out_specs, out_shape = o_spec, jax.ShapeDtypeStruct((m, n), out_dtype)
    if fuse is not None and fuse[0] == "relu2":
        ins, in_specs = ins + [fuse[1]], in_specs + [col_spec]
        out_specs, out_shape = [o_spec, o_spec], [jax.ShapeDtypeStruct((m, n), BF16)] * 2
    elif fuse is not None:
        ins, in_specs = ins + [fuse[1]], in_specs + [o_spec]
        out_specs = [o_spec, col_spec]
        out_shape = [jax.ShapeDtypeStruct((m, n), BF16), jax.ShapeDtypeStruct((1, n), F32)]
    return pl.pallas_call(
        body, name=name, grid=(n // tn, m // tm, nk),
        in_specs=in_specs, out_specs=out_specs, out_shape=out_shape,
        compiler_params=_cparams(3),
    )(*ins)


SUM_NT_TM = 256


def matmul_multi_nn(name, a, bs, out_dtype):
    m, k = a.shape
    nb = len(bs)

    def body(*refs):
        x = refs[0][...]
        for r in range(nb):
            refs[1 + nb + r][...] = jnp.dot(x, refs[1 + r][...], preferred_element_type=F32).astype(out_dtype)

    return pl.pallas_call(
        body, name=name, grid=(m // SUM_NT_TM,),
        in_specs=[pl.BlockSpec((SUM_NT_TM, k), lambda i: (i, 0))] + [_full_spec(b.shape) for b in bs],
        out_specs=[pl.BlockSpec((SUM_NT_TM, b.shape[1]), lambda i: (i, 0)) for b in bs],
        out_shape=[jax.ShapeDtypeStruct((m, b.shape[1]), out_dtype) for b in bs], compiler_params=_cparams(),
    )(a.astype(BF16), *[b.astype(BF16) for b in bs])


def matmul_sum_nt(name, pairs, out_dtype, ride=None):
    m, n = pairs[0][0].shape[0], pairs[0][1].shape[0]
    npair = len(pairs)
    r_ins, r_in_specs, r_out_specs, r_out_shape, r_scratch = _ride_args(ride)

    def body(*refs):
        acc = None
        for r in range(npair):
            prod = lax.dot_general(refs[2 * r][...], refs[2 * r + 1][...], (((1,), (1,)), ((), ())), preferred_element_type=F32)
            acc = prod if acc is None else acc + prod
        refs[2 * npair + len(r_ins)][...] = acc.astype(out_dtype)
        _ride(ride, 2 * npair, 1, refs, m // SUM_NT_TM)

    ins, in_specs = [], []
    for a, b in pairs:
        ins += [a.astype(BF16), b.astype(BF16)]
        in_specs += [pl.BlockSpec((SUM_NT_TM, a.shape[1]), lambda i: (i, 0)), _full_spec(b.shape)]
    return pl.pallas_call(
        body, name=name, grid=(m // SUM_NT_TM,), in_specs=in_specs + r_in_specs,
        out_specs=[pl.BlockSpec((SUM_NT_TM, n), lambda i: (i, 0))] + r_out_specs,
        out_shape=[jax.ShapeDtypeStruct((m, n), out_dtype)] + r_out_shape, scratch_shapes=r_scratch,
        compiler_params=_cparams(),
    )(*ins, *r_ins)


MESH_ID = pl.DeviceIdType.MESH
ANY_SPEC = pl.BlockSpec(memory_space=pl.ANY)


def _place():
    return lax.axis_index("x"), lax.axis_index("y"), lax.axis_index("c")


def _chips():
    x, y, c = _place()
    return x, y, c, 2 * x + y, [(1 - x, y), (x, 1 - y), (1 - x, 1 - y)]


def scatter_rider(arrs):
    n = len(arrs)

    def remote(ins, outs, sems, i, j, landing_from_me):
        x, y, c, me, others = _chips()
        ox, oy = others[j]
        peer = 2 * ox + oy
        return pltpu.make_async_remote_copy(src_ref=ins[i].at[peer], dst_ref=outs[i].at[me if landing_from_me else peer],
                                            send_sem=sems[0].at[3 * i + j], recv_sem=sems[1].at[3 * i + j],
                                            device_id=(ox, oy, c), device_id_type=MESH_ID)

    def local(ins, outs, sems, i):
        me = _chips()[3]
        return pltpu.make_async_copy(ins[i].at[me], outs[i].at[me], sems[2].at[i])

    def start(ins, outs, sems):
        for i in range(n):
            local(ins, outs, sems, i).start()
            for j in range(3):
                remote(ins, outs, sems, i, j, True).start()

    def finish(ins, outs, sems):
        for i in range(n):
            for j in range(3):
                remote(ins, outs, sems, i, j, False).wait_recv()
        for i in range(n):
            for j in range(3):
                remote(ins, outs, sems, i, j, True).wait_send()
            local(ins, outs, sems, i).wait()

    return dict(ins=list(arrs), out_shape=[jax.ShapeDtypeStruct(a.shape, a.dtype) for a in arrs],
                scratch=[pltpu.SemaphoreType.DMA((3 * n,)), pltpu.SemaphoreType.DMA((3 * n,)), pltpu.SemaphoreType.DMA((n,))],
                start=start, middle=lambda ins, outs, sems: None, finish=finish)


def gather_rider(split, flat):
    ns, n = len(split), len(split) + len(flat)
    arrs = list(split) + list(flat)
    half = [a.shape[0] // 2 for a in split]

    def ici(ins, outs, sems, i, j, landing_from_me):
        x, y, c, me, others = _chips()
        ox, oy = others[j]
        slot = me if landing_from_me else 2 * ox + oy
        if i < ns:
            rows = pl.ds(c * half[i], half[i])
            src, dst = ins[i].at[rows], outs[i].at[slot, rows]
        else:
            src, dst = ins[i], outs[i].at[slot]
        return pltpu.make_async_remote_copy(src_ref=src, dst_ref=dst, send_sem=sems[0].at[3 * i + j],
                                            recv_sem=sems[1].at[3 * i + j], device_id=(ox, oy, c), device_id_type=MESH_ID)

    def d2d(outs, sems, i, j, mine):
        x, y, c, me, others = _chips()
        ox, oy = others[j]
        ref = outs[i].at[2 * ox + oy, pl.ds((c if mine else 1 - c) * half[i], half[i])]
        return pltpu.make_async_remote_copy(src_ref=ref, dst_ref=ref, send_sem=sems[2].at[3 * i + j],
                                            recv_sem=sems[3].at[3 * i + j], device_id=(x, y, 1 - c), device_id_type=MESH_ID)

    def local(ins, outs, sems, i):
        return pltpu.make_async_copy(ins[i], outs[i].at[_chips()[3]], sems[4].at[i])

    def start(ins, outs, sems):
        for i in range(n):
            local(ins, outs, sems, i).start()
            for j in range(3):
                ici(ins, outs, sems, i, j, True).start()

    def middle(ins, outs, sems):
        for i in range(n):
            for j in range(3):
                ici(ins, outs, sems, i, j, False).wait_recv()
                if i < ns:
                    d2d(outs, sems, i, j, True).start()

    def finish(ins, outs, sems):
        for i in range(ns):
            for j in range(3):
                d2d(outs, sems, i, j, False).wait_recv()
        for i in range(n):
            for j in range(3):
                ici(ins, outs, sems, i, j, True).wait_send()
                if i < ns:
                    d2d(outs, sems, i, j, True).wait_send()
            local(ins, outs, sems, i).wait()

    return dict(ins=arrs, out_shape=[jax.ShapeDtypeStruct((N_CHIPS,) + a.shape, a.dtype) for a in arrs],
                scratch=[pltpu.SemaphoreType.DMA((3 * n,)), pltpu.SemaphoreType.DMA((3 * n,)),
                         pltpu.SemaphoreType.DMA((3 * max(ns, 1),)), pltpu.SemaphoreType.DMA((3 * max(ns, 1),)),
                         pltpu.SemaphoreType.DMA((n,))],
                start=start, middle=middle, finish=finish)


def run_alone(name, ride):
    n_in, n_out = len(ride["ins"]), len(ride["out_shape"])

    def body(*refs):
        parts = refs[:n_in], refs[n_in:n_in + n_out], refs[n_in + n_out:]
        ride["start"](*parts)
        ride["middle"](*parts)
        ride["finish"](*parts)

    return pl.pallas_call(body, name=name, in_specs=[ANY_SPEC] * n_in, out_specs=[ANY_SPEC] * n_out,
                          out_shape=ride["out_shape"], scratch_shapes=ride["scratch"])(*ride["ins"])


def sibling_exchange(name, arrs):
    n = len(arrs)

    def body(*refs):
        ins, outs = refs[:n], refs[n:2 * n]
        send_sems, recv_sems = refs[2 * n:]
        x, y, c = _place()
        cps = [pltpu.make_async_remote_copy(src_ref=ins[i], dst_ref=outs[i], send_sem=send_sems.at[i], recv_sem=recv_sems.at[i],
                                            device_id=(x, y, 1 - c), device_id_type=MESH_ID) for i in range(n)]
        for cp in cps:
            cp.start()
        for cp in cps:
            cp.wait_recv()
        for cp in cps:
            cp.wait_send()

    return pl.pallas_call(
        body, name=name, in_specs=[ANY_SPEC] * n, out_specs=[ANY_SPEC] * n,
        out_shape=[jax.ShapeDtypeStruct(a.shape, a.dtype) for a in arrs],
        scratch_shapes=[pltpu.SemaphoreType.DMA((n,)), pltpu.SemaphoreType.DMA((n,))],
    )(*arrs)


def gather_all_rider(vec):
    def remote(ins, outs, sems, mask, landing_from_me):
        x, y, c = _place()
        px, py, pc = (1 - x if mask & 4 else x, 1 - y if mask & 2 else y, 1 - c if mask & 1 else c)
        slot = 4 * x + 2 * y + c if landing_from_me else 4 * px + 2 * py + pc
        return pltpu.make_async_remote_copy(src_ref=ins[0], dst_ref=outs[0].at[slot], send_sem=sems[0].at[mask - 1],
                                            recv_sem=sems[1].at[mask - 1], device_id=(px, py, pc), device_id_type=MESH_ID)

    def local(ins, outs, sems):
        x, y, c = _place()
        return pltpu.make_async_copy(ins[0], outs[0].at[4 * x + 2 * y + c], sems[2].at[0])

    def start(ins, outs, sems):
        local(ins, outs, sems).start()
        for mask in range(1, N_DEV):
            remote(ins, outs, sems, mask, True).start()

    def finish(ins, outs, sems):
        for mask in range(1, N_DEV):
            remote(ins, outs, sems, mask, False).wait_recv()
        for mask in range(1, N_DEV):
            remote(ins, outs, sems, mask, True).wait_send()
        local(ins, outs, sems).wait()

    return dict(ins=[vec], out_shape=[jax.ShapeDtypeStruct((N_DEV,) + vec.shape, vec.dtype)],
                scratch=[pltpu.SemaphoreType.DMA((N_DEV - 1,)), pltpu.SemaphoreType.DMA((N_DEV - 1,)), pltpu.SemaphoreType.DMA((1,))],
                start=start, middle=lambda ins, outs, sems: None, finish=finish)


def sum_slots(name, arrs, tr):
    k, r, c = arrs[0].shape
    nb = r // tr

    def body(*refs):
        o_ref = refs[-1]
        for li, a_ref in enumerate(refs[:-1]):
            @pl.when(pl.program_id(0) == li)
            def _(a_ref=a_ref):
                acc = a_ref[0].astype(F32)
                for j in range(1, k):
                    acc = acc + a_ref[j].astype(F32)
                o_ref[...] = acc

    def in_spec(li):
        return pl.BlockSpec((k, tr, c), lambda l, i: (0, jnp.where(l == li, i, jnp.where(l < li, 0, nb - 1)), 0))

    return pl.pallas_call(
        body, name=name, grid=(len(arrs), nb),
        in_specs=[in_spec(li) for li in range(len(arrs))], out_specs=pl.BlockSpec((tr, c), lambda l, i: (l * nb + i, 0)),
        out_shape=jax.ShapeDtypeStruct((len(arrs) * r, c), F32), compiler_params=_cparams(2),
    )(*arrs)


def adamw(name, parts, w, m, v, tr, ride=None):
    r, c = w.shape
    npart = len(parts)
    bc1 = 1.0 - ADAM_B1 ** ADAM_STEP
    bc2 = 1.0 - ADAM_B2 ** ADAM_STEP
    r_ins, r_in_specs, r_out_specs, r_out_shape, r_scratch = _ride_args(ride)

    def body(*refs):
        _ride(ride, npart + 3, 4, refs, r // tr)
        g = refs[0][...]
        for j in range(1, npart):
            g = g + refs[j][...]
        w_ref, m_ref, v_ref = refs[npart:npart + 3]
        g_out, d_out, m_out, v_out = refs[npart + 3 + len(r_ins):npart + 7 + len(r_ins)]
        m_new = ADAM_B1 * m_ref[...] + (1.0 - ADAM_B1) * g
        v_new = ADAM_B2 * v_ref[...] + (1.0 - ADAM_B2) * jnp.square(g)
        m_hat = m_new / bc1
        v_hat = v_new / bc2
        g_out[...] = g
        d_out[...] = -ADAM_LR * (m_hat / (jnp.sqrt(v_hat) + ADAM_EPS) + ADAM_WD * w_ref[...])
        m_out[...] = m_new
        v_out[...] = v_new

    spec = pl.BlockSpec((tr, c), lambda i: (i, 0))
    return pl.pallas_call(
        body, name=name, grid=(r // tr,), in_specs=[spec] * (npart + 3) + r_in_specs, out_specs=[spec] * 4 + r_out_specs,
        out_shape=[jax.ShapeDtypeStruct((r, c), F32)] * 4 + r_out_shape, scratch_shapes=r_scratch,
        compiler_params=_cparams(),
    )(*parts, w, m, v, *r_ins)


ROW_TM = 256
POINT_TM = 512


def _row(v):
    return v.reshape(1, -1)


def _lane_pad(vals, at):
    return jnp.concatenate([jnp.zeros((at,), F32), vals, jnp.zeros((LANES - at - vals.shape[0],), F32)]).reshape(1, LANES)


def _layer_params(sm, l):
    return dict(
        conf=[sm["conv_dw_w"][l], _row(sm["conv_dw_b"][l]), _row(sm["conv_ln_g"][l]), _row(sm["conv_ln_b"][l])],
        pre=[sm["gdn_conv_q"][l], sm["gdn_conv_k"][l], sm["gdn_conv_v"][l],
             _lane_pad(sm["gdn_a_log"][l], GDN_HEADS), _lane_pad(sm["gdn_dt_bias"][l], GDN_HEADS)],
        scan=_row(sm["gdn_norm_g"][l]),
        sgu=[_row(sm["sgu_ln_g"][l]), _row(sm["sgu_ln_b"][l]), sm["sgu_w_s"][l].reshape(SGU_GROUPS * SGU_CHUNK, SGU_CHUNK),
             sm["sgu_b_s"][l].T],
        mix=[_row(sm["b_gate"][l])],
        ln1=[_row(sm["ln1_g"][l]), _row(sm["ln1_b"][l])],
        ff=[_row(sm["b_ff1"][l])],
        ln2=[_row(sm["b_ff2"][l]), _row(sm["ln2_g"][l]), _row(sm["ln2_b"][l])],
    )


def _split_w_in(w):
    zpad = jnp.zeros((D_MODEL, LANES - 2 * GDN_HEADS), w.dtype)
    return dict(A=w[:, 0:1024], B=jnp.concatenate([w[:, 1024:2560], w[:, 3072:3080], zpad], axis=1), Z=w[:, 2560:3072],
                C=w[:, 3080:4104], G=w[:, 4104:7176])


def _join_w_in(d):
    return jnp.concatenate([d["A"], d["B"][:, 0:1536], d["Z"], d["B"][:, 1536:1544], d["C"], d["G"]], axis=1)


def _hosted(get, host):
    got = get(host) if get is not None else None
    return got if got is not None else (None, lambda results: None)


def layer_forward(l, xin, xin_b, big, lp, fetch=None):
    tag = f"l{l}_"
    wi = big["w_in"][l]
    sv = dict(xin=xin, xin_b=xin_b)
    for r, p in zip("GACBZ", matmul_multi_nn(tag + "proj", xin_b, [wi[r] for r in "GACBZ"], ACT)):
        sv["p" + r] = p
    ride, done = _hosted(fetch, "conformer")
    sv["ya_in"], *rode = halo_fwd(tag + "conformer", f_conformer, ROW_TM, CONV_HALO, sv["pA"], lp["conf"], [[BF16]], ride)
    done(rode)
    ride, done = _hosted(fetch, "gdn_pre")
    sv["qn"], sv["kn"], sv["vv"], sv["gb"], *rode = halo_fwd(tag + "gdn_pre", f_gdn_pre, ROW_TM, GDN_HALO, sv["pB"], lp["pre"],
                                                             [[F32]] * 4, ride)
    done(rode)
    ride, done = _hosted(fetch, "gdn_prep")
    prep_out = prep_fwd(tag + "gdn_prep", sv["qn"], sv["kn"], sv["vv"], sv["gb"], ride)
    prep, sv["tinv"] = prep_out[:6], prep_out[6]
    done(prep_out[7:])
    sv["prep"] = prep
    ride, done = _hosted(fetch, "gdn_scan")
    sv["yb_in"], sv["states"], *rode = scan_fwd(tag + "gdn_scan", lp["scan"], *prep, sv["pZ"], ride)
    done(rode)
    (sv["yc_in"],) = stage_fwd(tag + "sgu", f_sgu, POINT_TM, [sv["pC"]], lp["sgu"], [[BF16]])
    sv["ya"] = matmul(tag + "out_a", sv["ya_in"], big["w_pa"][l], "nn", ACT)
    sv["yb"] = matmul(tag + "out_b", sv["yb_in"], big["w_pb"][l], "nn", ACT)
    sv["yc"] = matmul(tag + "out_c", sv["yc_in"], big["w_pc"][l], "nn", ACT)
    (sv["mix"],) = stage_fwd(tag + "mix", f_mix, POINT_TM, [sv["ya"], sv["yb"], sv["yc"], sv["pG"]], lp["mix"], [[BF16]])
    sv["m"] = matmul(tag + "w_o", sv["mix"], big["w_o"][l], "nn", F32)
    sv["x1"], sv["x1_b"] = stage_fwd(tag + "ln1", f_ln1, POINT_TM, [xin, sv["m"]], lp["ln1"], [[F32, BF16]])
    sv["h"], sv["hpre"] = matmul(tag + "ff1", sv["x1_b"], big["w_ff1"][l], "nn", BF16, fuse=("relu2", lp["ff"][0]))
    sv["f2"] = matmul(tag + "ff2", sv["h"], big["w_ff2"][l], "nn", F32)
    x2, x2_b = stage_fwd(tag + "ln2", f_ln2, POINT_TM, [sv["x1"], sv["f2"]], lp["ln2"], [[F32, BF16]])
    return x2, x2_b, sv


def layer_backward(l, dx2, sv, big, lp, send=None):
    tag = f"l{l}_b_"
    g = {}
    get = (lambda host: send(host, g)) if send is not None else None
    dx1_a, df2, db2, dg2, dbe2 = stage_bwd(tag + "ln2", f_ln2, POINT_TM, [sv["x1"], sv["f2"]], lp["ln2"], [dx2], [F32, BF16])
    g["b_ff2"], g["ln2_g"], g["ln2_b"] = db2[0], dg2[0], dbe2[0]
    g["w_ff2"] = matmul(tag + "dw_ff2", sv["h"], df2, "tn", F32)
    dhpre, db1 = matmul(tag + "dx_ff2", df2, big["w_ff2"][l], "nt", BF16, fuse=("relu2_bwd", sv["hpre"]))
    g["b_ff1"] = db1[0]
    g["w_ff1"] = matmul(tag + "dw_ff1", sv["x1_b"], dhpre, "tn", F32)
    dx1_b = matmul(tag + "dx_ff1", dhpre, big["w_ff1"][l], "nt", F32)
    dxin_a, dm, dg1, dbe1 = stage_bwd(tag + "ln1", f_ln1, POINT_TM, [sv["xin"], sv["m"]], lp["ln1"], [[dx1_a, dx1_b]], [F32, BF16])
    g["ln1_g"], g["ln1_b"] = dg1[0], dbe1[0]
    g["w_o"] = matmul(tag + "dw_o", sv["mix"], dm, "tn", F32)
    dmix = matmul(tag + "dx_o", dm, big["w_o"][l], "nt", ACT)
    dya, dyb, dyc, dp_g, dbg = stage_bwd(tag + "mix", f_mix, POINT_TM, [sv["ya"], sv["yb"], sv["yc"], sv["pG"]], lp["mix"],
                                         [[dmix]], [BF16] * 4)
    g["b_gate"] = dbg[0]
    g["w_pa"] = matmul(tag + "dw_pa", sv["ya_in"], dya, "tn", F32)
    g["w_pb"] = matmul(tag + "dw_pb", sv["yb_in"], dyb, "tn", F32)
    g["w_pc"] = matmul(tag + "dw_pc", sv["yc_in"], dyc, "tn", F32)
    dya_in = matmul(tag + "dx_pa", dya, big["w_pa"][l], "nt", ACT)
    dyb_in = matmul(tag + "dx_pb", dyb, big["w_pb"][l], "nt", ACT)
    dyc_in = matmul(tag + "dx_pc", dyc, big["w_pc"][l], "nt", ACT)
    ride, done = _hosted(get, "conformer")
    dp_a, dcw, dcb, dcg, dcbe, *rode = halo_bwd(tag + "conformer", f_conformer, ROW_TM, CONV_HALO, sv["pA"], lp["conf"], [dya_in],
                                                BF16, ride)
    done(rode)
    g["conv_dw_w"], g["conv_dw_b"], g["conv_ln_g"], g["conv_ln_b"] = dcw, dcb[0], dcg[0], dcbe[0]
    dp_c, dsg, dsb, dsw, dsbs = stage_bwd(tag + "sgu", f_sgu, POINT_TM, [sv["pC"]], lp["sgu"], [[dyc_in]], [BF16])
    g["sgu_ln_g"], g["sgu_ln_b"] = dsg[0], dsb[0]
    g["sgu_w_s"] = dsw.reshape(SGU_GROUPS, SGU_CHUNK, SGU_CHUNK)
    g["sgu_b_s"] = dsbs.T
    ride, done = _hosted(get, "gdn_scan")
    scan_out = scan_bwd(tag + "gdn_scan", lp["scan"], sv["states"], *sv["prep"], sv["pZ"], dyb_in, ride)
    dprep, dp_z, dng = scan_out[:6], scan_out[6], scan_out[7]
    done(scan_out[8:])
    g["gdn_norm_g"] = dng[0]
    dqn, dkn, dvv, dgb = prep_bwd(tag + "gdn_prep", sv["qn"], sv["kn"], sv["vv"], sv["gb"], sv["tinv"], dprep)
    ride, done = _hosted(get, "gdn_pre")
    dp_b, dq, dk, dv, dal, ddt, *rode = halo_bwd(tag + "gdn_pre", f_gdn_pre, ROW_TM, GDN_HALO, sv["pB"], lp["pre"],
                                                 [dqn, dkn, dvv, dgb], BF16, ride)
    done(rode)
    g["gdn_conv_q"], g["gdn_conv_k"], g["gdn_conv_v"] = dq, dk, dv
    g["gdn_a_log"] = dal[0, GDN_HEADS:2 * GDN_HEADS]
    g["gdn_dt_bias"] = ddt[0, GDN_HEADS:2 * GDN_HEADS]
    dps = dict(G=dp_g, A=dp_a, C=dp_c, B=dp_b, Z=dp_z)
    wi = big["w_in"][l]
    g["w_in"] = _join_w_in({r: matmul(tag + "dw_in_" + r, sv["xin_b"], dps[r], "tn", F32) for r in "GACBZ"})
    ride, done = _hosted(get, "dx_in")
    dxin_b, *rode = matmul_sum_nt(tag + "dx_in", [(dps[r], wi[r]) for r in "GACBZ"], F32, ride)
    done(rode)
    return [dxin_a, dxin_b], g


SMALL_PER_LAYER = ("b_gate", "conv_dw_w", "conv_dw_b", "conv_ln_g", "conv_ln_b", "gdn_conv_q", "gdn_conv_k", "gdn_conv_v",
                   "gdn_a_log", "gdn_dt_bias", "gdn_norm_g", "sgu_ln_g", "sgu_ln_b", "sgu_w_s", "sgu_b_s", "ln1_g", "ln1_b",
                   "b_ff1", "b_ff2", "ln2_g", "ln2_b")
BIG = ("w_in", "w_pa", "w_pb", "w_pc", "w_o", "w_ff1", "w_ff2")


def local_step(x, target, big, sm, fetch=None, send=None):
    p_in = [_row(sm["ln_in_g"]), _row(sm["ln_in_b"])]
    ride, done = _hosted(fetch, "ln_in")
    xc, xc_b, *rode = stage_fwd("ln_in", f_ln_in, POINT_TM, [x], p_in, [[F32, BF16]], ride)
    done(rode)
    lps = [_layer_params(sm, l) for l in range(DEPTH)]
    saved = []
    for l in range(DEPTH):
        xc, xc_b, sv = layer_forward(l, xc, xc_b, big, lps[l], fetch if l == 0 else None)
        saved.append(sv)
    loss_part, dy = loss_stage("loss", xc, target, POINT_TM)
    dx = [dy]
    per_layer = [None] * DEPTH

    def send_from(l):
        if send is None or l != 0:
            return None
        return lambda host, g: send(host, per_layer[:l] + [g] + per_layer[l + 1:])

    for l in reversed(range(DEPTH)):
        dx, per_layer[l] = layer_backward(l, dx, saved[l], big, lps[l], send_from(l))
    grad_x, dgi, dbi = stage_bwd("ln_in_b", f_ln_in, POINT_TM, [x], p_in, [dx], [F32])
    small = {n: jnp.stack([per_layer[l][n] for l in range(DEPTH)]) for n in SMALL_PER_LAYER}
    small["ln_in_g"], small["ln_in_b"] = dgi[0], dbi[0]
    return loss_part, grad_x, small, per_layer


PARAM_NAMES = ("ln_in_g", "ln_in_b", "w_in", "b_gate", "conv_dw_w", "conv_dw_b", "conv_ln_g", "conv_ln_b", "w_pa",
               "gdn_conv_q", "gdn_conv_k", "gdn_conv_v", "gdn_a_log", "gdn_dt_bias", "gdn_norm_g", "w_pb", "sgu_ln_g",
               "sgu_ln_b", "sgu_w_s", "sgu_b_s", "w_pc", "w_o", "ln1_g", "ln1_b", "w_ff1", "b_ff1", "w_ff2", "b_ff2",
               "ln2_g", "ln2_b")
SMALL = tuple(n for n in PARAM_NAMES if n not in BIG)
COL_SHARDED_SMALL = ("conv_dw_w", "gdn_conv_q", "gdn_conv_k", "gdn_conv_v")
ROW_SHARDED_BIG = ("w_o", "w_ff2")
ADAM_ROWS = 128
_OUT_PROJ = ("w_pa", "w_pb", "w_pc", "w_o")
FETCH_BEHIND = {"conformer": [("w_ff1", 0), ("w_ff2", 0)], "gdn_pre": [(n, l) for l in (0, 1) for n in _OUT_PROJ],
                "gdn_prep": [("w_in", 1), ("w_ff1", 1)], "gdn_scan": [("w_ff2", 1)]}
SEND_BEHIND = {"conformer": [(n, 1) for n in BIG], "gdn_scan": [("w_ff1", 0), ("w_ff2", 0)],
               "gdn_pre": [(n, 0) for n in _OUT_PROJ], "dx_in": [("w_in", 0)]}


def _unshard(g, rows):
    n, r, c = g.shape
    if rows:
        return g.reshape(n * r, c)
    return jnp.moveaxis(g, 0, 1).reshape(r, n * c)


def _to_shards(w, rows):
    r, c = w.shape
    if rows:
        return w.reshape(N_CHIPS, r // N_CHIPS, c)
    return jnp.moveaxis(w.reshape(r, N_CHIPS, c // N_CHIPS), 1, 0)


def _pack(vals, rows_multiple=8):
    flat = jnp.concatenate([v.reshape(-1) for v in vals])
    per = LANES * rows_multiple
    padded = -(-flat.shape[0] // per) * per
    return jnp.pad(flat, (0, padded - flat.shape[0])).reshape(-1, LANES)


def _unpack(packed, shapes):
    flat = packed.reshape(-1)
    out, at = [], 0
    for s in shapes:
        size = 1
        for d in s:
            size *= d
        out.append(flat[at:at + size].reshape(s))
        at += size
    return out


def kernel(x, ln_in_g, ln_in_b, w_in, b_gate, conv_dw_w, conv_dw_b, conv_ln_g, conv_ln_b, w_pa, gdn_conv_q, gdn_conv_k, gdn_conv_v, gdn_a_log, gdn_dt_bias, gdn_norm_g, w_pb, sgu_ln_g, sgu_ln_b, sgu_w_s, sgu_b_s, w_pc, w_o, ln1_g, ln1_b, w_ff1, b_ff1, w_ff2, b_ff2, ln2_g, ln2_b, loss_target, m_ln_in_g, m_ln_in_b, m_w_in, m_b_gate, m_conv_dw_w, m_conv_dw_b, m_conv_ln_g, m_conv_ln_b, m_w_pa, m_gdn_conv_q, m_gdn_conv_k, m_gdn_conv_v, m_gdn_a_log, m_gdn_dt_bias, m_gdn_norm_g, m_w_pb, m_sgu_ln_g, m_sgu_ln_b, m_sgu_w_s, m_sgu_b_s, m_w_pc, m_w_o, m_ln1_g, m_ln1_b, m_w_ff1, m_b_ff1, m_w_ff2, m_b_ff2, m_ln2_g, m_ln2_b, v_ln_in_g, v_ln_in_b, v_w_in, v_b_gate, v_conv_dw_w, v_conv_dw_b, v_conv_ln_g, v_conv_ln_b, v_w_pa, v_gdn_conv_q, v_gdn_conv_k, v_gdn_conv_v, v_gdn_a_log, v_gdn_dt_bias, v_gdn_norm_g, v_w_pb, v_sgu_ln_g, v_sgu_ln_b, v_sgu_w_s, v_sgu_b_s, v_w_pc, v_w_o, v_ln1_g, v_ln1_b, v_w_ff1, v_b_ff1, v_w_ff2, v_b_ff2, v_ln2_g, v_ln2_b):
    w = dict(zip(PARAM_NAMES, (ln_in_g, ln_in_b, w_in, b_gate, conv_dw_w, conv_dw_b, conv_ln_g, conv_ln_b, w_pa, gdn_conv_q, gdn_conv_k, gdn_conv_v, gdn_a_log, gdn_dt_bias, gdn_norm_g, w_pb, sgu_ln_g, sgu_ln_b, sgu_w_s, sgu_b_s, w_pc, w_o, ln1_g, ln1_b, w_ff1, b_ff1, w_ff2, b_ff2, ln2_g, ln2_b)))
    m = dict(zip(PARAM_NAMES, (m_ln_in_g, m_ln_in_b, m_w_in, m_b_gate, m_conv_dw_w, m_conv_dw_b, m_conv_ln_g, m_conv_ln_b, m_w_pa, m_gdn_conv_q, m_gdn_conv_k, m_gdn_conv_v, m_gdn_a_log, m_gdn_dt_bias, m_gdn_norm_g, m_w_pb, m_sgu_ln_g, m_sgu_ln_b, m_sgu_w_s, m_sgu_b_s, m_w_pc, m_w_o, m_ln1_g, m_ln1_b, m_w_ff1, m_b_ff1, m_w_ff2, m_b_ff2, m_ln2_g, m_ln2_b)))
    v = dict(zip(PARAM_NAMES, (v_ln_in_g, v_ln_in_b, v_w_in, v_b_gate, v_conv_dw_w, v_conv_dw_b, v_conv_ln_g, v_conv_ln_b, v_w_pa, v_gdn_conv_q, v_gdn_conv_k, v_gdn_conv_v, v_gdn_a_log, v_gdn_dt_bias, v_gdn_norm_g, v_w_pb, v_sgu_ln_g, v_sgu_ln_b, v_sgu_w_s, v_sgu_b_s, v_w_pc, v_w_o, v_ln1_g, v_ln1_b, v_w_ff1, v_b_ff1, v_w_ff2, v_b_ff2, v_ln2_g, v_ln2_b)))
    chip = 2 * lax.axis_index("x") + lax.axis_index("y")

    shards = {n: w[n].astype(BF16) for n in BIG}
    col_small_shapes = [w[n].shape for n in COL_SHARDED_SMALL]
    big = {n: [None] * DEPTH for n in BIG}

    def place(keys, gathered):
        for (n, l), g in zip(keys, gathered):
            whole = _unshard(g, n in ROW_SHARDED_BIG)
            big[n][l] = _split_w_in(whole) if n == "w_in" else whole

    sm = {n: w[n] for n in SMALL if n not in COL_SHARDED_SMALL}

    def place_first(gathered):
        place([("w_in", 0)], gathered[:1])
        per_chip = [_unpack(gathered[-1][s], col_small_shapes) for s in range(N_CHIPS)]
        for i, n in enumerate(COL_SHARDED_SMALL):
            sm[n] = jnp.concatenate([per_chip[s][i] for s in range(N_CHIPS)], axis=-1)

    def fetch(host):
        if host == "ln_in":
            return gather_rider([shards["w_in"][0]], [_pack([w[n] for n in COL_SHARDED_SMALL])]), place_first
        keys = FETCH_BEHIND[host]
        return gather_rider([shards[n][l] for n, l in keys], []), lambda gathered: place(keys, gathered)

    received = {}

    def send(host, per_layer):
        keys = SEND_BEHIND[host]
        ride = scatter_rider([_to_shards(per_layer[l][n], n in ROW_SHARDED_BIG).astype(BF16) for n, l in keys])
        return ride, lambda results: received.update(zip(keys, results))

    loss_part, grad_x, grads, _ = local_step(x[0], loss_target[0], big, sm, fetch, send)
    partial = [sum_slots("sum_chips_" + n, [received[(n, l)] for l in range(DEPTH)], ADAM_ROWS) for n in BIG]
    sibling = sibling_exchange("swap_cores", partial)
    small_shapes = [grads[n].shape for n in SMALL]
    vec = _pack([grads[n] for n in SMALL] + [jnp.sum(loss_part).reshape(1)])
    out = {}
    for n, mine, theirs in zip(BIG, partial, sibling):
        shape = w[n].shape
        res = adamw("adamw_" + n, [mine, theirs], w[n].reshape(-1, shape[-1]), m[n].reshape(-1, shape[-1]),
                    v[n].reshape(-1, shape[-1]), ADAM_ROWS, gather_all_rider(vec) if n == "w_in" else None)
        out[n] = [r.reshape(shape) for r in res[:4]]
        if n == "w_in":
            all_small = res[4]
    total = sum_slots("sum_small", [all_small], vec.shape[0])
    whole = _unpack(total, small_shapes + [(1,)])
    loss = whole[-1][0]
    g_small = {}
    for n, g in zip(SMALL, whole[:-1]):
        if n in COL_SHARDED_SMALL:
            width = g.shape[-1] // N_CHIPS
            g = lax.dynamic_slice_in_dim(g, chip * width, width, axis=g.ndim - 1)
        g_small[n] = g
    local_shapes = [w[n].shape for n in SMALL]
    packed_g = _pack([g_small[n] for n in SMALL])
    res = adamw("adamw_small", [packed_g], _pack([w[n] for n in SMALL]), _pack([m[n] for n in SMALL]),
                _pack([v[n] for n in SMALL]), packed_g.shape[0])
    unpacked = [_unpack(r, local_shapes) for r in res]
    for i, n in enumerate(SMALL):
        out[n] = [unpacked[k][i] for k in range(4)]

    return (loss, grad_x[None], *[out[n][0] for n in PARAM_NAMES], *[out[n][1] for n in PARAM_NAMES],
            *[out[n][2] for n in PARAM_NAMES], *[out[n][3] for n in PARAM_NAMES])
```

```python
import functools

import jax
import jax.numpy as jnp
from jax import lax
from jax.experimental import pallas as pl
from jax.experimental.pallas import tpu as pltpu

F32 = jnp.float32
BF16 = jnp.bfloat16
ACT = BF16

D_MODEL = 1024
DEPTH = 2
CONV_DIM = 512
CONV_WIDTH = 31
CONV_HALO = 32
CONV_WINDOW = 128
GDN_HEADS = 4
GDN_HD = 128
GDN_CHUNK = 64
GDN_PAIR = 2 * GDN_CHUNK
GDN_CONV = 4
GDN_HALO = 16
SGU_GROUPS = 4
SGU_GD = 128
SGU_CHUNK = 128
D_FF = 4096
ALPHA = (2 * DEPTH) ** 0.25
LN_EPS = 1e-5
RMS_EPS = 1e-6
ADAM_LR, ADAM_B1, ADAM_B2, ADAM_EPS, ADAM_WD, ADAM_STEP = 0.001, 0.9, 0.999, 1e-08, 0.01, 10
N_CHIPS = 4
N_DEV = 8
LANES = 128
VMEM_LIMIT = 52 * 1024 * 1024


def _dg(a, b, ca, cb):
    return lax.dot_general(a.astype(BF16), b.astype(BF16), (((ca,), (cb,)), ((), ())), preferred_element_type=F32)


@jax.custom_vjp
def mm_nn(a, b):
    return _dg(a, b, 1, 0)


mm_nn.defvjp(lambda a, b: (_dg(a, b, 1, 0), (a, b)), lambda r, g: (_dg(g, r[1], 1, 1), _dg(r[0], g, 0, 0)))


@jax.custom_vjp
def mm_nt(a, b):
    return _dg(a, b, 1, 1)


mm_nt.defvjp(lambda a, b: (_dg(a, b, 1, 1), (a, b)), lambda r, g: (_dg(g, r[1], 1, 0), _dg(g, r[0], 0, 0)))


@jax.custom_vjp
def mm_tn(a, b):
    return _dg(a, b, 0, 0)


mm_tn.defvjp(lambda a, b: (_dg(a, b, 0, 0), (a, b)), lambda r, g: (_dg(r[1], g, 1, 1), _dg(r[0], g, 1, 0)))


def _ln(x, g, b):
    mu = jnp.mean(x, -1, keepdims=True)
    xc = x - mu
    var = jnp.mean(xc * xc, -1, keepdims=True)
    return xc * lax.rsqrt(var + LN_EPS) * g + b


def _iota2(shape, dim):
    return lax.broadcasted_iota(jnp.int32, shape, dim)


def f_ln_in(params, x):
    g, b = params
    return (_ln(x, g, b),)


def _roll_rows(x, s):
    return x if s == 0 else pltpu.roll(x, s, 0)


@functools.partial(jax.custom_vjp, nondiff_argnums=(1,))
def shift_rows(x, s):
    return _roll_rows(x, s)


shift_rows.defvjp(lambda x, s: (_roll_rows(x, s), None),
                  lambda s, _, ct: (_roll_rows(ct, (ct.shape[0] - s) % ct.shape[0]),))


def _causal_conv(full, w, halo):
    width = w.shape[0]
    n_out = full.shape[0] - halo
    win = min(CONV_WINDOW, n_out)
    groups = []
    for g in range(full.shape[1] // LANES):
        ls = slice(g * LANES, (g + 1) * LANES)
        wg = w[:, ls]
        parts = []
        for r0 in range(0, n_out, win):
            xw = full[r0:r0 + halo + win, ls]
            acc = wg[width - 1:width, :] * xw[halo:, :]
            for j in range(width - 1):
                acc = acc + wg[j:j + 1, :] * shift_rows(xw, width - 1 - j)[halo:, :]
            parts.append(acc)
        groups.append(jnp.concatenate(parts, axis=0))
    return jnp.concatenate(groups, axis=1)


def f_conformer(params, halo, pa):
    w, b, g, be = params
    full = jnp.concatenate([halo, pa], axis=0)
    h = full[:, :CONV_DIM] * jax.nn.sigmoid(full[:, CONV_DIM:])
    return (jax.nn.silu(_ln(_causal_conv(h, w, CONV_HALO) + b, g, be)),)


def f_gdn_pre(params, halo, pb):
    cq, ck, cv, alog, dtb = params
    full = jnp.concatenate([halo, pb], axis=0)

    def conv(cols, w):
        return jax.nn.silu(_causal_conv(cols, w, GDN_HALO))

    def l2n(t, scale):
        parts = []
        for h in range(GDN_HEADS):
            th = t[:, h * GDN_HD:(h + 1) * GDN_HD]
            parts.append(th * (lax.rsqrt(jnp.sum(th * th, -1, keepdims=True) + RMS_EPS) * scale))
        return jnp.concatenate(parts, axis=1)

    q = l2n(conv(full[:, 0:512], cq), GDN_HD ** -0.5)
    k = l2n(conv(full[:, 512:1024], ck), 1.0)
    v = conv(full[:, 1024:1536], cv)
    logits = pb[:, 1536:1536 + LANES]
    z = logits + dtb
    softplus = jnp.maximum(z, 0.0) + jnp.log(1.0 + jnp.exp(-jnp.abs(z)))
    g = -jnp.exp(alog) * softplus
    beta = jax.nn.sigmoid(logits)
    lane = _iota2(logits.shape, 1)
    gb = jnp.where(lane < GDN_HEADS, beta, jnp.where(lane < 2 * GDN_HEADS, g, 0.0))
    return q, k, v, gb


def _split2(x):
    hi = x.astype(BF16)
    return hi, (x - hi.astype(F32)).astype(BF16)


def _dg3(a, b, ca, cb):
    ah, al = _split2(a)
    bh, bl = _split2(b)
    d = lambda p, q: lax.dot_general(p, q, (((ca,), (cb,)), ((), ())), preferred_element_type=F32)
    return d(ah, bh) + (d(ah, bl) + d(al, bh))


def _inv_unit_lower(lows):
    n = lows[0].shape[0]
    eye = jnp.where(_iota2((n, n), 0) == _iota2((n, n), 1), 1.0, 0.0).astype(F32)
    ps = [-low for low in lows]
    xs = [eye + p for p in ps]
    for _ in range(5):
        ps = [_dg3(p, p, 1, 0) for p in ps]
        xs = [x + _dg3(x, p, 1, 0) for x, p in zip(xs, ps)]
    return xs


def _pair_masks():
    n = GDN_PAIR
    r, c = _iota2((n, n), 0), _iota2((n, n), 1)
    same = (r >= GDN_CHUNK) == (c >= GDN_CHUNK)
    return same, jnp.logical_and(same, r >= c), jnp.logical_and(same, r > c), jnp.logical_and(same, r <= c)


def _masked_sum(mask, g):
    mk = jnp.where(mask, 1.0, 0.0).astype(BF16)
    g1 = g.astype(BF16)
    r1 = g - g1.astype(F32)
    g2 = r1.astype(BF16)
    g3 = (r1 - g2.astype(F32)).astype(BF16)
    d = lambda q: jnp.dot(mk, q, preferred_element_type=F32)
    return d(g1) + (d(g2) + d(g3))


@jax.custom_vjp
def chunk_cumsum(g):
    return _masked_sum(_pair_masks()[1], g)


chunk_cumsum.defvjp(lambda g: (_masked_sum(_pair_masks()[1], g), None), lambda _, ct: (_masked_sum(_pair_masks()[3], ct),))


def _prep_gates(gb):
    gam_all = chunk_cumsum(gb)
    first = _iota2((GDN_PAIR, LANES), 0) < GDN_CHUNK
    glast = jnp.where(first, gam_all[GDN_CHUNK - 1:GDN_CHUNK, :], gam_all[GDN_PAIR - 1:GDN_PAIR, :])
    return gam_all, gam_all.T, glast


def _head_gates(gb, gam_all, gam_t, h, causal):
    gc = gam_all[:, GDN_HEADS + h:GDN_HEADS + h + 1]
    gr = gam_t[GDN_HEADS + h:GDN_HEADS + h + 1, :]
    return gc, gb[:, h:h + 1], jnp.exp(jnp.where(causal, gc - gr, -jnp.inf))


def f_prep_low(kn, gb):
    _, causal, strict, _ = _pair_masks()
    gam_all, gam_t, _ = _prep_gates(gb)
    lows = []
    for h in range(GDN_HEADS):
        k = kn[:, h * GDN_HD:(h + 1) * GDN_HD]
        _, beta, decay = _head_gates(gb, gam_all, gam_t, h, causal)
        lows.append(jnp.where(strict, beta * mm_nt(k, k) * decay, 0.0))
    return jnp.concatenate(lows, axis=1)


def f_prep_rest(qn, kn, vv, gb, tinv_all):
    _, causal, _, _ = _pair_masks()
    gam_all, gam_t, glast = _prep_gates(gb)
    us, ws, as_, qds, kds = [], [], [], [], []
    for h in range(GDN_HEADS):
        hs = slice(h * GDN_HD, (h + 1) * GDN_HD)
        q, k, v, tinv = qn[:, hs], kn[:, hs], vv[:, hs], tinv_all[:, hs]
        gc, beta, decay = _head_gates(gb, gam_all, gam_t, h, causal)
        eg = jnp.exp(gc)
        us.append(mm_nn(tinv, beta * v))
        ws.append(mm_nn(tinv, beta * k * eg))
        as_.append(mm_nt(q, k) * decay)
        qds.append(q * eg)
        kds.append(k * jnp.exp(glast[:, GDN_HEADS + h:GDN_HEADS + h + 1] - gc))
    cat = lambda xs: jnp.concatenate(xs, axis=1)
    return cat(us), cat(ws), cat(as_), cat(qds), cat(kds), jnp.exp(glast)


PREP_TM = 2 * GDN_PAIR


def _ride(ride, n_in, n_out, refs, n_steps):
    if ride is None:
        return
    nri, nro = len(ride["ins"]), len(ride["out_shape"])
    r_in = refs[n_in:n_in + nri]
    r_out = refs[n_in + nri + n_out:n_in + nri + n_out + nro]
    sems = refs[len(refs) - len(ride["scratch"]):]
    step = pl.program_id(0)

    @pl.when(step == 0)
    def _():
        ride["start"](r_in, r_out, sems)

    @pl.when(step == (2 * n_steps) // 3)
    def _():
        ride["middle"](r_in, r_out, sems)

    @pl.when(step == n_steps - 1)
    def _():
        ride["finish"](r_in, r_out, sems)


def _ride_args(ride):
    if ride is None:
        return [], [], [], [], []
    n = len(ride["ins"])
    return list(ride["ins"]), [ANY_SPEC] * n, [ANY_SPEC] * len(ride["out_shape"]), list(ride["out_shape"]), list(ride["scratch"])


def prep_fwd(name, qn, kn, vv, gb, ride=None):
    t = qn.shape[0]
    hw = GDN_HEADS * GDN_HD
    n_steps = t // PREP_TM
    r_ins, r_in_specs, r_out_specs, r_out_shape, r_scratch = _ride_args(ride)

    def body(*refs):
        q_ref, k_ref, v_ref, gb_ref = refs[:4]
        outs = refs[4 + len(r_ins):4 + len(r_ins) + 7]
        pairs = [pl.ds(pi * GDN_PAIR, GDN_PAIR) for pi in range(PREP_TM // GDN_PAIR)]
        lows = [f_prep_low(k_ref[rs, :], gb_ref[rs, :]) for rs in pairs]
        invs = _inv_unit_lower([low[:, h * GDN_HD:(h + 1) * GDN_HD] for low in lows for h in range(GDN_HEADS)])
        for pi, rs in enumerate(pairs):
            tinv = jnp.concatenate(invs[pi * GDN_HEADS:(pi + 1) * GDN_HEADS], axis=1)
            res = f_prep_rest(q_ref[rs, :], k_ref[rs, :], v_ref[rs, :], gb_ref[rs, :], tinv)
            for o_ref, val in zip(outs, res + (tinv,)):
                o_ref[rs, :] = val
        _ride(ride, 4, 7, refs, n_steps)

    widths = [hw, hw, hw, hw, hw, LANES, hw]
    return pl.pallas_call(
        body, name=name, grid=(n_steps,),
        in_specs=[_row_spec(PREP_TM, hw)] * 3 + [_row_spec(PREP_TM, LANES)] + r_in_specs,
        out_specs=[_row_spec(PREP_TM, wd) for wd in widths] + r_out_specs,
        out_shape=[jax.ShapeDtypeStruct((t, wd), F32) for wd in widths] + r_out_shape,
        scratch_shapes=r_scratch, compiler_params=_cparams(),
    )(qn, kn, vv, gb, *r_ins)


def prep_bwd(name, qn, kn, vv, gb, tinv, cts):
    t = qn.shape[0]
    hw = GDN_HEADS * GDN_HD

    def body(q_ref, k_ref, v_ref, gb_ref, x_ref, *refs):
        outs = refs[6:]
        for pi in range(PREP_TM // GDN_PAIR):
            rs = pl.ds(pi * GDN_PAIR, GDN_PAIR)
            ct = tuple(r[rs, :] for r in refs[:6])
            kn_v, gb_v, x_all = k_ref[rs, :], gb_ref[rs, :], x_ref[rs, :]
            _, vjp_low = jax.vjp(f_prep_low, kn_v, gb_v)
            _, vjp_rest = jax.vjp(f_prep_rest, q_ref[rs, :], kn_v, v_ref[rs, :], gb_v, x_all)
            dq, dk, dv, dgb, dx_all = vjp_rest(ct)
            dlows = []
            for h in range(GDN_HEADS):
                hs = slice(h * GDN_HD, (h + 1) * GDN_HD)
                dlows.append(-_dg3(_dg3(x_all[:, hs], dx_all[:, hs], 0, 0), x_all[:, hs], 1, 1))
            dk2, dgb2 = vjp_low(jnp.concatenate(dlows, axis=1))
            outs[0][rs, :] = dq
            outs[1][rs, :] = dk + dk2
            outs[2][rs, :] = dv
            outs[3][rs, :] = dgb + dgb2

    in_w = [hw, hw, hw, LANES, hw] + [hw, hw, hw, hw, hw, LANES]
    out_w = [hw, hw, hw, LANES]
    return pl.pallas_call(
        body, name=name, grid=(t // PREP_TM,),
        in_specs=[_row_spec(PREP_TM, wd) for wd in in_w], out_specs=[_row_spec(PREP_TM, wd) for wd in out_w],
        out_shape=[jax.ShapeDtypeStruct((t, wd), F32) for wd in out_w], compiler_params=_cparams(),
    )(qn, kn, vv, gb, tinv, *cts)


def f_gdn_scan(params, state, u, w, a, qd, kd, egl, z):
    ng = params[0]
    tm = u.shape[0]
    st = [state[h * GDN_HD:(h + 1) * GDN_HD, :] for h in range(GDN_HEADS)]
    zeros = jnp.zeros((GDN_CHUNK, GDN_HD), F32)
    rows = []
    hss = [slice(h * GDN_HD, (h + 1) * GDN_HD) for h in range(GDN_HEADS)]
    for ci in range(tm // GDN_CHUNK):
        rs = slice(ci * GDN_CHUNK, (ci + 1) * GDN_CHUNK)
        vns = [u[rs, hs] - mm_nn(w[rs, hs], s) for hs, s in zip(hss, st)]
        qs = [mm_nn(qd[rs, hs], s) for hs, s in zip(hss, st)]
        upd = [mm_tn(kd[rs, hs], vn) for hs, vn in zip(hss, vns)]
        es = [egl[ci * GDN_CHUNK:ci * GDN_CHUNK + 1, GDN_HEADS + h:GDN_HEADS + h + 1] for h in range(GDN_HEADS)]
        st = [s * e + d for s, e, d in zip(st, es, upd)]
        vpads = [jnp.concatenate([vn, zeros] if ci % 2 == 0 else [zeros, vn], axis=0) for vn in vns]
        os_ = [q + mm_nn(a[rs, hs], vp) for q, hs, vp in zip(qs, hss, vpads)]
        heads = [o * lax.rsqrt(jnp.mean(o * o, -1, keepdims=True) + RMS_EPS) * ng * jax.nn.silu(z[rs, hs])
                 for o, hs in zip(os_, hss)]
        rows.append(jnp.concatenate(heads, axis=1))
    return jnp.concatenate(rows, axis=0), jnp.concatenate(st, axis=0)


def f_sgu(params, puv):
    g, b, ws, bst = params
    tm = puv.shape[0]
    uv = jax.nn.gelu(puv)
    u = uv[:, :512]
    v = _ln(uv[:, 512:], g, b)
    r, c = _iota2((SGU_CHUNK, SGU_CHUNK), 0), _iota2((SGU_CHUNK, SGU_CHUNK), 1)
    rows = []
    for ci in range(tm // SGU_CHUNK):
        rs = slice(ci * SGU_CHUNK, (ci + 1) * SGU_CHUNK)
        groups = []
        for gi in range(SGU_GROUPS):
            gs = slice(gi * SGU_GD, (gi + 1) * SGU_GD)
            wt = jnp.where(r >= c, ws[gs, :], 0.0)
            groups.append(mm_nn(wt, v[rs, gs]) + bst[:, gi:gi + 1])
        rows.append(jnp.concatenate(groups, axis=1))
    return (u * jnp.concatenate(rows, axis=0),)


def f_mix(params, ya, yb, yc, pg):
    bg = params[0]
    s = jax.nn.sigmoid(pg + bg)
    return (s[:, :D_MODEL] * ya + s[:, D_MODEL:2 * D_MODEL] * yb + s[:, 2 * D_MODEL:] * yc,)


def f_ln1(params, xin, m):
    g, b = params
    return (_ln(ALPHA * xin + m, g, b),)


def f_ln2(params, x1, f2):
    b2, g, b = params
    return (_ln(ALPHA * x1 + (f2 + b2), g, b),)


def _cparams(n_axes=1):
    return pltpu.CompilerParams(dimension_semantics=("arbitrary",) * n_axes, vmem_limit_bytes=VMEM_LIMIT)


def _row_spec(tm, width, n_tiles=None):
    if n_tiles is None:
        return pl.BlockSpec((tm, width), lambda i: (i, 0))
    return pl.BlockSpec((tm, width), lambda i: (n_tiles - 1 - i, 0))


def _full_spec(shape):
    return pl.BlockSpec(shape, lambda i: (0,) * len(shape))


def stage_fwd(name, fn, tm, rows, params, outs, ride=None):
    t = rows[0].shape[0]
    nr, npar = len(rows), len(params)
    widths = jax.eval_shape(lambda p, r: fn(p, *r), [jax.ShapeDtypeStruct(p.shape, F32) for p in params],
                            [jax.ShapeDtypeStruct((tm, r.shape[1]), F32) for r in rows])
    r_ins, r_in_specs, r_out_specs, r_out_shape, r_scratch = _ride_args(ride)
    n_out = sum(len(dts) for dts in outs)

    def body(*refs):
        r = [x[...].astype(F32) for x in refs[:nr]]
        p = [x[...] for x in refs[nr:nr + npar]]
        res = fn(p, *r)
        k = nr + npar + len(r_ins)
        for o, dts in zip(res, outs):
            for dt in dts:
                refs[k][...] = o.astype(dt)
                k += 1
        _ride(ride, nr + npar, n_out, refs, t // tm)

    out_shape, out_specs = [], []
    for wd, dts in zip(widths, outs):
        for dt in dts:
            out_shape.append(jax.ShapeDtypeStruct((t, wd.shape[1]), dt))
            out_specs.append(_row_spec(tm, wd.shape[1]))
    return pl.pallas_call(
        body, name=name, grid=(t // tm,),
        in_specs=[_row_spec(tm, r.shape[1]) for r in rows] + [_full_spec(p.shape) for p in params] + r_in_specs,
        out_specs=out_specs + r_out_specs, out_shape=out_shape + r_out_shape, scratch_shapes=r_scratch,
        compiler_params=_cparams(),
    )(*rows, *params, *r_ins)


def stage_bwd(name, fn, tm, rows, params, cts, drow_dtypes, ride=None):
    t = rows[0].shape[0]
    nr, npar = len(rows), len(params)
    flat_cts = [c for cl in cts if cl is not None for c in cl]
    nct = len(flat_cts)
    want = [i for i, dt in enumerate(drow_dtypes) if dt is not None]
    r_ins, r_in_specs, r_out_specs, r_out_shape, r_scratch = _ride_args(ride)

    def body(*refs):
        r = [x[...].astype(F32) for x in refs[:nr]]
        p = [x[...] for x in refs[nr:nr + npar]]
        res, vjp = jax.vjp(lambda pp, rr: fn(pp, *rr), p, r)
        k = nr + npar
        ct = []
        for o, cl in zip(res, cts):
            if cl is None:
                ct.append(jnp.zeros_like(o))
            else:
                acc = refs[k][...].astype(F32)
                for j in range(1, len(cl)):
                    acc = acc + refs[k + j][...].astype(F32)
                k += len(cl)
                ct.append(acc)
        dp, dr = vjp(tuple(ct))
        k += len(r_ins)
        for i in want:
            refs[k][...] = dr[i].astype(drow_dtypes[i])
            k += 1

        @pl.when(pl.program_id(0) == 0)
        def _():
            for j in range(npar):
                refs[k + j][...] = jnp.zeros_like(refs[k + j])

        for j in range(npar):
            refs[k + j][...] += dp[j]
        _ride(ride, nr + npar + nct, len(want) + npar, refs, t // tm)

    out_shape = [jax.ShapeDtypeStruct(rows[i].shape, drow_dtypes[i]) for i in want]
    out_specs = [_row_spec(tm, rows[i].shape[1]) for i in want]
    out_shape += [jax.ShapeDtypeStruct(p.shape, F32) for p in params]
    out_specs += [_full_spec(p.shape) for p in params]
    return pl.pallas_call(
        body, name=name, grid=(t // tm,),
        in_specs=[_row_spec(tm, r.shape[1]) for r in rows] + [_full_spec(p.shape) for p in params]
        + [_row_spec(tm, c.shape[1]) for c in flat_cts] + r_in_specs,
        out_specs=out_specs + r_out_specs, out_shape=out_shape + r_out_shape, scratch_shapes=r_scratch,
        compiler_params=_cparams(),
    )(*rows, *params, *flat_cts, *r_ins)


def _halo_spec(tm, halo, width, n_tiles=None):
    per = tm // halo
    if n_tiles is None:
        return pl.BlockSpec((halo, width), lambda i: (jnp.maximum(i * per - 1, 0), 0))
    return pl.BlockSpec((halo, width), lambda i: (jnp.maximum((n_tiles - 1 - i) * per - 1, 0), 0))


def halo_fwd(name, fn, tm, halo, src, params, outs, ride=None):
    t, width = src.shape
    npar = len(params)
    widths = jax.eval_shape(lambda p, h, r: fn(p, h, r), [jax.ShapeDtypeStruct(p.shape, F32) for p in params],
                            jax.ShapeDtypeStruct((halo, width), F32), jax.ShapeDtypeStruct((tm, width), F32))
    r_ins, r_in_specs, r_out_specs, r_out_shape, r_scratch = _ride_args(ride)
    n_out = sum(len(dts) for dts in outs)

    def body(h_ref, x_ref, *refs):
        hal = jnp.where(pl.program_id(0) == 0, 0.0, h_ref[...].astype(F32))
        res = fn([x[...] for x in refs[:npar]], hal, x_ref[...].astype(F32))
        k = npar + len(r_ins)
        for o, dts in zip(res, outs):
            for dt in dts:
                refs[k][...] = o.astype(dt)
                k += 1
        _ride(ride, 2 + npar, n_out, (h_ref, x_ref) + refs, t // tm)

    out_shape, out_specs = [], []
    for wd, dts in zip(widths, outs):
        for dt in dts:
            out_shape.append(jax.ShapeDtypeStruct((t, wd.shape[1]), dt))
            out_specs.append(_row_spec(tm, wd.shape[1]))
    return pl.pallas_call(
        body, name=name, grid=(t // tm,),
        in_specs=[_halo_spec(tm, halo, width), _row_spec(tm, width)] + [_full_spec(p.shape) for p in params] + r_in_specs,
        out_specs=out_specs + r_out_specs, out_shape=out_shape + r_out_shape, scratch_shapes=r_scratch,
        compiler_params=_cparams(),
    )(src, src, *params, *r_ins)


def halo_bwd(name, fn, tm, halo, src, params, cts, dsrc_dtype, ride=None):
    t, width = src.shape
    n_tiles = t // tm
    npar = len(params)
    n_in = 2 + npar + len(cts)
    r_ins, r_in_specs, r_out_specs, r_out_shape, r_scratch = _ride_args(ride)

    def body(h_ref, x_ref, *refs):
        carry = refs[n_in - 2 + len(r_ins) + 1 + npar + len(r_out_shape)]
        first_tile = pl.program_id(0) == n_tiles - 1
        hal = jnp.where(first_tile, 0.0, h_ref[...].astype(F32))
        p = [x[...] for x in refs[:npar]]
        res, vjp = jax.vjp(fn, p, hal, x_ref[...].astype(F32))
        k = npar
        ct = tuple(refs[k + j][...].astype(F32) for j in range(len(cts)))
        k += len(cts) + len(r_ins)
        dp, dh, dx = vjp(ct)

        @pl.when(pl.program_id(0) == 0)
        def _():
            carry[...] = jnp.zeros_like(carry)
            for j in range(npar):
                refs[k + 1 + j][...] = jnp.zeros_like(refs[k + 1 + j])

        dx = jnp.concatenate([dx[:tm - halo, :], dx[tm - halo:, :] + carry[...]], axis=0)
        refs[k][...] = dx.astype(dsrc_dtype)
        carry[...] = dh
        for j in range(npar):
            refs[k + 1 + j][...] += dp[j]
        _ride(ride, n_in, 1 + npar, (h_ref, x_ref) + refs, n_tiles)

    out_shape = [jax.ShapeDtypeStruct((t, width), dsrc_dtype)] + [jax.ShapeDtypeStruct(p.shape, F32) for p in params]
    out_specs = [_row_spec(tm, width, n_tiles)] + [_full_spec(p.shape) for p in params]
    return pl.pallas_call(
        body, name=name, grid=(n_tiles,),
        in_specs=[_halo_spec(tm, halo, width, n_tiles), _row_spec(tm, width, n_tiles)] + [_full_spec(p.shape) for p in params]
        + [_row_spec(tm, c.shape[1], n_tiles) for c in cts] + r_in_specs,
        out_specs=out_specs + r_out_specs, out_shape=out_shape + r_out_shape,
        scratch_shapes=[pltpu.VMEM((halo, width), F32)] + r_scratch, compiler_params=_cparams(),
    )(src, src, *params, *cts, *r_ins)


SCAN_TM = 512


def scan_fwd(name, norm_g, u, w, a, qd, kd, egl, z, ride=None):
    t = u.shape[0]
    n_tiles = t // SCAN_TM
    srows = GDN_HEADS * GDN_HD

    r_ins, r_in_specs, r_out_specs, r_out_shape, r_scratch = _ride_args(ride)

    def body(*refs):
        ng_ref, u_ref, w_ref, a_ref, qd_ref, kd_ref, e_ref, z_ref = refs[:8]
        o_ref, s_ref = refs[8 + len(r_ins):10 + len(r_ins)]
        state = refs[10 + len(r_ins) + len(r_out_shape)]

        @pl.when(pl.program_id(0) == 0)
        def _():
            state[...] = jnp.zeros_like(state)

        s_ref[0] = state[...]
        o, s_new = f_gdn_scan([ng_ref[...]], state[...], u_ref[...], w_ref[...], a_ref[...], qd_ref[...], kd_ref[...],
                              e_ref[...], z_ref[...].astype(F32))
        o_ref[...] = o.astype(BF16)
        state[...] = s_new
        _ride(ride, 8, 2, refs, n_tiles)

    rows = [u, w, a, qd, kd, egl, z]
    return pl.pallas_call(
        body, name=name, grid=(n_tiles,),
        in_specs=[_full_spec(norm_g.shape)] + [_row_spec(SCAN_TM, r.shape[1]) for r in rows] + r_in_specs,
        out_specs=[_row_spec(SCAN_TM, 512), pl.BlockSpec((1, srows, GDN_HD), lambda i: (i, 0, 0))] + r_out_specs,
        out_shape=[jax.ShapeDtypeStruct((t, 512), BF16), jax.ShapeDtypeStruct((n_tiles, srows, GDN_HD), F32)] + r_out_shape,
        scratch_shapes=[pltpu.VMEM((srows, GDN_HD), F32)] + r_scratch, compiler_params=_cparams(),
    )(norm_g, *rows, *r_ins)


def scan_bwd(name, norm_g, states, u, w, a, qd, kd, egl, z, dout, ride=None):
    t = u.shape[0]
    n_tiles = t // SCAN_TM
    srows = GDN_HEADS * GDN_HD
    r_ins, r_in_specs, r_out_specs, r_out_shape, r_scratch = _ride_args(ride)

    def body(*refs):
        ng_ref, s_ref, u_ref, w_ref, a_ref, qd_ref, kd_ref, e_ref, z_ref, do_ref = refs[:10]
        du_ref, dw_ref, da_ref, dqd_ref, dkd_ref, de_ref, dz_ref, dng_ref = refs[10 + len(r_ins):18 + len(r_ins)]
        dstate = refs[18 + len(r_ins) + len(r_out_shape)]

        @pl.when(pl.program_id(0) == 0)
        def _():
            dstate[...] = jnp.zeros_like(dstate)
            dng_ref[...] = jnp.zeros_like(dng_ref)

        args = ([ng_ref[...]], s_ref[0], u_ref[...], w_ref[...], a_ref[...], qd_ref[...], kd_ref[...], e_ref[...],
                z_ref[...].astype(F32))
        _, vjp = jax.vjp(f_gdn_scan, *args)
        dp, ds, du, dw, da, dqd, dkd, de, dz = vjp((do_ref[...].astype(F32), dstate[...]))
        du_ref[...] = du
        dw_ref[...] = dw
        da_ref[...] = da
        dqd_ref[...] = dqd
        dkd_ref[...] = dkd
        de_ref[...] = de
        dz_ref[...] = dz.astype(BF16)
        dng_ref[...] += dp[0]
        dstate[...] = ds
        _ride(ride, 10, 8, refs, n_tiles)

    rows = [u, w, a, qd, kd, egl, z, dout]
    return pl.pallas_call(
        body, name=name, grid=(n_tiles,),
        in_specs=[_full_spec(norm_g.shape), pl.BlockSpec((1, srows, GDN_HD), lambda i: (n_tiles - 1 - i, 0, 0))]
        + [_row_spec(SCAN_TM, r.shape[1], n_tiles) for r in rows] + r_in_specs,
        out_specs=[_row_spec(SCAN_TM, r.shape[1], n_tiles) for r in rows[:7]] + [_full_spec(norm_g.shape)] + r_out_specs,
        out_shape=[jax.ShapeDtypeStruct(r.shape, F32) for r in rows[:6]] + [jax.ShapeDtypeStruct(z.shape, BF16)]
        + [jax.ShapeDtypeStruct(norm_g.shape, F32)] + r_out_shape,
        scratch_shapes=[pltpu.VMEM((srows, GDN_HD), F32)] + r_scratch, compiler_params=_cparams(),
    )(norm_g, states, *rows, *r_ins)


def loss_stage(name, y, target, tm):
    t, d = y.shape

    def body(y_ref, t_ref, l_ref, dy_ref):
        @pl.when(pl.program_id(0) == 0)
        def _():
            l_ref[...] = jnp.zeros_like(l_ref)

        e = y_ref[...] - t_ref[...]
        dy_ref[...] = e * (1.0 / d)
        sq = e * e
        part = sq[:, 0:LANES]
        for j in range(1, d // LANES):
            part = part + sq[:, j * LANES:(j + 1) * LANES]
        acc = part[0:8, :]
        for j in range(1, tm // 8):
            acc = acc + part[j * 8:(j + 1) * 8, :]
        l_ref[...] += acc * (0.5 / d)

    return pl.pallas_call(
        body, name=name, grid=(t // tm,),
        in_specs=[_row_spec(tm, d), _row_spec(tm, d)],
        out_specs=[_full_spec((8, LANES)), _row_spec(tm, d)],
        out_shape=[jax.ShapeDtypeStruct((8, LANES), F32), jax.ShapeDtypeStruct((t, d), F32)],
        compiler_params=_cparams(),
    )(y, target)


def _pick(n, cands):
    for c in cands:
        if n % c == 0:
            return c
    return n


def matmul(name, a, b, form, out_dtype, fuse=None):
    if form == "nn":
        (m, k), n = a.shape, b.shape[1]
    elif form == "nt":
        (m, k), n = a.shape, b.shape[0]
    else:
        (k, m), n = a.shape, b.shape[1]
    if form == "tn":
        tm = _pick(m, (1024, 512, 256, 128))
        tn = n if n <= 2048 else _pick(n, (1536, 1024, 512, 256, 128))
        tk = _pick(k, (2048, 1024, 512, 256, 128))
    else:
        tm = _pick(m, (512, 256, 128))
        tk = k if k <= 4096 else _pick(k, (2048, 1024, 512))
        cap = 2048 * 2048 if fuse is None else 2048 * 1024
        tn = n if n * tk <= cap else _pick(n, (2048, 1536, 1024, 512, 256, 128))
    nk = k // tk
    ca, cb = {"nn": (1, 0), "nt": (1, 1), "tn": (0, 0)}[form]
    a_spec = (pl.BlockSpec((tm, tk), lambda j, i, kk: (i, kk)) if form != "tn"
              else pl.BlockSpec((tk, tm), lambda j, i, kk: (kk, i)))
    b_spec = (pl.BlockSpec((tn, tk), lambda j, i, kk: (j, kk)) if form == "nt"
              else pl.BlockSpec((tk, tn), lambda j, i, kk: (kk, j)))
    o_spec = pl.BlockSpec((tm, tn), lambda j, i, kk: (i, j))
    col_spec = pl.BlockSpec((1, tn), lambda j, i, kk: (0, j))

    assert nk == 1 or (out_dtype == F32 and fuse is None)

    def body(*refs):
        a_ref, b_ref = refs[0], refs[1]
        prod = lax.dot_general(a_ref[...], b_ref[...], (((ca,), (cb,)), ((), ())), preferred_element_type=F32)
        if fuse is not None and fuse[0] == "relu2":
            pre = prod + refs[2][...]
            refs[3][...] = jnp.square(jnp.maximum(pre, 0.0)).astype(BF16)
            refs[4][...] = pre.astype(BF16)
            return
        if fuse is not None and fuse[0] == "relu2_bwd":
            d = prod * (2.0 * jnp.maximum(refs[2][...].astype(F32), 0.0))
            refs[3][...] = d.astype(BF16)

            @pl.when(pl.program_id(1) == 0)
            def _():
                refs[4][...] = jnp.zeros_like(refs[4])

            refs[4][...] += jnp.sum(d, axis=0, keepdims=True)
            return
        o_ref = refs[-1]
        if nk == 1:
            o_ref[...] = prod.astype(out_dtype)
            return
        kk = pl.program_id(2)

        @pl.when(kk == 0)
        def _():
            o_ref[...] = prod

        @pl.when(kk > 0)
        def _():
            o_ref[...] += prod

    ins = [a.astype(BF16), b.astype(BF16)]
    in_specs = [a_spec, b_spec]
    out_specs, out_shape = o_spec, jax.ShapeDtypeStruct((m, n), out_dtype)
    if fuse is not None and fuse[0] == "relu2":
        ins, in_specs = ins + [fuse[1]], in_specs + [col_spec]
        out_specs, out_shape = [o_spec, o_spec], [jax.ShapeDtypeStruct((m, n), BF16)] * 2
    elif fuse is not None:
        ins, in_specs = ins + [fuse[1]], in_specs + [o_spec]
        out_specs = [o_spec, col_spec]
        out_shape = [jax.ShapeDtypeStruct((m, n), BF16), jax.ShapeDtypeStruct((1, n), F32)]
    return pl.pallas_call(
        body, name=name, grid=(n // tn, m // tm, nk),
        in_specs=in_specs, out_specs=out_specs, out_shape=out_shape,
        compiler_params=_cparams(3),
    )(*ins)


SUM_NT_TM = 512


def matmul_multi_nn(name, a, bs, out_dtype):
    m, k = a.shape
    nb = len(bs)

    def body(*refs):
        x = refs[0][...]
        for r in range(nb):
            refs[1 + nb + r][...] = jnp.dot(x, refs[1 + r][...], preferred_element_type=F32).astype(out_dtype)

    return pl.pallas_call(
        body, name=name, grid=(m // SUM_NT_TM,),
        in_specs=[pl.BlockSpec((SUM_NT_TM, k), lambda i: (i, 0))] + [_full_spec(b.shape) for b in bs],
        out_specs=[pl.BlockSpec((SUM_NT_TM, b.shape[1]), lambda i: (i, 0)) for b in bs],
        out_shape=[jax.ShapeDtypeStruct((m, b.shape[1]), out_dtype) for b in bs], compiler_params=_cparams(),
    )(a.astype(BF16), *[b.astype(BF16) for b in bs])


def matmul_sum_nt(name, pairs, out_dtype, ride=None):
    m, n = pairs[0][0].shape[0], pairs[0][1].shape[0]
    npair = len(pairs)
    r_ins, r_in_specs, r_out_specs, r_out_shape, r_scratch = _ride_args(ride)

    def body(*refs):
        acc = None
        for r in range(npair):
            prod = lax.dot_general(refs[2 * r][...], refs[2 * r + 1][...], (((1,), (1,)), ((), ())), preferred_element_type=F32)
            acc = prod if acc is None else acc + prod
        refs[2 * npair + len(r_ins)][...] = acc.astype(out_dtype)
        _ride(ride, 2 * npair, 1, refs, m // SUM_NT_TM)

    ins, in_specs = [], []
    for a, b in pairs:
        ins += [a.astype(BF16), b.astype(BF16)]
        in_specs += [pl.BlockSpec((SUM_NT_TM, a.shape[1]), lambda i: (i, 0)), _full_spec(b.shape)]
    return pl.pallas_call(
        body, name=name, grid=(m // SUM_NT_TM,), in_specs=in_specs + r_in_specs,
        out_specs=[pl.BlockSpec((SUM_NT_TM, n), lambda i: (i, 0))] + r_out_specs,
        out_shape=[jax.ShapeDtypeStruct((m, n), out_dtype)] + r_out_shape, scratch_shapes=r_scratch,
        compiler_params=_cparams(),
    )(*ins, *r_ins)


MESH_ID = pl.DeviceIdType.MESH
ANY_SPEC = pl.BlockSpec(memory_space=pl.ANY)


def _place():
    return lax.axis_index("x"), lax.axis_index("y"), lax.axis_index("c")


def _chips():
    x, y, c = _place()
    return x, y, c, 2 * x + y, [(1 - x, y), (x, 1 - y), (1 - x, 1 - y)]


def scatter_rider(arrs):
    n = len(arrs)

    def remote(ins, outs, sems, i, j, landing_from_me):
        x, y, c, me, others = _chips()
        ox, oy = others[j]
        peer = 2 * ox + oy
        return pltpu.make_async_remote_copy(src_ref=ins[i].at[peer], dst_ref=outs[i].at[me if landing_from_me else peer],
                                            send_sem=sems[0].at[3 * i + j], recv_sem=sems[1].at[3 * i + j],
                                            device_id=(ox, oy, c), device_id_type=MESH_ID)

    def local(ins, outs, sems, i):
        me = _chips()[3]
        return pltpu.make_async_copy(ins[i].at[me], outs[i].at[me], sems[2].at[i])

    def start(ins, outs, sems):
        for i in range(n):
            local(ins, outs, sems, i).start()
            for j in range(3):
                remote(ins, outs, sems, i, j, True).start()

    def finish(ins, outs, sems):
        for i in range(n):
            for j in range(3):
                remote(ins, outs, sems, i, j, False).wait_recv()
        for i in range(n):
            for j in range(3):
                remote(ins, outs, sems, i, j, True).wait_send()
            local(ins, outs, sems, i).wait()

    return dict(ins=list(arrs), out_shape=[jax.ShapeDtypeStruct(a.shape, a.dtype) for a in arrs],
                scratch=[pltpu.SemaphoreType.DMA((3 * n,)), pltpu.SemaphoreType.DMA((3 * n,)), pltpu.SemaphoreType.DMA((n,))],
                start=start, middle=lambda ins, outs, sems: None, finish=finish)


def gather_rider(split, flat):
    ns, n = len(split), len(split) + len(flat)
    arrs = list(split) + list(flat)
    half = [a.shape[0] // 2 for a in split]

    def ici(ins, outs, sems, i, j, landing_from_me):
        x, y, c, me, others = _chips()
        ox, oy = others[j]
        slot = me if landing_from_me else 2 * ox + oy
        if i < ns:
            rows = pl.ds(c * half[i], half[i])
            src, dst = ins[i].at[rows], outs[i].at[slot, rows]
        else:
            src, dst = ins[i], outs[i].at[slot]
        return pltpu.make_async_remote_copy(src_ref=src, dst_ref=dst, send_sem=sems[0].at[3 * i + j],
                                            recv_sem=sems[1].at[3 * i + j], device_id=(ox, oy, c), device_id_type=MESH_ID)

    def d2d(outs, sems, i, j, mine):
        x, y, c, me, others = _chips()
        ox, oy = others[j]
        ref = outs[i].at[2 * ox + oy, pl.ds((c if mine else 1 - c) * half[i], half[i])]
        return pltpu.make_async_remote_copy(src_ref=ref, dst_ref=ref, send_sem=sems[2].at[3 * i + j],
                                            recv_sem=sems[3].at[3 * i + j], device_id=(x, y, 1 - c), device_id_type=MESH_ID)

    def local(ins, outs, sems, i):
        return pltpu.make_async_copy(ins[i], outs[i].at[_chips()[3]], sems[4].at[i])

    def start(ins, outs, sems):
        for i in range(n):
            local(ins, outs, sems, i).start()
            for j in range(3):
                ici(ins, outs, sems, i, j, True).start()

    def middle(ins, outs, sems):
        for i in range(n):
            for j in range(3):
                ici(ins, outs, sems, i, j, False).wait_recv()
                if i < ns:
                    d2d(outs, sems, i, j, True).start()

    def finish(ins, outs, sems):
        for i in range(ns):
            for j in range(3):
                d2d(outs, sems, i, j, False).wait_recv()
        for i in range(n):
            for j in range(3):
                ici(ins, outs, sems, i, j, True).wait_send()
                if i < ns:
                    d2d(outs, sems, i, j, True).wait_send()
            local(ins, outs, sems, i).wait()

    return dict(ins=arrs, out_shape=[jax.ShapeDtypeStruct((N_CHIPS,) + a.shape, a.dtype) for a in arrs],
                scratch=[pltpu.SemaphoreType.DMA((3 * n,)), pltpu.SemaphoreType.DMA((3 * n,)),
                         pltpu.SemaphoreType.DMA((3 * max(ns, 1),)), pltpu.SemaphoreType.DMA((3 * max(ns, 1),)),
                         pltpu.SemaphoreType.DMA((n,))],
                start=start, middle=middle, finish=finish)


def run_alone(name, ride):
    n_in, n_out = len(ride["ins"]), len(ride["out_shape"])

    def body(*refs):
        parts = refs[:n_in], refs[n_in:n_in + n_out], refs[n_in + n_out:]
        ride["start"](*parts)
        ride["middle"](*parts)
        ride["finish"](*parts)

    return pl.pallas_call(body, name=name, in_specs=[ANY_SPEC] * n_in, out_specs=[ANY_SPEC] * n_out,
                          out_shape=ride["out_shape"], scratch_shapes=ride["scratch"])(*ride["ins"])


def sibling_exchange(name, arrs):
    n = len(arrs)

    def body(*refs):
        ins, outs = refs[:n], refs[n:2 * n]
        send_sems, recv_sems = refs[2 * n:]
        x, y, c = _place()
        cps = [pltpu.make_async_remote_copy(src_ref=ins[i], dst_ref=outs[i], send_sem=send_sems.at[i], recv_sem=recv_sems.at[i],
                                            device_id=(x, y, 1 - c), device_id_type=MESH_ID) for i in range(n)]
        for cp in cps:
            cp.start()
        for cp in cps:
            cp.wait_recv()
        for cp in cps:
            cp.wait_send()

    return pl.pallas_call(
        body, name=name, in_specs=[ANY_SPEC] * n, out_specs=[ANY_SPEC] * n,
        out_shape=[jax.ShapeDtypeStruct(a.shape, a.dtype) for a in arrs],
        scratch_shapes=[pltpu.SemaphoreType.DMA((n,)), pltpu.SemaphoreType.DMA((n,))],
    )(*arrs)


def gather_all_rider(vec):
    def remote(ins, outs, sems, mask, landing_from_me):
        x, y, c = _place()
        px, py, pc = (1 - x if mask & 4 else x, 1 - y if mask & 2 else y, 1 - c if mask & 1 else c)
        slot = 4 * x + 2 * y + c if landing_from_me else 4 * px + 2 * py + pc
        return pltpu.make_async_remote_copy(src_ref=ins[0], dst_ref=outs[0].at[slot], send_sem=sems[0].at[mask - 1],
                                            recv_sem=sems[1].at[mask - 1], device_id=(px, py, pc), device_id_type=MESH_ID)

    def local(ins, outs, sems):
        x, y, c = _place()
        return pltpu.make_async_copy(ins[0], outs[0].at[4 * x + 2 * y + c], sems[2].at[0])

    def start(ins, outs, sems):
        local(ins, outs, sems).start()
        for mask in range(1, N_DEV):
            remote(ins, outs, sems, mask, True).start()

    def finish(ins, outs, sems):
        for mask in range(1, N_DEV):
            remote(ins, outs, sems, mask, False).wait_recv()
        for mask in range(1, N_DEV):
            remote(ins, outs, sems, mask, True).wait_send()
        local(ins, outs, sems).wait()

    return dict(ins=[vec], out_shape=[jax.ShapeDtypeStruct((N_DEV,) + vec.shape, vec.dtype)],
                scratch=[pltpu.SemaphoreType.DMA((N_DEV - 1,)), pltpu.SemaphoreType.DMA((N_DEV - 1,)), pltpu.SemaphoreType.DMA((1,))],
                start=start, middle=lambda ins, outs, sems: None, finish=finish)


def sum_slots(name, arrs, tr):
    k, r, c = arrs[0].shape
    nb = r // tr

    def body(*refs):
        o_ref = refs[-1]
        for li, a_ref in enumerate(refs[:-1]):
            @pl.when(pl.program_id(0) == li)
            def _(a_ref=a_ref):
                acc = a_ref[0].astype(F32)
                for j in range(1, k):
                    acc = acc + a_ref[j].astype(F32)
                o_ref[...] = acc

    def in_spec(li):
        return pl.BlockSpec((k, tr, c), lambda l, i: (0, jnp.where(l == li, i, jnp.where(l < li, 0, nb - 1)), 0))

    return pl.pallas_call(
        body, name=name, grid=(len(arrs), nb),
        in_specs=[in_spec(li) for li in range(len(arrs))], out_specs=pl.BlockSpec((tr, c), lambda l, i: (l * nb + i, 0)),
        out_shape=jax.ShapeDtypeStruct((len(arrs) * r, c), F32), compiler_params=_cparams(2),
    )(*arrs)


def adamw(name, parts, w, m, v, tr, ride=None):
    r, c = w.shape
    npart = len(parts)
    bc1 = 1.0 - ADAM_B1 ** ADAM_STEP
    bc2 = 1.0 - ADAM_B2 ** ADAM_STEP
    r_ins, r_in_specs, r_out_specs, r_out_shape, r_scratch = _ride_args(ride)

    def body(*refs):
        _ride(ride, npart + 3, 4, refs, r // tr)
        g = refs[0][...]
        for j in range(1, npart):
            g = g + refs[j][...]
        w_ref, m_ref, v_ref = refs[npart:npart + 3]
        g_out, d_out, m_out, v_out = refs[npart + 3 + len(r_ins):npart + 7 + len(r_ins)]
        m_new = ADAM_B1 * m_ref[...] + (1.0 - ADAM_B1) * g
        v_new = ADAM_B2 * v_ref[...] + (1.0 - ADAM_B2) * jnp.square(g)
        m_hat = m_new / bc1
        v_hat = v_new / bc2
        g_out[...] = g
        d_out[...] = -ADAM_LR * (m_hat / (jnp.sqrt(v_hat) + ADAM_EPS) + ADAM_WD * w_ref[...])
        m_out[...] = m_new
        v_out[...] = v_new

    spec = pl.BlockSpec((tr, c), lambda i: (i, 0))
    return pl.pallas_call(
        body, name=name, grid=(r // tr,), in_specs=[spec] * (npart + 3) + r_in_specs, out_specs=[spec] * 4 + r_out_specs,
        out_shape=[jax.ShapeDtypeStruct((r, c), F32)] * 4 + r_out_shape, scratch_shapes=r_scratch,
        compiler_params=_cparams(),
    )(*parts, w, m, v, *r_ins)


ROW_TM = 256
POINT_TM = 512


def _row(v):
    return v.reshape(1, -1)


def _lane_pad(vals, at):
    return jnp.concatenate([jnp.zeros((at,), F32), vals, jnp.zeros((LANES - at - vals.shape[0],), F32)]).reshape(1, LANES)


def _layer_params(sm, l):
    return dict(
        conf=[sm["conv_dw_w"][l], _row(sm["conv_dw_b"][l]), _row(sm["conv_ln_g"][l]), _row(sm["conv_ln_b"][l])],
        pre=[sm["gdn_conv_q"][l], sm["gdn_conv_k"][l], sm["gdn_conv_v"][l],
             _lane_pad(sm["gdn_a_log"][l], GDN_HEADS), _lane_pad(sm["gdn_dt_bias"][l], GDN_HEADS)],
        scan=_row(sm["gdn_norm_g"][l]),
        sgu=[_row(sm["sgu_ln_g"][l]), _row(sm["sgu_ln_b"][l]), sm["sgu_w_s"][l].reshape(SGU_GROUPS * SGU_CHUNK, SGU_CHUNK),
             sm["sgu_b_s"][l].T],
        mix=[_row(sm["b_gate"][l])],
        ln1=[_row(sm["ln1_g"][l]), _row(sm["ln1_b"][l])],
        ff=[_row(sm["b_ff1"][l])],
        ln2=[_row(sm["b_ff2"][l]), _row(sm["ln2_g"][l]), _row(sm["ln2_b"][l])],
    )


def _split_w_in(w):
    zpad = jnp.zeros((D_MODEL, LANES - 2 * GDN_HEADS), w.dtype)
    return dict(A=w[:, 0:1024], B=jnp.concatenate([w[:, 1024:2560], w[:, 3072:3080], zpad], axis=1), Z=w[:, 2560:3072],
                C=w[:, 3080:4104], G=w[:, 4104:7176])


def _join_w_in(d):
    return jnp.concatenate([d["A"], d["B"][:, 0:1536], d["Z"], d["B"][:, 1536:1544], d["C"], d["G"]], axis=1)


def _hosted(get, host):
    got = get(host) if get is not None else None
    return got if got is not None else (None, lambda results: None)


def layer_forward(l, xin, xin_b, big, lp, fetch=None):
    tag = f"l{l}_"
    wi = big["w_in"][l]
    sv = dict(xin=xin, xin_b=xin_b)
    for r, p in zip("GACBZ", matmul_multi_nn(tag + "proj", xin_b, [wi[r] for r in "GACBZ"], ACT)):
        sv["p" + r] = p
    ride, done = _hosted(fetch, "conformer")
    sv["ya_in"], *rode = halo_fwd(tag + "conformer", f_conformer, ROW_TM, CONV_HALO, sv["pA"], lp["conf"], [[BF16]], ride)
    done(rode)
    ride, done = _hosted(fetch, "gdn_pre")
    sv["qn"], sv["kn"], sv["vv"], sv["gb"], *rode = halo_fwd(tag + "gdn_pre", f_gdn_pre, ROW_TM, GDN_HALO, sv["pB"], lp["pre"],
                                                             [[F32]] * 4, ride)
    done(rode)
    ride, done = _hosted(fetch, "gdn_prep")
    prep_out = prep_fwd(tag + "gdn_prep", sv["qn"], sv["kn"], sv["vv"], sv["gb"], ride)
    prep, sv["tinv"] = prep_out[:6], prep_out[6]
    done(prep_out[7:])
    sv["prep"] = prep
    ride, done = _hosted(fetch, "gdn_scan")
    sv["yb_in"], sv["states"], *rode = scan_fwd(tag + "gdn_scan", lp["scan"], *prep, sv["pZ"], ride)
    done(rode)
    (sv["yc_in"],) = stage_fwd(tag + "sgu", f_sgu, POINT_TM, [sv["pC"]], lp["sgu"], [[BF16]])
    sv["ya"] = matmul(tag + "out_a", sv["ya_in"], big["w_pa"][l], "nn", ACT)
    sv["yb"] = matmul(tag + "out_b", sv["yb_in"], big["w_pb"][l], "nn", ACT)
    sv["yc"] = matmul(tag + "out_c", sv["yc_in"], big["w_pc"][l], "nn", ACT)
    (sv["mix"],) = stage_fwd(tag + "mix", f_mix, POINT_TM, [sv["ya"], sv["yb"], sv["yc"], sv["pG"]], lp["mix"], [[BF16]])
    sv["m"] = matmul(tag + "w_o", sv["mix"], big["w_o"][l], "nn", F32)
    sv["x1"], sv["x1_b"] = stage_fwd(tag + "ln1", f_ln1, POINT_TM, [xin, sv["m"]], lp["ln1"], [[F32, BF16]])
    sv["h"], sv["hpre"] = matmul(tag + "ff1", sv["x1_b"], big["w_ff1"][l], "nn", BF16, fuse=("relu2", lp["ff"][0]))
    sv["f2"] = matmul(tag + "ff2", sv["h"], big["w_ff2"][l], "nn", F32)
    x2, x2_b = stage_fwd(tag + "ln2", f_ln2, POINT_TM, [sv["x1"], sv["f2"]], lp["ln2"], [[F32, BF16]])
    return x2, x2_b, sv


def layer_backward(l, dx2, sv, big, lp, send=None):
    tag = f"l{l}_b_"
    g = {}
    get = (lambda host: send(host, g)) if send is not None else None
    dx1_a, df2, db2, dg2, dbe2 = stage_bwd(tag + "ln2", f_ln2, POINT_TM, [sv["x1"], sv["f2"]], lp["ln2"], [dx2], [F32, BF16])
    g["b_ff2"], g["ln2_g"], g["ln2_b"] = db2[0], dg2[0], dbe2[0]
    g["w_ff2"] = matmul(tag + "dw_ff2", sv["h"], df2, "tn", F32)
    dhpre, db1 = matmul(tag + "dx_ff2", df2, big["w_ff2"][l], "nt", BF16, fuse=("relu2_bwd", sv["hpre"]))
    g["b_ff1"] = db1[0]
    g["w_ff1"] = matmul(tag + "dw_ff1", sv["x1_b"], dhpre, "tn", F32)
    dx1_b = matmul(tag + "dx_ff1", dhpre, big["w_ff1"][l], "nt", F32)
    dxin_a, dm, dg1, dbe1 = stage_bwd(tag + "ln1", f_ln1, POINT_TM, [sv["xin"], sv["m"]], lp["ln1"], [[dx1_a, dx1_b]], [F32, BF16])
    g["ln1_g"], g["ln1_b"] = dg1[0], dbe1[0]
    g["w_o"] = matmul(tag + "dw_o", sv["mix"], dm, "tn", F32)
    dmix = matmul(tag + "dx_o", dm, big["w_o"][l], "nt", ACT)
    dya, dyb, dyc, dp_g, dbg = stage_bwd(tag + "mix", f_mix, POINT_TM, [sv["ya"], sv["yb"], sv["yc"], sv["pG"]], lp["mix"],
                                         [[dmix]], [BF16] * 4)
    g["b_gate"] = dbg[0]
    g["w_pa"] = matmul(tag + "dw_pa", sv["ya_in"], dya, "tn", F32)
    g["w_pb"] = matmul(tag + "dw_pb", sv["yb_in"], dyb, "tn", F32)
    g["w_pc"] = matmul(tag + "dw_pc", sv["yc_in"], dyc, "tn", F32)
    dya_in = matmul(tag + "dx_pa", dya, big["w_pa"][l], "nt", ACT)
    dyb_in = matmul(tag + "dx_pb", dyb, big["w_pb"][l], "nt", ACT)
    dyc_in = matmul(tag + "dx_pc", dyc, big["w_pc"][l], "nt", ACT)
    ride, done = _hosted(get, "conformer")
    dp_a, dcw, dcb, dcg, dcbe, *rode = halo_bwd(tag + "conformer", f_conformer, ROW_TM, CONV_HALO, sv["pA"], lp["conf"], [dya_in],
                                                BF16, ride)
    done(rode)
    g["conv_dw_w"], g["conv_dw_b"], g["conv_ln_g"], g["conv_ln_b"] = dcw, dcb[0], dcg[0], dcbe[0]
    dp_c, dsg, dsb, dsw, dsbs = stage_bwd(tag + "sgu", f_sgu, POINT_TM, [sv["pC"]], lp["sgu"], [[dyc_in]], [BF16])
    g["sgu_ln_g"], g["sgu_ln_b"] = dsg[0], dsb[0]
    g["sgu_w_s"] = dsw.reshape(SGU_GROUPS, SGU_CHUNK, SGU_CHUNK)
    g["sgu_b_s"] = dsbs.T
    ride, done = _hosted(get, "gdn_scan")
    scan_out = scan_bwd(tag + "gdn_scan", lp["scan"], sv["states"], *sv["prep"], sv["pZ"], dyb_in, ride)
    dprep, dp_z, dng = scan_out[:6], scan_out[6], scan_out[7]
    done(scan_out[8:])
    g["gdn_norm_g"] = dng[0]
    dqn, dkn, dvv, dgb = prep_bwd(tag + "gdn_prep", sv["qn"], sv["kn"], sv["vv"], sv["gb"], sv["tinv"], dprep)
    ride, done = _hosted(get, "gdn_pre")
    dp_b, dq, dk, dv, dal, ddt, *rode = halo_bwd(tag + "gdn_pre", f_gdn_pre, ROW_TM, GDN_HALO, sv["pB"], lp["pre"],
                                                 [dqn, dkn, dvv, dgb], BF16, ride)
    done(rode)
    g["gdn_conv_q"], g["gdn_conv_k"], g["gdn_conv_v"] = dq, dk, dv
    g["gdn_a_log"] = dal[0, GDN_HEADS:2 * GDN_HEADS]
    g["gdn_dt_bias"] = ddt[0, GDN_HEADS:2 * GDN_HEADS]
    dps = dict(G=dp_g, A=dp_a, C=dp_c, B=dp_b, Z=dp_z)
    wi = big["w_in"][l]
    g["w_in"] = _join_w_in({r: matmul(tag + "dw_in_" + r, sv["xin_b"], dps[r], "tn", F32) for r in "GACBZ"})
    ride, done = _hosted(get, "dx_in")
    dxin_b, *rode = matmul_sum_nt(tag + "dx_in", [(dps[r], wi[r]) for r in "GACBZ"], F32, ride)
    done(rode)
    return [dxin_a, dxin_b], g


SMALL_PER_LAYER = ("b_gate", "conv_dw_w", "conv_dw_b", "conv_ln_g", "conv_ln_b", "gdn_conv_q", "gdn_conv_k", "gdn_conv_v",
                   "gdn_a_log", "gdn_dt_bias", "gdn_norm_g", "sgu_ln_g", "sgu_ln_b", "sgu_w_s", "sgu_b_s", "ln1_g", "ln1_b",
                   "b_ff1", "b_ff2", "ln2_g", "ln2_b")
BIG = ("w_in", "w_pa", "w_pb", "w_pc", "w_o", "w_ff1", "w_ff2")


def local_step(x, target, big, sm, fetch=None, send=None):
    p_in = [_row(sm["ln_in_g"]), _row(sm["ln_in_b"])]
    ride, done = _hosted(fetch, "ln_in")
    xc, xc_b, *rode = stage_fwd("ln_in", f_ln_in, POINT_TM, [x], p_in, [[F32, BF16]], ride)
    done(rode)
    lps = [_layer_params(sm, l) for l in range(DEPTH)]
    saved = []
    for l in range(DEPTH):
        xc, xc_b, sv = layer_forward(l, xc, xc_b, big, lps[l], fetch if l == 0 else None)
        saved.append(sv)
    loss_part, dy = loss_stage("loss", xc, target, POINT_TM)
    dx = [dy]
    per_layer = [None] * DEPTH

    def send_from(l):
        if send is None or l != 0:
            return None
        return lambda host, g: send(host, per_layer[:l] + [g] + per_layer[l + 1:])

    for l in reversed(range(DEPTH)):
        dx, per_layer[l] = layer_backward(l, dx, saved[l], big, lps[l], send_from(l))
    grad_x, dgi, dbi = stage_bwd("ln_in_b", f_ln_in, POINT_TM, [x], p_in, [dx], [F32])
    small = {n: jnp.stack([per_layer[l][n] for l in range(DEPTH)]) for n in SMALL_PER_LAYER}
    small["ln_in_g"], small["ln_in_b"] = dgi[0], dbi[0]
    return loss_part, grad_x, small, per_layer


PARAM_NAMES = ("ln_in_g", "ln_in_b", "w_in", "b_gate", "conv_dw_w", "conv_dw_b", "conv_ln_g", "conv_ln_b", "w_pa",
               "gdn_conv_q", "gdn_conv_k", "gdn_conv_v", "gdn_a_log", "gdn_dt_bias", "gdn_norm_g", "w_pb", "sgu_ln_g",
               "sgu_ln_b", "sgu_w_s", "sgu_b_s", "w_pc", "w_o", "ln1_g", "ln1_b", "w_ff1", "b_ff1", "w_ff2", "b_ff2",
               "ln2_g", "ln2_b")
SMALL = tuple(n for n in PARAM_NAMES if n not in BIG)
COL_SHARDED_SMALL = ("conv_dw_w", "gdn_conv_q", "gdn_conv_k", "gdn_conv_v")
ROW_SHARDED_BIG = ("w_o", "w_ff2")
ADAM_ROWS = 128
_OUT_PROJ = ("w_pa", "w_pb", "w_pc", "w_o")
FETCH_BEHIND = {"conformer": [("w_ff1", 0), ("w_ff2", 0)], "gdn_pre": [(n, l) for l in (0, 1) for n in _OUT_PROJ],
                "gdn_prep": [("w_in", 1), ("w_ff1", 1)], "gdn_scan": [("w_ff2", 1)]}
SEND_BEHIND = {"conformer": [(n, 1) for n in BIG], "gdn_scan": [("w_ff1", 0), ("w_ff2", 0)],
               "gdn_pre": [(n, 0) for n in _OUT_PROJ], "dx_in": [("w_in", 0)]}


def _unshard(g, rows):
    n, r, c = g.shape
    if rows:
        return g.reshape(n * r, c)
    return jnp.moveaxis(g, 0, 1).reshape(r, n * c)


def _to_shards(w, rows):
    r, c = w.shape
    if rows:
        return w.reshape(N_CHIPS, r // N_CHIPS, c)
    return jnp.moveaxis(w.reshape(r, N_CHIPS, c // N_CHIPS), 1, 0)


def _pack(vals, rows_multiple=8):
    flat = jnp.concatenate([v.reshape(-1) for v in vals])
    per = LANES * rows_multiple
    padded = -(-flat.shape[0] // per) * per
    return jnp.pad(flat, (0, padded - flat.shape[0])).reshape(-1, LANES)


def _unpack(packed, shapes):
    flat = packed.reshape(-1)
    out, at = [], 0
    for s in shapes:
        size = 1
        for d in s:
            size *= d
        out.append(flat[at:at + size].reshape(s))
        at += size
    return out


def kernel(x, ln_in_g, ln_in_b, w_in, b_gate, conv_dw_w, conv_dw_b, conv_ln_g, conv_ln_b, w_pa, gdn_conv_q, gdn_conv_k, gdn_conv_v, gdn_a_log, gdn_dt_bias, gdn_norm_g, w_pb, sgu_ln_g, sgu_ln_b, sgu_w_s, sgu_b_s, w_pc, w_o, ln1_g, ln1_b, w_ff1, b_ff1, w_ff2, b_ff2, ln2_g, ln2_b, loss_target, m_ln_in_g, m_ln_in_b, m_w_in, m_b_gate, m_conv_dw_w, m_conv_dw_b, m_conv_ln_g, m_conv_ln_b, m_w_pa, m_gdn_conv_q, m_gdn_conv_k, m_gdn_conv_v, m_gdn_a_log, m_gdn_dt_bias, m_gdn_norm_g, m_w_pb, m_sgu_ln_g, m_sgu_ln_b, m_sgu_w_s, m_sgu_b_s, m_w_pc, m_w_o, m_ln1_g, m_ln1_b, m_w_ff1, m_b_ff1, m_w_ff2, m_b_ff2, m_ln2_g, m_ln2_b, v_ln_in_g, v_ln_in_b, v_w_in, v_b_gate, v_conv_dw_w, v_conv_dw_b, v_conv_ln_g, v_conv_ln_b, v_w_pa, v_gdn_conv_q, v_gdn_conv_k, v_gdn_conv_v, v_gdn_a_log, v_gdn_dt_bias, v_gdn_norm_g, v_w_pb, v_sgu_ln_g, v_sgu_ln_b, v_sgu_w_s, v_sgu_b_s, v_w_pc, v_w_o, v_ln1_g, v_ln1_b, v_w_ff1, v_b_ff1, v_w_ff2, v_b_ff2, v_ln2_g, v_ln2_b):
    w = dict(zip(PARAM_NAMES, (ln_in_g, ln_in_b, w_in, b_gate, conv_dw_w, conv_dw_b, conv_ln_g, conv_ln_b, w_pa, gdn_conv_q, gdn_conv_k, gdn_conv_v, gdn_a_log, gdn_dt_bias, gdn_norm_g, w_pb, sgu_ln_g, sgu_ln_b, sgu_w_s, sgu_b_s, w_pc, w_o, ln1_g, ln1_b, w_ff1, b_ff1, w_ff2, b_ff2, ln2_g, ln2_b)))
    m = dict(zip(PARAM_NAMES, (m_ln_in_g, m_ln_in_b, m_w_in, m_b_gate, m_conv_dw_w, m_conv_dw_b, m_conv_ln_g, m_conv_ln_b, m_w_pa, m_gdn_conv_q, m_gdn_conv_k, m_gdn_conv_v, m_gdn_a_log, m_gdn_dt_bias, m_gdn_norm_g, m_w_pb, m_sgu_ln_g, m_sgu_ln_b, m_sgu_w_s, m_sgu_b_s, m_w_pc, m_w_o, m_ln1_g, m_ln1_b, m_w_ff1, m_b_ff1, m_w_ff2, m_b_ff2, m_ln2_g, m_ln2_b)))
    v = dict(zip(PARAM_NAMES, (v_ln_in_g, v_ln_in_b, v_w_in, v_b_gate, v_conv_dw_w, v_conv_dw_b, v_conv_ln_g, v_conv_ln_b, v_w_pa, v_gdn_conv_q, v_gdn_conv_k, v_gdn_conv_v, v_gdn_a_log, v_gdn_dt_bias, v_gdn_norm_g, v_w_pb, v_sgu_ln_g, v_sgu_ln_b, v_sgu_w_s, v_sgu_b_s, v_w_pc, v_w_o, v_ln1_g, v_ln1_b, v_w_ff1, v_b_ff1, v_w_ff2, v_b_ff2, v_ln2_g, v_ln2_b)))
    chip = 2 * lax.axis_index("x") + lax.axis_index("y")

    shards = {n: w[n].astype(BF16) for n in BIG}
    col_small_shapes = [w[n].shape for n in COL_SHARDED_SMALL]
    big = {n: [None] * DEPTH for n in BIG}

    def place(keys, gathered):
        for (n, l), g in zip(keys, gathered):
            whole = _unshard(g, n in ROW_SHARDED_BIG)
            big[n][l] = _split_w_in(whole) if n == "w_in" else whole

    sm = {n: w[n] for n in SMALL if n not in COL_SHARDED_SMALL}

    def place_first(gathered):
        place([("w_in", 0)], gathered[:1])
        per_chip = [_unpack(gathered[-1][s], col_small_shapes) for s in range(N_CHIPS)]
        for i, n in enumerate(COL_SHARDED_SMALL):
            sm[n] = jnp.concatenate([per_chip[s][i] for s in range(N_CHIPS)], axis=-1)

    def fetch(host):
        if host == "ln_in":
            return gather_rider([shards["w_in"][0]], [_pack([w[n] for n in COL_SHARDED_SMALL])]), place_first
        keys = FETCH_BEHIND[host]
        return gather_rider([shards[n][l] for n, l in keys], []), lambda gathered: place(keys, gathered)

    received = {}

    def send(host, per_layer):
        keys = SEND_BEHIND[host]
        ride = scatter_rider([_to_shards(per_layer[l][n], n in ROW_SHARDED_BIG).astype(BF16) for n, l in keys])
        return ride, lambda results: received.update(zip(keys, results))

    loss_part, grad_x, grads, _ = local_step(x[0], loss_target[0], big, sm, fetch, send)
    partial = [sum_slots("sum_chips_" + n, [received[(n, l)] for l in range(DEPTH)], ADAM_ROWS) for n in BIG]
    sibling = sibling_exchange("swap_cores", partial)
    small_shapes = [grads[n].shape for n in SMALL]
    vec = _pack([grads[n] for n in SMALL] + [jnp.sum(loss_part).reshape(1)])
    out = {}
    for n, mine, theirs in zip(BIG, partial, sibling):
        shape = w[n].shape
        res = adamw("adamw_" + n, [mine, theirs], w[n].reshape(-1, shape[-1]), m[n].reshape(-1, shape[-1]),
                    v[n].reshape(-1, shape[-1]), ADAM_ROWS, gather_all_rider(vec) if n == "w_in" else None)
        out[n] = [r.reshape(shape) for r in res[:4]]
        if n == "w_in":
            all_small = res[4]
    total = sum_slots("sum_small", [all_small], vec.shape[0])
    whole = _unpack(total, small_shapes + [(1,)])
    loss = whole[-1][0]
    g_small = {}
    for n, g in zip(SMALL, whole[:-1]):
        if n in COL_SHARDED_SMALL:
            width = g.shape[-1] // N_CHIPS
            g = lax.dynamic_slice_in_dim(g, chip * width, width, axis=g.ndim - 1)
        g_small[n] = g
    local_shapes = [w[n].shape for n in SMALL]
    packed_g = _pack([g_small[n] for n in SMALL])
    res = adamw("adamw_small", [packed_g], _pack([w[n] for n in SMALL]), _pack([m[n] for n in SMALL]),
                _pack([v[n] for n in SMALL]), packed_g.shape[0])
    unpacked = [_unpack(r, local_shapes) for r in res]
    for i, n in enumerate(SMALL):
        out[n] = [unpacked[k][i] for k in range(4)]

    return (loss, grad_x[None], *[out[n][0] for n in PARAM_NAMES], *[out[n][1] for n in PARAM_NAMES],
            *[out[n][2] for n in PARAM_NAMES], *[out[n][3] for n in PARAM_NAMES])
```
